```python
import math
import jax, jax.numpy as jnp
from jax import lax
import numpy as np

D_MODEL = 1024
BATCH = 8
SEQ = 2048
DEPTH = 2
DEC_BATCH = 128
DEC_SEQ = 8
PAST_LEN = 16384
PAGE_SIZE = 128

HEAD_DIM = 64
ROPE_THETA = 10000.0
BLOCK = 128
EPS = 1e-6
NEG_INF = -1e30
A_HEADS = 4
A_GROUPS = ((128, 1), (512, 4), (2048, 16))
A_NG = len(A_GROUPS)
A_QKV = A_NG * A_HEADS * HEAD_DIM
B_Q_HEADS = 8
B_KV_HEADS = 2
B_GROUP = B_Q_HEADS // B_KV_HEADS
B_WINDOW = 128
C_WIDTH = 256
C_CONV = 3
D_GROUP = 16
D_NGROUPS = 16
D_WIDTH = D_GROUP * D_NGROUPS
D_STATE = 64
N_BRANCH = 4
D_FF = -(-(-(-8 * D_MODEL // 3)) // 256) * 256
PLE_DIM = 256
IN_SPLITS = (A_QKV, A_QKV, A_QKV, B_Q_HEADS * HEAD_DIM, B_KV_HEADS * HEAD_DIM, B_KV_HEADS * HEAD_DIM,
             C_WIDTH, C_WIDTH, C_WIDTH, D_WIDTH, N_BRANCH * D_MODEL)
IN_WIDTH = sum(IN_SPLITS)
SPLIT_POINTS = tuple(sum(IN_SPLITS[:i + 1]) for i in range(len(IN_SPLITS) - 1))
N_STATE = 11

kernel_name = 'hybrid_gated_dilated_swa_conv_s5_decode_step'


def rmsnorm(x, g):
    xf = x.astype(jnp.float32)
    y = xf * lax.rsqrt(jnp.mean(xf * xf, axis=-1, keepdims=True) + EPS)
    return (y * g.astype(jnp.float32)).astype(x.dtype)


def rope(x, pos):
    half = HEAD_DIM // 2
    inv = ROPE_THETA ** (-jnp.arange(half, dtype=jnp.float32) / half)
    ang = pos.astype(jnp.float32)[:, None] * inv[None, :]
    shape = (1, pos.shape[0]) + (1,) * (x.ndim - 3) + (half,)
    cos = jnp.cos(ang).reshape(shape)
    sin = jnp.sin(ang).reshape(shape)
    xf = x.astype(jnp.float32)
    x1, x2 = xf[..., :half], xf[..., half:]
    return jnp.concatenate([x1 * cos - x2 * sin, x2 * cos + x1 * sin], axis=-1).astype(x.dtype)


def last_rows(x, n):
    L = x.shape[1]
    if L >= n:
        return x[:, L - n:]
    return jnp.pad(x, ((0, 0), (n - L, 0)) + ((0, 0),) * (x.ndim - 2))


def attend(q, k, v, mask, sinks=None):
    s = jnp.einsum('...qhgd,...khd->...hgqk', q.astype(jnp.float32), k.astype(jnp.float32)) * (HEAD_DIM ** -0.5)
    s = jnp.where(mask, s, NEG_INF)
    m = jnp.max(s, axis=-1)
    if sinks is not None:
        sk = sinks.astype(jnp.float32)[:, :, None]
        m = jnp.maximum(m, sk)
    p = jnp.exp(s - m[..., None])
    den = jnp.sum(p, axis=-1)
    if sinks is not None:
        den = den + jnp.exp(sk - m)
    o = jnp.einsum('...hgqk,...khd->...qhgd', p, v.astype(jnp.float32))
    den_q = jnp.moveaxis(den, -1, -3)
    lse = jnp.moveaxis(m, -1, -3) + jnp.log(den_q)
    return o / den_q[..., None], lse


def banded_attention(q, k, v, window, sinks=None):
    bsz, L = q.shape[0], q.shape[1]
    blk = min(BLOCK, L)
    nb = -(-L // blk)
    Lp = nb * blk
    pad = -(-window // blk) * blk
    qp = jnp.pad(q, ((0, 0), (0, Lp - L)) + ((0, 0),) * (q.ndim - 2))
    kp = jnp.pad(k, ((0, 0), (pad, Lp - L), (0, 0), (0, 0)))
    vp = jnp.pad(v, ((0, 0), (pad, Lp - L), (0, 0), (0, 0)))
    kidx = (jnp.arange(nb) * blk)[:, None] + jnp.arange(pad + blk)[None, :]
    kb = kp[:, kidx]
    vb = vp[:, kidx]
    qb = qp.reshape((bsz, nb, blk) + q.shape[2:])
    dist = jnp.arange(blk)[:, None] + pad - jnp.arange(pad + blk)[None, :]
    mask = ((dist >= 0) & (dist <= window))[None] & ((kidx - pad) >= 0)[:, None, :]
    o, lse = attend(qb, kb, vb, mask[:, None, None], sinks)
    o = o.reshape((bsz, Lp) + q.shape[2:])[:, :L]
    lse = lse.reshape((bsz, Lp) + q.shape[2:-1])[:, :L]
    return o, lse


def dilated_prompt(q, k, v, w, d):
    bsz, L, H, dh = q.shape
    Ls = L // d

    def strided(t):
        return t.reshape(bsz, Ls, d, H, dh).transpose(0, 2, 1, 3, 4).reshape(bsz * d, Ls, H, dh)

    o, lse = banded_attention(strided(q)[:, :, :, None], strided(k), strided(v), w // d)
    o = o.reshape(bsz, d, Ls, H, dh).transpose(0, 2, 1, 3, 4).reshape(bsz, L, H, dh)
    lse = lse.reshape(bsz, d, Ls, H).transpose(0, 2, 1, 3).reshape(bsz, L, H)
    return o, lse


def dilated_sample(q, fk, fv, w, d):
    T = q.shape[1]
    lb = fk.shape[1] - T
    nk = w // d + 1
    idx = lb + jnp.arange(T)[:, None] - d * jnp.arange(nk)[None, :]
    valid = idx >= 0
    idx = jnp.maximum(idx, 0)
    kg = fk[:, idx]
    vg = fv[:, idx]
    o, lse = attend(q[:, :, None, :, None, :], kg, vg, valid[:, None, None, None, :])
    bsz, _, H, dh = q.shape
    return o.reshape(bsz, T, H, dh), lse.reshape(bsz, T, H)


def window_sample(q, fk, fv, window, sinks):
    T = q.shape[1]
    K = fk.shape[1]
    dist = (K - T) + jnp.arange(T)[:, None] - jnp.arange(K)[None, :]
    mask = (dist >= 0) & (dist <= window)
    o, _ = attend(q, fk, fv, mask, sinks)
    return o


def short_conv(hc, bc, cc, conv_w, prev):
    L = hc.shape[1]
    zc = cc * hc
    zp = jnp.concatenate([prev.astype(zc.dtype), zc], axis=1)
    y = conv_w[0] * zp[:, 0:L]
    for j in range(1, C_CONV):
        y = y + conv_w[j] * zp[:, j:j + L]
    return bc * y, zp[:, zp.shape[1] - (C_CONV - 1):]


def _ssm_combine(e1, e2):
    a1, b1 = e1
    a2, b2 = e2
    return a1 * a2, a2 * b1 + b2


def ssm_mixer(u, lp, x0_re, x0_im):
    f32 = jnp.float32
    bsz, L, _ = u.shape
    uf = u.astype(f32).reshape(bsz, L, D_NGROUPS, D_GROUP)
    lam = lax.complex(lp['ssm_lam_re'].astype(f32), lp['ssm_lam_im'].astype(f32))
    dt = jnp.exp(lp['ssm_log_dt'].astype(f32))[:, None]
    a_bar = jnp.exp(lam * dt)
    b_mat = lax.complex(lp['ssm_b_re'].astype(f32), lp['ssm_b_im'].astype(f32))
    c_mat = lax.complex(lp['ssm_c_re'].astype(f32), lp['ssm_c_im'].astype(f32))
    b_bar = ((a_bar - 1.0) / lam)[..., None] * b_mat
    bu = jnp.einsum('blgi,gni->blgn', uf.astype(jnp.complex64), b_bar)
    x0 = lax.complex(x0_re.astype(f32), x0_im.astype(f32))
    bu = bu.at[:, 0].add(a_bar * x0)
    a_seq = jnp.broadcast_to(a_bar, bu.shape)
    _, xs = lax.associative_scan(_ssm_combine, (a_seq, bu), axis=1)
    y = jnp.einsum('blgn,gin->blgi', xs, c_mat).real + lp['ssm_d'].astype(f32) * uf
    y = jax.nn.gelu(y.reshape(bsz, L, D_WIDTH)) @ lp['w_d_glu'].astype(f32)
    y = y[..., :D_WIDTH] * jax.nn.sigmoid(y[..., D_WIDTH:])
    return y.astype(u.dtype), jnp.real(xs[:, -1]), jnp.imag(xs[:, -1])


def token_mix(h, pos, lp, cache):
    bsz, L, _ = h.shape
    z = h @ lp['w_in']
    qa, ka, va, qb, kb, vb, hc, bc, cc, ud, gl = jnp.split(z, SPLIT_POINTS, axis=-1)
    a_shape = (bsz, L, A_NG, A_HEADS, HEAD_DIM)
    qa = rope(qa.reshape(a_shape), pos)
    ka = rope(ka.reshape(a_shape), pos)
    va = va.reshape(a_shape)
    qb = rope(qb.reshape(bsz, L, B_KV_HEADS, B_GROUP, HEAD_DIM), pos)
    kb = rope(kb.reshape(bsz, L, B_KV_HEADS, HEAD_DIM), pos)
    vb = vb.reshape(bsz, L, B_KV_HEADS, HEAD_DIM)
    new = []
    outs, lses = [], []
    for g, (w, d) in enumerate(A_GROUPS):
        lb = min(w, PAST_LEN)
        if cache is None:
            fk, fv = ka[:, :, g], va[:, :, g]
            o, lse = dilated_prompt(qa[:, :, g], fk, fv, w, d)
        else:
            fk = jnp.concatenate([cache[2 * g].astype(ka.dtype), ka[:, :, g]], axis=1)
            fv = jnp.concatenate([cache[2 * g + 1].astype(va.dtype), va[:, :, g]], axis=1)
            o, lse = dilated_sample(qa[:, :, g], fk, fv, w, d)
        outs.append(o)
        lses.append(lse)
        new += [last_rows(fk, lb), last_rows(fv, lb)]
    alpha = jax.nn.softmax(jnp.stack(lses), axis=0)
    o_a = jnp.sum(alpha[..., None] * jnp.stack(outs), axis=0).reshape(bsz, L, A_HEADS * HEAD_DIM).astype(h.dtype)
    lbb = min(B_WINDOW, PAST_LEN)
    if cache is None:
        fk, fv = kb, vb
        o_b, _ = banded_attention(qb, kb, vb, B_WINDOW, lp['attn_sinks'])
    else:
        fk = jnp.concatenate([cache[6].astype(kb.dtype), kb], axis=1)
        fv = jnp.concatenate([cache[7].astype(vb.dtype), vb], axis=1)
        o_b = window_sample(qb, fk, fv, B_WINDOW, lp['attn_sinks'])
    new += [last_rows(fk, lbb), last_rows(fv, lbb)]
    o_b = o_b.reshape(bsz, L, B_Q_HEADS * HEAD_DIM).astype(h.dtype)
    if cache is None:
        prev = jnp.zeros((bsz, C_CONV - 1, C_WIDTH), h.dtype)
    else:
        prev = cache[8]
    o_c, conv_state = short_conv(hc, bc, cc, lp['conv_c_w'], prev)
    new.append(conv_state)
    if cache is None:
        x0_re = jnp.zeros((bsz, D_NGROUPS, D_STATE), jnp.float32)
        x0_im = jnp.zeros((bsz, D_NGROUPS, D_STATE), jnp.float32)
    else:
        x0_re, x0_im = cache[9], cache[10]
    o_d, s_re, s_im = ssm_mixer(ud, lp, x0_re, x0_im)
    new += [s_re, s_im]
    gate = jax.nn.sigmoid(gl.reshape(bsz, L, N_BRANCH, D_MODEL))
    merged = (gate[:, :, 0] * (o_a @ lp['w_br_a']) + gate[:, :, 1] * (o_b @ lp['w_br_b'])
              + gate[:, :, 2] * (o_c @ lp['w_br_c']) + gate[:, :, 3] * (o_d @ lp['w_br_d']))
    return merged @ lp['w_out'], new


def layer(x, p, pos, lp, cache):
    h = rmsnorm(x, lp['norm_mix_pre'])
    mix, new = token_mix(h, pos, lp, cache)
    x = x + rmsnorm(mix, lp['norm_mix_post'])
    h = rmsnorm(x, lp['norm_ffn_pre'])
    f = (jax.nn.silu(h @ lp['w_ffn_gate']) * (h @ lp['w_ffn_up'])) @ lp['w_ffn_down']
    x = x + rmsnorm(f, lp['norm_ffn_post'])
    x = x + jax.nn.sigmoid(x @ lp['w_ple_gate']) * (p @ lp['w_ple'])
    return x, new


def setup_inputs(seed: int = 0) -> dict:
    key = jax.random.key(seed)
    ks = jax.random.split(key, 48)
    f32 = jnp.float32

    def nrm(i, shape, scale=1.0):
        return jax.random.normal(ks[i], shape, f32) * scale

    a_lb = [min(w, PAST_LEN) for (w, _) in A_GROUPS]
    lbb = min(B_WINDOW, PAST_LEN)
    n_idx = jnp.arange(D_STATE, dtype=f32)
    ssm_shape = (DEPTH, D_NGROUPS, D_STATE)
    return {
        'x_prompt': nrm(0, (BATCH, SEQ, D_MODEL)),
        'x_sample': nrm(1, (DEC_BATCH, DEC_SEQ, D_MODEL)),
        'p_prompt': nrm(2, (DEPTH, BATCH, SEQ, PLE_DIM)),
        'p_sample': nrm(3, (DEPTH, DEC_BATCH, DEC_SEQ, PLE_DIM)),
        'cache_a1_k': nrm(4, (DEPTH, DEC_BATCH, a_lb[0], A_HEADS, HEAD_DIM)),
        'cache_a1_v': nrm(5, (DEPTH, DEC_BATCH, a_lb[0], A_HEADS, HEAD_DIM)),
        'cache_a2_k': nrm(6, (DEPTH, DEC_BATCH, a_lb[1], A_HEADS, HEAD_DIM)),
        'cache_a2_v': nrm(7, (DEPTH, DEC_BATCH, a_lb[1], A_HEADS, HEAD_DIM)),
        'cache_a3_k': nrm(8, (DEPTH, DEC_BATCH, a_lb[2], A_HEADS, HEAD_DIM)),
        'cache_a3_v': nrm(9, (DEPTH, DEC_BATCH, a_lb[2], A_HEADS, HEAD_DIM)),
        'cache_b_k': nrm(10, (DEPTH, DEC_BATCH, lbb, B_KV_HEADS, HEAD_DIM)),
        'cache_b_v': nrm(11, (DEPTH, DEC_BATCH, lbb, B_KV_HEADS, HEAD_DIM)),
        'state_c_conv': nrm(12, (DEPTH, DEC_BATCH, C_CONV - 1, C_WIDTH)),
        'state_d_re': nrm(13, (DEPTH, DEC_BATCH, D_NGROUPS, D_STATE), 0.1),
        'state_d_im': nrm(14, (DEPTH, DEC_BATCH, D_NGROUPS, D_STATE), 0.1),
        'norm_mix_pre': 1.0 + nrm(15, (DEPTH, D_MODEL), 0.05),
        'norm_mix_post': 1.0 + nrm(16, (DEPTH, D_MODEL), 0.05),
        'norm_ffn_pre': 1.0 + nrm(17, (DEPTH, D_MODEL), 0.05),
        'norm_ffn_post': 1.0 + nrm(18, (DEPTH, D_MODEL), 0.05),
        'w_in': nrm(19, (DEPTH, D_MODEL, IN_WIDTH), D_MODEL ** -0.5),
        'attn_sinks': nrm(20, (DEPTH, B_KV_HEADS, B_GROUP), 0.5),
        'conv_c_w': nrm(21, (DEPTH, C_CONV, C_WIDTH), C_CONV ** -0.5),
        'ssm_lam_re': -0.5 + nrm(22, ssm_shape, 0.01),
        'ssm_lam_im': math.pi * n_idx + nrm(23, ssm_shape, 0.01),
        'ssm_log_dt': jax.random.uniform(ks[24], (DEPTH, D_NGROUPS), f32, math.log(0.001), math.log(0.1)),
        'ssm_b_re': nrm(25, (DEPTH, D_NGROUPS, D_STATE, D_GROUP), (2 * D_GROUP) ** -0.5),
        'ssm_b_im': nrm(26, (DEPTH, D_NGROUPS, D_STATE, D_GROUP), (2 * D_GROUP) ** -0.5),
        'ssm_c_re': nrm(27, (DEPTH, D_NGROUPS, D_GROUP, D_STATE), D_STATE ** -0.5),
        'ssm_c_im': nrm(28, (DEPTH, D_NGROUPS, D_GROUP, D_STATE), D_STATE ** -0.5),
        'ssm_d': nrm(29, (DEPTH, D_NGROUPS, D_GROUP)),
        'w_d_glu': nrm(30, (DEPTH, D_WIDTH, 2 * D_WIDTH), D_WIDTH ** -0.5),
        'w_br_a': nrm(31, (DEPTH, A_HEADS * HEAD_DIM, D_MODEL), (A_HEADS * HEAD_DIM) ** -0.5),
        'w_br_b': nrm(32, (DEPTH, B_Q_HEADS * HEAD_DIM, D_MODEL), (B_Q_HEADS * HEAD_DIM) ** -0.5),
        'w_br_c': nrm(33, (DEPTH, C_WIDTH, D_MODEL), C_WIDTH ** -0.5),
        'w_br_d': nrm(34, (DEPTH, D_WIDTH, D_MODEL), D_WIDTH ** -0.5),
        'w_out': nrm(35, (DEPTH, D_MODEL, D_MODEL), D_MODEL ** -0.5),
        'w_ffn_gate': nrm(36, (DEPTH, D_MODEL, D_FF), D_MODEL ** -0.5),
        'w_ffn_up': nrm(37, (DEPTH, D_MODEL, D_FF), D_MODEL ** -0.5),
        'w_ffn_down': nrm(38, (DEPTH, D_FF, D_MODEL), D_FF ** -0.5),
        'w_ple': nrm(39, (DEPTH, PLE_DIM, D_MODEL), PLE_DIM ** -0.5),
        'w_ple_gate': nrm(40, (DEPTH, D_MODEL, D_MODEL), D_MODEL ** -0.5),
    }


def reference(x_prompt, x_sample, p_prompt, p_sample,
              cache_a1_k, cache_a1_v, cache_a2_k, cache_a2_v, cache_a3_k, cache_a3_v,
              cache_b_k, cache_b_v, state_c_conv, state_d_re, state_d_im,
              norm_mix_pre, norm_mix_post, norm_ffn_pre, norm_ffn_post,
              w_in, attn_sinks, conv_c_w, ssm_lam_re, ssm_lam_im, ssm_log_dt,
              ssm_b_re, ssm_b_im, ssm_c_re, ssm_c_im, ssm_d, w_d_glu,
              w_br_a, w_br_b, w_br_c, w_br_d, w_out,
              w_ffn_gate, w_ffn_up, w_ffn_down, w_ple, w_ple_gate):
    pos_p = jnp.arange(SEQ, dtype=jnp.int32)
    pos_s = PAST_LEN + jnp.arange(DEC_SEQ, dtype=jnp.int32)
    yp, ys = x_prompt, x_sample
    new_p, new_s = [], []
    for i in range(DEPTH):
        lp = {
            'norm_mix_pre': norm_mix_pre[i], 'norm_mix_post': norm_mix_post[i],
            'norm_ffn_pre': norm_ffn_pre[i], 'norm_ffn_post': norm_ffn_post[i],
            'w_in': w_in[i], 'attn_sinks': attn_sinks[i], 'conv_c_w': conv_c_w[i],
            'ssm_lam_re': ssm_lam_re[i], 'ssm_lam_im': ssm_lam_im[i], 'ssm_log_dt': ssm_log_dt[i],
            'ssm_b_re': ssm_b_re[i], 'ssm_b_im': ssm_b_im[i], 'ssm_c_re': ssm_c_re[i], 'ssm_c_im': ssm_c_im[i],
            'ssm_d': ssm_d[i], 'w_d_glu': w_d_glu[i],
            'w_br_a': w_br_a[i], 'w_br_b': w_br_b[i], 'w_br_c': w_br_c[i], 'w_br_d': w_br_d[i],
            'w_out': w_out[i], 'w_ffn_gate': w_ffn_gate[i], 'w_ffn_up': w_ffn_up[i],
            'w_ffn_down': w_ffn_down[i], 'w_ple': w_ple[i], 'w_ple_gate': w_ple_gate[i],
        }
        cache = (cache_a1_k[i], cache_a1_v[i], cache_a2_k[i], cache_a2_v[i], cache_a3_k[i], cache_a3_v[i],
                 cache_b_k[i], cache_b_v[i], state_c_conv[i], state_d_re[i], state_d_im[i])
        yp, st_p = layer(yp, p_prompt[i], pos_p, lp, None)
        new_p.append(st_p)
        ys, st_s = layer(ys, p_sample[i], pos_s, lp, cache)
        new_s.append(st_s)
    (a1k_p, a1v_p, a2k_p, a2v_p, a3k_p, a3v_p, bk_p, bv_p, conv_p, dre_p, dim_p) = [
        jnp.stack([st[j] for st in new_p]) for j in range(N_STATE)]
    (a1k_s, a1v_s, a2k_s, a2v_s, a3k_s, a3v_s, bk_s, bv_s, conv_s, dre_s, dim_s) = [
        jnp.stack([st[j] for st in new_s]) for j in range(N_STATE)]
    return (yp, ys,
            a1k_p, a1k_s, a1v_p, a1v_s, a2k_p, a2k_s, a2v_p, a2v_s, a3k_p, a3k_s, a3v_p, a3v_s,
            bk_p, bk_s, bv_p, bv_s, conv_p, conv_s, dre_p, dre_s, dim_p, dim_s)
```

```python
import functools
import math

import jax
import jax.numpy as jnp
from jax import lax
from jax.experimental import pallas as pl
from jax.experimental.pallas import tpu as pltpu

F32 = jnp.float32
BF16 = jnp.bfloat16

D_MODEL = 1024
BATCH = 8
SEQ = 2048
DEPTH = 2
DEC_BATCH = 128
DEC_SEQ = 8
PAST_LEN = 16384
HEAD_DIM = 64
HALF = HEAD_DIM // 2
ROPE_THETA = 10000.0
BLOCK = 128
EPS = 1e-6
NEG_INF = -1e30
A_HEADS = 4
A_GROUPS = ((128, 1), (512, 4), (2048, 16))
A_NG = len(A_GROUPS)
A_W = A_HEADS * HEAD_DIM
A_QKV = A_NG * A_W
B_Q_HEADS = 8
B_KV_HEADS = 2
B_GROUP = B_Q_HEADS // B_KV_HEADS
B_WINDOW = 128
B_QW = B_Q_HEADS * HEAD_DIM
B_KW = B_KV_HEADS * HEAD_DIM
C_WIDTH = 256
C_CONV = 3
D_GROUP = 16
D_NGROUPS = 16
D_WIDTH = 256
D_STATE = 64
D_NS = D_NGROUPS * D_STATE
N_BRANCH = 4
D_FF = 2816
PLE_DIM = 256
MIX_W = 4096
SCALE = HEAD_DIM ** -0.5

LANES = 128
SUBLANES = 8
VMEM_LIMIT = 56 * 1024 * 1024

SLAB_A = 3 * A_QKV
SLAB_B = B_QW + 2 * B_KW
SLAB_C = 3 * C_WIDTH
ROPE_A = 2 * A_QKV
ROPE_B = B_QW + B_KW


def _params(n_axes):
    return pltpu.CompilerParams(dimension_semantics=("arbitrary",) * n_axes, vmem_limit_bytes=VMEM_LIMIT)


def _resident(shape):
    nd = len(shape)
    return pl.BlockSpec(shape, lambda *_: (0,) * nd, pipeline_mode=pl.Buffered(1))


def _rmsnorm(x, g):
    return x * lax.rsqrt(jnp.mean(x * x, axis=-1, keepdims=True) + EPS) * g


def _mm(a, b):
    return jnp.dot(a, b, preferred_element_type=F32)


def _mm_nt(a, b):
    return lax.dot_general(a, b, (((1,), (1,)), ((), ())), preferred_element_type=F32)


def _inproj_body(x_ref, g_ref, w_ref, cos_ref, sin_ref, a_ref, b_ref, c_ref):
    tm = x_ref.shape[0]
    h = _rmsnorm(x_ref[...], g_ref[...]).astype(BF16)
    cos = cos_ref[...]
    sin = sin_ref[...]
    lane = lax.broadcasted_iota(jnp.int32, (tm, LANES), 1)
    first_half = (lane & (HEAD_DIM - 1)) < HALF

    def rope(z):
        partner = jnp.where(first_half, pltpu.roll(z, LANES - HALF, axis=1), pltpu.roll(z, HALF, axis=1))
        return z * cos + partner * sin

    za = _mm(h, w_ref[:, 0:SLAB_A])
    for c in range(SLAB_A // LANES):
        blk = za[:, c * LANES:(c + 1) * LANES]
        a_ref[:, c * LANES:(c + 1) * LANES] = rope(blk) if c * LANES < ROPE_A else blk
    zb = _mm(h, w_ref[:, SLAB_A:SLAB_A + SLAB_B])
    for c in range(SLAB_B // LANES):
        blk = zb[:, c * LANES:(c + 1) * LANES]
        b_ref[:, c * LANES:(c + 1) * LANES] = rope(blk) if c * LANES < ROPE_B else blk
    zc = _mm(h, w_ref[:, SLAB_A + SLAB_B:MIX_W])
    c_ref[:, 0:C_WIDTH] = zc[:, 2 * C_WIDTH:3 * C_WIDTH] * zc[:, 0:C_WIDTH]
    c_ref[:, C_WIDTH:2 * C_WIDTH] = zc[:, C_WIDTH:2 * C_WIDTH]
    c_ref[:, 2 * C_WIDTH:3 * C_WIDTH] = zc[:, 3 * C_WIDTH:4 * C_WIDTH]


def _inproj(x, g, w_mix, cos, sin, tm, table_blocks):
    m = x.shape[0]
    row = lambda i: (i, 0)
    tab = lambda i: (i % table_blocks, 0)
    return pl.pallas_call(
        _inproj_body,
        grid=(m // tm,),
        in_specs=[pl.BlockSpec((tm, D_MODEL), row), _resident((1, D_MODEL)), _resident((D_MODEL, MIX_W)),
                  pl.BlockSpec((tm, LANES), tab), pl.BlockSpec((tm, LANES), tab)],
        out_specs=[pl.BlockSpec((tm, SLAB_A), row), pl.BlockSpec((tm, SLAB_B), row), pl.BlockSpec((tm, SLAB_C), row)],
        out_shape=[jax.ShapeDtypeStruct((m, SLAB_A), F32), jax.ShapeDtypeStruct((m, SLAB_B), F32),
                   jax.ShapeDtypeStruct((m, SLAB_C), F32)],
        compiler_params=_params(1),
        name="in_proj",
    )(x, g, w_mix, cos, sin)


def _band_mask(nk):
    qi = lax.broadcasted_iota(jnp.int32, (BLOCK, nk), 0)
    kj = lax.broadcasted_iota(jnp.int32, (BLOCK, nk), 1)
    dist = qi + (nk - BLOCK) - kj
    return (dist >= 0) & (dist <= BLOCK)


def _softmax_pv(s, mask, v, sink=None):
    s = jnp.where(mask, s, NEG_INF)
    m = jnp.max(s, axis=-1, keepdims=True)
    if sink is not None:
        m = jnp.maximum(m, sink)
    p = jnp.exp(s - m)
    den = jnp.sum(p, axis=-1, keepdims=True)
    if sink is not None:
        den = den + jnp.exp(sink - m)
    o = _mm(p.astype(BF16), v) * (1.0 / den)
    return o, m + jnp.log(den)


def _attn_a_prompt_body(q_ref, k_ref, v_ref, o_ref, lse_ref, *, nb):
    lane = lax.broadcasted_iota(jnp.int32, (BLOCK, LANES), 1)
    lo = lane < HEAD_DIM

    def block(q0, k0, nk):
        mask = _band_mask(nk)
        for hp in range(A_W // LANES):
            cols = slice(hp * LANES, (hp + 1) * LANES)
            q = q_ref[0, pl.ds(q0, BLOCK), cols] * SCALE
            k = k_ref[0, pl.ds(k0, nk), cols].astype(BF16)
            v = v_ref[0, pl.ds(k0, nk), cols].astype(BF16)
            o0, l0 = _softmax_pv(_mm_nt(jnp.where(lo, q, 0.0).astype(BF16), k), mask, v)
            o1, l1 = _softmax_pv(_mm_nt(jnp.where(lo, 0.0, q).astype(BF16), k), mask, v)
            o_ref[0, pl.ds(q0, BLOCK), cols] = jnp.where(lo, o0, o1)
            lse_ref[0, pl.ds(q0, BLOCK), cols] = jnp.where(lo, l0, l1)

    block(0, 0, BLOCK)
    if nb > 1:
        def body(i, carry):
            q0 = pl.multiple_of(i * BLOCK, BLOCK)
            block(q0, pl.multiple_of(q0 - BLOCK, BLOCK), 2 * BLOCK)
            return carry
        lax.fori_loop(1, nb, body, 0)


def _attn_a_prompt(slab_a, g, d):
    ls = SEQ // d
    nb = ls // BLOCK
    view = slab_a.reshape(BATCH, ls, d * SLAB_A)
    ncol = SLAB_A // A_W
    spec = lambda off: pl.BlockSpec((1, ls, A_W), lambda b, r: (b, 0, r * ncol + off))
    ospec = pl.BlockSpec((1, ls, A_W), lambda b, r: (b, 0, r))
    oshape = jax.ShapeDtypeStruct((BATCH, ls, d * A_W), F32)
    o, lse = pl.pallas_call(
        functools.partial(_attn_a_prompt_body, nb=nb),
        grid=(BATCH, d),
        in_specs=[spec(g), spec(A_NG + g), spec(2 * A_NG + g)],
        out_specs=[ospec, ospec],
        out_shape=[oshape, oshape],
        compiler_params=_params(2),
        name=f"attn_a{g + 1}_prompt",
    )(view, view, view)
    return o.reshape(BATCH * SEQ, A_W), lse.reshape(BATCH * SEQ, A_W)


def _attn_b_prompt_body(q_ref, k_ref, v_ref, sink_ref, o_ref, *, nb):
    lane = lax.broadcasted_iota(jnp.int32, (BLOCK, LANES), 1)
    lo = lane < HEAD_DIM

    def block(q0, k0, nk):
        mask = _band_mask(nk)
        k = k_ref[0, pl.ds(k0, nk), :]
        v = v_ref[0, pl.ds(k0, nk), :]
        ks = (k.astype(BF16), pltpu.roll(k, HEAD_DIM, axis=1).astype(BF16))
        vs = (v.astype(BF16), pltpu.roll(v, HEAD_DIM, axis=1).astype(BF16))
        for c in range(B_QW // LANES):
            cols = slice(c * LANES, (c + 1) * LANES)
            q = q_ref[0, pl.ds(q0, BLOCK), cols] * SCALE
            outs = []
            for half in range(2):
                head = 2 * c + half
                kv = head // B_GROUP
                swapped = kv ^ half
                qm = jnp.where(lo, q, 0.0) if half == 0 else jnp.where(lo, 0.0, q)
                sink = sink_ref[head:head + 1, 0:1]
                o, _ = _softmax_pv(_mm_nt(qm.astype(BF16), ks[swapped]), mask, vs[swapped], sink)
                outs.append(o)
            o_ref[0, pl.ds(q0, BLOCK), cols] = jnp.where(lo, outs[0], outs[1])

    block(0, 0, BLOCK)
    def body(i, carry):
        q0 = pl.multiple_of(i * BLOCK, BLOCK)
        block(q0, pl.multiple_of(q0 - BLOCK, BLOCK), 2 * BLOCK)
        return carry
    lax.fori_loop(1, nb, body, 0)


def _attn_b_prompt(slab_b, sinks):
    view = slab_b.reshape(BATCH, SEQ, SLAB_B)
    nq = B_QW // LANES
    o = pl.pallas_call(
        functools.partial(_attn_b_prompt_body, nb=SEQ // BLOCK),
        grid=(BATCH,),
        in_specs=[pl.BlockSpec((1, SEQ, B_QW), lambda b: (b, 0, 0)),
                  pl.BlockSpec((1, SEQ, B_KW), lambda b: (b, 0, nq)),
                  pl.BlockSpec((1, SEQ, B_KW), lambda b: (b, 0, nq + 1)),
                  _resident((B_Q_HEADS, LANES))],
        out_specs=pl.BlockSpec((1, SEQ, B_QW), lambda b: (b, 0, 0)),
        out_shape=jax.ShapeDtypeStruct((BATCH, SEQ, B_QW), F32),
        compiler_params=_params(1),
        name="attn_b_prompt",
    )(view, view, view, sinks)
    return o.reshape(BATCH * SEQ, B_QW)


def _sample_attn_body(*refs, lb, window, dil, n_kv, q_per_kv, with_sinks, aliased):
    refs = list(refs)
    q_ref, kc_ref, vc_ref, kn_ref, vn_ref = refs[:5]
    pos = 5
    sink_ref = None
    if with_sinks:
        sink_ref = refs[pos]
        pos += 1
    if aliased:
        pos += 2
    o_ref, lse_ref, ko_ref, vo_ref = refs[pos:pos + 4]

    rows = q_per_kv * DEC_SEQ
    assert dil & (dil - 1) == 0 and DEC_SEQ & (DEC_SEQ - 1) == 0
    t_c = lax.broadcasted_iota(jnp.int32, (rows, lb), 0) & (DEC_SEQ - 1)
    dist_c = lb + t_c - lax.broadcasted_iota(jnp.int32, (rows, lb), 1)
    mask_c = (dist_c >= 0) & (dist_c <= window) & ((dist_c & (dil - 1)) == 0)
    t_n = lax.broadcasted_iota(jnp.int32, (rows, LANES), 0) & (DEC_SEQ - 1)
    col_n = lax.broadcasted_iota(jnp.int32, (rows, LANES), 1) - (LANES - DEC_SEQ)
    dist_n = t_n - col_n
    mask_n = (col_n >= 0) & (dist_n >= 0) & (dist_n <= window) & ((dist_n & (dil - 1)) == 0)
    new_lane = lax.broadcasted_iota(jnp.int32, (HEAD_DIM, LANES), 1) >= LANES - DEC_SEQ

    for h in range(n_kv):
        kc = kc_ref[0, 0, h]
        vc = vc_ref[0, 0, h]
        kn = kn_ref[0, h]
        vn = vn_ref[0, h]
        q = (q_ref[0, h] * SCALE).astype(BF16)
        s_c = jnp.where(mask_c, _mm(q, kc.astype(BF16)), NEG_INF)
        s_n = jnp.where(mask_n, _mm(q, kn.astype(BF16)), NEG_INF)
        m = jnp.maximum(jnp.max(s_c, axis=-1, keepdims=True), jnp.max(s_n, axis=-1, keepdims=True))
        if with_sinks:
            sink = sink_ref[h * rows:(h + 1) * rows, 0:1]
            m = jnp.maximum(m, sink)
        p_c = jnp.exp(s_c - m)
        p_n = jnp.exp(s_n - m)
        den = jnp.sum(p_c, axis=-1, keepdims=True) + jnp.sum(p_n, axis=-1, keepdims=True)
        if with_sinks:
            den = den + jnp.exp(sink - m)
        o = _mm_nt(p_c.astype(BF16), vc.astype(BF16)) + _mm_nt(p_n.astype(BF16), vn.astype(BF16))
        o_ref[0, h] = o * (1.0 / den)
        lse_ref[0, h] = jnp.broadcast_to(m + jnp.log(den), (rows, HEAD_DIM))
        for src, new, dst in ((kc, kn, ko_ref), (vc, vn, vo_ref)):
            if lb > LANES:
                shifted = pltpu.roll(src, lb - DEC_SEQ, axis=1)
                dst[0, 0, h, :, 0:lb - LANES] = shifted[:, 0:lb - LANES]
                tail = shifted[:, lb - LANES:lb]
            else:
                tail = pltpu.roll(src, lb - DEC_SEQ, axis=1)
            dst[0, 0, h, :, lb - LANES:lb] = jnp.where(new_lane, new, tail)


def _sample_attn(layer, q, kc, vc, kn, vn, prev_k, prev_v, *, window, dil, sinks=None):
    n_kv, lb = kc.shape[2], kc.shape[4]
    rows = q.shape[2]
    aliased = prev_k is not None
    cache_spec = pl.BlockSpec((1, 1, n_kv, HEAD_DIM, lb), lambda b: (layer, b, 0, 0, 0))
    q_spec = pl.BlockSpec((1, n_kv, rows, HEAD_DIM), lambda b: (b, 0, 0, 0))
    new_spec = pl.BlockSpec((1, n_kv, HEAD_DIM, LANES), lambda b: (b, 0, 0, 0))
    in_specs = [q_spec, cache_spec, cache_spec, new_spec, new_spec]
    args = [q, kc, vc, kn, vn]
    if sinks is not None:
        in_specs.append(_resident(sinks.shape))
        args.append(sinks)
    aliases = {}
    if aliased:
        aliases = {len(args): 2, len(args) + 1: 3}
        in_specs += [pl.BlockSpec(memory_space=pl.ANY)] * 2
        args += [prev_k, prev_v]
    o_shape = jax.ShapeDtypeStruct(q.shape, F32)
    return pl.pallas_call(
        functools.partial(_sample_attn_body, lb=lb, window=window, dil=dil, n_kv=n_kv,
                          q_per_kv=rows // DEC_SEQ, with_sinks=sinks is not None, aliased=aliased),
        grid=(DEC_BATCH,),
        in_specs=in_specs,
        out_specs=[q_spec, q_spec, cache_spec, cache_spec],
        out_shape=[o_shape, o_shape, jax.ShapeDtypeStruct(kc.shape, F32), jax.ShapeDtypeStruct(vc.shape, F32)],
        input_output_aliases=aliases,
        compiler_params=_params(1),
        name=f"sample_attn_w{window}_d{dil}",
    )(*args)


def _ssm_body(u_ref, are_ref, aim_ref, bbig_ref, cbig_ref, dvec_ref, wglu_ref, x0re_ref, x0im_ref,
              od_ref, sre_ref, sim_ref, bu_ref, xs_ref, *, steps):
    @pl.when(pl.program_id(1) == 0)
    def _():
        sre_ref[0] = x0re_ref[0]
        sim_ref[0] = x0im_ref[0]

    u = u_ref[0]
    bu_ref[...] = jnp.dot(u, bbig_ref[...], preferred_element_type=F32, precision=lax.Precision.HIGHEST)
    a_re = are_ref[...]
    a_im = aim_ref[...]

    def step(t, carry):
        re, im = carry
        r0 = pl.multiple_of(t * SUBLANES, SUBLANES)
        n_re = a_re * re - a_im * im + bu_ref[pl.ds(r0, SUBLANES), 0:D_NS]
        n_im = a_re * im + a_im * re + bu_ref[pl.ds(r0, SUBLANES), D_NS:2 * D_NS]
        xs_ref[pl.ds(r0, SUBLANES), 0:D_NS] = n_re
        xs_ref[pl.ds(r0, SUBLANES), D_NS:2 * D_NS] = n_im
        return n_re, n_im

    re, im = lax.fori_loop(0, steps, step, (sre_ref[0], sim_ref[0]))
    sre_ref[0] = re
    sim_ref[0] = im
    y = _mm(xs_ref[...].astype(BF16), cbig_ref[...]) + dvec_ref[...] * u
    y = _mm(jax.nn.gelu(y).astype(BF16), wglu_ref[...])
    od_ref[0] = y[:, 0:D_WIDTH] * jax.nn.sigmoid(y[:, D_WIDTH:2 * D_WIDTH])


def _ssm(u_tm, a_re, a_im, bbig, cbig, dvec, wglu, x0_re, x0_im, steps):
    n, rows, _ = u_tm.shape
    chunk = steps * SUBLANES
    u_spec = pl.BlockSpec((1, chunk, D_WIDTH), lambda i, c: (i, c, 0))
    st_spec = pl.BlockSpec((1, SUBLANES, D_NS), lambda i, c: (i, 0, 0))
    st_shape = jax.ShapeDtypeStruct((n, SUBLANES, D_NS), F32)
    return pl.pallas_call(
        functools.partial(_ssm_body, steps=steps),
        grid=(n, rows // chunk),
        in_specs=[u_spec, _resident((SUBLANES, D_NS)), _resident((SUBLANES, D_NS)), _resident((D_WIDTH, 2 * D_NS)),
                  _resident((2 * D_NS, D_WIDTH)), _resident((1, D_WIDTH)), _resident((D_WIDTH, 2 * D_WIDTH)),
                  st_spec, st_spec],
        out_specs=[u_spec, st_spec, st_spec],
        out_shape=[jax.ShapeDtypeStruct(u_tm.shape, F32), st_shape, st_shape],
        scratch_shapes=[pltpu.VMEM((chunk, 2 * D_NS), F32), pltpu.VMEM((chunk, 2 * D_NS), F32)],
        compiler_params=_params(2),
        name="ssm",
    )(u_tm, a_re, a_im, bbig, cbig, dvec, wglu, x0_re, x0_im)


def _merge_body(x_ref, o1_ref, l1_ref, o2_ref, l2_ref, o3_ref, l3_ref, ob_ref, bc_ref, z0_ref, z1_ref, z2_ref,
                od_ref, gpre_ref, wgl_ref, convw_ref, wa_ref, wb_ref, wc_ref, wd_ref, wout_ref, gpost_ref, y_ref):
    x = x_ref[...]
    h = _rmsnorm(x, gpre_ref[...]).astype(BF16)
    l1, l2, l3 = l1_ref[...], l2_ref[...], l3_ref[...]
    lm = jnp.maximum(jnp.maximum(l1, l2), l3)
    e1, e2, e3 = jnp.exp(l1 - lm), jnp.exp(l2 - lm), jnp.exp(l3 - lm)
    es = e1 + e2 + e3
    o_a = (e1 / es) * o1_ref[...] + (e2 / es) * o2_ref[...] + (e3 / es) * o3_ref[...]
    conv = convw_ref[0:1, :] * z0_ref[...] + convw_ref[1:2, :] * z1_ref[...] + convw_ref[2:3, :] * z2_ref[...]
    o_c = bc_ref[...] * conv
    branches = ((o_a, wa_ref), (ob_ref[...], wb_ref), (o_c, wc_ref), (od_ref[...], wd_ref))
    merged = None
    for j, (o, w_ref) in enumerate(branches):
        gate = jax.nn.sigmoid(_mm(h, wgl_ref[:, j * D_MODEL:(j + 1) * D_MODEL]))
        term = gate * _mm(o.astype(BF16), w_ref[...])
        merged = term if merged is None else merged + term
    mix = _mm(merged.astype(BF16), wout_ref[...])
    y_ref[...] = x + _rmsnorm(mix, gpost_ref[...])


def _merge(x, o1, l1, o2, l2, o3, l3, ob, slab_c, z0, z1, z2, od, gpre, wgl, convw, wa, wb, wc, wd, wout, gpost, tm):
    m = x.shape[0]
    row = lambda i: (i, 0)
    r256 = pl.BlockSpec((tm, 256), row)
    return pl.pallas_call(
        _merge_body,
        grid=(m // tm,),
        in_specs=[pl.BlockSpec((tm, D_MODEL), row), r256, r256, r256, r256, r256, r256,
                  pl.BlockSpec((tm, B_QW), row), pl.BlockSpec((tm, C_WIDTH), lambda i: (i, 1)), r256, r256, r256, r256,
                  _resident((1, D_MODEL)), _resident((D_MODEL, N_BRANCH * D_MODEL)), _resident((SUBLANES, C_WIDTH)),
                  _resident((A_W, D_MODEL)), _resident((B_QW, D_MODEL)), _resident((C_WIDTH, D_MODEL)),
                  _resident((D_WIDTH, D_MODEL)), _resident((D_MODEL, D_MODEL)), _resident((1, D_MODEL))],
        out_specs=pl.BlockSpec((tm, D_MODEL), row),
        out_shape=jax.ShapeDtypeStruct((m, D_MODEL), F32),
        compiler_params=_params(1),
        name="merge",
    )(x, o1, l1, o2, l2, o3, l3, ob, slab_c, z0, z1, z2, od, gpre, wgl, convw, wa, wb, wc, wd, wout, gpost)


def _ffn_body(x_ref, p_ref, gpre_ref, wg_ref, wu_ref, wdn_ref, gpost_ref, wple_ref, wpg_ref, y_ref):
    x = x_ref[...]
    h = _rmsnorm(x, gpre_ref[...]).astype(BF16)
    act = jax.nn.silu(_mm(h, wg_ref[...])) * _mm(h, wu_ref[...])
    f = _mm(act.astype(BF16), wdn_ref[...])
    x = x + _rmsnorm(f, gpost_ref[...])
    gate = jax.nn.sigmoid(_mm(x.astype(BF16), wpg_ref[...]))
    y_ref[...] = x + gate * _mm(p_ref[...].astype(BF16), wple_ref[...])


def _ffn(x, p, gpre, wg, wu, wdn, gpost, wple, wpg, tm):
    m = x.shape[0]
    row = lambda i: (i, 0)
    return pl.pallas_call(
        _ffn_body,
        grid=(m // tm,),
        in_specs=[pl.BlockSpec((tm, D_MODEL), row), pl.BlockSpec((tm, PLE_DIM), row),
                  _resident((1, D_MODEL)), _resident((D_MODEL, D_FF)), _resident((D_MODEL, D_FF)),
                  _resident((D_FF, D_MODEL)), _resident((1, D_MODEL)), _resident((PLE_DIM, D_MODEL)),
                  _resident((D_MODEL, D_MODEL))],
        out_specs=pl.BlockSpec((tm, D_MODEL), row),
        out_shape=jax.ShapeDtypeStruct((m, D_MODEL), F32),
        compiler_params=_params(1),
        name="ffn_ple",
    )(x, p, gpre, wg, wu, wdn, gpost, wple, wpg)


def _rope_tables(pos):
    inv = ROPE_THETA ** (-jnp.arange(HALF, dtype=F32) / HALF)
    ang = pos.astype(F32)[:, None] * inv[None, :]
    cos, sin = jnp.cos(ang), jnp.sin(ang)
    cos_t = jnp.tile(cos, (1, LANES // HALF))
    sin_t = jnp.tile(jnp.concatenate([-sin, sin], axis=1), (1, LANES // HEAD_DIM))
    return cos_t, sin_t


def _ssm_weights(lam_re, lam_im, log_dt, b_re, b_im, c_re, c_im):
    lam = lax.complex(lam_re, lam_im)
    dt = jnp.exp(log_dt)[:, None]
    a_bar = jnp.exp(lam * dt)
    b_bar = ((a_bar - 1.0) / lam)[..., None] * lax.complex(b_re, b_im)
    eye = jnp.eye(D_NGROUPS, dtype=F32)
    pack_b = lambda b: jnp.einsum('gni,gh->gihn', b, eye).reshape(D_WIDTH, D_NS)
    pack_c = lambda c: jnp.einsum('gin,gh->gnhi', c, eye).reshape(D_NS, D_WIDTH)
    bbig = jnp.concatenate([pack_b(jnp.real(b_bar)), pack_b(jnp.imag(b_bar))], axis=1)
    cbig = jnp.concatenate([pack_c(c_re), -pack_c(c_im)], axis=0)
    tile8 = lambda a: jnp.broadcast_to(a.reshape(1, D_NS), (SUBLANES, D_NS))
    return tile8(jnp.real(a_bar)), tile8(jnp.imag(a_bar)), bbig, cbig.astype(BF16)


def _to_time_major(u, bsz, length):
    w = u.shape[-1]
    return u.reshape(bsz // SUBLANES, SUBLANES, length, w).transpose(0, 2, 1, 3).reshape(bsz // SUBLANES, length * SUBLANES, w)


def _from_time_major(u, bsz, length):
    w = u.shape[-1]
    return u.reshape(bsz // SUBLANES, length, SUBLANES, w).transpose(0, 2, 1, 3).reshape(bsz * length, w)


def _conv_taps(zc, prev, bsz, length):
    zp = jnp.concatenate([prev, zc.reshape(bsz, length, C_WIDTH)], axis=1)
    taps = [zp[:, j:j + length].reshape(bsz * length, C_WIDTH) for j in range(C_CONV)]
    return taps, zp[:, length:]


def _key_major(c):
    return jnp.transpose(c, (0, 1, 3, 4, 2))


def _row_major(c):
    return jnp.transpose(c, (0, 1, 4, 2, 3))


def _sample_heads(cols, n_kv, q_per_kv):
    x = cols.reshape(DEC_BATCH, DEC_SEQ, n_kv, q_per_kv, HEAD_DIM).transpose(0, 2, 3, 1, 4)
    return x.reshape(DEC_BATCH, n_kv, q_per_kv * DEC_SEQ, HEAD_DIM)


def _sample_unheads(x, n_kv, q_per_kv):
    x = x.reshape(DEC_BATCH, n_kv, q_per_kv, DEC_SEQ, HEAD_DIM).transpose(0, 3, 1, 2, 4)
    return x.reshape(DEC_BATCH * DEC_SEQ, n_kv * q_per_kv * HEAD_DIM)


def _sample_new_rows(cols, n_kv):
    x = cols.reshape(DEC_BATCH, DEC_SEQ, n_kv, HEAD_DIM).transpose(0, 2, 3, 1)
    return jnp.pad(x, ((0, 0), (0, 0), (0, 0), (LANES - DEC_SEQ, 0)))


def kernel(x_prompt, x_sample, p_prompt, p_sample, cache_a1_k, cache_a1_v, cache_a2_k, cache_a2_v, cache_a3_k, cache_a3_v, cache_b_k, cache_b_v, state_c_conv, state_d_re, state_d_im, norm_mix_pre, norm_mix_post, norm_ffn_pre, norm_ffn_post, w_in, attn_sinks, conv_c_w, ssm_lam_re, ssm_lam_im, ssm_log_dt, ssm_b_re, ssm_b_im, ssm_c_re, ssm_c_im, ssm_d, w_d_glu, w_br_a, w_br_b, w_br_c, w_br_d, w_out, w_ffn_gate, w_ffn_up, w_ffn_down, w_ple, w_ple_gate):
    assert x_prompt.shape == (BATCH, SEQ, D_MODEL) and x_sample.shape == (DEC_BATCH, DEC_SEQ, D_MODEL)
    assert w_in.shape == (DEPTH, D_MODEL, 2 * MIX_W)
    assert all(min(w, PAST_LEN) == w for w, _ in A_GROUPS) and min(B_WINDOW, PAST_LEN) == B_WINDOW
    mp, ms = BATCH * SEQ, DEC_BATCH * DEC_SEQ
    tm_p, tm_s = 512, 256

    cos_p, sin_p = _rope_tables(jnp.arange(SEQ, dtype=jnp.int32))
    cos_s, sin_s = _rope_tables(PAST_LEN + jnp.arange(DEC_SEQ, dtype=jnp.int32))
    cos_s, sin_s = jnp.tile(cos_s, (tm_s // DEC_SEQ, 1)), jnp.tile(sin_s, (tm_s // DEC_SEQ, 1))

    caches_a = ((cache_a1_k, cache_a1_v), (cache_a2_k, cache_a2_v), (cache_a3_k, cache_a3_v))
    caches_a = [(_key_major(k), _key_major(v)) for k, v in caches_a]
    cache_bk, cache_bv = _key_major(cache_b_k), _key_major(cache_b_v)

    yp = x_prompt.reshape(mp, D_MODEL)
    ys = x_sample.reshape(ms, D_MODEL)
    new_a = [[None, None] for _ in range(A_NG)]
    new_b = [None, None]
    st_p = [[] for _ in range(11)]
    conv_s, dre_s, dim_s = [], [], []

    for i in range(DEPTH):
        row = lambda a: a[i].reshape(1, -1)
        w_mix = w_in[i, :, :MIX_W].astype(BF16)
        w_gl = w_in[i, :, MIX_W:].astype(BF16)
        a_re, a_im, bbig, cbig = _ssm_weights(ssm_lam_re[i], ssm_lam_im[i], ssm_log_dt[i], ssm_b_re[i], ssm_b_im[i],
                                              ssm_c_re[i], ssm_c_im[i])
        dvec = ssm_d[i].reshape(1, D_WIDTH)
        wglu = w_d_glu[i].astype(BF16)
        sinks_p = jnp.broadcast_to(attn_sinks[i].reshape(B_Q_HEADS, 1), (B_Q_HEADS, LANES))
        sinks_s = jnp.broadcast_to(jnp.repeat(attn_sinks[i].reshape(B_Q_HEADS), DEC_SEQ)[:, None],
                                   (B_Q_HEADS * DEC_SEQ, LANES))
        convw = jnp.pad(conv_c_w[i], ((0, SUBLANES - C_CONV), (0, 0)))
        merge_w = (row(norm_mix_pre), w_gl, convw, w_br_a[i].astype(BF16), w_br_b[i].astype(BF16),
                   w_br_c[i].astype(BF16), w_br_d[i].astype(BF16), w_out[i].astype(BF16), row(norm_mix_post))
        ffn_w = (row(norm_ffn_pre), w_ffn_gate[i].astype(BF16), w_ffn_up[i].astype(BF16), w_ffn_down[i].astype(BF16),
                 row(norm_ffn_post), w_ple[i].astype(BF16), w_ple_gate[i].astype(BF16))

        sa, sb, sc = _inproj(yp, row(norm_mix_pre), w_mix, cos_p, sin_p, tm_p, SEQ // tm_p)
        oa = []
        for g, (w, d) in enumerate(A_GROUPS):
            oa += list(_attn_a_prompt(sa, g, d))
        ob = _attn_b_prompt(sb, sinks_p)
        zero_state = jnp.zeros((BATCH // SUBLANES, SUBLANES, D_NS), F32)
        od_tm, s_re, s_im = _ssm(_to_time_major(sc[:, 2 * C_WIDTH:], BATCH, SEQ), a_re, a_im, bbig, cbig, dvec, wglu,
                                 zero_state, zero_state, 128)
        od = _from_time_major(od_tm, BATCH, SEQ)
        taps, conv_state = _conv_taps(sc[:, :C_WIDTH], jnp.zeros((BATCH, C_CONV - 1, C_WIDTH), F32), BATCH, SEQ)
        x1 = _merge(yp, *oa, ob, sc, *taps, od, *merge_w, tm_p)
        yp = _ffn(x1, p_prompt[i].reshape(mp, PLE_DIM), *ffn_w, tm_p)
        sa3 = sa.reshape(BATCH, SEQ, SLAB_A)
        for g, (w, d) in enumerate(A_GROUPS):
            for j in range(2):
                c0 = (1 + j) * A_QKV + g * A_W
                st_p[2 * g + j].append(sa3[:, SEQ - w:, c0:c0 + A_W].reshape(BATCH, w, A_HEADS, HEAD_DIM))
        sb3 = sb.reshape(BATCH, SEQ, SLAB_B)
        for j in range(2):
            c0 = B_QW + j * B_KW
            st_p[6 + j].append(sb3[:, SEQ - B_WINDOW:, c0:c0 + B_KW].reshape(BATCH, B_WINDOW, B_KV_HEADS, HEAD_DIM))
        st_p[8].append(conv_state)
        st_p[9].append(s_re.reshape(BATCH, D_NGROUPS, D_STATE))
        st_p[10].append(s_im.reshape(BATCH, D_NGROUPS, D_STATE))

        sa, sb, sc = _inproj(ys, row(norm_mix_pre), w_mix, cos_s, sin_s, tm_s, 1)
        oa = []
        for g, (w, d) in enumerate(A_GROUPS):
            q = _sample_heads(sa[:, g * A_W:(g + 1) * A_W], A_HEADS, 1)
            kn = _sample_new_rows(sa[:, A_QKV + g * A_W:A_QKV + (g + 1) * A_W], A_HEADS)
            vn = _sample_new_rows(sa[:, 2 * A_QKV + g * A_W:2 * A_QKV + (g + 1) * A_W], A_HEADS)
            o, lse, new_a[g][0], new_a[g][1] = _sample_attn(i, q, caches_a[g][0], caches_a[g][1], kn, vn,
                                                            new_a[g][0], new_a[g][1], window=w, dil=d)
            oa += [_sample_unheads(o, A_HEADS, 1), _sample_unheads(lse, A_HEADS, 1)]
        q = _sample_heads(sb[:, :B_QW], B_KV_HEADS, B_GROUP)
        kn = _sample_new_rows(sb[:, B_QW:B_QW + B_KW], B_KV_HEADS)
        vn = _sample_new_rows(sb[:, B_QW + B_KW:], B_KV_HEADS)
        o, _, new_b[0], new_b[1] = _sample_attn(i, q, cache_bk, cache_bv, kn, vn, new_b[0], new_b[1],
                                                window=B_WINDOW, dil=1, sinks=sinks_s)
        ob = _sample_unheads(o, B_KV_HEADS, B_GROUP)
        x0_re = state_d_re[i].reshape(DEC_BATCH // SUBLANES, SUBLANES, D_NS)
        x0_im = state_d_im[i].reshape(DEC_BATCH // SUBLANES, SUBLANES, D_NS)
        od_tm, s_re, s_im = _ssm(_to_time_major(sc[:, 2 * C_WIDTH:], DEC_BATCH, DEC_SEQ), a_re, a_im, bbig, cbig, dvec,
                                 wglu, x0_re, x0_im, DEC_SEQ)
        od = _from_time_major(od_tm, DEC_BATCH, DEC_SEQ)
        taps, conv_state = _conv_taps(sc[:, :C_WIDTH], state_c_conv[i], DEC_BATCH, DEC_SEQ)
        x1 = _merge(ys, *oa, ob, sc, *taps, od, *merge_w, tm_s)
        ys = _ffn(x1, p_sample[i].reshape(ms, PLE_DIM), *ffn_w, tm_s)
        conv_s.append(conv_state)
        dre_s.append(s_re.reshape(DEC_BATCH, D_NGROUPS, D_STATE))
        dim_s.append(s_im.reshape(DEC_BATCH, D_NGROUPS, D_STATE))

    p_states = [jnp.stack(s) for s in st_p]
    s_states = ([_row_major(c) for pair in new_a for c in pair] + [_row_major(new_b[0]), _row_major(new_b[1])]
                + [jnp.stack(conv_s), jnp.stack(dre_s), jnp.stack(dim_s)])
    out = [yp.reshape(BATCH, SEQ, D_MODEL), ys.reshape(DEC_BATCH, DEC_SEQ, D_MODEL)]
    for a, b in zip(p_states, s_states):
        out += [a, b]
    return tuple(out)
```

```python
import functools

import jax
import jax.numpy as jnp
from jax import lax
from jax.experimental import pallas as pl
from jax.experimental.pallas import tpu as pltpu

F32 = jnp.float32
BF16 = jnp.bfloat16

D_MODEL = 1024
BATCH = 8
SEQ = 2048
DEPTH = 2
DEC_BATCH = 128
DEC_SEQ = 8
PAST_LEN = 16384
HEAD_DIM = 64
HALF = HEAD_DIM // 2
ROPE_THETA = 10000.0
BLOCK = 128
EPS = 1e-6
NEG_INF = -1e30
A_HEADS = 4
A_GROUPS = ((128, 1), (512, 4), (2048, 16))
A_NG = len(A_GROUPS)
A_W = A_HEADS * HEAD_DIM
A_QKV = A_NG * A_W
B_Q_HEADS = 8
B_KV_HEADS = 2
B_GROUP = B_Q_HEADS // B_KV_HEADS
B_WINDOW = 128
B_QW = B_Q_HEADS * HEAD_DIM
B_KW = B_KV_HEADS * HEAD_DIM
C_WIDTH = 256
C_CONV = 3
D_GROUP = 16
D_NGROUPS = 16
D_WIDTH = 256
D_STATE = 64
D_NS = D_NGROUPS * D_STATE
N_BRANCH = 4
D_FF = 2816
PLE_DIM = 256
MIX_W = 4096
SCALE = HEAD_DIM ** -0.5

LANES = 128
SUBLANES = 8
VMEM_LIMIT = 56 * 1024 * 1024

SLAB_A = 3 * A_QKV
SLAB_B = B_QW + 2 * B_KW
SLAB_C = C_WIDTH + D_WIDTH
ROPE_A = 2 * A_QKV
ROPE_B = B_QW + B_KW


def _params(n_axes):
    return pltpu.CompilerParams(dimension_semantics=("arbitrary",) * n_axes, vmem_limit_bytes=VMEM_LIMIT)


def _resident(shape):
    nd = len(shape)
    return pl.BlockSpec(shape, lambda *_: (0,) * nd, pipeline_mode=pl.Buffered(1))


def _rmsnorm(x, g):
    return x * lax.rsqrt(jnp.mean(x * x, axis=-1, keepdims=True) + EPS) * g


def _mm(a, b):
    return jnp.dot(a, b, preferred_element_type=F32)


def _mm_nt(a, b):
    return lax.dot_general(a, b, (((1,), (1,)), ((), ())), preferred_element_type=F32)


def _low_lanes(rows):
    return lax.broadcasted_iota(jnp.int32, (rows, LANES), 1) < HEAD_DIM


def _inproj_body(*refs, seq_tiles, per_row_state):
    if per_row_state:
        x_ref, g_ref, w_ref, cos_ref, sin_ref, cw_ref, s1_ref, s2_ref, a_ref, b_ref, c_ref, tail_ref = refs
    else:
        x_ref, g_ref, w_ref, cos_ref, sin_ref, cw_ref, a_ref, b_ref, c_ref, tail_ref, carry_ref = refs
    tm = x_ref.shape[0]
    h = _rmsnorm(x_ref[...], g_ref[...]).astype(BF16)
    cos = cos_ref[...]
    sin = sin_ref[...]
    lane = lax.broadcasted_iota(jnp.int32, (tm, LANES), 1)
    first_half = (lane & (HEAD_DIM - 1)) < HALF

    def rope(z):
        partner = jnp.where(first_half, pltpu.roll(z, LANES - HALF, axis=1), pltpu.roll(z, HALF, axis=1))
        return z * cos + partner * sin

    za = _mm(h, w_ref[:, 0:SLAB_A])
    for c in range(SLAB_A // LANES):
        blk = za[:, c * LANES:(c + 1) * LANES]
        a_ref[:, c * LANES:(c + 1) * LANES] = rope(blk) if c * LANES < ROPE_A else blk
    zb = _mm(h, w_ref[:, SLAB_A:SLAB_A + SLAB_B])
    for c in range(SLAB_B // LANES):
        blk = zb[:, c * LANES:(c + 1) * LANES]
        b_ref[:, c * LANES:(c + 1) * LANES] = rope(blk) if c * LANES < ROPE_B else blk
    z = _mm(h, w_ref[:, SLAB_A + SLAB_B:MIX_W])
    zc = z[:, 2 * C_WIDTH:3 * C_WIDTH] * z[:, 0:C_WIDTH]

    row = lax.broadcasted_iota(jnp.int32, (tm, C_WIDTH), 0)
    z1 = pltpu.roll(zc, 1, axis=0)
    z2 = pltpu.roll(zc, 2, axis=0)
    if per_row_state:
        t = row & (DEC_SEQ - 1)
        z1 = jnp.where(t == 0, s1_ref[...], z1)
        z2 = jnp.where(t == 0, s2_ref[...], jnp.where(t == 1, s1_ref[...], z2))
        tail_ref[...] = zc
    else:
        @pl.when(pl.program_id(0) % seq_tiles == 0)
        def _():
            carry_ref[...] = jnp.zeros(carry_ref.shape, F32)
        z1 = jnp.where(row == 0, carry_ref[SUBLANES - 1:SUBLANES, :], z1)
        z2 = jnp.where(row == 0, carry_ref[SUBLANES - 2:SUBLANES - 1, :],
                       jnp.where(row == 1, carry_ref[SUBLANES - 1:SUBLANES, :], z2))
        carry_ref[...] = zc[tm - SUBLANES:tm, :]
        tail_ref[...] = zc[tm - SUBLANES:tm, :]
    conv = cw_ref[0:1, :] * z2 + cw_ref[1:2, :] * z1 + cw_ref[2:3, :] * zc
    c_ref[:, 0:C_WIDTH] = z[:, C_WIDTH:2 * C_WIDTH] * conv
    c_ref[:, C_WIDTH:SLAB_C] = z[:, 3 * C_WIDTH:4 * C_WIDTH]


def _inproj(x, g, w_mix, cos, sin, convw, tm, table_blocks, seq_tiles, row_state=None):
    m = x.shape[0]
    per_row_state = row_state is not None
    row = lambda i: (i, 0)
    tab = lambda i: (i % table_blocks, 0)
    in_specs = [pl.BlockSpec((tm, D_MODEL), row), _resident((1, D_MODEL)), _resident((D_MODEL, MIX_W)),
                pl.BlockSpec((tm, LANES), tab), pl.BlockSpec((tm, LANES), tab), _resident((SUBLANES, C_WIDTH))]
    args = [x, g, w_mix, cos, sin, convw]
    scratch = []
    if per_row_state:
        in_specs += [pl.BlockSpec((tm, C_WIDTH), row)] * 2
        args += list(row_state)
        tail_rows, tail_total = tm, m
    else:
        scratch = [pltpu.VMEM((SUBLANES, C_WIDTH), F32)]
        tail_rows, tail_total = SUBLANES, m // tm * SUBLANES
    return pl.pallas_call(
        functools.partial(_inproj_body, seq_tiles=seq_tiles, per_row_state=per_row_state),
        grid=(m // tm,),
        in_specs=in_specs,
        out_specs=[pl.BlockSpec((tm, SLAB_A), row), pl.BlockSpec((tm, SLAB_B), row), pl.BlockSpec((tm, SLAB_C), row),
                   pl.BlockSpec((tail_rows, C_WIDTH), row)],
        out_shape=[jax.ShapeDtypeStruct((m, SLAB_A), F32), jax.ShapeDtypeStruct((m, SLAB_B), F32),
                   jax.ShapeDtypeStruct((m, SLAB_C), F32), jax.ShapeDtypeStruct((tail_total, C_WIDTH), F32)],
        scratch_shapes=scratch,
        compiler_params=_params(1),
        name="in_proj",
    )(*args)


def _band_mask(nk):
    qi = lax.broadcasted_iota(jnp.int32, (BLOCK, nk), 0)
    kj = lax.broadcasted_iota(jnp.int32, (BLOCK, nk), 1)
    dist = qi + (nk - BLOCK) - kj
    return (dist >= 0) & (dist <= BLOCK)


def _softmax_pv(s, mask, v, sink=None):
    s = jnp.where(mask, s, NEG_INF)
    m = jnp.max(s, axis=-1, keepdims=True)
    if sink is not None:
        m = jnp.maximum(m, sink)
    p = jnp.exp(s - m)
    den = jnp.sum(p, axis=-1, keepdims=True)
    if sink is not None:
        den = den + jnp.exp(sink - m)
    o = _mm(p.astype(BF16), v) * (1.0 / den)
    return o, m + jnp.log(den)


def _attn_a_prompt_body(q_ref, k_ref, v_ref, o_ref, lse_ref, *, d):
    nb = SEQ // d // BLOCK
    lo = _low_lanes(BLOCK)

    def rows(start, n):
        return pl.ds(start, n) if d == 1 else pl.ds(start, n, stride=d)

    def block(q0, k0, nk):
        mask = _band_mask(nk)
        q = q_ref[0, rows(q0, BLOCK), :] * SCALE
        k = k_ref[0, rows(k0, nk), :].astype(BF16)
        v = v_ref[0, rows(k0, nk), :].astype(BF16)
        o0, l0 = _softmax_pv(_mm_nt(jnp.where(lo, q, 0.0).astype(BF16), k), mask, v)
        o1, l1 = _softmax_pv(_mm_nt(jnp.where(lo, 0.0, q).astype(BF16), k), mask, v)
        o_ref[0, rows(q0, BLOCK), :] = jnp.where(lo, o0, o1)
        lse_ref[0, rows(q0, BLOCK), :] = jnp.where(lo, l0, l1)

    for r in range(d):
        block(r, r, BLOCK)
        if nb > 1:
            def body(i, carry):
                q0 = r + i * (BLOCK * d)
                block(q0, q0 - BLOCK * d, 2 * BLOCK)
                return carry
            lax.fori_loop(1, nb, body, 0)


def _attn_a_prompt(slab_a, g, d):
    view = slab_a.reshape(BATCH, SEQ, SLAB_A)
    pairs = A_W // LANES
    spec = lambda off: pl.BlockSpec((1, SEQ, LANES), lambda b, hp: (b, 0, off * pairs + hp))
    oshape = jax.ShapeDtypeStruct((BATCH, SEQ, A_W), F32)
    o, lse = pl.pallas_call(
        functools.partial(_attn_a_prompt_body, d=d),
        grid=(BATCH, pairs),
        in_specs=[spec(g), spec(A_NG + g), spec(2 * A_NG + g)],
        out_specs=[spec(0), spec(0)],
        out_shape=[oshape, oshape],
        compiler_params=_params(2),
        name=f"attn_a{g + 1}_prompt",
    )(view, view, view)
    return o.reshape(BATCH * SEQ, A_W), lse.reshape(BATCH * SEQ, A_W)


def _attn_b_prompt_body(q_ref, k_ref, v_ref, sink_ref, o_ref, *, nb):
    lo = _low_lanes(BLOCK)

    def block(q0, k0, nk):
        mask = _band_mask(nk)
        k = k_ref[0, pl.ds(k0, nk), :]
        v = v_ref[0, pl.ds(k0, nk), :]
        ks = (k.astype(BF16), pltpu.roll(k, HEAD_DIM, axis=1).astype(BF16))
        vs = (v.astype(BF16), pltpu.roll(v, HEAD_DIM, axis=1).astype(BF16))
        for c in range(B_QW // LANES):
            cols = slice(c * LANES, (c + 1) * LANES)
            q = q_ref[0, pl.ds(q0, BLOCK), cols] * SCALE
            outs = []
            for half in range(2):
                head = 2 * c + half
                kv = head // B_GROUP
                swapped = kv ^ half
                qm = jnp.where(lo, q, 0.0) if half == 0 else jnp.where(lo, 0.0, q)
                sink = sink_ref[head:head + 1, 0:1]
                o, _ = _softmax_pv(_mm_nt(qm.astype(BF16), ks[swapped]), mask, vs[swapped], sink)
                outs.append(o)
            o_ref[0, pl.ds(q0, BLOCK), cols] = jnp.where(lo, outs[0], outs[1])

    block(0, 0, BLOCK)

    def body(i, carry):
        q0 = pl.multiple_of(i * BLOCK, BLOCK)
        block(q0, pl.multiple_of(q0 - BLOCK, BLOCK), 2 * BLOCK)
        return carry
    lax.fori_loop(1, nb, body, 0)


def _attn_b_prompt(slab_b, sinks):
    view = slab_b.reshape(BATCH, SEQ, SLAB_B)
    nq = B_QW // LANES
    o = pl.pallas_call(
        functools.partial(_attn_b_prompt_body, nb=SEQ // BLOCK),
        grid=(BATCH,),
        in_specs=[pl.BlockSpec((1, SEQ, B_QW), lambda b: (b, 0, 0)),
                  pl.BlockSpec((1, SEQ, B_KW), lambda b: (b, 0, nq)),
                  pl.BlockSpec((1, SEQ, B_KW), lambda b: (b, 0, nq + 1)),
                  _resident((B_Q_HEADS, LANES))],
        out_specs=pl.BlockSpec((1, SEQ, B_QW), lambda b: (b, 0, 0)),
        out_shape=jax.ShapeDtypeStruct((BATCH, SEQ, B_QW), F32),
        compiler_params=_params(1),
        name="attn_b_prompt",
    )(view, view, view, sinks)
    return o.reshape(BATCH * SEQ, B_QW)


def _sample_masks(rows, lb, window, dil):
    assert dil & (dil - 1) == 0 and DEC_SEQ & (DEC_SEQ - 1) == 0
    t_c = lax.broadcasted_iota(jnp.int32, (rows, lb), 0) & (DEC_SEQ - 1)
    dist_c = lb + t_c - lax.broadcasted_iota(jnp.int32, (rows, lb), 1)
    mask_c = (dist_c >= 0) & (dist_c <= window) & ((dist_c & (dil - 1)) == 0)
    t_n = lax.broadcasted_iota(jnp.int32, (rows, DEC_SEQ), 0) & (DEC_SEQ - 1)
    dist_n = t_n - lax.broadcasted_iota(jnp.int32, (rows, DEC_SEQ), 1)
    mask_n = (dist_n >= 0) & (dist_n <= window) & ((dist_n & (dil - 1)) == 0)
    return mask_c, mask_n


def _sample_attend(q, kt, vt, kn, vn, masks, sink=None):
    mask_c, mask_n = masks
    s_c = jnp.where(mask_c, _mm(q.astype(BF16), kt), NEG_INF)
    s_n = jnp.where(mask_n, _mm_nt(q, kn), NEG_INF)
    m = jnp.maximum(jnp.max(s_c, axis=-1, keepdims=True), jnp.max(s_n, axis=-1, keepdims=True))
    if sink is not None:
        m = jnp.maximum(m, sink)
    p_c = jnp.exp(s_c - m)
    p_n = jnp.exp(s_n - m)
    den = jnp.sum(p_c, axis=-1, keepdims=True) + jnp.sum(p_n, axis=-1, keepdims=True)
    if sink is not None:
        den = den + jnp.exp(sink - m)
    o = _mm_nt(p_c.astype(BF16), vt) + _mm(p_n, vn)
    return o * (1.0 / den), m + jnp.log(den)


def _shifted_cache(old, new_rows, lb):
    padded = jnp.concatenate([jnp.zeros((LANES - DEC_SEQ, LANES), F32), new_rows], axis=0)
    new_t = padded.T
    new_lane = lax.broadcasted_iota(jnp.int32, (LANES, LANES), 1) >= LANES - DEC_SEQ
    shifted = pltpu.roll(old, lb - DEC_SEQ, axis=1)
    tail = jnp.where(new_lane, new_t, shifted[:, lb - LANES:lb])
    if lb == LANES:
        return tail
    return jnp.concatenate([shifted[:, 0:lb - LANES], tail], axis=1)


def _sample_attn_body(*refs, aliased):
    sa_ref, sb_ref, sink_ref = refs[:3]
    cache_refs = refs[3:11]
    pos = 11 + (8 if aliased else 0)
    oa_ref, ob_ref = refs[pos:pos + 2]
    out_refs = refs[pos + 2:pos + 10]
    lo8 = _low_lanes(DEC_SEQ)

    o_g, l_g = [], []
    for g, (window, dil) in enumerate(A_GROUPS):
        kc_ref, vc_ref, ko_ref, vo_ref = cache_refs[2 * g], cache_refs[2 * g + 1], out_refs[2 * g], out_refs[2 * g + 1]
        lb = kc_ref.shape[-1]
        masks = _sample_masks(2 * DEC_SEQ, lb, window, dil)
        o_pairs, l_pairs = [], []
        for c in range(A_W // LANES):
            col = lambda part: slice(part * A_QKV + g * A_W + c * LANES, part * A_QKV + g * A_W + (c + 1) * LANES)
            q = sa_ref[:, col(0)] * SCALE
            kn, vn = sa_ref[:, col(1)], sa_ref[:, col(2)]
            kt = kc_ref[0, 0, 2 * c:2 * c + 2].reshape(LANES, lb)
            vt = vc_ref[0, 0, 2 * c:2 * c + 2].reshape(LANES, lb)
            q2 = jnp.concatenate([jnp.where(lo8, q, 0.0), jnp.where(lo8, 0.0, q)], axis=0)
            o, lse = _sample_attend(q2, kt.astype(BF16), vt.astype(BF16), kn, vn, masks)
            o_pairs.append(jnp.where(lo8, o[0:DEC_SEQ], o[DEC_SEQ:]))
            l_pairs.append(jnp.where(lo8, lse[0:DEC_SEQ], lse[DEC_SEQ:]))
            ko_ref[0, 0, 2 * c:2 * c + 2] = _shifted_cache(kt, kn, lb).reshape(2, HEAD_DIM, lb)
            vo_ref[0, 0, 2 * c:2 * c + 2] = _shifted_cache(vt, vn, lb).reshape(2, HEAD_DIM, lb)
        o_g.append(o_pairs)
        l_g.append(l_pairs)
    for c in range(A_W // LANES):
        l1, l2, l3 = l_g[0][c], l_g[1][c], l_g[2][c]
        lm = jnp.maximum(jnp.maximum(l1, l2), l3)
        e1, e2, e3 = jnp.exp(l1 - lm), jnp.exp(l2 - lm), jnp.exp(l3 - lm)
        es = e1 + e2 + e3
        oa_ref[:, c * LANES:(c + 1) * LANES] = (e1 / es) * o_g[0][c] + (e2 / es) * o_g[1][c] + (e3 / es) * o_g[2][c]

    kc_ref, vc_ref, ko_ref, vo_ref = cache_refs[6], cache_refs[7], out_refs[6], out_refs[7]
    lb = kc_ref.shape[-1]
    rows = B_GROUP * DEC_SEQ
    masks = _sample_masks(rows, lb, B_WINDOW, 1)
    kt = kc_ref[0, 0].reshape(LANES, lb)
    vt = vc_ref[0, 0].reshape(LANES, lb)
    kn, vn = sb_ref[:, B_QW:B_QW + B_KW], sb_ref[:, B_QW + B_KW:SLAB_B]
    ktb, vtb = kt.astype(BF16), vt.astype(BF16)
    halves = {}
    for kv in range(B_KV_HEADS):
        parts = []
        for gq in range(B_GROUP):
            head = kv * B_GROUP + gq
            q = sb_ref[:, (head // 2) * LANES:(head // 2 + 1) * LANES] * SCALE
            q = jnp.where(lo8, q, 0.0) if head % 2 == 0 else jnp.where(lo8, 0.0, q)
            parts.append(q if head % 2 == kv else pltpu.roll(q, HEAD_DIM, axis=1))
        q4 = jnp.concatenate(parts, axis=0)
        sink = sink_ref[kv * rows:(kv + 1) * rows, 0:1]
        o, _ = _sample_attend(q4, ktb, vtb, kn, vn, masks, sink)
        for gq in range(B_GROUP):
            head = kv * B_GROUP + gq
            oh = o[gq * DEC_SEQ:(gq + 1) * DEC_SEQ]
            halves[head] = oh if head % 2 == kv else pltpu.roll(oh, HEAD_DIM, axis=1)
    for c in range(B_QW // LANES):
        ob_ref[:, c * LANES:(c + 1) * LANES] = jnp.where(lo8, halves[2 * c], halves[2 * c + 1])
    ko_ref[0, 0] = _shifted_cache(kt, kn, lb).reshape(B_KV_HEADS, HEAD_DIM, lb)
    vo_ref[0, 0] = _shifted_cache(vt, vn, lb).reshape(B_KV_HEADS, HEAD_DIM, lb)


def _sample_attn(layer, slab_a, slab_b, sinks, caches, prev):
    aliased = prev is not None
    row = lambda b: (b, 0)
    cspec = lambda c: pl.BlockSpec((1, 1) + c.shape[2:], lambda b: (layer, b, 0, 0, 0))
    in_specs = [pl.BlockSpec((DEC_SEQ, SLAB_A), row), pl.BlockSpec((DEC_SEQ, SLAB_B), row), _resident(sinks.shape)]
    in_specs += [cspec(c) for c in caches]
    args = [slab_a, slab_b, sinks] + list(caches)
    aliases = {}
    if aliased:
        aliases = {len(args) + j: 2 + j for j in range(8)}
        in_specs += [pl.BlockSpec(memory_space=pl.ANY)] * 8
        args += list(prev)
    m = slab_a.shape[0]
    res = pl.pallas_call(
        functools.partial(_sample_attn_body, aliased=aliased),
        grid=(DEC_BATCH,),
        in_specs=in_specs,
        out_specs=[pl.BlockSpec((DEC_SEQ, A_W), row), pl.BlockSpec((DEC_SEQ, B_QW), row)] + [cspec(c) for c in caches],
        out_shape=[jax.ShapeDtypeStruct((m, A_W), F32), jax.ShapeDtypeStruct((m, B_QW), F32)]
                  + [jax.ShapeDtypeStruct(c.shape, F32) for c in caches],
        input_output_aliases=aliases,
        compiler_params=_params(1),
        name="sample_attn",
    )(*args)
    return res[0], res[1], list(res[2:])


def _ssm_body(u_ref, are_ref, aim_ref, bhi_ref, blo_ref, cbig_ref, dvec_ref, wglu_ref, x0re_ref, x0im_ref,
              od_ref, sre_ref, sim_ref, bu_ref, xs_ref, *, steps, pitch):
    @pl.when(pl.program_id(1) == 0)
    def _():
        sre_ref[0] = x0re_ref[0]
        sim_ref[0] = x0im_ref[0]

    n_re = D_NS // LANES
    tiles = lambda x: [x[:, j * LANES:(j + 1) * LANES] for j in range(x.shape[1] // LANES)]

    for s in range(SUBLANES):
        u = u_ref[s]
        u_hi = u.astype(BF16)
        u_lo = (u - u_hi.astype(F32)).astype(BF16)
        bu = _mm(u_hi, bhi_ref[...]) + _mm(u_lo, bhi_ref[...]) + _mm(u_hi, blo_ref[...])
        for j, tile in enumerate(tiles(bu)):
            bu_ref[j, s * pitch:s * pitch + steps, :] = tile
    a_re = are_ref[...]
    a_im = aim_ref[...]

    def step(t, carry):
        re, im = carry
        at_t = pl.ds(t, SUBLANES, stride=pitch)
        b_re = jnp.concatenate([bu_ref[j, at_t, :] for j in range(n_re)], axis=1)
        b_im = jnp.concatenate([bu_ref[n_re + j, at_t, :] for j in range(n_re)], axis=1)
        re, im = a_re * re - a_im * im + b_re, a_re * im + a_im * re + b_im
        for j, (tr, ti) in enumerate(zip(tiles(re), tiles(im))):
            xs_ref[j, at_t, :] = tr
            xs_ref[n_re + j, at_t, :] = ti
        return re, im

    re, im = lax.fori_loop(0, steps, step, (sre_ref[0], sim_ref[0]), unroll=min(steps, SUBLANES))
    sre_ref[0] = re
    sim_ref[0] = im
    for s in range(SUBLANES):
        xs = jnp.concatenate([xs_ref[j, s * pitch:s * pitch + steps, :] for j in range(2 * n_re)], axis=1)
        y = _mm(xs.astype(BF16), cbig_ref[...]) + dvec_ref[...] * u_ref[s]
        y = _mm(jax.nn.gelu(y).astype(BF16), wglu_ref[...])
        od_ref[s] = y[:, 0:D_WIDTH] * jax.nn.sigmoid(y[:, D_WIDTH:2 * D_WIDTH])


def _ssm(slab_c, bsz, length, a_re, a_im, b_hi, b_lo, cbig, dvec, wglu, x0_re, x0_im, steps):
    n = bsz // SUBLANES
    pitch = steps + SUBLANES
    view = slab_c.reshape(bsz, length, SLAB_C)
    u_spec = pl.BlockSpec((SUBLANES, steps, D_WIDTH), lambda i, c: (i, c, SLAB_C // D_WIDTH - 1))
    o_spec = pl.BlockSpec((SUBLANES, steps, D_WIDTH), lambda i, c: (i, c, 0))
    st_spec = pl.BlockSpec((1, SUBLANES, D_NS), lambda i, c: (i, 0, 0))
    st_shape = jax.ShapeDtypeStruct((n, SUBLANES, D_NS), F32)
    od, s_re, s_im = pl.pallas_call(
        functools.partial(_ssm_body, steps=steps, pitch=pitch),
        grid=(n, length // steps),
        in_specs=[u_spec, _resident((SUBLANES, D_NS)), _resident((SUBLANES, D_NS)), _resident((D_WIDTH, 2 * D_NS)),
                  _resident((D_WIDTH, 2 * D_NS)), _resident((2 * D_NS, D_WIDTH)), _resident((1, D_WIDTH)),
                  _resident((D_WIDTH, 2 * D_WIDTH)), st_spec, st_spec],
        out_specs=[o_spec, st_spec, st_spec],
        out_shape=[jax.ShapeDtypeStruct((bsz, length, D_WIDTH), F32), st_shape, st_shape],
        scratch_shapes=[pltpu.VMEM((2 * D_NS // LANES, SUBLANES * pitch, LANES), F32)] * 2,
        compiler_params=_params(2),
        name="ssm",
    )(view, a_re, a_im, b_hi, b_lo, cbig, dvec, wglu, x0_re, x0_im)
    return od.reshape(bsz * length, D_WIDTH), s_re, s_im


def _merge_body(*refs, n_groups):
    x_ref = refs[0]
    a_refs = refs[1:1 + (2 * n_groups if n_groups > 1 else 1)]
    ob_ref, oc_ref, od_ref, gpre_ref, wgl_ref, wa_ref, wb_ref, wc_ref, wd_ref, wout_ref, gpost_ref, y_ref = \
        refs[1 + len(a_refs):]
    x = x_ref[...]
    h = _rmsnorm(x, gpre_ref[...]).astype(BF16)
    if n_groups > 1:
        lses = [a_refs[2 * g + 1][...] for g in range(n_groups)]
        lm = functools.reduce(jnp.maximum, lses)
        es = [jnp.exp(l - lm) for l in lses]
        den = functools.reduce(lambda a, b: a + b, es)
        o_a = functools.reduce(lambda a, b: a + b, [(es[g] / den) * a_refs[2 * g][...] for g in range(n_groups)])
    else:
        o_a = a_refs[0][...]
    branches = ((o_a, wa_ref), (ob_ref[...], wb_ref), (oc_ref[...], wc_ref), (od_ref[...], wd_ref))
    merged = None
    for j, (o, w_ref) in enumerate(branches):
        gate = jax.nn.sigmoid(_mm(h, wgl_ref[:, j * D_MODEL:(j + 1) * D_MODEL]))
        term = gate * _mm(o.astype(BF16), w_ref[...])
        merged = term if merged is None else merged + term
    mix = _mm(merged.astype(BF16), wout_ref[...])
    y_ref[...] = x + _rmsnorm(mix, gpost_ref[...])


def _merge(x, a_parts, ob, slab_c, od, gpre, wgl, wa, wb, wc, wd, wout, gpost, tm):
    m = x.shape[0]
    row = lambda i: (i, 0)
    r256 = pl.BlockSpec((tm, 256), row)
    n_groups = max(1, len(a_parts) // 2)
    return pl.pallas_call(
        functools.partial(_merge_body, n_groups=n_groups),
        grid=(m // tm,),
        in_specs=[pl.BlockSpec((tm, D_MODEL), row)] + [r256] * len(a_parts)
                 + [pl.BlockSpec((tm, B_QW), row), r256, r256,
                    _resident((1, D_MODEL)), _resident((D_MODEL, N_BRANCH * D_MODEL)),
                    _resident((A_W, D_MODEL)), _resident((B_QW, D_MODEL)), _resident((C_WIDTH, D_MODEL)),
                    _resident((D_WIDTH, D_MODEL)), _resident((D_MODEL, D_MODEL)), _resident((1, D_MODEL))],
        out_specs=pl.BlockSpec((tm, D_MODEL), row),
        out_shape=jax.ShapeDtypeStruct((m, D_MODEL), F32),
        compiler_params=_params(1),
        name="merge",
    )(x, *a_parts, ob, slab_c, od, gpre, wgl, wa, wb, wc, wd, wout, gpost)


def _ffn_body(x_ref, p_ref, gpre_ref, wg_ref, wu_ref, wdn_ref, gpost_ref, wple_ref, wpg_ref, y_ref):
    x = x_ref[...]
    h = _rmsnorm(x, gpre_ref[...]).astype(BF16)
    act = jax.nn.silu(_mm(h, wg_ref[...])) * _mm(h, wu_ref[...])
    f = _mm(act.astype(BF16), wdn_ref[...])
    x = x + _rmsnorm(f, gpost_ref[...])
    gate = jax.nn.sigmoid(_mm(x.astype(BF16), wpg_ref[...]))
    y_ref[...] = x + gate * _mm(p_ref[...].astype(BF16), wple_ref[...])


def _ffn(x, p, gpre, wg, wu, wdn, gpost, wple, wpg, tm):
    m = x.shape[0]
    row = lambda i: (i, 0)
    return pl.pallas_call(
        _ffn_body,
        grid=(m // tm,),
        in_specs=[pl.BlockSpec((tm, D_MODEL), row), pl.BlockSpec((tm, PLE_DIM), row),
                  _resident((1, D_MODEL)), _resident((D_MODEL, D_FF)), _resident((D_MODEL, D_FF)),
                  _resident((D_FF, D_MODEL)), _resident((1, D_MODEL)), _resident((PLE_DIM, D_MODEL)),
                  _resident((D_MODEL, D_MODEL))],
        out_specs=pl.BlockSpec((tm, D_MODEL), row),
        out_shape=jax.ShapeDtypeStruct((m, D_MODEL), F32),
        compiler_params=_params(1),
        name="ffn_ple",
    )(x, p, gpre, wg, wu, wdn, gpost, wple, wpg)


def _rope_tables(pos):
    inv = ROPE_THETA ** (-jnp.arange(HALF, dtype=F32) / HALF)
    ang = pos.astype(F32)[:, None] * inv[None, :]
    cos, sin = jnp.cos(ang), jnp.sin(ang)
    cos_t = jnp.tile(cos, (1, LANES // HALF))
    sin_t = jnp.tile(jnp.concatenate([-sin, sin], axis=1), (1, LANES // HEAD_DIM))
    return cos_t, sin_t


def _ssm_weights(lam_re, lam_im, log_dt, b_re, b_im, c_re, c_im):
    lam = lax.complex(lam_re, lam_im)
    dt = jnp.exp(log_dt)[:, None]
    a_bar = jnp.exp(lam * dt)
    b_bar = ((a_bar - 1.0) / lam)[..., None] * lax.complex(b_re, b_im)
    eye = jnp.eye(D_NGROUPS, dtype=F32)
    pack_b = lambda b: jnp.einsum('gni,gh->gihn', b, eye).reshape(D_WIDTH, D_NS)
    pack_c = lambda c: jnp.einsum('gin,gh->gnhi', c, eye).reshape(D_NS, D_WIDTH)
    bbig = jnp.concatenate([pack_b(jnp.real(b_bar)), pack_b(jnp.imag(b_bar))], axis=1)
    cbig = jnp.concatenate([pack_c(c_re), -pack_c(c_im)], axis=0)
    b_hi = bbig.astype(BF16)
    b_lo = (bbig - b_hi.astype(F32)).astype(BF16)
    tile8 = lambda a: jnp.broadcast_to(a.reshape(1, D_NS), (SUBLANES, D_NS))
    return tile8(jnp.real(a_bar)), tile8(jnp.imag(a_bar)), b_hi, b_lo, cbig.astype(BF16)


def _key_major(c):
    return jnp.transpose(c, (0, 1, 3, 4, 2))


def _row_major(c):
    return jnp.transpose(c, (0, 1, 4, 2, 3))


def kernel(x_prompt, x_sample, p_prompt, p_sample, cache_a1_k, cache_a1_v, cache_a2_k, cache_a2_v, cache_a3_k, cache_a3_v, cache_b_k, cache_b_v, state_c_conv, state_d_re, state_d_im, norm_mix_pre, norm_mix_post, norm_ffn_pre, norm_ffn_post, w_in, attn_sinks, conv_c_w, ssm_lam_re, ssm_lam_im, ssm_log_dt, ssm_b_re, ssm_b_im, ssm_c_re, ssm_c_im, ssm_d, w_d_glu, w_br_a, w_br_b, w_br_c, w_br_d, w_out, w_ffn_gate, w_ffn_up, w_ffn_down, w_ple, w_ple_gate):
    assert x_prompt.shape == (BATCH, SEQ, D_MODEL) and x_sample.shape == (DEC_BATCH, DEC_SEQ, D_MODEL)
    assert w_in.shape == (DEPTH, D_MODEL, 2 * MIX_W)
    assert all(min(w, PAST_LEN) == w for w, _ in A_GROUPS) and min(B_WINDOW, PAST_LEN) == B_WINDOW
    mp, ms = BATCH * SEQ, DEC_BATCH * DEC_SEQ
    tm_p, tm_s = 512, 256

    cos_p, sin_p = _rope_tables(jnp.arange(SEQ, dtype=jnp.int32))
    cos_s, sin_s = _rope_tables(PAST_LEN + jnp.arange(DEC_SEQ, dtype=jnp.int32))
    cos_s, sin_s = jnp.tile(cos_s, (tm_s // DEC_SEQ, 1)), jnp.tile(sin_s, (tm_s // DEC_SEQ, 1))

    caches = [_key_major(c) for c in (cache_a1_k, cache_a1_v, cache_a2_k, cache_a2_v, cache_a3_k, cache_a3_v,
                                      cache_b_k, cache_b_v)]
    yp = x_prompt.reshape(mp, D_MODEL)
    ys = x_sample.reshape(ms, D_MODEL)
    new_caches = None
    st_p = [[] for _ in range(11)]
    conv_s, dre_s, dim_s = [], [], []

    for i in range(DEPTH):
        row = lambda a: a[i].reshape(1, -1)
        w_mix = w_in[i, :, :MIX_W].astype(BF16)
        w_gl = w_in[i, :, MIX_W:].astype(BF16)
        ssm_w = _ssm_weights(ssm_lam_re[i], ssm_lam_im[i], ssm_log_dt[i], ssm_b_re[i], ssm_b_im[i],
                             ssm_c_re[i], ssm_c_im[i]) + (ssm_d[i].reshape(1, D_WIDTH), w_d_glu[i].astype(BF16))
        sinks_p = jnp.broadcast_to(attn_sinks[i].reshape(B_Q_HEADS, 1), (B_Q_HEADS, LANES))
        sinks_s = jnp.broadcast_to(jnp.repeat(attn_sinks[i].reshape(B_Q_HEADS), DEC_SEQ)[:, None],
                                   (B_Q_HEADS * DEC_SEQ, LANES))
        convw = jnp.pad(conv_c_w[i], ((0, SUBLANES - C_CONV), (0, 0)))
        merge_w = (row(norm_mix_pre), w_gl, w_br_a[i].astype(BF16), w_br_b[i].astype(BF16),
                   w_br_c[i].astype(BF16), w_br_d[i].astype(BF16), w_out[i].astype(BF16), row(norm_mix_post))
        ffn_w = (row(norm_ffn_pre), w_ffn_gate[i].astype(BF16), w_ffn_up[i].astype(BF16), w_ffn_down[i].astype(BF16),
                 row(norm_ffn_post), w_ple[i].astype(BF16), w_ple_gate[i].astype(BF16))

        sa, sb, sc, tail = _inproj(yp, row(norm_mix_pre), w_mix, cos_p, sin_p, convw, tm_p, SEQ // tm_p, SEQ // tm_p)
        a_parts = []
        for g, (w, d) in enumerate(A_GROUPS):
            a_parts += list(_attn_a_prompt(sa, g, d))
        ob = _attn_b_prompt(sb, sinks_p)
        zero_state = jnp.zeros((BATCH // SUBLANES, SUBLANES, D_NS), F32)
        od, s_re, s_im = _ssm(sc, BATCH, SEQ, *ssm_w, zero_state, zero_state, 128)
        x1 = _merge(yp, a_parts, ob, sc, od, *merge_w, tm_p)
        yp = _ffn(x1, p_prompt[i].reshape(mp, PLE_DIM), *ffn_w, tm_p)
        sa3 = sa.reshape(BATCH, SEQ, SLAB_A)
        for g, (w, d) in enumerate(A_GROUPS):
            for j in range(2):
                c0 = (1 + j) * A_QKV + g * A_W
                st_p[2 * g + j].append(sa3[:, SEQ - w:, c0:c0 + A_W].reshape(BATCH, w, A_HEADS, HEAD_DIM))
        sb3 = sb.reshape(BATCH, SEQ, SLAB_B)
        for j in range(2):
            c0 = B_QW + j * B_KW
            st_p[6 + j].append(sb3[:, SEQ - B_WINDOW:, c0:c0 + B_KW].reshape(BATCH, B_WINDOW, B_KV_HEADS, HEAD_DIM))
        st_p[8].append(tail.reshape(BATCH, SEQ // tm_p, SUBLANES, C_WIDTH)[:, -1, SUBLANES - (C_CONV - 1):])
        st_p[9].append(s_re.reshape(BATCH, D_NGROUPS, D_STATE))
        st_p[10].append(s_im.reshape(BATCH, D_NGROUPS, D_STATE))

        s1 = jnp.broadcast_to(state_c_conv[i][:, None, 1, :], (DEC_BATCH, DEC_SEQ, C_WIDTH)).reshape(ms, C_WIDTH)
        s2 = jnp.broadcast_to(state_c_conv[i][:, None, 0, :], (DEC_BATCH, DEC_SEQ, C_WIDTH)).reshape(ms, C_WIDTH)
        sa, sb, sc, tail = _inproj(ys, row(norm_mix_pre), w_mix, cos_s, sin_s, convw, tm_s, 1, 1, row_state=(s1, s2))
        o_a, ob, new_caches = _sample_attn(i, sa, sb, sinks_s, caches, new_caches)
        x0_re = state_d_re[i].reshape(DEC_BATCH // SUBLANES, SUBLANES, D_NS)
        x0_im = state_d_im[i].reshape(DEC_BATCH // SUBLANES, SUBLANES, D_NS)
        od, s_re, s_im = _ssm(sc, DEC_BATCH, DEC_SEQ, *ssm_w, x0_re, x0_im, DEC_SEQ)
        x1 = _merge(ys, [o_a], ob, sc, od, *merge_w, tm_s)
        ys = _ffn(x1, p_sample[i].reshape(ms, PLE_DIM), *ffn_w, tm_s)
        conv_s.append(tail.reshape(DEC_BATCH, DEC_SEQ, C_WIDTH)[:, DEC_SEQ - (C_CONV - 1):])
        dre_s.append(s_re.reshape(DEC_BATCH, D_NGROUPS, D_STATE))
        dim_s.append(s_im.reshape(DEC_BATCH, D_NGROUPS, D_STATE))

    p_states = [jnp.stack(s) for s in st_p]
    s_states = [_row_major(c) for c in new_caches] + [jnp.stack(conv_s), jnp.stack(dre_s), jnp.stack(dim_s)]
    out = [yp.reshape(BATCH, SEQ, D_MODEL), ys.reshape(DEC_BATCH, DEC_SEQ, D_MODEL)]
    for a, b in zip(p_states, s_states):
        out += [a, b]
    return tuple(out)
```

```python
import functools

import jax
import jax.numpy as jnp
from jax import lax
from jax.experimental import pallas as pl
from jax.experimental.pallas import tpu as pltpu

F32 = jnp.float32
BF16 = jnp.bfloat16

D_MODEL = 1024
BATCH = 8
SEQ = 2048
DEPTH = 2
DEC_BATCH = 128
DEC_SEQ = 8
PAST_LEN = 16384
HEAD_DIM = 64
HALF = HEAD_DIM // 2
ROPE_THETA = 10000.0
BLOCK = 128
EPS = 1e-6
NEG_INF = -1e30
A_HEADS = 4
A_GROUPS = ((128, 1), (512, 4), (2048, 16))
A_NG = len(A_GROUPS)
A_W = A_HEADS * HEAD_DIM
A_QKV = A_NG * A_W
B_Q_HEADS = 8
B_KV_HEADS = 2
B_GROUP = B_Q_HEADS // B_KV_HEADS
B_WINDOW = 128
B_QW = B_Q_HEADS * HEAD_DIM
B_KW = B_KV_HEADS * HEAD_DIM
C_WIDTH = 256
C_CONV = 3
D_GROUP = 16
D_NGROUPS = 16
D_WIDTH = 256
D_STATE = 64
D_NS = D_NGROUPS * D_STATE
N_BRANCH = 4
D_FF = 2816
PLE_DIM = 256
MIX_W = 4096
SCALE = HEAD_DIM ** -0.5

LANES = 128
SUBLANES = 8
VMEM_LIMIT = 56 * 1024 * 1024

SLAB_A = 3 * A_QKV
SLAB_B = B_QW + 2 * B_KW
SLAB_C = C_WIDTH + D_WIDTH
ROPE_A = 2 * A_QKV
ROPE_B = B_QW + B_KW


def _params(n_axes):
    return pltpu.CompilerParams(dimension_semantics=("arbitrary",) * n_axes, vmem_limit_bytes=VMEM_LIMIT)


def _resident(shape):
    nd = len(shape)
    return pl.BlockSpec(shape, lambda *_: (0,) * nd, pipeline_mode=pl.Buffered(1))


def _rmsnorm(x, g):
    return x * lax.rsqrt(jnp.mean(x * x, axis=-1, keepdims=True) + EPS) * g


def _mm(a, b):
    return jnp.dot(a, b, preferred_element_type=F32)


def _mm_nt(a, b):
    return lax.dot_general(a, b, (((1,), (1,)), ((), ())), preferred_element_type=F32)


def _low_lanes(rows):
    return lax.broadcasted_iota(jnp.int32, (rows, LANES), 1) < HEAD_DIM


def _inproj_body(*refs, seq_tiles, per_row_state):
    if per_row_state:
        x_ref, g_ref, w_ref, cos_ref, sin_ref, cw_ref, s1_ref, s2_ref, a_ref, b_ref, c_ref, tail_ref = refs
    else:
        x_ref, g_ref, w_ref, cos_ref, sin_ref, cw_ref, a_ref, b_ref, c_ref, tail_ref, carry_ref = refs
    tm = x_ref.shape[0]
    h = _rmsnorm(x_ref[...], g_ref[...]).astype(BF16)
    cos = cos_ref[...]
    sin = sin_ref[...]
    lane = lax.broadcasted_iota(jnp.int32, (tm, LANES), 1)
    first_half = (lane & (HEAD_DIM - 1)) < HALF

    def rope(z):
        partner = jnp.where(first_half, pltpu.roll(z, LANES - HALF, axis=1), pltpu.roll(z, HALF, axis=1))
        return z * cos + partner * sin

    za = _mm(h, w_ref[:, 0:SLAB_A])
    for c in range(SLAB_A // LANES):
        blk = za[:, c * LANES:(c + 1) * LANES]
        a_ref[:, c * LANES:(c + 1) * LANES] = rope(blk) if c * LANES < ROPE_A else blk
    zb = _mm(h, w_ref[:, SLAB_A:SLAB_A + SLAB_B])
    for c in range(SLAB_B // LANES):
        blk = zb[:, c * LANES:(c + 1) * LANES]
        b_ref[:, c * LANES:(c + 1) * LANES] = rope(blk) if c * LANES < ROPE_B else blk
    z = _mm(h, w_ref[:, SLAB_A + SLAB_B:MIX_W])
    zc = z[:, 2 * C_WIDTH:3 * C_WIDTH] * z[:, 0:C_WIDTH]

    row = lax.broadcasted_iota(jnp.int32, (tm, C_WIDTH), 0)
    z1 = pltpu.roll(zc, 1, axis=0)
    z2 = pltpu.roll(zc, 2, axis=0)
    if per_row_state:
        t = row & (DEC_SEQ - 1)
        z1 = jnp.where(t == 0, s1_ref[...], z1)
        z2 = jnp.where(t == 0, s2_ref[...], jnp.where(t == 1, s1_ref[...], z2))
        tail_ref[...] = zc
    else:
        @pl.when(pl.program_id(0) % seq_tiles == 0)
        def _():
            carry_ref[...] = jnp.zeros(carry_ref.shape, F32)
        z1 = jnp.where(row == 0, carry_ref[SUBLANES - 1:SUBLANES, :], z1)
        z2 = jnp.where(row == 0, carry_ref[SUBLANES - 2:SUBLANES - 1, :],
                       jnp.where(row == 1, carry_ref[SUBLANES - 1:SUBLANES, :], z2))
        carry_ref[...] = zc[tm - SUBLANES:tm, :]
        tail_ref[...] = zc[tm - SUBLANES:tm, :]
    conv = cw_ref[0:1, :] * z2 + cw_ref[1:2, :] * z1 + cw_ref[2:3, :] * zc
    c_ref[:, 0:C_WIDTH] = z[:, C_WIDTH:2 * C_WIDTH] * conv
    c_ref[:, C_WIDTH:SLAB_C] = z[:, 3 * C_WIDTH:4 * C_WIDTH]


def _inproj(x, g, w_mix, cos, sin, convw, tm, table_blocks, seq_tiles, row_state=None):
    m = x.shape[0]
    per_row_state = row_state is not None
    row = lambda i: (i, 0)
    tab = lambda i: (i % table_blocks, 0)
    in_specs = [pl.BlockSpec((tm, D_MODEL), row), _resident((1, D_MODEL)), _resident((D_MODEL, MIX_W)),
                pl.BlockSpec((tm, LANES), tab), pl.BlockSpec((tm, LANES), tab), _resident((SUBLANES, C_WIDTH))]
    args = [x, g, w_mix, cos, sin, convw]
    scratch = []
    if per_row_state:
        in_specs += [pl.BlockSpec((tm, C_WIDTH), row)] * 2
        args += list(row_state)
        tail_rows, tail_total = tm, m
    else:
        scratch = [pltpu.VMEM((SUBLANES, C_WIDTH), F32)]
        tail_rows, tail_total = SUBLANES, m // tm * SUBLANES
    return pl.pallas_call(
        functools.partial(_inproj_body, seq_tiles=seq_tiles, per_row_state=per_row_state),
        grid=(m // tm,),
        in_specs=in_specs,
        out_specs=[pl.BlockSpec((tm, SLAB_A), row), pl.BlockSpec((tm, SLAB_B), row), pl.BlockSpec((tm, SLAB_C), row),
                   pl.BlockSpec((tail_rows, C_WIDTH), row)],
        out_shape=[jax.ShapeDtypeStruct((m, SLAB_A), F32), jax.ShapeDtypeStruct((m, SLAB_B), F32),
                   jax.ShapeDtypeStruct((m, SLAB_C), F32), jax.ShapeDtypeStruct((tail_total, C_WIDTH), F32)],
        scratch_shapes=scratch,
        compiler_params=_params(1),
        name="in_proj",
    )(*args)


def _band_mask_t(nk):
    kj = lax.broadcasted_iota(jnp.int32, (nk, BLOCK), 0)
    qi = lax.broadcasted_iota(jnp.int32, (nk, BLOCK), 1)
    dist = qi + (nk - BLOCK) - kj
    return (dist >= 0) & (dist <= BLOCK)


def _softmax_t(s, mask, sink=None):
    s = jnp.where(mask, s, NEG_INF)
    m = jnp.max(s, axis=0, keepdims=True)
    if sink is not None:
        m = jnp.maximum(m, sink)
    p = jnp.exp(s - m)
    den = jnp.sum(p, axis=0, keepdims=True)
    if sink is not None:
        den = den + jnp.exp(sink - m)
    return p, 1.0 / den, m + jnp.log(den)


def _attn_a_prompt_body(q_ref, k_ref, v_ref, o_ref, lse_ref, *, d):
    nb = SEQ // d // BLOCK
    first_head_rows = lax.broadcasted_iota(jnp.int32, (LANES, BLOCK), 0) < HEAD_DIM

    def rows(start, n):
        return pl.ds(start, n) if d == 1 else pl.ds(start, n, stride=d)

    def block(q0, k0, nk):
        mask = _band_mask_t(nk)
        lo = _low_lanes(nk)
        q = (q_ref[0, rows(q0, BLOCK), :] * SCALE).astype(BF16)
        k = k_ref[0, rows(k0, nk), :]
        v = v_ref[0, rows(k0, nk), :]
        k2 = jnp.concatenate([jnp.where(lo, k, 0.0), jnp.where(lo, 0.0, k)], axis=0).astype(BF16)
        st = _mm_nt(k2, q)
        p0, r0, l0 = _softmax_t(st[0:nk], mask)
        p1, r1, l1 = _softmax_t(st[nk:2 * nk], mask)
        ot = _mm(v.T.astype(BF16), jnp.concatenate([p0, p1], axis=1).astype(BF16))
        o_t = jnp.where(first_head_rows, ot[:, 0:BLOCK] * r0, ot[:, BLOCK:2 * BLOCK] * r1)
        l_t = jnp.where(first_head_rows, jnp.broadcast_to(l0, (LANES, BLOCK)), jnp.broadcast_to(l1, (LANES, BLOCK)))
        o_ref[0, rows(q0, BLOCK), :] = o_t.T
        lse_ref[0, rows(q0, BLOCK), :] = l_t.T

    for r in range(d):
        block(r, r, BLOCK)
        if nb > 1:
            def body(i, carry):
                q0 = r + i * (BLOCK * d)
                block(q0, q0 - BLOCK * d, 2 * BLOCK)
                return carry
            lax.fori_loop(1, nb, body, 0, unroll=min(nb - 1, 3))


def _attn_a_prompt(slab_a, g, d):
    view = slab_a.reshape(BATCH, SEQ, SLAB_A)
    pairs = A_W // LANES
    spec = lambda off: pl.BlockSpec((1, SEQ, LANES), lambda b, hp: (b, 0, off * pairs + hp))
    oshape = jax.ShapeDtypeStruct((BATCH, SEQ, A_W), F32)
    o, lse = pl.pallas_call(
        functools.partial(_attn_a_prompt_body, d=d),
        grid=(BATCH, pairs),
        in_specs=[spec(g), spec(A_NG + g), spec(2 * A_NG + g)],
        out_specs=[spec(0), spec(0)],
        out_shape=[oshape, oshape],
        compiler_params=_params(2),
        name=f"attn_a{g + 1}_prompt",
    )(view, view, view)
    return o.reshape(BATCH * SEQ, A_W), lse.reshape(BATCH * SEQ, A_W)


def _attn_b_prompt_body(q_ref, k_ref, v_ref, sink_ref, o_ref, *, nb):
    def block(q0, k0, nk):
        mask = _band_mask_t(nk)
        lo = _low_lanes(nk)
        k = k_ref[0, pl.ds(k0, nk), :]
        k_sw = pltpu.roll(k, HEAD_DIM, axis=1)
        vt = v_ref[0, pl.ds(k0, nk), :].T.astype(BF16)
        for kv in range(B_KV_HEADS):
            on_lo, on_hi = (k, k_sw) if kv == 0 else (k_sw, k)
            k2 = jnp.concatenate([jnp.where(lo, on_lo, 0.0), jnp.where(lo, 0.0, on_hi)], axis=0).astype(BF16)
            cols = slice(2 * kv * LANES, (2 * kv + 2) * LANES)
            q2 = q_ref[0, pl.ds(q0, BLOCK), cols] * SCALE
            q2 = jnp.concatenate([q2[:, 0:LANES], q2[:, LANES:2 * LANES]], axis=0).astype(BF16)
            st = _mm_nt(k2, q2)
            ps, rs = [], []
            for g in range(B_GROUP):
                half, chunk = g % 2, g // 2
                sink = sink_ref[kv * B_GROUP + g:kv * B_GROUP + g + 1, :]
                p, r, _ = _softmax_t(st[half * nk:(half + 1) * nk, chunk * BLOCK:(chunk + 1) * BLOCK], mask, sink)
                ps.append(p)
                rs.append(r)
            ot = _mm(vt, jnp.concatenate(ps, axis=1).astype(BF16))
            oj = ot[kv * HEAD_DIM:(kv + 1) * HEAD_DIM, :] * jnp.concatenate(rs, axis=1)
            for chunk in range(2):
                o_t = jnp.concatenate([oj[:, (2 * chunk) * BLOCK:(2 * chunk + 1) * BLOCK],
                                       oj[:, (2 * chunk + 1) * BLOCK:(2 * chunk + 2) * BLOCK]], axis=0)
                c = 2 * kv + chunk
                o_ref[0, pl.ds(q0, BLOCK), c * LANES:(c + 1) * LANES] = o_t.T

    block(0, 0, BLOCK)

    def body(i, carry):
        q0 = pl.multiple_of(i * BLOCK, BLOCK)
        block(q0, pl.multiple_of(q0 - BLOCK, BLOCK), 2 * BLOCK)
        return carry
    lax.fori_loop(1, nb, body, 0, unroll=3)


def _attn_b_prompt(slab_b, sinks):
    view = slab_b.reshape(BATCH, SEQ, SLAB_B)
    nq = B_QW // LANES
    o = pl.pallas_call(
        functools.partial(_attn_b_prompt_body, nb=SEQ // BLOCK),
        grid=(BATCH,),
        in_specs=[pl.BlockSpec((1, SEQ, B_QW), lambda b: (b, 0, 0)),
                  pl.BlockSpec((1, SEQ, B_KW), lambda b: (b, 0, nq)),
                  pl.BlockSpec((1, SEQ, B_KW), lambda b: (b, 0, nq + 1)),
                  _resident((B_Q_HEADS, LANES))],
        out_specs=pl.BlockSpec((1, SEQ, B_QW), lambda b: (b, 0, 0)),
        out_shape=jax.ShapeDtypeStruct((BATCH, SEQ, B_QW), F32),
        compiler_params=_params(1),
        name="attn_b_prompt",
    )(view, view, view, sinks)
    return o.reshape(BATCH * SEQ, B_QW)


def _sample_masks(rows, lb, window, dil):
    assert dil & (dil - 1) == 0 and DEC_SEQ & (DEC_SEQ - 1) == 0
    t_c = lax.broadcasted_iota(jnp.int32, (rows, lb), 0) & (DEC_SEQ - 1)
    dist_c = lb + t_c - lax.broadcasted_iota(jnp.int32, (rows, lb), 1)
    mask_c = (dist_c >= 0) & (dist_c <= window) & ((dist_c & (dil - 1)) == 0)
    t_n = lax.broadcasted_iota(jnp.int32, (rows, DEC_SEQ), 0) & (DEC_SEQ - 1)
    dist_n = t_n - lax.broadcasted_iota(jnp.int32, (rows, DEC_SEQ), 1)
    mask_n = (dist_n >= 0) & (dist_n <= window) & ((dist_n & (dil - 1)) == 0)
    return mask_c, mask_n


def _sample_attend(q, kt, vt, kn, vn, masks, sink=None):
    mask_c, mask_n = masks
    s_c = jnp.where(mask_c, _mm(q.astype(BF16), kt), NEG_INF)
    s_n = jnp.where(mask_n, _mm_nt(q, kn), NEG_INF)
    m = jnp.maximum(jnp.max(s_c, axis=-1, keepdims=True), jnp.max(s_n, axis=-1, keepdims=True))
    if sink is not None:
        m = jnp.maximum(m, sink)
    p_c = jnp.exp(s_c - m)
    p_n = jnp.exp(s_n - m)
    den = jnp.sum(p_c, axis=-1, keepdims=True) + jnp.sum(p_n, axis=-1, keepdims=True)
    if sink is not None:
        den = den + jnp.exp(sink - m)
    o = _mm_nt(p_c.astype(BF16), vt) + _mm(p_n, vn)
    return o * (1.0 / den), m + jnp.log(den)


def _shifted_cache(old, new_rows, lb):
    padded = jnp.concatenate([jnp.zeros((LANES - DEC_SEQ, LANES), F32), new_rows], axis=0)
    new_t = padded.T
    new_lane = lax.broadcasted_iota(jnp.int32, (LANES, LANES), 1) >= LANES - DEC_SEQ
    shifted = pltpu.roll(old, lb - DEC_SEQ, axis=1)
    tail = jnp.where(new_lane, new_t, shifted[:, lb - LANES:lb])
    if lb == LANES:
        return tail
    return jnp.concatenate([shifted[:, 0:lb - LANES], tail], axis=1)


def _sample_attn_body(*refs, aliased):
    sa_ref, sb_ref, sink_ref = refs[:3]
    cache_refs = refs[3:11]
    pos = 11 + (8 if aliased else 0)
    oa_ref, ob_ref = refs[pos:pos + 2]
    out_refs = refs[pos + 2:pos + 10]
    lo8 = _low_lanes(DEC_SEQ)

    o_g, l_g = [], []
    for g, (window, dil) in enumerate(A_GROUPS):
        kc_ref, vc_ref, ko_ref, vo_ref = cache_refs[2 * g], cache_refs[2 * g + 1], out_refs[2 * g], out_refs[2 * g + 1]
        lb = kc_ref.shape[-1]
        masks = _sample_masks(2 * DEC_SEQ, lb, window, dil)
        o_pairs, l_pairs = [], []
        for c in range(A_W // LANES):
            col = lambda part: slice(part * A_QKV + g * A_W + c * LANES, part * A_QKV + g * A_W + (c + 1) * LANES)
            q = sa_ref[:, col(0)] * SCALE
            kn, vn = sa_ref[:, col(1)], sa_ref[:, col(2)]
            kt = kc_ref[0, 0, 2 * c:2 * c + 2].reshape(LANES, lb)
            vt = vc_ref[0, 0, 2 * c:2 * c + 2].reshape(LANES, lb)
            q2 = jnp.concatenate([jnp.where(lo8, q, 0.0), jnp.where(lo8, 0.0, q)], axis=0)
            o, lse = _sample_attend(q2, kt.astype(BF16), vt.astype(BF16), kn, vn, masks)
            o_pairs.append(jnp.where(lo8, o[0:DEC_SEQ], o[DEC_SEQ:]))
            l_pairs.append(jnp.where(lo8, lse[0:DEC_SEQ], lse[DEC_SEQ:]))
            ko_ref[0, 0, 2 * c:2 * c + 2] = _shifted_cache(kt, kn, lb).reshape(2, HEAD_DIM, lb)
            vo_ref[0, 0, 2 * c:2 * c + 2] = _shifted_cache(vt, vn, lb).reshape(2, HEAD_DIM, lb)
        o_g.append(o_pairs)
        l_g.append(l_pairs)
    for c in range(A_W // LANES):
        l1, l2, l3 = l_g[0][c], l_g[1][c], l_g[2][c]
        lm = jnp.maximum(jnp.maximum(l1, l2), l3)
        e1, e2, e3 = jnp.exp(l1 - lm), jnp.exp(l2 - lm), jnp.exp(l3 - lm)
        es = e1 + e2 + e3
        oa_ref[:, c * LANES:(c + 1) * LANES] = (e1 / es) * o_g[0][c] + (e2 / es) * o_g[1][c] + (e3 / es) * o_g[2][c]

    kc_ref, vc_ref, ko_ref, vo_ref = cache_refs[6], cache_refs[7], out_refs[6], out_refs[7]
    lb = kc_ref.shape[-1]
    rows = B_GROUP * DEC_SEQ
    masks = _sample_masks(rows, lb, B_WINDOW, 1)
    kt = kc_ref[0, 0].reshape(LANES, lb)
    vt = vc_ref[0, 0].reshape(LANES, lb)
    kn, vn = sb_ref[:, B_QW:B_QW + B_KW], sb_ref[:, B_QW + B_KW:SLAB_B]
    ktb, vtb = kt.astype(BF16), vt.astype(BF16)
    halves = {}
    for kv in range(B_KV_HEADS):
        parts = []
        for gq in range(B_GROUP):
            head = kv * B_GROUP + gq
            q = sb_ref[:, (head // 2) * LANES:(head // 2 + 1) * LANES] * SCALE
            q = jnp.where(lo8, q, 0.0) if head % 2 == 0 else jnp.where(lo8, 0.0, q)
            parts.append(q if head % 2 == kv else pltpu.roll(q, HEAD_DIM, axis=1))
        q4 = jnp.concatenate(parts, axis=0)
        sink = sink_ref[kv * rows:(kv + 1) * rows, 0:1]
        o, _ = _sample_attend(q4, ktb, vtb, kn, vn, masks, sink)
        for gq in range(B_GROUP):
            head = kv * B_GROUP + gq
            oh = o[gq * DEC_SEQ:(gq + 1) * DEC_SEQ]
            halves[head] = oh if head % 2 == kv else pltpu.roll(oh, HEAD_DIM, axis=1)
    for c in range(B_QW // LANES):
        ob_ref[:, c * LANES:(c + 1) * LANES] = jnp.where(lo8, halves[2 * c], halves[2 * c + 1])
    ko_ref[0, 0] = _shifted_cache(kt, kn, lb).reshape(B_KV_HEADS, HEAD_DIM, lb)
    vo_ref[0, 0] = _shifted_cache(vt, vn, lb).reshape(B_KV_HEADS, HEAD_DIM, lb)


def _sample_attn(layer, slab_a, slab_b, sinks, caches, prev):
    aliased = prev is not None
    row = lambda b: (b, 0)
    cspec = lambda c: pl.BlockSpec((1, 1) + c.shape[2:], lambda b: (layer, b, 0, 0, 0))
    in_specs = [pl.BlockSpec((DEC_SEQ, SLAB_A), row), pl.BlockSpec((DEC_SEQ, SLAB_B), row), _resident(sinks.shape)]
    in_specs += [cspec(c) for c in caches]
    args = [slab_a, slab_b, sinks] + list(caches)
    aliases = {}
    if aliased:
        aliases = {len(args) + j: 2 + j for j in range(8)}
        in_specs += [pl.BlockSpec(memory_space=pl.ANY)] * 8
        args += list(prev)
    m = slab_a.shape[0]
    res = pl.pallas_call(
        functools.partial(_sample_attn_body, aliased=aliased),
        grid=(DEC_BATCH,),
        in_specs=in_specs,
        out_specs=[pl.BlockSpec((DEC_SEQ, A_W), row), pl.BlockSpec((DEC_SEQ, B_QW), row)] + [cspec(c) for c in caches],
        out_shape=[jax.ShapeDtypeStruct((m, A_W), F32), jax.ShapeDtypeStruct((m, B_QW), F32)]
                  + [jax.ShapeDtypeStruct(c.shape, F32) for c in caches],
        input_output_aliases=aliases,
        compiler_params=_params(1),
        name="sample_attn",
    )(*args)
    return res[0], res[1], list(res[2:])


def _ssm_body(u_ref, are_ref, aim_ref, bhi_ref, blo_ref, cbig_ref, dvec_ref, wglu_ref, x0re_ref, x0im_ref,
              od_ref, sre_ref, sim_ref, bu_ref, xs_ref, *, steps, pitch):
    @pl.when(pl.program_id(1) == 0)
    def _():
        sre_ref[0] = x0re_ref[0]
        sim_ref[0] = x0im_ref[0]

    n_re = D_NS // LANES
    tiles = lambda x: [x[:, j * LANES:(j + 1) * LANES] for j in range(x.shape[1] // LANES)]

    if pitch == steps:
        planes = [(slice(None), slice(0, SUBLANES * steps))]
    else:
        planes = [(s, slice(s * pitch, s * pitch + steps)) for s in range(SUBLANES)]
    load_u = lambda sel: u_ref[sel].reshape(-1, D_WIDTH)

    for sel, plane_rows in planes:
        u = load_u(sel)
        u_hi = u.astype(BF16)
        u_lo = (u - u_hi.astype(F32)).astype(BF16)
        bu = _mm(u_hi, bhi_ref[...]) + _mm(u_lo, bhi_ref[...]) + _mm(u_hi, blo_ref[...])
        for j, tile in enumerate(tiles(bu)):
            bu_ref[j, plane_rows, :] = tile
    a_re = are_ref[...]
    a_im = aim_ref[...]

    def step(t, carry):
        re, im = carry
        at_t = pl.ds(t, SUBLANES, stride=pitch)
        b_re = jnp.concatenate([bu_ref[j, at_t, :] for j in range(n_re)], axis=1)
        b_im = jnp.concatenate([bu_ref[n_re + j, at_t, :] for j in range(n_re)], axis=1)
        re, im = a_re * re - a_im * im + b_re, a_re * im + a_im * re + b_im
        for j, (tr, ti) in enumerate(zip(tiles(re), tiles(im))):
            xs_ref[j, at_t, :] = tr
            xs_ref[n_re + j, at_t, :] = ti
        return re, im

    re, im = lax.fori_loop(0, steps, step, (sre_ref[0], sim_ref[0]), unroll=min(steps, SUBLANES))
    sre_ref[0] = re
    sim_ref[0] = im
    for sel, plane_rows in planes:
        xs = jnp.concatenate([xs_ref[j, plane_rows, :] for j in range(2 * n_re)], axis=1)
        y = _mm(xs.astype(BF16), cbig_ref[...]) + dvec_ref[...] * load_u(sel)
        y = _mm(jax.nn.gelu(y).astype(BF16), wglu_ref[...])
        od = y[:, 0:D_WIDTH] * jax.nn.sigmoid(y[:, D_WIDTH:2 * D_WIDTH])
        od_ref[sel] = od.reshape((SUBLANES, steps, D_WIDTH) if pitch == steps else (steps, D_WIDTH))


def _ssm(slab_c, bsz, length, a_re, a_im, b_hi, b_lo, cbig, dvec, wglu, x0_re, x0_im, steps):
    n = bsz // SUBLANES
    pitch = steps if steps <= SUBLANES else steps + SUBLANES
    view = slab_c.reshape(bsz, length, SLAB_C)
    u_spec = pl.BlockSpec((SUBLANES, steps, D_WIDTH), lambda i, c: (i, c, SLAB_C // D_WIDTH - 1))
    o_spec = pl.BlockSpec((SUBLANES, steps, D_WIDTH), lambda i, c: (i, c, 0))
    st_spec = pl.BlockSpec((1, SUBLANES, D_NS), lambda i, c: (i, 0, 0))
    st_shape = jax.ShapeDtypeStruct((n, SUBLANES, D_NS), F32)
    od, s_re, s_im = pl.pallas_call(
        functools.partial(_ssm_body, steps=steps, pitch=pitch),
        grid=(n, length // steps),
        in_specs=[u_spec, _resident((SUBLANES, D_NS)), _resident((SUBLANES, D_NS)), _resident((D_WIDTH, 2 * D_NS)),
                  _resident((D_WIDTH, 2 * D_NS)), _resident((2 * D_NS, D_WIDTH)), _resident((1, D_WIDTH)),
                  _resident((D_WIDTH, 2 * D_WIDTH)), st_spec, st_spec],
        out_specs=[o_spec, st_spec, st_spec],
        out_shape=[jax.ShapeDtypeStruct((bsz, length, D_WIDTH), F32), st_shape, st_shape],
        scratch_shapes=[pltpu.VMEM((2 * D_NS // LANES, SUBLANES * pitch, LANES), F32)] * 2,
        compiler_params=_params(2),
        name="ssm",
    )(view, a_re, a_im, b_hi, b_lo, cbig, dvec, wglu, x0_re, x0_im)
    return od.reshape(bsz * length, D_WIDTH), s_re, s_im


def _merge_body(*refs, n_groups):
    x_ref = refs[0]
    a_refs = refs[1:1 + (2 * n_groups if n_groups > 1 else 1)]
    ob_ref, oc_ref, od_ref, gpre_ref, wgl_ref, wa_ref, wb_ref, wc_ref, wd_ref, wout_ref, gpost_ref, y_ref = \
        refs[1 + len(a_refs):]
    x = x_ref[...]
    h = _rmsnorm(x, gpre_ref[...]).astype(BF16)
    if n_groups > 1:
        lses = [a_refs[2 * g + 1][...] for g in range(n_groups)]
        lm = functools.reduce(jnp.maximum, lses)
        es = [jnp.exp(l - lm) for l in lses]
        den = functools.reduce(lambda a, b: a + b, es)
        o_a = functools.reduce(lambda a, b: a + b, [(es[g] / den) * a_refs[2 * g][...] for g in range(n_groups)])
    else:
        o_a = a_refs[0][...]
    branches = ((o_a, wa_ref), (ob_ref[...], wb_ref), (oc_ref[...], wc_ref), (od_ref[...], wd_ref))
    merged = None
    for j, (o, w_ref) in enumerate(branches):
        gate = jax.nn.sigmoid(_mm(h, wgl_ref[:, j * D_MODEL:(j + 1) * D_MODEL]))
        term = gate * _mm(o.astype(BF16), w_ref[...])
        merged = term if merged is None else merged + term
    mix = _mm(merged.astype(BF16), wout_ref[...])
    y_ref[...] = x + _rmsnorm(mix, gpost_ref[...])


def _merge(x, a_parts, ob, slab_c, od, gpre, wgl, wa, wb, wc, wd, wout, gpost, tm):
    m = x.shape[0]
    row = lambda i: (i, 0)
    r256 = pl.BlockSpec((tm, 256), row)
    n_groups = max(1, len(a_parts) // 2)
    return pl.pallas_call(
        functools.partial(_merge_body, n_groups=n_groups),
        grid=(m // tm,),
        in_specs=[pl.BlockSpec((tm, D_MODEL), row)] + [r256] * len(a_parts)
                 + [pl.BlockSpec((tm, B_QW), row), r256, r256,
                    _resident((1, D_MODEL)), _resident((D_MODEL, N_BRANCH * D_MODEL)),
                    _resident((A_W, D_MODEL)), _resident((B_QW, D_MODEL)), _resident((C_WIDTH, D_MODEL)),
                    _resident((D_WIDTH, D_MODEL)), _resident((D_MODEL, D_MODEL)), _resident((1, D_MODEL))],
        out_specs=pl.BlockSpec((tm, D_MODEL), row),
        out_shape=jax.ShapeDtypeStruct((m, D_MODEL), F32),
        compiler_params=_params(1),
        name="merge",
    )(x, *a_parts, ob, slab_c, od, gpre, wgl, wa, wb, wc, wd, wout, gpost)


def _ffn_body(x_ref, p_ref, gpre_ref, wg_ref, wu_ref, wdn_ref, gpost_ref, wple_ref, wpg_ref, y_ref):
    x = x_ref[...]
    h = _rmsnorm(x, gpre_ref[...]).astype(BF16)
    act = jax.nn.silu(_mm(h, wg_ref[...])) * _mm(h, wu_ref[...])
    f = _mm(act.astype(BF16), wdn_ref[...])
    x = x + _rmsnorm(f, gpost_ref[...])
    gate = jax.nn.sigmoid(_mm(x.astype(BF16), wpg_ref[...]))
    y_ref[...] = x + gate * _mm(p_ref[...].astype(BF16), wple_ref[...])


def _ffn(x, p_all, layer, gpre, wg, wu, wdn, gpost, wple, wpg, tm):
    m = x.shape[0]
    row = lambda i: (i, 0)
    return pl.pallas_call(
        _ffn_body,
        grid=(m // tm,),
        in_specs=[pl.BlockSpec((tm, D_MODEL), row), pl.BlockSpec((None, tm, PLE_DIM), lambda i: (layer, i, 0)),
                  _resident((1, D_MODEL)), _resident((D_MODEL, D_FF)), _resident((D_MODEL, D_FF)),
                  _resident((D_FF, D_MODEL)), _resident((1, D_MODEL)), _resident((PLE_DIM, D_MODEL)),
                  _resident((D_MODEL, D_MODEL))],
        out_specs=pl.BlockSpec((tm, D_MODEL), row),
        out_shape=jax.ShapeDtypeStruct((m, D_MODEL), F32),
        compiler_params=_params(1),
        name="ffn_ple",
    )(x, p_all, gpre, wg, wu, wdn, gpost, wple, wpg)


def _rope_tables(pos):
    inv = ROPE_THETA ** (-jnp.arange(HALF, dtype=F32) / HALF)
    ang = pos.astype(F32)[:, None] * inv[None, :]
    cos, sin = jnp.cos(ang), jnp.sin(ang)
    cos_t = jnp.tile(cos, (1, LANES // HALF))
    sin_t = jnp.tile(jnp.concatenate([-sin, sin], axis=1), (1, LANES // HEAD_DIM))
    return cos_t, sin_t


def _ssm_weights(lam_re, lam_im, log_dt, b_re, b_im, c_re, c_im):
    lam = lax.complex(lam_re, lam_im)
    dt = jnp.exp(log_dt)[:, None]
    a_bar = jnp.exp(lam * dt)
    b_bar = ((a_bar - 1.0) / lam)[..., None] * lax.complex(b_re, b_im)
    eye = jnp.eye(D_NGROUPS, dtype=F32)
    pack_b = lambda b: jnp.einsum('gni,gh->gihn', b, eye).reshape(D_WIDTH, D_NS)
    pack_c = lambda c: jnp.einsum('gin,gh->gnhi', c, eye).reshape(D_NS, D_WIDTH)
    bbig = jnp.concatenate([pack_b(jnp.real(b_bar)), pack_b(jnp.imag(b_bar))], axis=1)
    cbig = jnp.concatenate([pack_c(c_re), -pack_c(c_im)], axis=0)
    b_hi = bbig.astype(BF16)
    b_lo = (bbig - b_hi.astype(F32)).astype(BF16)
    tile8 = lambda a: jnp.broadcast_to(a.reshape(1, D_NS), (SUBLANES, D_NS))
    return tile8(jnp.real(a_bar)), tile8(jnp.imag(a_bar)), b_hi, b_lo, cbig.astype(BF16)


def _key_major(c):
    return jnp.transpose(c, (0, 1, 3, 4, 2))


def _row_major(c):
    return jnp.transpose(c, (0, 1, 4, 2, 3))


def kernel(x_prompt, x_sample, p_prompt, p_sample, cache_a1_k, cache_a1_v, cache_a2_k, cache_a2_v, cache_a3_k, cache_a3_v, cache_b_k, cache_b_v, state_c_conv, state_d_re, state_d_im, norm_mix_pre, norm_mix_post, norm_ffn_pre, norm_ffn_post, w_in, attn_sinks, conv_c_w, ssm_lam_re, ssm_lam_im, ssm_log_dt, ssm_b_re, ssm_b_im, ssm_c_re, ssm_c_im, ssm_d, w_d_glu, w_br_a, w_br_b, w_br_c, w_br_d, w_out, w_ffn_gate, w_ffn_up, w_ffn_down, w_ple, w_ple_gate):
    assert x_prompt.shape == (BATCH, SEQ, D_MODEL) and x_sample.shape == (DEC_BATCH, DEC_SEQ, D_MODEL)
    assert w_in.shape == (DEPTH, D_MODEL, 2 * MIX_W)
    assert all(min(w, PAST_LEN) == w for w, _ in A_GROUPS) and min(B_WINDOW, PAST_LEN) == B_WINDOW
    mp, ms = BATCH * SEQ, DEC_BATCH * DEC_SEQ
    tm_p, tm_s = 512, 256

    cos_p, sin_p = _rope_tables(jnp.arange(SEQ, dtype=jnp.int32))
    cos_s, sin_s = _rope_tables(PAST_LEN + jnp.arange(DEC_SEQ, dtype=jnp.int32))
    cos_s, sin_s = jnp.tile(cos_s, (tm_s // DEC_SEQ, 1)), jnp.tile(sin_s, (tm_s // DEC_SEQ, 1))

    caches = [_key_major(c) for c in (cache_a1_k, cache_a1_v, cache_a2_k, cache_a2_v, cache_a3_k, cache_a3_v,
                                      cache_b_k, cache_b_v)]
    yp = x_prompt.reshape(mp, D_MODEL)
    ys = x_sample.reshape(ms, D_MODEL)
    new_caches = None
    st_p = [[] for _ in range(11)]
    conv_s, dre_s, dim_s = [], [], []

    for i in range(DEPTH):
        row = lambda a: a[i].reshape(1, -1)
        w_mix = w_in[i, :, :MIX_W].astype(BF16)
        w_gl = w_in[i, :, MIX_W:].astype(BF16)
        ssm_w = _ssm_weights(ssm_lam_re[i], ssm_lam_im[i], ssm_log_dt[i], ssm_b_re[i], ssm_b_im[i],
                             ssm_c_re[i], ssm_c_im[i]) + (ssm_d[i].reshape(1, D_WIDTH), w_d_glu[i].astype(BF16))
        sinks_p = jnp.broadcast_to(attn_sinks[i].reshape(B_Q_HEADS, 1), (B_Q_HEADS, LANES))
        sinks_s = jnp.broadcast_to(jnp.repeat(attn_sinks[i].reshape(B_Q_HEADS), DEC_SEQ)[:, None],
                                   (B_Q_HEADS * DEC_SEQ, LANES))
        convw = jnp.pad(conv_c_w[i], ((0, SUBLANES - C_CONV), (0, 0)))
        merge_w = (row(norm_mix_pre), w_gl, w_br_a[i].astype(BF16), w_br_b[i].astype(BF16),
                   w_br_c[i].astype(BF16), w_br_d[i].astype(BF16), w_out[i].astype(BF16), row(norm_mix_post))
        ffn_w = (row(norm_ffn_pre), w_ffn_gate[i].astype(BF16), w_ffn_up[i].astype(BF16), w_ffn_down[i].astype(BF16),
                 row(norm_ffn_post), w_ple[i].astype(BF16), w_ple_gate[i].astype(BF16))

        sa, sb, sc, tail = _inproj(yp, row(norm_mix_pre), w_mix, cos_p, sin_p, convw, tm_p, SEQ // tm_p, SEQ // tm_p)
        a_parts = []
        for g, (w, d) in enumerate(A_GROUPS):
            a_parts += list(_attn_a_prompt(sa, g, d))
        ob = _attn_b_prompt(sb, sinks_p)
        zero_state = jnp.zeros((BATCH // SUBLANES, SUBLANES, D_NS), F32)
        od, s_re, s_im = _ssm(sc, BATCH, SEQ, *ssm_w, zero_state, zero_state, 128)
        x1 = _merge(yp, a_parts, ob, sc, od, *merge_w, tm_p)
        yp = _ffn(x1, p_prompt.reshape(DEPTH, mp, PLE_DIM), i, *ffn_w, tm_p)
        sa3 = sa.reshape(BATCH, SEQ, SLAB_A)
        for g, (w, d) in enumerate(A_GROUPS):
            for j in range(2):
                c0 = (1 + j) * A_QKV + g * A_W
                st_p[2 * g + j].append(sa3[:, SEQ - w:, c0:c0 + A_W].reshape(BATCH, w, A_HEADS, HEAD_DIM))
        sb3 = sb.reshape(BATCH, SEQ, SLAB_B)
        for j in range(2):
            c0 = B_QW + j * B_KW
            st_p[6 + j].append(sb3[:, SEQ - B_WINDOW:, c0:c0 + B_KW].reshape(BATCH, B_WINDOW, B_KV_HEADS, HEAD_DIM))
        st_p[8].append(tail.reshape(BATCH, SEQ // tm_p, SUBLANES, C_WIDTH)[:, -1, SUBLANES - (C_CONV - 1):])
        st_p[9].append(s_re.reshape(BATCH, D_NGROUPS, D_STATE))
        st_p[10].append(s_im.reshape(BATCH, D_NGROUPS, D_STATE))

        s1 = jnp.broadcast_to(state_c_conv[i][:, None, 1, :], (DEC_BATCH, DEC_SEQ, C_WIDTH)).reshape(ms, C_WIDTH)
        s2 = jnp.broadcast_to(state_c_conv[i][:, None, 0, :], (DEC_BATCH, DEC_SEQ, C_WIDTH)).reshape(ms, C_WIDTH)
        sa, sb, sc, tail = _inproj(ys, row(norm_mix_pre), w_mix, cos_s, sin_s, convw, tm_s, 1, 1, row_state=(s1, s2))
        o_a, ob, new_caches = _sample_attn(i, sa, sb, sinks_s, caches, new_caches)
        x0_re = state_d_re[i].reshape(DEC_BATCH // SUBLANES, SUBLANES, D_NS)
        x0_im = state_d_im[i].reshape(DEC_BATCH // SUBLANES, SUBLANES, D_NS)
        od, s_re, s_im = _ssm(sc, DEC_BATCH, DEC_SEQ, *ssm_w, x0_re, x0_im, DEC_SEQ)
        x1 = _merge(ys, [o_a], ob, sc, od, *merge_w, tm_s)
        ys = _ffn(x1, p_sample.reshape(DEPTH, ms, PLE_DIM), i, *ffn_w, tm_s)
        conv_s.append(tail.reshape(DEC_BATCH, DEC_SEQ, C_WIDTH)[:, DEC_SEQ - (C_CONV - 1):])
        dre_s.append(s_re.reshape(DEC_BATCH, D_NGROUPS, D_STATE))
        dim_s.append(s_im.reshape(DEC_BATCH, D_NGROUPS, D_STATE))

    p_states = [jnp.stack(s) for s in st_p]
    s_states = [_row_major(c) for c in new_caches] + [jnp.stack(conv_s), jnp.stack(dre_s), jnp.stack(dim_s)]
    out = [yp.reshape(BATCH, SEQ, D_MODEL), ys.reshape(DEC_BATCH, DEC_SEQ, D_MODEL)]
    for a, b in zip(p_states, s_states):
        out += [a, b]
    return tuple(out)
```

```python
import functools

import jax
import jax.numpy as jnp
from jax import lax
from jax.experimental import pallas as pl
from jax.experimental.pallas import tpu as pltpu

F32 = jnp.float32
BF16 = jnp.bfloat16

D_MODEL = 1024
BATCH = 8
SEQ = 2048
DEPTH = 2
DEC_BATCH = 128
DEC_SEQ = 8
PAST_LEN = 16384
HEAD_DIM = 64
HALF = HEAD_DIM // 2
ROPE_THETA = 10000.0
BLOCK = 128
EPS = 1e-6
NEG_INF = -1e30
A_HEADS = 4
A_GROUPS = ((128, 1), (512, 4), (2048, 16))
A_NG = len(A_GROUPS)
A_W = A_HEADS * HEAD_DIM
A_QKV = A_NG * A_W
B_Q_HEADS = 8
B_KV_HEADS = 2
B_GROUP = B_Q_HEADS // B_KV_HEADS
B_WINDOW = 128
B_QW = B_Q_HEADS * HEAD_DIM
B_KW = B_KV_HEADS * HEAD_DIM
C_WIDTH = 256
C_CONV = 3
D_GROUP = 16
D_NGROUPS = 16
D_WIDTH = 256
D_STATE = 64
D_NS = D_NGROUPS * D_STATE
N_BRANCH = 4
D_FF = 2816
PLE_DIM = 256
MIX_W = 4096
SCALE = HEAD_DIM ** -0.5

LANES = 128
SUBLANES = 8
VMEM_LIMIT = 56 * 1024 * 1024

SLAB_A = 3 * A_QKV
SLAB_B = B_QW + 2 * B_KW
SLAB_C = C_WIDTH + D_WIDTH
ROPE_A = 2 * A_QKV
ROPE_B = B_QW + B_KW


def _params(n_axes):
    return pltpu.CompilerParams(dimension_semantics=("arbitrary",) * n_axes, vmem_limit_bytes=VMEM_LIMIT)


def _resident(shape):
    nd = len(shape)
    return pl.BlockSpec(shape, lambda *_: (0,) * nd, pipeline_mode=pl.Buffered(1))


def _rmsnorm(x, g):
    return x * lax.rsqrt(jnp.mean(x * x, axis=-1, keepdims=True) + EPS) * g


def _mm(a, b):
    return jnp.dot(a, b, preferred_element_type=F32)


def _mm_nt(a, b):
    return lax.dot_general(a, b, (((1,), (1,)), ((), ())), preferred_element_type=F32)


def _low_lanes(rows):
    return lax.broadcasted_iota(jnp.int32, (rows, LANES), 1) < HEAD_DIM


def _inproj_body(*refs, seq_tiles, per_row_state):
    if per_row_state:
        x_ref, g_ref, w_ref, cos_ref, sin_ref, cw_ref, s1_ref, s2_ref, a_ref, b_ref, c_ref, tail_ref = refs
    else:
        x_ref, g_ref, w_ref, cos_ref, sin_ref, cw_ref, a_ref, b_ref, c_ref, tail_ref, carry_ref = refs
    tm = x_ref.shape[0]
    h = _rmsnorm(x_ref[...], g_ref[...]).astype(BF16)
    cos = cos_ref[...]
    sin = sin_ref[...]
    lane = lax.broadcasted_iota(jnp.int32, (tm, LANES), 1)
    first_half = (lane & (HEAD_DIM - 1)) < HALF

    def rope(z):
        partner = jnp.where(first_half, pltpu.roll(z, LANES - HALF, axis=1), pltpu.roll(z, HALF, axis=1))
        return z * cos + partner * sin

    za = _mm(h, w_ref[:, 0:SLAB_A])
    for c in range(SLAB_A // LANES):
        blk = za[:, c * LANES:(c + 1) * LANES]
        a_ref[:, c * LANES:(c + 1) * LANES] = rope(blk) if c * LANES < ROPE_A else blk
    zb = _mm(h, w_ref[:, SLAB_A:SLAB_A + SLAB_B])
    for c in range(SLAB_B // LANES):
        blk = zb[:, c * LANES:(c + 1) * LANES]
        b_ref[:, c * LANES:(c + 1) * LANES] = rope(blk) if c * LANES < ROPE_B else blk
    z = _mm(h, w_ref[:, SLAB_A + SLAB_B:MIX_W])
    zc = z[:, 2 * C_WIDTH:3 * C_WIDTH] * z[:, 0:C_WIDTH]

    row = lax.broadcasted_iota(jnp.int32, (tm, C_WIDTH), 0)
    z1 = pltpu.roll(zc, 1, axis=0)
    z2 = pltpu.roll(zc, 2, axis=0)
    if per_row_state:
        t = row & (DEC_SEQ - 1)
        z1 = jnp.where(t == 0, s1_ref[...], z1)
        z2 = jnp.where(t == 0, s2_ref[...], jnp.where(t == 1, s1_ref[...], z2))
        tail_ref[...] = zc
    else:
        @pl.when(pl.program_id(0) % seq_tiles == 0)
        def _():
            carry_ref[...] = jnp.zeros(carry_ref.shape, F32)
        z1 = jnp.where(row == 0, carry_ref[SUBLANES - 1:SUBLANES, :], z1)
        z2 = jnp.where(row == 0, carry_ref[SUBLANES - 2:SUBLANES - 1, :],
                       jnp.where(row == 1, carry_ref[SUBLANES - 1:SUBLANES, :], z2))
        carry_ref[...] = zc[tm - SUBLANES:tm, :]
        tail_ref[...] = zc[tm - SUBLANES:tm, :]
    conv = cw_ref[0:1, :] * z2 + cw_ref[1:2, :] * z1 + cw_ref[2:3, :] * zc
    c_ref[:, 0:C_WIDTH] = z[:, C_WIDTH:2 * C_WIDTH] * conv
    c_ref[:, C_WIDTH:SLAB_C] = z[:, 3 * C_WIDTH:4 * C_WIDTH]


def _inproj(x, g, w_mix, cos, sin, convw, tm, table_blocks, seq_tiles, row_state=None):
    m = x.shape[0]
    per_row_state = row_state is not None
    row = lambda i: (i, 0)
    tab = lambda i: (i % table_blocks, 0)
    in_specs = [pl.BlockSpec((tm, D_MODEL), row), _resident((1, D_MODEL)), _resident((D_MODEL, MIX_W)),
                pl.BlockSpec((tm, LANES), tab), pl.BlockSpec((tm, LANES), tab), _resident((SUBLANES, C_WIDTH))]
    args = [x, g, w_mix, cos, sin, convw]
    scratch = []
    if per_row_state:
        in_specs += [pl.BlockSpec((tm, C_WIDTH), row)] * 2
        args += list(row_state)
        tail_rows, tail_total = tm, m
    else:
        scratch = [pltpu.VMEM((SUBLANES, C_WIDTH), F32)]
        tail_rows, tail_total = SUBLANES, m // tm * SUBLANES
    return pl.pallas_call(
        functools.partial(_inproj_body, seq_tiles=seq_tiles, per_row_state=per_row_state),
        grid=(m // tm,),
        in_specs=in_specs,
        out_specs=[pl.BlockSpec((tm, SLAB_A), row), pl.BlockSpec((tm, SLAB_B), row), pl.BlockSpec((tm, SLAB_C), row),
                   pl.BlockSpec((tail_rows, C_WIDTH), row)],
        out_shape=[jax.ShapeDtypeStruct((m, SLAB_A), F32), jax.ShapeDtypeStruct((m, SLAB_B), F32),
                   jax.ShapeDtypeStruct((m, SLAB_C), F32), jax.ShapeDtypeStruct((tail_total, C_WIDTH), F32)],
        scratch_shapes=scratch,
        compiler_params=_params(1),
        name="in_proj",
    )(*args)


def _band_mask_t(nk):
    kj = lax.broadcasted_iota(jnp.int32, (nk, BLOCK), 0)
    qi = lax.broadcasted_iota(jnp.int32, (nk, BLOCK), 1)
    dist = qi + (nk - BLOCK) - kj
    return (dist >= 0) & (dist <= BLOCK)


def _softmax_t(s, mask, sink=None):
    s = jnp.where(mask, s, NEG_INF)
    m = jnp.max(s, axis=0, keepdims=True)
    if sink is not None:
        m = jnp.maximum(m, sink)
    p = jnp.exp(s - m)
    den = jnp.sum(p, axis=0, keepdims=True)
    if sink is not None:
        den = den + jnp.exp(sink - m)
    return p, 1.0 / den, m + jnp.log(den)


def _attn_a_prompt_body(q_ref, k_ref, v_ref, o_ref, lse_ref, *, d):
    nb = SEQ // d // BLOCK
    first_head_rows = lax.broadcasted_iota(jnp.int32, (LANES, BLOCK), 0) < HEAD_DIM

    def rows(start, n):
        return pl.ds(start, n) if d == 1 else pl.ds(start, n, stride=d)

    def block(q0, k0, nk):
        mask = _band_mask_t(nk)
        lo = _low_lanes(nk)
        q = (q_ref[0, rows(q0, BLOCK), :] * SCALE).astype(BF16)
        k = k_ref[0, rows(k0, nk), :]
        v = v_ref[0, rows(k0, nk), :]
        k2 = jnp.concatenate([jnp.where(lo, k, 0.0), jnp.where(lo, 0.0, k)], axis=0).astype(BF16)
        st = _mm_nt(k2, q)
        p0, r0, l0 = _softmax_t(st[0:nk], mask)
        p1, r1, l1 = _softmax_t(st[nk:2 * nk], mask)
        ot = _mm(v.T.astype(BF16), jnp.concatenate([p0, p1], axis=1).astype(BF16))
        o_t = jnp.where(first_head_rows, ot[:, 0:BLOCK] * r0, ot[:, BLOCK:2 * BLOCK] * r1)
        l_t = jnp.where(first_head_rows, jnp.broadcast_to(l0, (LANES, BLOCK)), jnp.broadcast_to(l1, (LANES, BLOCK)))
        o_ref[0, rows(q0, BLOCK), :] = o_t.T
        lse_ref[0, rows(q0, BLOCK), :] = l_t.T

    for r in range(d):
        block(r, r, BLOCK)
        if nb > 1:
            def body(i, carry):
                q0 = r + i * (BLOCK * d)
                block(q0, q0 - BLOCK * d, 2 * BLOCK)
                return carry
            lax.fori_loop(1, nb, body, 0, unroll=min(nb - 1, 3))


def _attn_a_prompt(slab_a, g, d):
    view = slab_a.reshape(BATCH, SEQ, SLAB_A)
    pairs = A_W // LANES
    spec = lambda off: pl.BlockSpec((1, SEQ, LANES), lambda b, hp: (b, 0, off * pairs + hp))
    oshape = jax.ShapeDtypeStruct((BATCH, SEQ, A_W), F32)
    o, lse = pl.pallas_call(
        functools.partial(_attn_a_prompt_body, d=d),
        grid=(BATCH, pairs),
        in_specs=[spec(g), spec(A_NG + g), spec(2 * A_NG + g)],
        out_specs=[spec(0), spec(0)],
        out_shape=[oshape, oshape],
        compiler_params=_params(2),
        name=f"attn_a{g + 1}_prompt",
    )(view, view, view)
    return o.reshape(BATCH * SEQ, A_W), lse.reshape(BATCH * SEQ, A_W)


def _attn_b_prompt_body(q_ref, k_ref, v_ref, sink_ref, o_ref, *, nb):
    def block(q0, k0, nk):
        mask = _band_mask_t(nk)
        lo = _low_lanes(nk)
        k = k_ref[0, pl.ds(k0, nk), :]
        k_sw = pltpu.roll(k, HEAD_DIM, axis=1)
        vt = v_ref[0, pl.ds(k0, nk), :].T.astype(BF16)
        for kv in range(B_KV_HEADS):
            on_lo, on_hi = (k, k_sw) if kv == 0 else (k_sw, k)
            k2 = jnp.concatenate([jnp.where(lo, on_lo, 0.0), jnp.where(lo, 0.0, on_hi)], axis=0).astype(BF16)
            cols = slice(2 * kv * LANES, (2 * kv + 2) * LANES)
            q2 = q_ref[0, pl.ds(q0, BLOCK), cols] * SCALE
            q2 = jnp.concatenate([q2[:, 0:LANES], q2[:, LANES:2 * LANES]], axis=0).astype(BF16)
            st = _mm_nt(k2, q2)
            ps, rs = [], []
            for g in range(B_GROUP):
                half, chunk = g % 2, g // 2
                sink = sink_ref[kv * B_GROUP + g:kv * B_GROUP + g + 1, :]
                p, r, _ = _softmax_t(st[half * nk:(half + 1) * nk, chunk * BLOCK:(chunk + 1) * BLOCK], mask, sink)
                ps.append(p)
                rs.append(r)
            ot = _mm(vt, jnp.concatenate(ps, axis=1).astype(BF16))
            oj = ot[kv * HEAD_DIM:(kv + 1) * HEAD_DIM, :] * jnp.concatenate(rs, axis=1)
            for chunk in range(2):
                o_t = jnp.concatenate([oj[:, (2 * chunk) * BLOCK:(2 * chunk + 1) * BLOCK],
                                       oj[:, (2 * chunk + 1) * BLOCK:(2 * chunk + 2) * BLOCK]], axis=0)
                c = 2 * kv + chunk
                o_ref[0, pl.ds(q0, BLOCK), c * LANES:(c + 1) * LANES] = o_t.T

    block(0, 0, BLOCK)

    def body(i, carry):
        q0 = pl.multiple_of(i * BLOCK, BLOCK)
        block(q0, pl.multiple_of(q0 - BLOCK, BLOCK), 2 * BLOCK)
        return carry
    lax.fori_loop(1, nb, body, 0, unroll=3)


def _attn_b_prompt(slab_b, sinks):
    view = slab_b.reshape(BATCH, SEQ, SLAB_B)
    nq = B_QW // LANES
    o = pl.pallas_call(
        functools.partial(_attn_b_prompt_body, nb=SEQ // BLOCK),
        grid=(BATCH,),
        in_specs=[pl.BlockSpec((1, SEQ, B_QW), lambda b: (b, 0, 0)),
                  pl.BlockSpec((1, SEQ, B_KW), lambda b: (b, 0, nq)),
                  pl.BlockSpec((1, SEQ, B_KW), lambda b: (b, 0, nq + 1)),
                  _resident((B_Q_HEADS, LANES))],
        out_specs=pl.BlockSpec((1, SEQ, B_QW), lambda b: (b, 0, 0)),
        out_shape=jax.ShapeDtypeStruct((BATCH, SEQ, B_QW), F32),
        compiler_params=_params(1),
        name="attn_b_prompt",
    )(view, view, view, sinks)
    return o.reshape(BATCH * SEQ, B_QW)


def _sample_masks(rows, lb, window, dil):
    assert dil & (dil - 1) == 0 and DEC_SEQ & (DEC_SEQ - 1) == 0
    t_c = lax.broadcasted_iota(jnp.int32, (rows, lb), 0) & (DEC_SEQ - 1)
    dist_c = lb + t_c - lax.broadcasted_iota(jnp.int32, (rows, lb), 1)
    mask_c = (dist_c >= 0) & (dist_c <= window) & ((dist_c & (dil - 1)) == 0)
    t_n = lax.broadcasted_iota(jnp.int32, (rows, DEC_SEQ), 0) & (DEC_SEQ - 1)
    dist_n = t_n - lax.broadcasted_iota(jnp.int32, (rows, DEC_SEQ), 1)
    mask_n = (dist_n >= 0) & (dist_n <= window) & ((dist_n & (dil - 1)) == 0)
    return mask_c, mask_n


def _sample_attend(q, kt, vt, kn, vn, masks, sink=None):
    mask_c, mask_n = masks
    s_c = jnp.where(mask_c, _mm(q.astype(BF16), kt), NEG_INF)
    s_n = jnp.where(mask_n, _mm_nt(q, kn), NEG_INF)
    m = jnp.maximum(jnp.max(s_c, axis=-1, keepdims=True), jnp.max(s_n, axis=-1, keepdims=True))
    if sink is not None:
        m = jnp.maximum(m, sink)
    p_c = jnp.exp(s_c - m)
    p_n = jnp.exp(s_n - m)
    den = jnp.sum(p_c, axis=-1, keepdims=True) + jnp.sum(p_n, axis=-1, keepdims=True)
    if sink is not None:
        den = den + jnp.exp(sink - m)
    o = _mm_nt(p_c.astype(BF16), vt) + _mm(p_n, vn)
    return o * (1.0 / den), m + jnp.log(den)


def _write_shifted_cache(dst_ref, src_ref, heads, new_rows, lb):
    padded = jnp.concatenate([jnp.zeros((LANES - DEC_SEQ, LANES), F32), new_rows], axis=0)
    new_t = padded.T
    keep = lax.broadcasted_iota(jnp.int32, (LANES, LANES), 1) < LANES - DEC_SEQ
    rotated = lambda j: pltpu.roll(src_ref[0, 0, heads, :, j * LANES:(j + 1) * LANES].reshape(LANES, LANES),
                                   LANES - DEC_SEQ, axis=1)
    n_tiles = lb // LANES
    cur = rotated(0)
    for j in range(n_tiles):
        nxt = rotated(j + 1) if j + 1 < n_tiles else new_t
        dst_ref[0, 0, heads, :, j * LANES:(j + 1) * LANES] = jnp.where(keep, cur, nxt).reshape(2, HEAD_DIM, LANES)
        cur = nxt


def _sample_attn_body(*refs, aliased):
    sa_ref, sb_ref, sink_ref = refs[:3]
    cache_refs = refs[3:11]
    pos = 11 + (8 if aliased else 0)
    oa_ref, ob_ref = refs[pos:pos + 2]
    out_refs = refs[pos + 2:pos + 10]
    lo8 = _low_lanes(DEC_SEQ)

    o_g, l_g = [], []
    for g, (window, dil) in enumerate(A_GROUPS):
        kc_ref, vc_ref, ko_ref, vo_ref = cache_refs[2 * g], cache_refs[2 * g + 1], out_refs[2 * g], out_refs[2 * g + 1]
        lb = kc_ref.shape[-1]
        masks = _sample_masks(2 * DEC_SEQ, lb, window, dil)
        o_pairs, l_pairs = [], []
        for c in range(A_W // LANES):
            col = lambda part: slice(part * A_QKV + g * A_W + c * LANES, part * A_QKV + g * A_W + (c + 1) * LANES)
            heads = slice(2 * c, 2 * c + 2)
            q = sa_ref[:, col(0)] * SCALE
            kn, vn = sa_ref[:, col(1)], sa_ref[:, col(2)]
            kt = kc_ref[0, 0, heads].reshape(LANES, lb)
            vt = vc_ref[0, 0, heads].reshape(LANES, lb)
            q2 = jnp.concatenate([jnp.where(lo8, q, 0.0), jnp.where(lo8, 0.0, q)], axis=0)
            o, lse = _sample_attend(q2, kt.astype(BF16), vt.astype(BF16), kn, vn, masks)
            o_pairs.append(jnp.where(lo8, o[0:DEC_SEQ], o[DEC_SEQ:]))
            l_pairs.append(jnp.where(lo8, lse[0:DEC_SEQ], lse[DEC_SEQ:]))
            _write_shifted_cache(ko_ref, kc_ref, heads, kn, lb)
            _write_shifted_cache(vo_ref, vc_ref, heads, vn, lb)
        o_g.append(o_pairs)
        l_g.append(l_pairs)
    for c in range(A_W // LANES):
        l1, l2, l3 = l_g[0][c], l_g[1][c], l_g[2][c]
        lm = jnp.maximum(jnp.maximum(l1, l2), l3)
        e1, e2, e3 = jnp.exp(l1 - lm), jnp.exp(l2 - lm), jnp.exp(l3 - lm)
        es = e1 + e2 + e3
        oa_ref[:, c * LANES:(c + 1) * LANES] = (e1 / es) * o_g[0][c] + (e2 / es) * o_g[1][c] + (e3 / es) * o_g[2][c]

    kc_ref, vc_ref, ko_ref, vo_ref = cache_refs[6], cache_refs[7], out_refs[6], out_refs[7]
    lb = kc_ref.shape[-1]
    rows = B_GROUP * DEC_SEQ
    masks = _sample_masks(rows, lb, B_WINDOW, 1)
    both = slice(0, B_KV_HEADS)
    ktb = kc_ref[0, 0].reshape(LANES, lb).astype(BF16)
    vtb = vc_ref[0, 0].reshape(LANES, lb).astype(BF16)
    kn, vn = sb_ref[:, B_QW:B_QW + B_KW], sb_ref[:, B_QW + B_KW:SLAB_B]
    halves = {}
    for kv in range(B_KV_HEADS):
        parts = []
        for gq in range(B_GROUP):
            head = kv * B_GROUP + gq
            q = sb_ref[:, (head // 2) * LANES:(head // 2 + 1) * LANES] * SCALE
            q = jnp.where(lo8, q, 0.0) if head % 2 == 0 else jnp.where(lo8, 0.0, q)
            parts.append(q if head % 2 == kv else pltpu.roll(q, HEAD_DIM, axis=1))
        q4 = jnp.concatenate(parts, axis=0)
        sink = sink_ref[kv * rows:(kv + 1) * rows, 0:1]
        o, _ = _sample_attend(q4, ktb, vtb, kn, vn, masks, sink)
        for gq in range(B_GROUP):
            head = kv * B_GROUP + gq
            oh = o[gq * DEC_SEQ:(gq + 1) * DEC_SEQ]
            halves[head] = oh if head % 2 == kv else pltpu.roll(oh, HEAD_DIM, axis=1)
    for c in range(B_QW // LANES):
        ob_ref[:, c * LANES:(c + 1) * LANES] = jnp.where(lo8, halves[2 * c], halves[2 * c + 1])
    _write_shifted_cache(ko_ref, kc_ref, both, kn, lb)
    _write_shifted_cache(vo_ref, vc_ref, both, vn, lb)


N_FFN_IN = 9


def _ffn_sample_body(*refs, aliased):
    n_s = 11 + (8 if aliased else 0)
    _ffn_body(*refs[:N_FFN_IN], refs[N_FFN_IN + n_s])
    _sample_attn_body(*refs[N_FFN_IN:N_FFN_IN + n_s], *refs[N_FFN_IN + n_s + 1:], aliased=aliased)


def _ffn_with_sample_attn(x, p_all, layer, gpre, wg, wu, wdn, gpost, wple, wpg, slab_a, slab_b, sinks, caches, prev):
    m = x.shape[0]
    tm = m // DEC_BATCH
    assert tm * DEC_BATCH == m and tm % SUBLANES == 0
    aliased = prev is not None
    row = lambda b: (b, 0)
    cspec = lambda c: pl.BlockSpec((1, 1) + c.shape[2:], lambda b: (layer, b, 0, 0, 0))
    in_specs = [pl.BlockSpec((tm, D_MODEL), row), pl.BlockSpec((None, tm, PLE_DIM), lambda i: (layer, i, 0)),
                _resident((1, D_MODEL)), _resident((D_MODEL, D_FF)), _resident((D_MODEL, D_FF)),
                _resident((D_FF, D_MODEL)), _resident((1, D_MODEL)), _resident((PLE_DIM, D_MODEL)),
                _resident((D_MODEL, D_MODEL))]
    args = [x, p_all, gpre, wg, wu, wdn, gpost, wple, wpg]
    assert len(args) == N_FFN_IN
    in_specs += [pl.BlockSpec((DEC_SEQ, SLAB_A), row), pl.BlockSpec((DEC_SEQ, SLAB_B), row), _resident(sinks.shape)]
    in_specs += [cspec(c) for c in caches]
    args += [slab_a, slab_b, sinks] + list(caches)
    aliases = {}
    if aliased:
        aliases = {len(args) + j: 3 + j for j in range(8)}
        in_specs += [pl.BlockSpec(memory_space=pl.ANY)] * 8
        args += list(prev)
    ms = slab_a.shape[0]
    res = pl.pallas_call(
        functools.partial(_ffn_sample_body, aliased=aliased),
        grid=(DEC_BATCH,),
        in_specs=in_specs,
        out_specs=[pl.BlockSpec((tm, D_MODEL), row), pl.BlockSpec((DEC_SEQ, A_W), row),
                   pl.BlockSpec((DEC_SEQ, B_QW), row)] + [cspec(c) for c in caches],
        out_shape=[jax.ShapeDtypeStruct((m, D_MODEL), F32), jax.ShapeDtypeStruct((ms, A_W), F32),
                   jax.ShapeDtypeStruct((ms, B_QW), F32)] + [jax.ShapeDtypeStruct(c.shape, F32) for c in caches],
        input_output_aliases=aliases,
        compiler_params=_params(1),
        name="ffn_ple_sample_attn",
    )(*args)
    return res[0], res[1], res[2], list(res[3:])


def _ssm_body(u_ref, are_ref, aim_ref, bhi_ref, blo_ref, cbig_ref, dvec_ref, wglu_ref, x0re_ref, x0im_ref,
              od_ref, sre_ref, sim_ref, bu_ref, xs_ref, *, steps, pitch):
    @pl.when(pl.program_id(1) == 0)
    def _():
        sre_ref[0] = x0re_ref[0]
        sim_ref[0] = x0im_ref[0]

    n_re = D_NS // LANES
    tiles = lambda x: [x[:, j * LANES:(j + 1) * LANES] for j in range(x.shape[1] // LANES)]

    if pitch == steps:
        planes = [(slice(None), slice(0, SUBLANES * steps))]
    else:
        planes = [(s, slice(s * pitch, s * pitch + steps)) for s in range(SUBLANES)]
    load_u = lambda sel: u_ref[sel].reshape(-1, D_WIDTH)

    for sel, plane_rows in planes:
        u = load_u(sel)
        u_hi = u.astype(BF16)
        u_lo = (u - u_hi.astype(F32)).astype(BF16)
        bu = _mm(u_hi, bhi_ref[...]) + _mm(u_lo, bhi_ref[...]) + _mm(u_hi, blo_ref[...])
        for j, tile in enumerate(tiles(bu)):
            bu_ref[j, plane_rows, :] = tile
    a_re = are_ref[...]
    a_im = aim_ref[...]

    def step(t, carry):
        re, im = carry
        at_t = pl.ds(t, SUBLANES, stride=pitch)
        b_re = jnp.concatenate([bu_ref[j, at_t, :] for j in range(n_re)], axis=1)
        b_im = jnp.concatenate([bu_ref[n_re + j, at_t, :] for j in range(n_re)], axis=1)
        re, im = a_re * re - a_im * im + b_re, a_re * im + a_im * re + b_im
        for j, (tr, ti) in enumerate(zip(tiles(re), tiles(im))):
            xs_ref[j, at_t, :] = tr
            xs_ref[n_re + j, at_t, :] = ti
        return re, im

    re, im = lax.fori_loop(0, steps, step, (sre_ref[0], sim_ref[0]), unroll=min(steps, SUBLANES))
    sre_ref[0] = re
    sim_ref[0] = im
    for sel, plane_rows in planes:
        xs = jnp.concatenate([xs_ref[j, plane_rows, :] for j in range(2 * n_re)], axis=1)
        y = _mm(xs.astype(BF16), cbig_ref[...]) + dvec_ref[...] * load_u(sel)
        y = _mm(jax.nn.gelu(y).astype(BF16), wglu_ref[...])
        od = y[:, 0:D_WIDTH] * jax.nn.sigmoid(y[:, D_WIDTH:2 * D_WIDTH])
        od_ref[sel] = od.reshape((SUBLANES, steps, D_WIDTH) if pitch == steps else (steps, D_WIDTH))


def _ssm(slab_c, bsz, length, a_re, a_im, b_hi, b_lo, cbig, dvec, wglu, x0_re, x0_im, steps):
    n = bsz // SUBLANES
    pitch = steps if steps <= SUBLANES else steps + SUBLANES
    view = slab_c.reshape(bsz, length, SLAB_C)
    u_spec = pl.BlockSpec((SUBLANES, steps, D_WIDTH), lambda i, c: (i, c, SLAB_C // D_WIDTH - 1))
    o_spec = pl.BlockSpec((SUBLANES, steps, D_WIDTH), lambda i, c: (i, c, 0))
    st_spec = pl.BlockSpec((1, SUBLANES, D_NS), lambda i, c: (i, 0, 0))
    st_shape = jax.ShapeDtypeStruct((n, SUBLANES, D_NS), F32)
    od, s_re, s_im = pl.pallas_call(
        functools.partial(_ssm_body, steps=steps, pitch=pitch),
        grid=(n, length // steps),
        in_specs=[u_spec, _resident((SUBLANES, D_NS)), _resident((SUBLANES, D_NS)), _resident((D_WIDTH, 2 * D_NS)),
                  _resident((D_WIDTH, 2 * D_NS)), _resident((2 * D_NS, D_WIDTH)), _resident((1, D_WIDTH)),
                  _resident((D_WIDTH, 2 * D_WIDTH)), st_spec, st_spec],
        out_specs=[o_spec, st_spec, st_spec],
        out_shape=[jax.ShapeDtypeStruct((bsz, length, D_WIDTH), F32), st_shape, st_shape],
        scratch_shapes=[pltpu.VMEM((2 * D_NS // LANES, SUBLANES * pitch, LANES), F32)] * 2,
        compiler_params=_params(2),
        name="ssm",
    )(view, a_re, a_im, b_hi, b_lo, cbig, dvec, wglu, x0_re, x0_im)
    return od.reshape(bsz * length, D_WIDTH), s_re, s_im


def _merge_body(*refs, n_groups):
    x_ref = refs[0]
    a_refs = refs[1:1 + (2 * n_groups if n_groups > 1 else 1)]
    ob_ref, oc_ref, od_ref, gpre_ref, wgl_ref, wa_ref, wb_ref, wc_ref, wd_ref, wout_ref, gpost_ref, y_ref = \
        refs[1 + len(a_refs):]
    x = x_ref[...]
    h = _rmsnorm(x, gpre_ref[...]).astype(BF16)
    if n_groups > 1:
        lses = [a_refs[2 * g + 1][...] for g in range(n_groups)]
        lm = functools.reduce(jnp.maximum, lses)
        es = [jnp.exp(l - lm) for l in lses]
        den = functools.reduce(lambda a, b: a + b, es)
        o_a = functools.reduce(lambda a, b: a + b, [(es[g] / den) * a_refs[2 * g][...] for g in range(n_groups)])
    else:
        o_a = a_refs[0][...]
    branches = ((o_a, wa_ref), (ob_ref[...], wb_ref), (oc_ref[...], wc_ref), (od_ref[...], wd_ref))
    merged = None
    for j, (o, w_ref) in enumerate(branches):
        gate = jax.nn.sigmoid(_mm(h, wgl_ref[:, j * D_MODEL:(j + 1) * D_MODEL]))
        term = gate * _mm(o.astype(BF16), w_ref[...])
        merged = term if merged is None else merged + term
    mix = _mm(merged.astype(BF16), wout_ref[...])
    y_ref[...] = x + _rmsnorm(mix, gpost_ref[...])


def _merge(x, a_parts, ob, slab_c, od, gpre, wgl, wa, wb, wc, wd, wout, gpost, tm):
    m = x.shape[0]
    row = lambda i: (i, 0)
    r256 = pl.BlockSpec((tm, 256), row)
    n_groups = max(1, len(a_parts) // 2)
    return pl.pallas_call(
        functools.partial(_merge_body, n_groups=n_groups),
        grid=(m // tm,),
        in_specs=[pl.BlockSpec((tm, D_MODEL), row)] + [r256] * len(a_parts)
                 + [pl.BlockSpec((tm, B_QW), row), r256, r256,
                    _resident((1, D_MODEL)), _resident((D_MODEL, N_BRANCH * D_MODEL)),
                    _resident((A_W, D_MODEL)), _resident((B_QW, D_MODEL)), _resident((C_WIDTH, D_MODEL)),
                    _resident((D_WIDTH, D_MODEL)), _resident((D_MODEL, D_MODEL)), _resident((1, D_MODEL))],
        out_specs=pl.BlockSpec((tm, D_MODEL), row),
        out_shape=jax.ShapeDtypeStruct((m, D_MODEL), F32),
        compiler_params=_params(1),
        name="merge",
    )(x, *a_parts, ob, slab_c, od, gpre, wgl, wa, wb, wc, wd, wout, gpost)


def _ffn_body(x_ref, p_ref, gpre_ref, wg_ref, wu_ref, wdn_ref, gpost_ref, wple_ref, wpg_ref, y_ref):
    x = x_ref[...]
    h = _rmsnorm(x, gpre_ref[...]).astype(BF16)
    act = jax.nn.silu(_mm(h, wg_ref[...])) * _mm(h, wu_ref[...])
    f = _mm(act.astype(BF16), wdn_ref[...])
    x = x + _rmsnorm(f, gpost_ref[...])
    gate = jax.nn.sigmoid(_mm(x.astype(BF16), wpg_ref[...]))
    y_ref[...] = x + gate * _mm(p_ref[...].astype(BF16), wple_ref[...])


def _ffn(x, p_all, layer, gpre, wg, wu, wdn, gpost, wple, wpg, tm):
    m = x.shape[0]
    row = lambda i: (i, 0)
    return pl.pallas_call(
        _ffn_body,
        grid=(m // tm,),
        in_specs=[pl.BlockSpec((tm, D_MODEL), row), pl.BlockSpec((None, tm, PLE_DIM), lambda i: (layer, i, 0)),
                  _resident((1, D_MODEL)), _resident((D_MODEL, D_FF)), _resident((D_MODEL, D_FF)),
                  _resident((D_FF, D_MODEL)), _resident((1, D_MODEL)), _resident((PLE_DIM, D_MODEL)),
                  _resident((D_MODEL, D_MODEL))],
        out_specs=pl.BlockSpec((tm, D_MODEL), row),
        out_shape=jax.ShapeDtypeStruct((m, D_MODEL), F32),
        compiler_params=_params(1),
        name="ffn_ple",
    )(x, p_all, gpre, wg, wu, wdn, gpost, wple, wpg)


def _rope_tables(pos):
    inv = ROPE_THETA ** (-jnp.arange(HALF, dtype=F32) / HALF)
    ang = pos.astype(F32)[:, None] * inv[None, :]
    cos, sin = jnp.cos(ang), jnp.sin(ang)
    cos_t = jnp.tile(cos, (1, LANES // HALF))
    sin_t = jnp.tile(jnp.concatenate([-sin, sin], axis=1), (1, LANES // HEAD_DIM))
    return cos_t, sin_t


def _ssm_weights(lam_re, lam_im, log_dt, b_re, b_im, c_re, c_im):
    lam = lax.complex(lam_re, lam_im)
    dt = jnp.exp(log_dt)[:, None]
    a_bar = jnp.exp(lam * dt)
    b_bar = ((a_bar - 1.0) / lam)[..., None] * lax.complex(b_re, b_im)
    eye = jnp.eye(D_NGROUPS, dtype=F32)
    pack_b = lambda b: jnp.einsum('gni,gh->gihn', b, eye).reshape(D_WIDTH, D_NS)
    pack_c = lambda c: jnp.einsum('gin,gh->gnhi', c, eye).reshape(D_NS, D_WIDTH)
    bbig = jnp.concatenate([pack_b(jnp.real(b_bar)), pack_b(jnp.imag(b_bar))], axis=1)
    cbig = jnp.concatenate([pack_c(c_re), -pack_c(c_im)], axis=0)
    b_hi = bbig.astype(BF16)
    b_lo = (bbig - b_hi.astype(F32)).astype(BF16)
    tile8 = lambda a: jnp.broadcast_to(a.reshape(1, D_NS), (SUBLANES, D_NS))
    return tile8(jnp.real(a_bar)), tile8(jnp.imag(a_bar)), b_hi, b_lo, cbig.astype(BF16)


def _key_major(c):
    return jnp.transpose(c, (0, 1, 3, 4, 2))


def _row_major(c):
    return jnp.transpose(c, (0, 1, 4, 2, 3))


def kernel(x_prompt, x_sample, p_prompt, p_sample, cache_a1_k, cache_a1_v, cache_a2_k, cache_a2_v, cache_a3_k, cache_a3_v, cache_b_k, cache_b_v, state_c_conv, state_d_re, state_d_im, norm_mix_pre, norm_mix_post, norm_ffn_pre, norm_ffn_post, w_in, attn_sinks, conv_c_w, ssm_lam_re, ssm_lam_im, ssm_log_dt, ssm_b_re, ssm_b_im, ssm_c_re, ssm_c_im, ssm_d, w_d_glu, w_br_a, w_br_b, w_br_c, w_br_d, w_out, w_ffn_gate, w_ffn_up, w_ffn_down, w_ple, w_ple_gate):
    assert x_prompt.shape == (BATCH, SEQ, D_MODEL) and x_sample.shape == (DEC_BATCH, DEC_SEQ, D_MODEL)
    assert w_in.shape == (DEPTH, D_MODEL, 2 * MIX_W)
    assert all(min(w, PAST_LEN) == w for w, _ in A_GROUPS) and min(B_WINDOW, PAST_LEN) == B_WINDOW
    mp, ms = BATCH * SEQ, DEC_BATCH * DEC_SEQ
    tm_p, tm_s = 512, 256

    cos_p, sin_p = _rope_tables(jnp.arange(SEQ, dtype=jnp.int32))
    cos_s, sin_s = _rope_tables(PAST_LEN + jnp.arange(DEC_SEQ, dtype=jnp.int32))
    cos_s, sin_s = jnp.tile(cos_s, (tm_s // DEC_SEQ, 1)), jnp.tile(sin_s, (tm_s // DEC_SEQ, 1))

    caches = [_key_major(c) for c in (cache_a1_k, cache_a1_v, cache_a2_k, cache_a2_v, cache_a3_k, cache_a3_v,
                                      cache_b_k, cache_b_v)]
    yp = x_prompt.reshape(mp, D_MODEL)
    ys = x_sample.reshape(ms, D_MODEL)
    new_caches = None
    st_p = [[] for _ in range(11)]
    conv_s, dre_s, dim_s = [], [], []

    for i in range(DEPTH):
        row = lambda a: a[i].reshape(1, -1)
        w_mix = w_in[i, :, :MIX_W].astype(BF16)
        w_gl = w_in[i, :, MIX_W:].astype(BF16)
        ssm_w = _ssm_weights(ssm_lam_re[i], ssm_lam_im[i], ssm_log_dt[i], ssm_b_re[i], ssm_b_im[i],
                             ssm_c_re[i], ssm_c_im[i]) + (ssm_d[i].reshape(1, D_WIDTH), w_d_glu[i].astype(BF16))
        sinks_p = jnp.broadcast_to(attn_sinks[i].reshape(B_Q_HEADS, 1), (B_Q_HEADS, LANES))
        sinks_s = jnp.broadcast_to(jnp.repeat(attn_sinks[i].reshape(B_Q_HEADS), DEC_SEQ)[:, None],
                                   (B_Q_HEADS * DEC_SEQ, LANES))
        convw = jnp.pad(conv_c_w[i], ((0, SUBLANES - C_CONV), (0, 0)))
        merge_w = (row(norm_mix_pre), w_gl, w_br_a[i].astype(BF16), w_br_b[i].astype(BF16),
                   w_br_c[i].astype(BF16), w_br_d[i].astype(BF16), w_out[i].astype(BF16), row(norm_mix_post))
        ffn_w = (row(norm_ffn_pre), w_ffn_gate[i].astype(BF16), w_ffn_up[i].astype(BF16), w_ffn_down[i].astype(BF16),
                 row(norm_ffn_post), w_ple[i].astype(BF16), w_ple_gate[i].astype(BF16))

        sa, sb, sc, tail = _inproj(yp, row(norm_mix_pre), w_mix, cos_p, sin_p, convw, tm_p, SEQ // tm_p, SEQ // tm_p)
        a_parts = []
        for g, (w, d) in enumerate(A_GROUPS):
            a_parts += list(_attn_a_prompt(sa, g, d))
        ob = _attn_b_prompt(sb, sinks_p)
        zero_state = jnp.zeros((BATCH // SUBLANES, SUBLANES, D_NS), F32)
        od, s_re, s_im = _ssm(sc, BATCH, SEQ, *ssm_w, zero_state, zero_state, 128)
        x1_prompt = _merge(yp, a_parts, ob, sc, od, *merge_w, tm_p)
        sa3 = sa.reshape(BATCH, SEQ, SLAB_A)
        for g, (w, d) in enumerate(A_GROUPS):
            for j in range(2):
                c0 = (1 + j) * A_QKV + g * A_W
                st_p[2 * g + j].append(sa3[:, SEQ - w:, c0:c0 + A_W].reshape(BATCH, w, A_HEADS, HEAD_DIM))
        sb3 = sb.reshape(BATCH, SEQ, SLAB_B)
        for j in range(2):
            c0 = B_QW + j * B_KW
            st_p[6 + j].append(sb3[:, SEQ - B_WINDOW:, c0:c0 + B_KW].reshape(BATCH, B_WINDOW, B_KV_HEADS, HEAD_DIM))
        st_p[8].append(tail.reshape(BATCH, SEQ // tm_p, SUBLANES, C_WIDTH)[:, -1, SUBLANES - (C_CONV - 1):])
        st_p[9].append(s_re.reshape(BATCH, D_NGROUPS, D_STATE))
        st_p[10].append(s_im.reshape(BATCH, D_NGROUPS, D_STATE))

        s1 = jnp.broadcast_to(state_c_conv[i][:, None, 1, :], (DEC_BATCH, DEC_SEQ, C_WIDTH)).reshape(ms, C_WIDTH)
        s2 = jnp.broadcast_to(state_c_conv[i][:, None, 0, :], (DEC_BATCH, DEC_SEQ, C_WIDTH)).reshape(ms, C_WIDTH)
        sa, sb, sc, tail = _inproj(ys, row(norm_mix_pre), w_mix, cos_s, sin_s, convw, tm_s, 1, 1, row_state=(s1, s2))
        yp, o_a, ob, new_caches = _ffn_with_sample_attn(x1_prompt, p_prompt.reshape(DEPTH, mp, PLE_DIM), i, *ffn_w,
                                                        sa, sb, sinks_s, caches, new_caches)
        x0_re = state_d_re[i].reshape(DEC_BATCH // SUBLANES, SUBLANES, D_NS)
        x0_im = state_d_im[i].reshape(DEC_BATCH // SUBLANES, SUBLANES, D_NS)
        od, s_re, s_im = _ssm(sc, DEC_BATCH, DEC_SEQ, *ssm_w, x0_re, x0_im, DEC_SEQ)
        x1 = _merge(ys, [o_a], ob, sc, od, *merge_w, tm_s)
        ys = _ffn(x1, p_sample.reshape(DEPTH, ms, PLE_DIM), i, *ffn_w, tm_s)
        conv_s.append(tail.reshape(DEC_BATCH, DEC_SEQ, C_WIDTH)[:, DEC_SEQ - (C_CONV - 1):])
        dre_s.append(s_re.reshape(DEC_BATCH, D_NGROUPS, D_STATE))
        dim_s.append(s_im.reshape(DEC_BATCH, D_NGROUPS, D_STATE))

    p_states = [jnp.stack(s) for s in st_p]
    s_states = [_row_major(c) for c in new_caches] + [jnp.stack(conv_s), jnp.stack(dre_s), jnp.stack(dim_s)]
    out = [yp.reshape(BATCH, SEQ, D_MODEL), ys.reshape(DEC_BATCH, DEC_SEQ, D_MODEL)]
    for a, b in zip(p_states, s_states):
        out += [a, b]
    return tuple(out)
```

```python
import functools

import jax
import jax.numpy as jnp
from jax import lax
from jax.experimental import pallas as pl
from jax.experimental.pallas import tpu as pltpu

F32 = jnp.float32
BF16 = jnp.bfloat16

D_MODEL = 1024
BATCH = 8
SEQ = 2048
DEPTH = 2
DEC_BATCH = 128
DEC_SEQ = 8
PAST_LEN = 16384
HEAD_DIM = 64
HALF = HEAD_DIM // 2
ROPE_THETA = 10000.0
BLOCK = 128
EPS = 1e-6
NEG_INF = -1e30
A_HEADS = 4
A_GROUPS = ((128, 1), (512, 4), (2048, 16))
A_NG = len(A_GROUPS)
A_W = A_HEADS * HEAD_DIM
A_QKV = A_NG * A_W
B_Q_HEADS = 8
B_KV_HEADS = 2
B_GROUP = B_Q_HEADS // B_KV_HEADS
B_WINDOW = 128
B_QW = B_Q_HEADS * HEAD_DIM
B_KW = B_KV_HEADS * HEAD_DIM
C_WIDTH = 256
C_CONV = 3
D_GROUP = 16
D_NGROUPS = 16
D_WIDTH = 256
D_STATE = 64
D_NS = D_NGROUPS * D_STATE
N_BRANCH = 4
D_FF = 2816
PLE_DIM = 256
MIX_W = 4096
SCALE = HEAD_DIM ** -0.5

LANES = 128
SUBLANES = 8
VMEM_LIMIT = 56 * 1024 * 1024

SLAB_A = 3 * A_QKV
SLAB_B = B_QW + 2 * B_KW
SLAB_C = C_WIDTH + D_WIDTH
ROPE_A = 2 * A_QKV
ROPE_B = B_QW + B_KW


def _params(n_axes):
    return pltpu.CompilerParams(dimension_semantics=("arbitrary",) * n_axes, vmem_limit_bytes=VMEM_LIMIT)


def _resident(shape):
    nd = len(shape)
    return pl.BlockSpec(shape, lambda *_: (0,) * nd, pipeline_mode=pl.Buffered(1))


def _rmsnorm(x, g):
    return x * lax.rsqrt(jnp.mean(x * x, axis=-1, keepdims=True) + EPS) * g


def _mm(a, b):
    return jnp.dot(a, b, preferred_element_type=F32)


def _mm_nt(a, b):
    return lax.dot_general(a, b, (((1,), (1,)), ((), ())), preferred_element_type=F32)


def _low_lanes(rows):
    return lax.broadcasted_iota(jnp.int32, (rows, LANES), 1) < HEAD_DIM


def _inproj_body(*refs, seq_tiles, per_row_state):
    if per_row_state:
        x_ref, g_ref, w_ref, cos_ref, sin_ref, cw_ref, s1_ref, s2_ref, a_ref, b_ref, c_ref, tail_ref = refs
    else:
        x_ref, g_ref, w_ref, cos_ref, sin_ref, cw_ref, a_ref, b_ref, c_ref, tail_ref, carry_ref = refs
    tm = x_ref.shape[0]
    h = _rmsnorm(x_ref[...], g_ref[...]).astype(BF16)
    cos = cos_ref[...]
    sin = sin_ref[...]
    lane = lax.broadcasted_iota(jnp.int32, (tm, LANES), 1)
    first_half = (lane & (HEAD_DIM - 1)) < HALF

    def rope(z):
        partner = jnp.where(first_half, pltpu.roll(z, LANES - HALF, axis=1), pltpu.roll(z, HALF, axis=1))
        return z * cos + partner * sin

    za = _mm(h, w_ref[:, 0:SLAB_A])
    for c in range(SLAB_A // LANES):
        blk = za[:, c * LANES:(c + 1) * LANES]
        a_ref[:, c * LANES:(c + 1) * LANES] = rope(blk) if c * LANES < ROPE_A else blk
    zb = _mm(h, w_ref[:, SLAB_A:SLAB_A + SLAB_B])
    for c in range(SLAB_B // LANES):
        blk = zb[:, c * LANES:(c + 1) * LANES]
        b_ref[:, c * LANES:(c + 1) * LANES] = rope(blk) if c * LANES < ROPE_B else blk
    z = _mm(h, w_ref[:, SLAB_A + SLAB_B:MIX_W])
    zc = z[:, 2 * C_WIDTH:3 * C_WIDTH] * z[:, 0:C_WIDTH]

    row = lax.broadcasted_iota(jnp.int32, (tm, C_WIDTH), 0)
    z1 = pltpu.roll(zc, 1, axis=0)
    z2 = pltpu.roll(zc, 2, axis=0)
    if per_row_state:
        t = row & (DEC_SEQ - 1)
        z1 = jnp.where(t == 0, s1_ref[...], z1)
        z2 = jnp.where(t == 0, s2_ref[...], jnp.where(t == 1, s1_ref[...], z2))
        tail_ref[...] = zc
    else:
        @pl.when(pl.program_id(0) % seq_tiles == 0)
        def _():
            carry_ref[...] = jnp.zeros(carry_ref.shape, F32)
        z1 = jnp.where(row == 0, carry_ref[SUBLANES - 1:SUBLANES, :], z1)
        z2 = jnp.where(row == 0, carry_ref[SUBLANES - 2:SUBLANES - 1, :],
                       jnp.where(row == 1, carry_ref[SUBLANES - 1:SUBLANES, :], z2))
        carry_ref[...] = zc[tm - SUBLANES:tm, :]
        tail_ref[...] = zc[tm - SUBLANES:tm, :]
    conv = cw_ref[0:1, :] * z2 + cw_ref[1:2, :] * z1 + cw_ref[2:3, :] * zc
    c_ref[:, 0:C_WIDTH] = z[:, C_WIDTH:2 * C_WIDTH] * conv
    c_ref[:, C_WIDTH:SLAB_C] = z[:, 3 * C_WIDTH:4 * C_WIDTH]


def _inproj(x, g, w_mix, cos, sin, convw, tm, table_blocks, seq_tiles, row_state=None):
    m = x.shape[0]
    per_row_state = row_state is not None
    row = lambda i: (i, 0)
    tab = lambda i: (i % table_blocks, 0)
    in_specs = [pl.BlockSpec((tm, D_MODEL), row), _resident((1, D_MODEL)), _resident((D_MODEL, MIX_W)),
                pl.BlockSpec((tm, LANES), tab), pl.BlockSpec((tm, LANES), tab), _resident((SUBLANES, C_WIDTH))]
    args = [x, g, w_mix, cos, sin, convw]
    scratch = []
    if per_row_state:
        in_specs += [pl.BlockSpec((tm, C_WIDTH), row)] * 2
        args += list(row_state)
        tail_rows, tail_total = tm, m
    else:
        scratch = [pltpu.VMEM((SUBLANES, C_WIDTH), F32)]
        tail_rows, tail_total = SUBLANES, m // tm * SUBLANES
    return pl.pallas_call(
        functools.partial(_inproj_body, seq_tiles=seq_tiles, per_row_state=per_row_state),
        grid=(m // tm,),
        in_specs=in_specs,
        out_specs=[pl.BlockSpec((tm, SLAB_A), row), pl.BlockSpec((tm, SLAB_B), row), pl.BlockSpec((tm, SLAB_C), row),
                   pl.BlockSpec((tail_rows, C_WIDTH), row)],
        out_shape=[jax.ShapeDtypeStruct((m, SLAB_A), F32), jax.ShapeDtypeStruct((m, SLAB_B), F32),
                   jax.ShapeDtypeStruct((m, SLAB_C), F32), jax.ShapeDtypeStruct((tail_total, C_WIDTH), F32)],
        scratch_shapes=scratch,
        compiler_params=_params(1),
        name="in_proj",
    )(*args)


def _band_mask_t(nk):
    kj = lax.broadcasted_iota(jnp.int32, (nk, BLOCK), 0)
    qi = lax.broadcasted_iota(jnp.int32, (nk, BLOCK), 1)
    dist = qi + (nk - BLOCK) - kj
    return (dist >= 0) & (dist <= BLOCK)


def _softmax_t(s, mask, sink=None):
    s = jnp.where(mask, s, NEG_INF)
    m = jnp.max(s, axis=0, keepdims=True)
    if sink is not None:
        m = jnp.maximum(m, sink)
    p = jnp.exp(s - m)
    den = jnp.sum(p, axis=0, keepdims=True)
    if sink is not None:
        den = den + jnp.exp(sink - m)
    return p, 1.0 / den, m + jnp.log(den)


def _attn_a_prompt_body(q_ref, k_ref, v_ref, o_ref, lse_ref, *, d):
    nb = SEQ // d // BLOCK
    first_head_rows = lax.broadcasted_iota(jnp.int32, (LANES, BLOCK), 0) < HEAD_DIM

    def rows(start, n):
        return pl.ds(start, n) if d == 1 else pl.ds(start, n, stride=d)

    def block(q0, k0, nk):
        mask = _band_mask_t(nk)
        lo = _low_lanes(nk)
        q = (q_ref[0, rows(q0, BLOCK), :] * SCALE).astype(BF16)
        k = k_ref[0, rows(k0, nk), :]
        v = v_ref[0, rows(k0, nk), :]
        k2 = jnp.concatenate([jnp.where(lo, k, 0.0), jnp.where(lo, 0.0, k)], axis=0).astype(BF16)
        st = _mm_nt(k2, q)
        p0, r0, l0 = _softmax_t(st[0:nk], mask)
        p1, r1, l1 = _softmax_t(st[nk:2 * nk], mask)
        ot = _mm(v.T.astype(BF16), jnp.concatenate([p0, p1], axis=1).astype(BF16))
        o_t = jnp.where(first_head_rows, ot[:, 0:BLOCK] * r0, ot[:, BLOCK:2 * BLOCK] * r1)
        l_t = jnp.where(first_head_rows, jnp.broadcast_to(l0, (LANES, BLOCK)), jnp.broadcast_to(l1, (LANES, BLOCK)))
        o_ref[0, rows(q0, BLOCK), :] = o_t.T
        lse_ref[0, rows(q0, BLOCK), :] = l_t.T

    for r in range(d):
        block(r, r, BLOCK)
        if nb > 1:
            def body(i, carry):
                q0 = r + i * (BLOCK * d)
                block(q0, q0 - BLOCK * d, 2 * BLOCK)
                return carry
            lax.fori_loop(1, nb, body, 0, unroll=min(nb - 1, 3))


def _attn_a_prompt(slab_a, g, d):
    view = slab_a.reshape(BATCH, SEQ, SLAB_A)
    pairs = A_W // LANES
    spec = lambda off: pl.BlockSpec((1, SEQ, LANES), lambda b, hp: (b, 0, off * pairs + hp))
    oshape = jax.ShapeDtypeStruct((BATCH, SEQ, A_W), F32)
    o, lse = pl.pallas_call(
        functools.partial(_attn_a_prompt_body, d=d),
        grid=(BATCH, pairs),
        in_specs=[spec(g), spec(A_NG + g), spec(2 * A_NG + g)],
        out_specs=[spec(0), spec(0)],
        out_shape=[oshape, oshape],
        compiler_params=_params(2),
        name=f"attn_a{g + 1}_prompt",
    )(view, view, view)
    return o.reshape(BATCH * SEQ, A_W), lse.reshape(BATCH * SEQ, A_W)


def _attn_b_prompt_body(q_ref, k_ref, v_ref, sink_ref, o_ref, *, nb):
    def block(q0, k0, nk):
        mask = _band_mask_t(nk)
        lo = _low_lanes(nk)
        k = k_ref[0, pl.ds(k0, nk), :]
        k_sw = pltpu.roll(k, HEAD_DIM, axis=1)
        vt = v_ref[0, pl.ds(k0, nk), :].T.astype(BF16)
        for kv in range(B_KV_HEADS):
            on_lo, on_hi = (k, k_sw) if kv == 0 else (k_sw, k)
            k2 = jnp.concatenate([jnp.where(lo, on_lo, 0.0), jnp.where(lo, 0.0, on_hi)], axis=0).astype(BF16)
            cols = slice(2 * kv * LANES, (2 * kv + 2) * LANES)
            q2 = q_ref[0, pl.ds(q0, BLOCK), cols] * SCALE
            q2 = jnp.concatenate([q2[:, 0:LANES], q2[:, LANES:2 * LANES]], axis=0).astype(BF16)
            st = _mm_nt(k2, q2)
            ps, rs = [], []
            for g in range(B_GROUP):
                half, chunk = g % 2, g // 2
                sink = sink_ref[kv * B_GROUP + g:kv * B_GROUP + g + 1, :]
                p, r, _ = _softmax_t(st[half * nk:(half + 1) * nk, chunk * BLOCK:(chunk + 1) * BLOCK], mask, sink)
                ps.append(p)
                rs.append(r)
            ot = _mm(vt, jnp.concatenate(ps, axis=1).astype(BF16))
            oj = ot[kv * HEAD_DIM:(kv + 1) * HEAD_DIM, :] * jnp.concatenate(rs, axis=1)
            for chunk in range(2):
                o_t = jnp.concatenate([oj[:, (2 * chunk) * BLOCK:(2 * chunk + 1) * BLOCK],
                                       oj[:, (2 * chunk + 1) * BLOCK:(2 * chunk + 2) * BLOCK]], axis=0)
                c = 2 * kv + chunk
                o_ref[0, pl.ds(q0, BLOCK), c * LANES:(c + 1) * LANES] = o_t.T

    block(0, 0, BLOCK)

    def body(i, carry):
        q0 = pl.multiple_of(i * BLOCK, BLOCK)
        block(q0, pl.multiple_of(q0 - BLOCK, BLOCK), 2 * BLOCK)
        return carry
    lax.fori_loop(1, nb, body, 0, unroll=3)


def _attn_b_prompt(slab_b, sinks):
    view = slab_b.reshape(BATCH, SEQ, SLAB_B)
    nq = B_QW // LANES
    o = pl.pallas_call(
        functools.partial(_attn_b_prompt_body, nb=SEQ // BLOCK),
        grid=(BATCH,),
        in_specs=[pl.BlockSpec((1, SEQ, B_QW), lambda b: (b, 0, 0)),
                  pl.BlockSpec((1, SEQ, B_KW), lambda b: (b, 0, nq)),
                  pl.BlockSpec((1, SEQ, B_KW), lambda b: (b, 0, nq + 1)),
                  _resident((B_Q_HEADS, LANES))],
        out_specs=pl.BlockSpec((1, SEQ, B_QW), lambda b: (b, 0, 0)),
        out_shape=jax.ShapeDtypeStruct((BATCH, SEQ, B_QW), F32),
        compiler_params=_params(1),
        name="attn_b_prompt",
    )(view, view, view, sinks)
    return o.reshape(BATCH * SEQ, B_QW)


def _sample_masks(rows, lb, window, dil):
    assert dil & (dil - 1) == 0 and DEC_SEQ & (DEC_SEQ - 1) == 0
    t_c = lax.broadcasted_iota(jnp.int32, (rows, lb), 0) & (DEC_SEQ - 1)
    dist_c = lb + t_c - lax.broadcasted_iota(jnp.int32, (rows, lb), 1)
    mask_c = (dist_c >= 0) & (dist_c <= window) & ((dist_c & (dil - 1)) == 0)
    t_n = lax.broadcasted_iota(jnp.int32, (rows, DEC_SEQ), 0) & (DEC_SEQ - 1)
    dist_n = t_n - lax.broadcasted_iota(jnp.int32, (rows, DEC_SEQ), 1)
    mask_n = (dist_n >= 0) & (dist_n <= window) & ((dist_n & (dil - 1)) == 0)
    return mask_c, mask_n


def _sample_attend(q, kt, vt, kn, vn, masks, sink=None):
    mask_c, mask_n = masks
    s_c = jnp.where(mask_c, _mm(q.astype(BF16), kt), NEG_INF)
    s_n = jnp.where(mask_n, _mm_nt(q, kn), NEG_INF)
    m = jnp.maximum(jnp.max(s_c, axis=-1, keepdims=True), jnp.max(s_n, axis=-1, keepdims=True))
    if sink is not None:
        m = jnp.maximum(m, sink)
    p_c = jnp.exp(s_c - m)
    p_n = jnp.exp(s_n - m)
    den = jnp.sum(p_c, axis=-1, keepdims=True) + jnp.sum(p_n, axis=-1, keepdims=True)
    if sink is not None:
        den = den + jnp.exp(sink - m)
    o = _mm_nt(p_c.astype(BF16), vt) + _mm(p_n, vn)
    return o * (1.0 / den), m + jnp.log(den)


def _write_shifted_cache(dst_ref, src_ref, s, heads, new_rows, lb):
    padded = jnp.concatenate([jnp.zeros((LANES - DEC_SEQ, LANES), F32), new_rows], axis=0)
    new_t = padded.T
    keep = lax.broadcasted_iota(jnp.int32, (LANES, LANES), 1) < LANES - DEC_SEQ
    rotated = lambda j: pltpu.roll(src_ref[0, s, heads, :, j * LANES:(j + 1) * LANES].reshape(LANES, LANES),
                                   LANES - DEC_SEQ, axis=1)
    n_tiles = lb // LANES
    cur = rotated(0)
    for j in range(n_tiles):
        nxt = rotated(j + 1) if j + 1 < n_tiles else new_t
        dst_ref[0, s, heads, :, j * LANES:(j + 1) * LANES] = jnp.where(keep, cur, nxt).reshape(2, HEAD_DIM, LANES)
        cur = nxt


SAMPLE_SEQS_PER_STEP = 2


def _sample_attn_body(*refs, aliased, n_seq):
    for s in range(n_seq):
        _sample_attn_one(refs, aliased, s)


def _sample_attn_one(refs, aliased, s):
    sa_ref, sb_ref, sink_ref = refs[:3]
    cache_refs = refs[3:11]
    pos = 11 + (8 if aliased else 0)
    oa_ref, ob_ref = refs[pos:pos + 2]
    out_refs = refs[pos + 2:pos + 10]
    lo8 = _low_lanes(DEC_SEQ)
    seq_rows = slice(s * DEC_SEQ, (s + 1) * DEC_SEQ)

    o_g, l_g = [], []
    for g, (window, dil) in enumerate(A_GROUPS):
        kc_ref, vc_ref, ko_ref, vo_ref = cache_refs[2 * g], cache_refs[2 * g + 1], out_refs[2 * g], out_refs[2 * g + 1]
        lb = kc_ref.shape[-1]
        masks = _sample_masks(2 * DEC_SEQ, lb, window, dil)
        o_pairs, l_pairs = [], []
        for c in range(A_W // LANES):
            col = lambda part: slice(part * A_QKV + g * A_W + c * LANES, part * A_QKV + g * A_W + (c + 1) * LANES)
            heads = slice(2 * c, 2 * c + 2)
            q = sa_ref[seq_rows, col(0)] * SCALE
            kn, vn = sa_ref[seq_rows, col(1)], sa_ref[seq_rows, col(2)]
            kt = kc_ref[0, s, heads].reshape(LANES, lb)
            vt = vc_ref[0, s, heads].reshape(LANES, lb)
            q2 = jnp.concatenate([jnp.where(lo8, q, 0.0), jnp.where(lo8, 0.0, q)], axis=0)
            o, lse = _sample_attend(q2, kt.astype(BF16), vt.astype(BF16), kn, vn, masks)
            o_pairs.append(jnp.where(lo8, o[0:DEC_SEQ], o[DEC_SEQ:]))
            l_pairs.append(jnp.where(lo8, lse[0:DEC_SEQ], lse[DEC_SEQ:]))
            _write_shifted_cache(ko_ref, kc_ref, s, heads, kn, lb)
            _write_shifted_cache(vo_ref, vc_ref, s, heads, vn, lb)
        o_g.append(o_pairs)
        l_g.append(l_pairs)
    for c in range(A_W // LANES):
        l1, l2, l3 = l_g[0][c], l_g[1][c], l_g[2][c]
        lm = jnp.maximum(jnp.maximum(l1, l2), l3)
        e1, e2, e3 = jnp.exp(l1 - lm), jnp.exp(l2 - lm), jnp.exp(l3 - lm)
        es = e1 + e2 + e3
        oa_ref[seq_rows, c * LANES:(c + 1) * LANES] = ((e1 / es) * o_g[0][c] + (e2 / es) * o_g[1][c]
                                                       + (e3 / es) * o_g[2][c])

    kc_ref, vc_ref, ko_ref, vo_ref = cache_refs[6], cache_refs[7], out_refs[6], out_refs[7]
    lb = kc_ref.shape[-1]
    rows = B_GROUP * DEC_SEQ
    masks = _sample_masks(rows, lb, B_WINDOW, 1)
    both = slice(0, B_KV_HEADS)
    ktb = kc_ref[0, s].reshape(LANES, lb).astype(BF16)
    vtb = vc_ref[0, s].reshape(LANES, lb).astype(BF16)
    kn, vn = sb_ref[seq_rows, B_QW:B_QW + B_KW], sb_ref[seq_rows, B_QW + B_KW:SLAB_B]
    halves = {}
    for kv in range(B_KV_HEADS):
        parts = []
        for gq in range(B_GROUP):
            head = kv * B_GROUP + gq
            q = sb_ref[seq_rows, (head // 2) * LANES:(head // 2 + 1) * LANES] * SCALE
            q = jnp.where(lo8, q, 0.0) if head % 2 == 0 else jnp.where(lo8, 0.0, q)
            parts.append(q if head % 2 == kv else pltpu.roll(q, HEAD_DIM, axis=1))
        q4 = jnp.concatenate(parts, axis=0)
        sink = sink_ref[kv * rows:(kv + 1) * rows, 0:1]
        o, _ = _sample_attend(q4, ktb, vtb, kn, vn, masks, sink)
        for gq in range(B_GROUP):
            head = kv * B_GROUP + gq
            oh = o[gq * DEC_SEQ:(gq + 1) * DEC_SEQ]
            halves[head] = oh if head % 2 == kv else pltpu.roll(oh, HEAD_DIM, axis=1)
    for c in range(B_QW // LANES):
        ob_ref[seq_rows, c * LANES:(c + 1) * LANES] = jnp.where(lo8, halves[2 * c], halves[2 * c + 1])
    _write_shifted_cache(ko_ref, kc_ref, s, both, kn, lb)
    _write_shifted_cache(vo_ref, vc_ref, s, both, vn, lb)


def _sample_attn(layer, slab_a, slab_b, sinks, caches, prev):
    aliased = prev is not None
    n_seq = SAMPLE_SEQS_PER_STEP
    rows = n_seq * DEC_SEQ
    row = lambda b: (b, 0)
    cspec = lambda c: pl.BlockSpec((1, n_seq) + c.shape[2:], lambda b: (layer, b, 0, 0, 0))
    in_specs = [pl.BlockSpec((rows, SLAB_A), row), pl.BlockSpec((rows, SLAB_B), row), _resident(sinks.shape)]
    in_specs += [cspec(c) for c in caches]
    args = [slab_a, slab_b, sinks] + list(caches)
    aliases = {}
    if aliased:
        aliases = {len(args) + j: 2 + j for j in range(8)}
        in_specs += [pl.BlockSpec(memory_space=pl.ANY)] * 8
        args += list(prev)
    m = slab_a.shape[0]
    res = pl.pallas_call(
        functools.partial(_sample_attn_body, aliased=aliased, n_seq=n_seq),
        grid=(DEC_BATCH // n_seq,),
        in_specs=in_specs,
        out_specs=[pl.BlockSpec((rows, A_W), row), pl.BlockSpec((rows, B_QW), row)] + [cspec(c) for c in caches],
        out_shape=[jax.ShapeDtypeStruct((m, A_W), F32), jax.ShapeDtypeStruct((m, B_QW), F32)]
                  + [jax.ShapeDtypeStruct(c.shape, F32) for c in caches],
        input_output_aliases=aliases,
        compiler_params=_params(1),
        name="sample_attn",
    )(*args)
    return res[0], res[1], list(res[2:])


def _ssm_body(u_ref, are_ref, aim_ref, b_ref, cbig_ref, dvec_ref, wglu_ref, x0re_ref, x0im_ref,
              od_ref, sre_ref, sim_ref, bu_ref, xs_ref, *, steps, pitch):
    @pl.when(pl.program_id(1) == 0)
    def _():
        sre_ref[0] = x0re_ref[0]
        sim_ref[0] = x0im_ref[0]

    n_re = D_NS // LANES
    tiles = lambda x: [x[:, j * LANES:(j + 1) * LANES] for j in range(x.shape[1] // LANES)]

    if pitch == steps:
        planes = [(slice(None), slice(0, SUBLANES * steps))]
    else:
        planes = [(s, slice(s * pitch, s * pitch + steps)) for s in range(SUBLANES)]
    load_u = lambda sel: u_ref[sel].reshape(-1, D_WIDTH)

    for sel, plane_rows in planes:
        bu = _mm(load_u(sel).astype(BF16), b_ref[...])
        for j, tile in enumerate(tiles(bu)):
            bu_ref[j, plane_rows, :] = tile
    a_re = are_ref[...]
    a_im = aim_ref[...]

    def step(t, carry):
        re, im = carry
        at_t = pl.ds(t, SUBLANES, stride=pitch)
        b_re = jnp.concatenate([bu_ref[j, at_t, :] for j in range(n_re)], axis=1)
        b_im = jnp.concatenate([bu_ref[n_re + j, at_t, :] for j in range(n_re)], axis=1)
        re, im = a_re * re - a_im * im + b_re, a_re * im + a_im * re + b_im
        for j, (tr, ti) in enumerate(zip(tiles(re), tiles(im))):
            xs_ref[j, at_t, :] = tr
            xs_ref[n_re + j, at_t, :] = ti
        return re, im

    re, im = lax.fori_loop(0, steps, step, (sre_ref[0], sim_ref[0]), unroll=min(steps, SUBLANES))
    sre_ref[0] = re
    sim_ref[0] = im
    for sel, plane_rows in planes:
        xs = jnp.concatenate([xs_ref[j, plane_rows, :] for j in range(2 * n_re)], axis=1)
        y = _mm(xs.astype(BF16), cbig_ref[...]) + dvec_ref[...] * load_u(sel)
        y = _mm(jax.nn.gelu(y).astype(BF16), wglu_ref[...])
        od = y[:, 0:D_WIDTH] * jax.nn.sigmoid(y[:, D_WIDTH:2 * D_WIDTH])
        od_ref[sel] = od.reshape((SUBLANES, steps, D_WIDTH) if pitch == steps else (steps, D_WIDTH))


def _ssm(slab_c, bsz, length, a_re, a_im, bbig, cbig, dvec, wglu, x0_re, x0_im, steps):
    n = bsz // SUBLANES
    pitch = steps if steps <= SUBLANES else steps + SUBLANES
    view = slab_c.reshape(bsz, length, SLAB_C)
    u_spec = pl.BlockSpec((SUBLANES, steps, D_WIDTH), lambda i, c: (i, c, SLAB_C // D_WIDTH - 1))
    o_spec = pl.BlockSpec((SUBLANES, steps, D_WIDTH), lambda i, c: (i, c, 0))
    st_spec = pl.BlockSpec((1, SUBLANES, D_NS), lambda i, c: (i, 0, 0))
    st_shape = jax.ShapeDtypeStruct((n, SUBLANES, D_NS), F32)
    od, s_re, s_im = pl.pallas_call(
        functools.partial(_ssm_body, steps=steps, pitch=pitch),
        grid=(n, length // steps),
        in_specs=[u_spec, _resident((SUBLANES, D_NS)), _resident((SUBLANES, D_NS)), _resident((D_WIDTH, 2 * D_NS)),
                  _resident((2 * D_NS, D_WIDTH)), _resident((1, D_WIDTH)),
                  _resident((D_WIDTH, 2 * D_WIDTH)), st_spec, st_spec],
        out_specs=[o_spec, st_spec, st_spec],
        out_shape=[jax.ShapeDtypeStruct((bsz, length, D_WIDTH), F32), st_shape, st_shape],
        scratch_shapes=[pltpu.VMEM((2 * D_NS // LANES, SUBLANES * pitch, LANES), F32)] * 2,
        compiler_params=_params(2),
        name="ssm",
    )(view, a_re, a_im, bbig, cbig, dvec, wglu, x0_re, x0_im)
    return od.reshape(bsz * length, D_WIDTH), s_re, s_im


def _merge_body(*refs, n_groups):
    x_ref = refs[0]
    a_refs = refs[1:1 + (2 * n_groups if n_groups > 1 else 1)]
    ob_ref, oc_ref, od_ref, gpre_ref, wgl_ref, wa_ref, wb_ref, wc_ref, wd_ref, wout_ref, gpost_ref, y_ref = \
        refs[1 + len(a_refs):]
    x = x_ref[...]
    h = _rmsnorm(x, gpre_ref[...]).astype(BF16)
    if n_groups > 1:
        lses = [a_refs[2 * g + 1][...] for g in range(n_groups)]
        lm = functools.reduce(jnp.maximum, lses)
        es = [jnp.exp(l - lm) for l in lses]
        den = functools.reduce(lambda a, b: a + b, es)
        o_a = functools.reduce(lambda a, b: a + b, [(es[g] / den) * a_refs[2 * g][...] for g in range(n_groups)])
    else:
        o_a = a_refs[0][...]
    branches = ((o_a, wa_ref), (ob_ref[...], wb_ref), (oc_ref[...], wc_ref), (od_ref[...], wd_ref))
    merged = None
    for j, (o, w_ref) in enumerate(branches):
        gate = jax.nn.sigmoid(_mm(h, wgl_ref[:, j * D_MODEL:(j + 1) * D_MODEL]))
        term = gate * _mm(o.astype(BF16), w_ref[...])
        merged = term if merged is None else merged + term
    mix = _mm(merged.astype(BF16), wout_ref[...])
    y_ref[...] = x + _rmsnorm(mix, gpost_ref[...])


def _merge(x, a_parts, ob, slab_c, od, gpre, wgl, wa, wb, wc, wd, wout, gpost, tm):
    m = x.shape[0]
    row = lambda i: (i, 0)
    r256 = pl.BlockSpec((tm, 256), row)
    n_groups = max(1, len(a_parts) // 2)
    return pl.pallas_call(
        functools.partial(_merge_body, n_groups=n_groups),
        grid=(m // tm,),
        in_specs=[pl.BlockSpec((tm, D_MODEL), row)] + [r256] * len(a_parts)
                 + [pl.BlockSpec((tm, B_QW), row), r256, r256,
                    _resident((1, D_MODEL)), _resident((D_MODEL, N_BRANCH * D_MODEL)),
                    _resident((A_W, D_MODEL)), _resident((B_QW, D_MODEL)), _resident((C_WIDTH, D_MODEL)),
                    _resident((D_WIDTH, D_MODEL)), _resident((D_MODEL, D_MODEL)), _resident((1, D_MODEL))],
        out_specs=pl.BlockSpec((tm, D_MODEL), row),
        out_shape=jax.ShapeDtypeStruct((m, D_MODEL), F32),
        compiler_params=_params(1),
        name="merge",
    )(x, *a_parts, ob, slab_c, od, gpre, wgl, wa, wb, wc, wd, wout, gpost)


def _ffn_body(x_ref, p_ref, gpre_ref, wg_ref, wu_ref, wdn_ref, gpost_ref, wple_ref, wpg_ref, y_ref):
    x = x_ref[...]
    h = _rmsnorm(x, gpre_ref[...]).astype(BF16)
    act = jax.nn.silu(_mm(h, wg_ref[...])) * _mm(h, wu_ref[...])
    f = _mm(act.astype(BF16), wdn_ref[...])
    x = x + _rmsnorm(f, gpost_ref[...])
    gate = jax.nn.sigmoid(_mm(x.astype(BF16), wpg_ref[...]))
    y_ref[...] = x + gate * _mm(p_ref[...].astype(BF16), wple_ref[...])


def _ffn(x, p_all, layer, gpre, wg, wu, wdn, gpost, wple, wpg, tm):
    m = x.shape[0]
    row = lambda i: (i, 0)
    return pl.pallas_call(
        _ffn_body,
        grid=(m // tm,),
        in_specs=[pl.BlockSpec((tm, D_MODEL), row), pl.BlockSpec((None, tm, PLE_DIM), lambda i: (layer, i, 0)),
                  _resident((1, D_MODEL)), _resident((D_MODEL, D_FF)), _resident((D_MODEL, D_FF)),
                  _resident((D_FF, D_MODEL)), _resident((1, D_MODEL)), _resident((PLE_DIM, D_MODEL)),
                  _resident((D_MODEL, D_MODEL))],
        out_specs=pl.BlockSpec((tm, D_MODEL), row),
        out_shape=jax.ShapeDtypeStruct((m, D_MODEL), F32),
        compiler_params=_params(1),
        name="ffn_ple",
    )(x, p_all, gpre, wg, wu, wdn, gpost, wple, wpg)


def _rope_tables(pos):
    inv = ROPE_THETA ** (-jnp.arange(HALF, dtype=F32) / HALF)
    ang = pos.astype(F32)[:, None] * inv[None, :]
    cos, sin = jnp.cos(ang), jnp.sin(ang)
    cos_t = jnp.tile(cos, (1, LANES // HALF))
    sin_t = jnp.tile(jnp.concatenate([-sin, sin], axis=1), (1, LANES // HEAD_DIM))
    return cos_t, sin_t


def _ssm_weights(lam_re, lam_im, log_dt, b_re, b_im, c_re, c_im):
    lam = lax.complex(lam_re, lam_im)
    dt = jnp.exp(log_dt)[:, None]
    a_bar = jnp.exp(lam * dt)
    b_bar = ((a_bar - 1.0) / lam)[..., None] * lax.complex(b_re, b_im)
    eye = jnp.eye(D_NGROUPS, dtype=F32)
    pack_b = lambda b: jnp.einsum('gni,gh->gihn', b, eye).reshape(D_WIDTH, D_NS)
    pack_c = lambda c: jnp.einsum('gin,gh->gnhi', c, eye).reshape(D_NS, D_WIDTH)
    bbig = jnp.concatenate([pack_b(jnp.real(b_bar)), pack_b(jnp.imag(b_bar))], axis=1)
    cbig = jnp.concatenate([pack_c(c_re), -pack_c(c_im)], axis=0)
    tile8 = lambda a: jnp.broadcast_to(a.reshape(1, D_NS), (SUBLANES, D_NS))
    return tile8(jnp.real(a_bar)), tile8(jnp.imag(a_bar)), bbig.astype(BF16), cbig.astype(BF16)


def _key_major(c):
    return jnp.transpose(c, (0, 1, 3, 4, 2))


def _row_major(c):
    return jnp.transpose(c, (0, 1, 4, 2, 3))


def kernel(x_prompt, x_sample, p_prompt, p_sample, cache_a1_k, cache_a1_v, cache_a2_k, cache_a2_v, cache_a3_k, cache_a3_v, cache_b_k, cache_b_v, state_c_conv, state_d_re, state_d_im, norm_mix_pre, norm_mix_post, norm_ffn_pre, norm_ffn_post, w_in, attn_sinks, conv_c_w, ssm_lam_re, ssm_lam_im, ssm_log_dt, ssm_b_re, ssm_b_im, ssm_c_re, ssm_c_im, ssm_d, w_d_glu, w_br_a, w_br_b, w_br_c, w_br_d, w_out, w_ffn_gate, w_ffn_up, w_ffn_down, w_ple, w_ple_gate):
    assert x_prompt.shape == (BATCH, SEQ, D_MODEL) and x_sample.shape == (DEC_BATCH, DEC_SEQ, D_MODEL)
    assert w_in.shape == (DEPTH, D_MODEL, 2 * MIX_W)
    assert all(min(w, PAST_LEN) == w for w, _ in A_GROUPS) and min(B_WINDOW, PAST_LEN) == B_WINDOW
    mp, ms = BATCH * SEQ, DEC_BATCH * DEC_SEQ
    tm_p, tm_s = 512, 256

    cos_p, sin_p = _rope_tables(jnp.arange(SEQ, dtype=jnp.int32))
    cos_s, sin_s = _rope_tables(PAST_LEN + jnp.arange(DEC_SEQ, dtype=jnp.int32))
    cos_s, sin_s = jnp.tile(cos_s, (tm_s // DEC_SEQ, 1)), jnp.tile(sin_s, (tm_s // DEC_SEQ, 1))

    caches = [_key_major(c) for c in (cache_a1_k, cache_a1_v, cache_a2_k, cache_a2_v, cache_a3_k, cache_a3_v,
                                      cache_b_k, cache_b_v)]
    yp = x_prompt.reshape(mp, D_MODEL)
    ys = x_sample.reshape(ms, D_MODEL)
    new_caches = None
    st_p = [[] for _ in range(11)]
    conv_s, dre_s, dim_s = [], [], []

    for i in range(DEPTH):
        row = lambda a: a[i].reshape(1, -1)
        w_mix = w_in[i, :, :MIX_W].astype(BF16)
        w_gl = w_in[i, :, MIX_W:].astype(BF16)
        ssm_w = _ssm_weights(ssm_lam_re[i], ssm_lam_im[i], ssm_log_dt[i], ssm_b_re[i], ssm_b_im[i],
                             ssm_c_re[i], ssm_c_im[i]) + (ssm_d[i].reshape(1, D_WIDTH), w_d_glu[i].astype(BF16))
        sinks_p = jnp.broadcast_to(attn_sinks[i].reshape(B_Q_HEADS, 1), (B_Q_HEADS, LANES))
        sinks_s = jnp.broadcast_to(jnp.repeat(attn_sinks[i].reshape(B_Q_HEADS), DEC_SEQ)[:, None],
                                   (B_Q_HEADS * DEC_SEQ, LANES))
        convw = jnp.pad(conv_c_w[i], ((0, SUBLANES - C_CONV), (0, 0)))
        merge_w = (row(norm_mix_pre), w_gl, w_br_a[i].astype(BF16), w_br_b[i].astype(BF16),
                   w_br_c[i].astype(BF16), w_br_d[i].astype(BF16), w_out[i].astype(BF16), row(norm_mix_post))
        ffn_w = (row(norm_ffn_pre), w_ffn_gate[i].astype(BF16), w_ffn_up[i].astype(BF16), w_ffn_down[i].astype(BF16),
                 row(norm_ffn_post), w_ple[i].astype(BF16), w_ple_gate[i].astype(BF16))

        sa, sb, sc, tail = _inproj(yp, row(norm_mix_pre), w_mix, cos_p, sin_p, convw, tm_p, SEQ // tm_p, SEQ // tm_p)
        a_parts = []
        for g, (w, d) in enumerate(A_GROUPS):
            a_parts += list(_attn_a_prompt(sa, g, d))
        ob = _attn_b_prompt(sb, sinks_p)
        zero_state = jnp.zeros((BATCH // SUBLANES, SUBLANES, D_NS), F32)
        od, s_re, s_im = _ssm(sc, BATCH, SEQ, *ssm_w, zero_state, zero_state, 128)
        x1 = _merge(yp, a_parts, ob, sc, od, *merge_w, tm_p)
        yp = _ffn(x1, p_prompt.reshape(DEPTH, mp, PLE_DIM), i, *ffn_w, tm_p)
        sa3 = sa.reshape(BATCH, SEQ, SLAB_A)
        for g, (w, d) in enumerate(A_GROUPS):
            for j in range(2):
                c0 = (1 + j) * A_QKV + g * A_W
                st_p[2 * g + j].append(sa3[:, SEQ - w:, c0:c0 + A_W].reshape(BATCH, w, A_HEADS, HEAD_DIM))
        sb3 = sb.reshape(BATCH, SEQ, SLAB_B)
        for j in range(2):
            c0 = B_QW + j * B_KW
            st_p[6 + j].append(sb3[:, SEQ - B_WINDOW:, c0:c0 + B_KW].reshape(BATCH, B_WINDOW, B_KV_HEADS, HEAD_DIM))
        st_p[8].append(tail.reshape(BATCH, SEQ // tm_p, SUBLANES, C_WIDTH)[:, -1, SUBLANES - (C_CONV - 1):])
        st_p[9].append(s_re.reshape(BATCH, D_NGROUPS, D_STATE))
        st_p[10].append(s_im.reshape(BATCH, D_NGROUPS, D_STATE))

        s1 = jnp.broadcast_to(state_c_conv[i][:, None, 1, :], (DEC_BATCH, DEC_SEQ, C_WIDTH)).reshape(ms, C_WIDTH)
        s2 = jnp.broadcast_to(state_c_conv[i][:, None, 0, :], (DEC_BATCH, DEC_SEQ, C_WIDTH)).reshape(ms, C_WIDTH)
        sa, sb, sc, tail = _inproj(ys, row(norm_mix_pre), w_mix, cos_s, sin_s, convw, tm_s, 1, 1, row_state=(s1, s2))
        o_a, ob, new_caches = _sample_attn(i, sa, sb, sinks_s, caches, new_caches)
        x0_re = state_d_re[i].reshape(DEC_BATCH // SUBLANES, SUBLANES, D_NS)
        x0_im = state_d_im[i].reshape(DEC_BATCH // SUBLANES, SUBLANES, D_NS)
        od, s_re, s_im = _ssm(sc, DEC_BATCH, DEC_SEQ, *ssm_w, x0_re, x0_im, DEC_SEQ)
        x1 = _merge(ys, [o_a], ob, sc, od, *merge_w, tm_s)
        ys = _ffn(x1, p_sample.reshape(DEPTH, ms, PLE_DIM), i, *ffn_w, tm_s)
        conv_s.append(tail.reshape(DEC_BATCH, DEC_SEQ, C_WIDTH)[:, DEC_SEQ - (C_CONV - 1):])
        dre_s.append(s_re.reshape(DEC_BATCH, D_NGROUPS, D_STATE))
        dim_s.append(s_im.reshape(DEC_BATCH, D_NGROUPS, D_STATE))

    p_states = [jnp.stack(s) for s in st_p]
    s_states = [_row_major(c) for c in new_caches] + [jnp.stack(conv_s), jnp.stack(dre_s), jnp.stack(dim_s)]
    out = [yp.reshape(BATCH, SEQ, D_MODEL), ys.reshape(DEC_BATCH, DEC_SEQ, D_MODEL)]
    for a, b in zip(p_states, s_states):
        out += [a, b]
    return tuple(out)
```

```python
import functools

import jax
import jax.numpy as jnp
from jax import lax
from jax.experimental import pallas as pl
from jax.experimental.pallas import tpu as pltpu

F32 = jnp.float32
BF16 = jnp.bfloat16

D_MODEL = 1024
BATCH = 8
SEQ = 2048
DEPTH = 2
DEC_BATCH = 128
DEC_SEQ = 8
PAST_LEN = 16384
HEAD_DIM = 64
HALF = HEAD_DIM // 2
ROPE_THETA = 10000.0
BLOCK = 128
EPS = 1e-6
NEG_INF = -1e30
A_HEADS = 4
A_GROUPS = ((128, 1), (512, 4), (2048, 16))
A_NG = len(A_GROUPS)
A_W = A_HEADS * HEAD_DIM
A_QKV = A_NG * A_W
B_Q_HEADS = 8
B_KV_HEADS = 2
B_GROUP = B_Q_HEADS // B_KV_HEADS
B_WINDOW = 128
B_QW = B_Q_HEADS * HEAD_DIM
B_KW = B_KV_HEADS * HEAD_DIM
C_WIDTH = 256
C_CONV = 3
D_GROUP = 16
D_NGROUPS = 16
D_WIDTH = 256
D_STATE = 64
D_NS = D_NGROUPS * D_STATE
N_BRANCH = 4
D_FF = 2816
PLE_DIM = 256
MIX_W = 4096
SCALE = HEAD_DIM ** -0.5

LANES = 128
SUBLANES = 8
VMEM_LIMIT = 56 * 1024 * 1024

SLAB_A = 3 * A_QKV
SLAB_B = B_QW + 2 * B_KW
SLAB_C = C_WIDTH + D_WIDTH
ROPE_A = 2 * A_QKV
ROPE_B = B_QW + B_KW


def _params(n_axes):
    return pltpu.CompilerParams(dimension_semantics=("arbitrary",) * n_axes, vmem_limit_bytes=VMEM_LIMIT)


def _resident(shape, layer, col=0):
    return pl.BlockSpec((None,) + tuple(shape), lambda *_: (layer, 0, col), pipeline_mode=pl.Buffered(1))


def _rmsnorm(x, g):
    return x * lax.rsqrt(jnp.mean(x * x, axis=-1, keepdims=True) + EPS) * g


def _mm(a, b):
    return jnp.dot(a, b, preferred_element_type=F32)


def _mm_nt(a, b):
    return lax.dot_general(a, b, (((1,), (1,)), ((), ())), preferred_element_type=F32)


def _low_lanes(rows):
    return lax.broadcasted_iota(jnp.int32, (rows, LANES), 1) < HEAD_DIM


def _inproj_body(*refs, seq_tiles, per_row_state):
    if per_row_state:
        x_ref, g_ref, w_ref, cos_ref, sin_ref, cw_ref, s1_ref, s2_ref, a_ref, b_ref, c_ref, tail_ref = refs
    else:
        x_ref, g_ref, w_ref, cos_ref, sin_ref, cw_ref, a_ref, b_ref, c_ref, tail_ref, carry_ref = refs
    tm = x_ref.shape[0]
    h = _rmsnorm(x_ref[...], g_ref[...]).astype(BF16)
    cos = cos_ref[...]
    sin = sin_ref[...]
    lane = lax.broadcasted_iota(jnp.int32, (tm, LANES), 1)
    first_half = (lane & (HEAD_DIM - 1)) < HALF

    def rope(z):
        partner = jnp.where(first_half, pltpu.roll(z, LANES - HALF, axis=1), pltpu.roll(z, HALF, axis=1))
        return z * cos + partner * sin

    za = _mm(h, w_ref[:, 0:SLAB_A])
    for c in range(SLAB_A // LANES):
        blk = za[:, c * LANES:(c + 1) * LANES]
        a_ref[:, c * LANES:(c + 1) * LANES] = rope(blk) if c * LANES < ROPE_A else blk
    zb = _mm(h, w_ref[:, SLAB_A:SLAB_A + SLAB_B])
    for c in range(SLAB_B // LANES):
        blk = zb[:, c * LANES:(c + 1) * LANES]
        b_ref[:, c * LANES:(c + 1) * LANES] = rope(blk) if c * LANES < ROPE_B else blk
    z = _mm(h, w_ref[:, SLAB_A + SLAB_B:MIX_W])
    zc = z[:, 2 * C_WIDTH:3 * C_WIDTH] * z[:, 0:C_WIDTH]

    row = lax.broadcasted_iota(jnp.int32, (tm, C_WIDTH), 0)
    z1 = pltpu.roll(zc, 1, axis=0)
    z2 = pltpu.roll(zc, 2, axis=0)
    if per_row_state:
        t = row & (DEC_SEQ - 1)
        z1 = jnp.where(t == 0, s1_ref[...], z1)
        z2 = jnp.where(t == 0, s2_ref[...], jnp.where(t == 1, s1_ref[...], z2))
        tail_ref[...] = zc
    else:
        @pl.when(pl.program_id(0) % seq_tiles == 0)
        def _():
            carry_ref[...] = jnp.zeros(carry_ref.shape, F32)
        z1 = jnp.where(row == 0, carry_ref[SUBLANES - 1:SUBLANES, :], z1)
        z2 = jnp.where(row == 0, carry_ref[SUBLANES - 2:SUBLANES - 1, :],
                       jnp.where(row == 1, carry_ref[SUBLANES - 1:SUBLANES, :], z2))
        carry_ref[...] = zc[tm - SUBLANES:tm, :]
        tail_ref[...] = zc[tm - SUBLANES:tm, :]
    conv = cw_ref[0:1, :] * z2 + cw_ref[1:2, :] * z1 + cw_ref[2:3, :] * zc
    c_ref[:, 0:C_WIDTH] = z[:, C_WIDTH:2 * C_WIDTH] * conv
    c_ref[:, C_WIDTH:SLAB_C] = z[:, 3 * C_WIDTH:4 * C_WIDTH]


def _inproj(x, layer, g, w_in, cos, sin, convw, tm, table_blocks, seq_tiles, row_state=None):
    m = x.shape[0]
    per_row_state = row_state is not None
    row = lambda i: (i, 0)
    tab = lambda i: (i % table_blocks, 0)
    in_specs = [pl.BlockSpec((tm, D_MODEL), row), _resident((1, D_MODEL), layer), _resident((D_MODEL, MIX_W), layer),
                pl.BlockSpec((tm, LANES), tab), pl.BlockSpec((tm, LANES), tab), _resident((C_CONV, C_WIDTH), layer)]
    args = [x, g, w_in, cos, sin, convw]
    scratch = []
    if per_row_state:
        in_specs += [pl.BlockSpec((tm, C_WIDTH), row)] * 2
        args += list(row_state)
        tail_rows, tail_total = tm, m
    else:
        scratch = [pltpu.VMEM((SUBLANES, C_WIDTH), F32)]
        tail_rows, tail_total = SUBLANES, m // tm * SUBLANES
    return pl.pallas_call(
        functools.partial(_inproj_body, seq_tiles=seq_tiles, per_row_state=per_row_state),
        grid=(m // tm,),
        in_specs=in_specs,
        out_specs=[pl.BlockSpec((tm, SLAB_A), row), pl.BlockSpec((tm, SLAB_B), row), pl.BlockSpec((tm, SLAB_C), row),
                   pl.BlockSpec((tail_rows, C_WIDTH), row)],
        out_shape=[jax.ShapeDtypeStruct((m, SLAB_A), F32), jax.ShapeDtypeStruct((m, SLAB_B), F32),
                   jax.ShapeDtypeStruct((m, SLAB_C), F32), jax.ShapeDtypeStruct((tail_total, C_WIDTH), F32)],
        scratch_shapes=scratch,
        compiler_params=_params(1),
        name="in_proj",
    )(*args)


def _band_mask_t(nk):
    kj = lax.broadcasted_iota(jnp.int32, (nk, BLOCK), 0)
    qi = lax.broadcasted_iota(jnp.int32, (nk, BLOCK), 1)
    dist = qi + (nk - BLOCK) - kj
    return (dist >= 0) & (dist <= BLOCK)


def _softmax_t(s, mask, sink=None):
    s = jnp.where(mask, s, NEG_INF)
    m = jnp.max(s, axis=0, keepdims=True)
    if sink is not None:
        m = jnp.maximum(m, sink)
    p = jnp.exp(s - m)
    den = jnp.sum(p, axis=0, keepdims=True)
    if sink is not None:
        den = den + jnp.exp(sink - m)
    return p, 1.0 / den, m + jnp.log(den)


def _attn_a_prompt_body(q_ref, k_ref, v_ref, o_ref, lse_ref, *, d):
    nb = SEQ // d // BLOCK
    first_head_rows = lax.broadcasted_iota(jnp.int32, (LANES, BLOCK), 0) < HEAD_DIM

    def rows(start, n):
        return pl.ds(start, n) if d == 1 else pl.ds(start, n, stride=d)

    def block(q0, k0, nk):
        mask = _band_mask_t(nk)
        lo = _low_lanes(nk)
        q = (q_ref[0, rows(q0, BLOCK), :] * SCALE).astype(BF16)
        k = k_ref[0, rows(k0, nk), :]
        v = v_ref[0, rows(k0, nk), :]
        k2 = jnp.concatenate([jnp.where(lo, k, 0.0), jnp.where(lo, 0.0, k)], axis=0).astype(BF16)
        st = _mm_nt(k2, q)
        p0, r0, l0 = _softmax_t(st[0:nk], mask)
        p1, r1, l1 = _softmax_t(st[nk:2 * nk], mask)
        ot = _mm(v.T.astype(BF16), jnp.concatenate([p0, p1], axis=1).astype(BF16))
        o_t = jnp.where(first_head_rows, ot[:, 0:BLOCK] * r0, ot[:, BLOCK:2 * BLOCK] * r1)
        l_t = jnp.where(first_head_rows, jnp.broadcast_to(l0, (LANES, BLOCK)), jnp.broadcast_to(l1, (LANES, BLOCK)))
        o_ref[0, rows(q0, BLOCK), :] = o_t.T
        lse_ref[0, rows(q0, BLOCK), :] = l_t.T

    for r in range(d):
        block(r, r, BLOCK)
        if nb > 1:
            def body(i, carry):
                q0 = r + i * (BLOCK * d)
                block(q0, q0 - BLOCK * d, 2 * BLOCK)
                return carry
            lax.fori_loop(1, nb, body, 0, unroll=min(nb - 1, 5))


def _attn_a_prompt(slab_a, g, d):
    view = slab_a.reshape(BATCH, SEQ, SLAB_A)
    pairs = A_W // LANES
    spec = lambda off: pl.BlockSpec((1, SEQ, LANES), lambda b, hp: (b, 0, off * pairs + hp))
    oshape = jax.ShapeDtypeStruct((BATCH, SEQ, A_W), F32)
    o, lse = pl.pallas_call(
        functools.partial(_attn_a_prompt_body, d=d),
        grid=(BATCH, pairs),
        in_specs=[spec(g), spec(A_NG + g), spec(2 * A_NG + g)],
        out_specs=[spec(0), spec(0)],
        out_shape=[oshape, oshape],
        compiler_params=_params(2),
        name=f"attn_a{g + 1}_prompt",
    )(view, view, view)
    return o.reshape(BATCH * SEQ, A_W), lse.reshape(BATCH * SEQ, A_W)


def _attn_b_prompt_body(q_ref, k_ref, v_ref, sink_ref, o_ref, *, nb):
    def block(q0, k0, nk):
        mask = _band_mask_t(nk)
        lo = _low_lanes(nk)
        k = k_ref[0, pl.ds(k0, nk), :]
        k_sw = pltpu.roll(k, HEAD_DIM, axis=1)
        vt = v_ref[0, pl.ds(k0, nk), :].T.astype(BF16)
        for kv in range(B_KV_HEADS):
            on_lo, on_hi = (k, k_sw) if kv == 0 else (k_sw, k)
            k2 = jnp.concatenate([jnp.where(lo, on_lo, 0.0), jnp.where(lo, 0.0, on_hi)], axis=0).astype(BF16)
            cols = slice(2 * kv * LANES, (2 * kv + 2) * LANES)
            q2 = q_ref[0, pl.ds(q0, BLOCK), cols] * SCALE
            q2 = jnp.concatenate([q2[:, 0:LANES], q2[:, LANES:2 * LANES]], axis=0).astype(BF16)
            st = _mm_nt(k2, q2)
            ps, rs = [], []
            for g in range(B_GROUP):
                half, chunk = g % 2, g // 2
                sink = sink_ref[kv * B_GROUP + g:kv * B_GROUP + g + 1, :]
                p, r, _ = _softmax_t(st[half * nk:(half + 1) * nk, chunk * BLOCK:(chunk + 1) * BLOCK], mask, sink)
                ps.append(p)
                rs.append(r)
            ot = _mm(vt, jnp.concatenate(ps, axis=1).astype(BF16))
            oj = ot[kv * HEAD_DIM:(kv + 1) * HEAD_DIM, :] * jnp.concatenate(rs, axis=1)
            for chunk in range(2):
                o_t = jnp.concatenate([oj[:, (2 * chunk) * BLOCK:(2 * chunk + 1) * BLOCK],
                                       oj[:, (2 * chunk + 1) * BLOCK:(2 * chunk + 2) * BLOCK]], axis=0)
                c = 2 * kv + chunk
                o_ref[0, pl.ds(q0, BLOCK), c * LANES:(c + 1) * LANES] = o_t.T

    block(0, 0, BLOCK)

    def body(i, carry):
        q0 = pl.multiple_of(i * BLOCK, BLOCK)
        block(q0, pl.multiple_of(q0 - BLOCK, BLOCK), 2 * BLOCK)
        return carry
    lax.fori_loop(1, nb, body, 0, unroll=5)


def _attn_b_prompt(slab_b, layer, sinks):
    view = slab_b.reshape(BATCH, SEQ, SLAB_B)
    nq = B_QW // LANES
    o = pl.pallas_call(
        functools.partial(_attn_b_prompt_body, nb=SEQ // BLOCK),
        grid=(BATCH,),
        in_specs=[pl.BlockSpec((1, SEQ, B_QW), lambda b: (b, 0, 0)),
                  pl.BlockSpec((1, SEQ, B_KW), lambda b: (b, 0, nq)),
                  pl.BlockSpec((1, SEQ, B_KW), lambda b: (b, 0, nq + 1)),
                  _resident((B_Q_HEADS, LANES), layer)],
        out_specs=pl.BlockSpec((1, SEQ, B_QW), lambda b: (b, 0, 0)),
        out_shape=jax.ShapeDtypeStruct((BATCH, SEQ, B_QW), F32),
        compiler_params=_params(1),
        name="attn_b_prompt",
    )(view, view, view, sinks)
    return o.reshape(BATCH * SEQ, B_QW)


def _sample_masks(rows, lb, window, dil):
    assert dil & (dil - 1) == 0 and DEC_SEQ & (DEC_SEQ - 1) == 0
    t_c = lax.broadcasted_iota(jnp.int32, (rows, lb), 0) & (DEC_SEQ - 1)
    dist_c = lb + t_c - lax.broadcasted_iota(jnp.int32, (rows, lb), 1)
    mask_c = (dist_c >= 0) & (dist_c <= window) & ((dist_c & (dil - 1)) == 0)
    t_n = lax.broadcasted_iota(jnp.int32, (rows, DEC_SEQ), 0) & (DEC_SEQ - 1)
    dist_n = t_n - lax.broadcasted_iota(jnp.int32, (rows, DEC_SEQ), 1)
    mask_n = (dist_n >= 0) & (dist_n <= window) & ((dist_n & (dil - 1)) == 0)
    return mask_c, mask_n


def _sample_attend(q, kt, vt, kn, vn, masks, sink=None):
    mask_c, mask_n = masks
    s_c = jnp.where(mask_c, _mm(q.astype(BF16), kt), NEG_INF)
    s_n = jnp.where(mask_n, _mm_nt(q, kn), NEG_INF)
    m = jnp.maximum(jnp.max(s_c, axis=-1, keepdims=True), jnp.max(s_n, axis=-1, keepdims=True))
    if sink is not None:
        m = jnp.maximum(m, sink)
    p_c = jnp.exp(s_c - m)
    p_n = jnp.exp(s_n - m)
    den = jnp.sum(p_c, axis=-1, keepdims=True) + jnp.sum(p_n, axis=-1, keepdims=True)
    if sink is not None:
        den = den + jnp.exp(sink - m)
    o = _mm_nt(p_c.astype(BF16), vt) + _mm(p_n, vn)
    return o * (1.0 / den), m + jnp.log(den)


def _write_shifted_cache(dst_ref, src_ref, s, heads, new_rows, lb):
    padded = jnp.concatenate([jnp.zeros((LANES - DEC_SEQ, LANES), F32), new_rows], axis=0)
    new_t = padded.T
    keep = lax.broadcasted_iota(jnp.int32, (LANES, LANES), 1) < LANES - DEC_SEQ
    rotated = lambda j: pltpu.roll(src_ref[0, s, heads, :, j * LANES:(j + 1) * LANES].reshape(LANES, LANES),
                                   LANES - DEC_SEQ, axis=1)
    n_tiles = lb // LANES
    cur = rotated(0)
    for j in range(n_tiles):
        nxt = rotated(j + 1) if j + 1 < n_tiles else new_t
        dst_ref[0, s, heads, :, j * LANES:(j + 1) * LANES] = jnp.where(keep, cur, nxt).reshape(2, HEAD_DIM, LANES)
        cur = nxt


SAMPLE_SEQS_PER_STEP = 2


def _sample_attn_body(*refs, aliased, n_seq):
    for s in range(n_seq):
        _sample_attn_one(refs, aliased, s)


def _sample_attn_one(refs, aliased, s):
    sa_ref, sb_ref, sink_ref = refs[:3]
    cache_refs = refs[3:11]
    pos = 11 + (8 if aliased else 0)
    oa_ref, ob_ref = refs[pos:pos + 2]
    out_refs = refs[pos + 2:pos + 10]
    lo8 = _low_lanes(DEC_SEQ)
    seq_rows = slice(s * DEC_SEQ, (s + 1) * DEC_SEQ)

    o_g, l_g = [], []
    for g, (window, dil) in enumerate(A_GROUPS):
        kc_ref, vc_ref, ko_ref, vo_ref = cache_refs[2 * g], cache_refs[2 * g + 1], out_refs[2 * g], out_refs[2 * g + 1]
        lb = kc_ref.shape[-1]
        masks = _sample_masks(2 * DEC_SEQ, lb, window, dil)
        o_pairs, l_pairs = [], []
        for c in range(A_W // LANES):
            col = lambda part: slice(part * A_QKV + g * A_W + c * LANES, part * A_QKV + g * A_W + (c + 1) * LANES)
            heads = slice(2 * c, 2 * c + 2)
            q = sa_ref[seq_rows, col(0)] * SCALE
            kn, vn = sa_ref[seq_rows, col(1)], sa_ref[seq_rows, col(2)]
            kt = kc_ref[0, s, heads].reshape(LANES, lb)
            vt = vc_ref[0, s, heads].reshape(LANES, lb)
            q2 = jnp.concatenate([jnp.where(lo8, q, 0.0), jnp.where(lo8, 0.0, q)], axis=0)
            o, lse = _sample_attend(q2, kt.astype(BF16), vt.astype(BF16), kn, vn, masks)
            o_pairs.append(jnp.where(lo8, o[0:DEC_SEQ], o[DEC_SEQ:]))
            l_pairs.append(jnp.where(lo8, lse[0:DEC_SEQ], lse[DEC_SEQ:]))
            _write_shifted_cache(ko_ref, kc_ref, s, heads, kn, lb)
            _write_shifted_cache(vo_ref, vc_ref, s, heads, vn, lb)
        o_g.append(o_pairs)
        l_g.append(l_pairs)
    for c in range(A_W // LANES):
        l1, l2, l3 = l_g[0][c], l_g[1][c], l_g[2][c]
        lm = jnp.maximum(jnp.maximum(l1, l2), l3)
        e1, e2, e3 = jnp.exp(l1 - lm), jnp.exp(l2 - lm), jnp.exp(l3 - lm)
        es = e1 + e2 + e3
        oa_ref[seq_rows, c * LANES:(c + 1) * LANES] = ((e1 / es) * o_g[0][c] + (e2 / es) * o_g[1][c]
                                                       + (e3 / es) * o_g[2][c])

    kc_ref, vc_ref, ko_ref, vo_ref = cache_refs[6], cache_refs[7], out_refs[6], out_refs[7]
    lb = kc_ref.shape[-1]
    rows = B_GROUP * DEC_SEQ
    masks = _sample_masks(rows, lb, B_WINDOW, 1)
    both = slice(0, B_KV_HEADS)
    ktb = kc_ref[0, s].reshape(LANES, lb).astype(BF16)
    vtb = vc_ref[0, s].reshape(LANES, lb).astype(BF16)
    kn, vn = sb_ref[seq_rows, B_QW:B_QW + B_KW], sb_ref[seq_rows, B_QW + B_KW:SLAB_B]
    halves = {}
    for kv in range(B_KV_HEADS):
        parts = []
        for gq in range(B_GROUP):
            head = kv * B_GROUP + gq
            q = sb_ref[seq_rows, (head // 2) * LANES:(head // 2 + 1) * LANES] * SCALE
            q = jnp.where(lo8, q, 0.0) if head % 2 == 0 else jnp.where(lo8, 0.0, q)
            parts.append(q if head % 2 == kv else pltpu.roll(q, HEAD_DIM, axis=1))
        q4 = jnp.concatenate(parts, axis=0)
        sink = sink_ref[kv * rows:(kv + 1) * rows, 0:1]
        o, _ = _sample_attend(q4, ktb, vtb, kn, vn, masks, sink)
        for gq in range(B_GROUP):
            head = kv * B_GROUP + gq
            oh = o[gq * DEC_SEQ:(gq + 1) * DEC_SEQ]
            halves[head] = oh if head % 2 == kv else pltpu.roll(oh, HEAD_DIM, axis=1)
    for c in range(B_QW // LANES):
        ob_ref[seq_rows, c * LANES:(c + 1) * LANES] = jnp.where(lo8, halves[2 * c], halves[2 * c + 1])
    _write_shifted_cache(ko_ref, kc_ref, s, both, kn, lb)
    _write_shifted_cache(vo_ref, vc_ref, s, both, vn, lb)


def _sample_attn(layer, slab_a, slab_b, sinks, caches, prev):
    aliased = prev is not None
    n_seq = SAMPLE_SEQS_PER_STEP
    rows = n_seq * DEC_SEQ
    row = lambda b: (b, 0)
    cspec = lambda c: pl.BlockSpec((1, n_seq) + c.shape[2:], lambda b: (layer, b, 0, 0, 0))
    in_specs = [pl.BlockSpec((rows, SLAB_A), row), pl.BlockSpec((rows, SLAB_B), row),
                _resident(sinks.shape[1:], layer)]
    in_specs += [cspec(c) for c in caches]
    args = [slab_a, slab_b, sinks] + list(caches)
    aliases = {}
    if aliased:
        aliases = {len(args) + j: 2 + j for j in range(8)}
        in_specs += [pl.BlockSpec(memory_space=pl.ANY)] * 8
        args += list(prev)
    m = slab_a.shape[0]
    res = pl.pallas_call(
        functools.partial(_sample_attn_body, aliased=aliased, n_seq=n_seq),
        grid=(DEC_BATCH // n_seq,),
        in_specs=in_specs,
        out_specs=[pl.BlockSpec((rows, A_W), row), pl.BlockSpec((rows, B_QW), row)] + [cspec(c) for c in caches],
        out_shape=[jax.ShapeDtypeStruct((m, A_W), F32), jax.ShapeDtypeStruct((m, B_QW), F32)]
                  + [jax.ShapeDtypeStruct(c.shape, F32) for c in caches],
        input_output_aliases=aliases,
        compiler_params=_params(1),
        name="sample_attn",
    )(*args)
    return res[0], res[1], list(res[2:])


def _ssm_body(u_ref, are_ref, aim_ref, b_ref, cbig_ref, dvec_ref, wglu_ref, x0re_ref, x0im_ref,
              od_ref, sre_ref, sim_ref, bu_ref, xs_ref, *, steps, pitch):
    @pl.when(pl.program_id(1) == 0)
    def _():
        sre_ref[0] = x0re_ref[0]
        sim_ref[0] = x0im_ref[0]

    n_re = D_NS // LANES
    tiles = lambda x: [x[:, j * LANES:(j + 1) * LANES] for j in range(x.shape[1] // LANES)]

    if pitch == steps:
        planes = [(slice(None), slice(0, SUBLANES * steps))]
    else:
        planes = [(s, slice(s * pitch, s * pitch + steps)) for s in range(SUBLANES)]
    load_u = lambda sel: u_ref[sel].reshape(-1, D_WIDTH)

    for sel, plane_rows in planes:
        bu = _mm(load_u(sel).astype(BF16), b_ref[...])
        for j, tile in enumerate(tiles(bu)):
            bu_ref[j, plane_rows, :] = tile
    a_re = are_ref[...]
    a_im = aim_ref[...]

    def step(t, carry):
        re, im = carry
        at_t = pl.ds(t, SUBLANES, stride=pitch)
        b_re = jnp.concatenate([bu_ref[j, at_t, :] for j in range(n_re)], axis=1)
        b_im = jnp.concatenate([bu_ref[n_re + j, at_t, :] for j in range(n_re)], axis=1)
        re, im = a_re * re - a_im * im + b_re, a_re * im + a_im * re + b_im
        for j, (tr, ti) in enumerate(zip(tiles(re), tiles(im))):
            xs_ref[j, at_t, :] = tr
            xs_ref[n_re + j, at_t, :] = ti
        return re, im

    re, im = lax.fori_loop(0, steps, step, (sre_ref[0], sim_ref[0]), unroll=min(steps, SUBLANES))
    sre_ref[0] = re
    sim_ref[0] = im
    for sel, plane_rows in planes:
        xs = jnp.concatenate([xs_ref[j, plane_rows, :] for j in range(2 * n_re)], axis=1)
        y = _mm(xs.astype(BF16), cbig_ref[...]) + dvec_ref[...] * load_u(sel)
        y = _mm(jax.nn.gelu(y).astype(BF16), wglu_ref[...])
        od = y[:, 0:D_WIDTH] * jax.nn.sigmoid(y[:, D_WIDTH:2 * D_WIDTH])
        od_ref[sel] = od.reshape((SUBLANES, steps, D_WIDTH) if pitch == steps else (steps, D_WIDTH))


def _ssm(slab_c, bsz, length, layer, a_re, a_im, bbig, cbig, dvec, wglu, x0_re, x0_im, steps):
    n = bsz // SUBLANES
    pitch = steps if steps <= SUBLANES else steps + SUBLANES
    view = slab_c.reshape(bsz, length, SLAB_C)
    u_spec = pl.BlockSpec((SUBLANES, steps, D_WIDTH), lambda i, c: (i, c, SLAB_C // D_WIDTH - 1))
    o_spec = pl.BlockSpec((SUBLANES, steps, D_WIDTH), lambda i, c: (i, c, 0))
    st_spec = pl.BlockSpec((1, SUBLANES, D_NS), lambda i, c: (i, 0, 0))
    st_shape = jax.ShapeDtypeStruct((n, SUBLANES, D_NS), F32)
    od, s_re, s_im = pl.pallas_call(
        functools.partial(_ssm_body, steps=steps, pitch=pitch),
        grid=(n, length // steps),
        in_specs=[u_spec, _resident((SUBLANES, D_NS), layer), _resident((SUBLANES, D_NS), layer),
                  _resident((D_WIDTH, 2 * D_NS), layer), _resident((2 * D_NS, D_WIDTH), layer),
                  _resident((1, D_WIDTH), layer), _resident((D_WIDTH, 2 * D_WIDTH), layer), st_spec, st_spec],
        out_specs=[o_spec, st_spec, st_spec],
        out_shape=[jax.ShapeDtypeStruct((bsz, length, D_WIDTH), F32), st_shape, st_shape],
        scratch_shapes=[pltpu.VMEM((2 * D_NS // LANES, SUBLANES * pitch, LANES), F32)] * 2,
        compiler_params=_params(2),
        name="ssm",
    )(view, a_re, a_im, bbig, cbig, dvec, wglu, x0_re, x0_im)
    return od.reshape(bsz * length, D_WIDTH), s_re, s_im


def _merge_body(*refs, n_groups):
    x_ref = refs[0]
    a_refs = refs[1:1 + (2 * n_groups if n_groups > 1 else 1)]
    ob_ref, oc_ref, od_ref, gpre_ref, wgl_ref, wa_ref, wb_ref, wc_ref, wd_ref, wout_ref, gpost_ref, y_ref = \
        refs[1 + len(a_refs):]
    x = x_ref[...]
    h = _rmsnorm(x, gpre_ref[...]).astype(BF16)
    if n_groups > 1:
        lses = [a_refs[2 * g + 1][...] for g in range(n_groups)]
        lm = functools.reduce(jnp.maximum, lses)
        es = [jnp.exp(l - lm) for l in lses]
        den = functools.reduce(lambda a, b: a + b, es)
        o_a = functools.reduce(lambda a, b: a + b, [(es[g] / den) * a_refs[2 * g][...] for g in range(n_groups)])
    else:
        o_a = a_refs[0][...]
    branches = ((o_a, wa_ref), (ob_ref[...], wb_ref), (oc_ref[...], wc_ref), (od_ref[...], wd_ref))
    merged = None
    for j, (o, w_ref) in enumerate(branches):
        gate = jax.nn.sigmoid(_mm(h, wgl_ref[:, j * D_MODEL:(j + 1) * D_MODEL]))
        term = gate * _mm(o.astype(BF16), w_ref[...])
        merged = term if merged is None else merged + term
    mix = _mm(merged.astype(BF16), wout_ref[...])
    y_ref[...] = x + _rmsnorm(mix, gpost_ref[...])


def _merge(x, a_parts, ob, slab_c, od, layer, gpre, w_in, wa, wb, wc, wd, wout, gpost, tm):
    m = x.shape[0]
    row = lambda i: (i, 0)
    r256 = pl.BlockSpec((tm, 256), row)
    n_groups = max(1, len(a_parts) // 2)
    return pl.pallas_call(
        functools.partial(_merge_body, n_groups=n_groups),
        grid=(m // tm,),
        in_specs=[pl.BlockSpec((tm, D_MODEL), row)] + [r256] * len(a_parts)
                 + [pl.BlockSpec((tm, B_QW), row), r256, r256,
                    _resident((1, D_MODEL), layer), _resident((D_MODEL, N_BRANCH * D_MODEL), layer, col=1),
                    _resident((A_W, D_MODEL), layer), _resident((B_QW, D_MODEL), layer),
                    _resident((C_WIDTH, D_MODEL), layer), _resident((D_WIDTH, D_MODEL), layer),
                    _resident((D_MODEL, D_MODEL), layer), _resident((1, D_MODEL), layer)],
        out_specs=pl.BlockSpec((tm, D_MODEL), row),
        out_shape=jax.ShapeDtypeStruct((m, D_MODEL), F32),
        compiler_params=_params(1),
        name="merge",
    )(x, *a_parts, ob, slab_c, od, gpre, w_in, wa, wb, wc, wd, wout, gpost)


def _ffn_body(x_ref, p_ref, gpre_ref, wg_ref, wu_ref, wdn_ref, gpost_ref, wple_ref, wpg_ref, y_ref):
    x = x_ref[...]
    h = _rmsnorm(x, gpre_ref[...]).astype(BF16)
    act = jax.nn.silu(_mm(h, wg_ref[...])) * _mm(h, wu_ref[...])
    f = _mm(act.astype(BF16), wdn_ref[...])
    x = x + _rmsnorm(f, gpost_ref[...])
    gate = jax.nn.sigmoid(_mm(x.astype(BF16), wpg_ref[...]))
    y_ref[...] = x + gate * _mm(p_ref[...].astype(BF16), wple_ref[...])


def _ffn(x, p_all, layer, gpre, wg, wu, wdn, gpost, wple, wpg, tm):
    m = x.shape[0]
    row = lambda i: (i, 0)
    return pl.pallas_call(
        _ffn_body,
        grid=(m // tm,),
        in_specs=[pl.BlockSpec((tm, D_MODEL), row), pl.BlockSpec((None, tm, PLE_DIM), lambda i: (layer, i, 0)),
                  _resident((1, D_MODEL), layer), _resident((D_MODEL, D_FF), layer), _resident((D_MODEL, D_FF), layer),
                  _resident((D_FF, D_MODEL), layer), _resident((1, D_MODEL), layer),
                  _resident((PLE_DIM, D_MODEL), layer), _resident((D_MODEL, D_MODEL), layer)],
        out_specs=pl.BlockSpec((tm, D_MODEL), row),
        out_shape=jax.ShapeDtypeStruct((m, D_MODEL), F32),
        compiler_params=_params(1),
        name="ffn_ple",
    )(x, p_all, gpre, wg, wu, wdn, gpost, wple, wpg)


def _rope_tables(pos):
    inv = ROPE_THETA ** (-jnp.arange(HALF, dtype=F32) / HALF)
    ang = pos.astype(F32)[:, None] * inv[None, :]
    cos, sin = jnp.cos(ang), jnp.sin(ang)
    cos_t = jnp.tile(cos, (1, LANES // HALF))
    sin_t = jnp.tile(jnp.concatenate([-sin, sin], axis=1), (1, LANES // HEAD_DIM))
    return cos_t, sin_t


def _ssm_weights(lam_re, lam_im, log_dt, b_re, b_im, c_re, c_im):
    lam = lax.complex(lam_re, lam_im)
    dt = jnp.exp(log_dt)[..., None]
    a_bar = jnp.exp(lam * dt)
    b_bar = ((a_bar - 1.0) / lam)[..., None] * lax.complex(b_re, b_im)
    eye = jnp.eye(D_NGROUPS, dtype=F32)
    pack_b = lambda b: jnp.einsum('lgni,gh->lgihn', b, eye).reshape(DEPTH, D_WIDTH, D_NS)
    pack_c = lambda c: jnp.einsum('lgin,gh->lgnhi', c, eye).reshape(DEPTH, D_NS, D_WIDTH)
    bbig = jnp.concatenate([pack_b(jnp.real(b_bar)), pack_b(jnp.imag(b_bar))], axis=2)
    cbig = jnp.concatenate([pack_c(c_re), -pack_c(c_im)], axis=1)
    tile8 = lambda a: jnp.broadcast_to(a.reshape(DEPTH, 1, D_NS), (DEPTH, SUBLANES, D_NS))
    return tile8(jnp.real(a_bar)), tile8(jnp.imag(a_bar)), bbig.astype(BF16), cbig.astype(BF16)


def _key_major(c):
    return jnp.transpose(c, (0, 1, 3, 4, 2))


def _row_major(c):
    return jnp.transpose(c, (0, 1, 4, 2, 3))


def kernel(x_prompt, x_sample, p_prompt, p_sample, cache_a1_k, cache_a1_v, cache_a2_k, cache_a2_v, cache_a3_k, cache_a3_v, cache_b_k, cache_b_v, state_c_conv, state_d_re, state_d_im, norm_mix_pre, norm_mix_post, norm_ffn_pre, norm_ffn_post, w_in, attn_sinks, conv_c_w, ssm_lam_re, ssm_lam_im, ssm_log_dt, ssm_b_re, ssm_b_im, ssm_c_re, ssm_c_im, ssm_d, w_d_glu, w_br_a, w_br_b, w_br_c, w_br_d, w_out, w_ffn_gate, w_ffn_up, w_ffn_down, w_ple, w_ple_gate):
    assert x_prompt.shape == (BATCH, SEQ, D_MODEL) and x_sample.shape == (DEC_BATCH, DEC_SEQ, D_MODEL)
    assert w_in.shape == (DEPTH, D_MODEL, 2 * MIX_W)
    assert all(min(w, PAST_LEN) == w for w, _ in A_GROUPS) and min(B_WINDOW, PAST_LEN) == B_WINDOW
    mp, ms = BATCH * SEQ, DEC_BATCH * DEC_SEQ
    tm_p, tm_s = 512, 256

    cos_p, sin_p = _rope_tables(jnp.arange(SEQ, dtype=jnp.int32))
    cos_s, sin_s = _rope_tables(PAST_LEN + jnp.arange(DEC_SEQ, dtype=jnp.int32))
    cos_s, sin_s = jnp.tile(cos_s, (tm_s // DEC_SEQ, 1)), jnp.tile(sin_s, (tm_s // DEC_SEQ, 1))

    caches = [_key_major(c) for c in (cache_a1_k, cache_a1_v, cache_a2_k, cache_a2_v, cache_a3_k, cache_a3_v,
                                      cache_b_k, cache_b_v)]
    yp = x_prompt.reshape(mp, D_MODEL)
    ys = x_sample.reshape(ms, D_MODEL)
    new_caches = None
    st_p = [[] for _ in range(11)]
    conv_s, dre_s, dim_s = [], [], []

    rows3 = lambda a: a.reshape(DEPTH, 1, -1)
    w_in_b = w_in.astype(BF16)
    g_mix_pre = rows3(norm_mix_pre)
    ssm_w = _ssm_weights(ssm_lam_re, ssm_lam_im, ssm_log_dt, ssm_b_re, ssm_b_im, ssm_c_re, ssm_c_im) + (
        rows3(ssm_d), w_d_glu.astype(BF16))
    sinks_p = jnp.broadcast_to(attn_sinks.reshape(DEPTH, B_Q_HEADS, 1), (DEPTH, B_Q_HEADS, LANES))
    sinks_s = jnp.broadcast_to(jnp.repeat(attn_sinks.reshape(DEPTH, B_Q_HEADS), DEC_SEQ, axis=1)[..., None],
                               (DEPTH, B_Q_HEADS * DEC_SEQ, LANES))
    merge_w = (g_mix_pre, w_in_b, w_br_a.astype(BF16), w_br_b.astype(BF16), w_br_c.astype(BF16),
               w_br_d.astype(BF16), w_out.astype(BF16), rows3(norm_mix_post))
    ffn_w = (rows3(norm_ffn_pre), w_ffn_gate.astype(BF16), w_ffn_up.astype(BF16), w_ffn_down.astype(BF16),
             rows3(norm_ffn_post), w_ple.astype(BF16), w_ple_gate.astype(BF16))
    zero_state = jnp.zeros((BATCH // SUBLANES, SUBLANES, D_NS), F32)

    for i in range(DEPTH):
        sa, sb, sc, tail = _inproj(yp, i, g_mix_pre, w_in_b, cos_p, sin_p, conv_c_w, tm_p, SEQ // tm_p, SEQ // tm_p)
        a_parts = []
        for g, (w, d) in enumerate(A_GROUPS):
            a_parts += list(_attn_a_prompt(sa, g, d))
        ob = _attn_b_prompt(sb, i, sinks_p)
        od, s_re, s_im = _ssm(sc, BATCH, SEQ, i, *ssm_w, zero_state, zero_state, 128)
        x1 = _merge(yp, a_parts, ob, sc, od, i, *merge_w, tm_p)
        yp = _ffn(x1, p_prompt.reshape(DEPTH, mp, PLE_DIM), i, *ffn_w, tm_p)
        sa3 = sa.reshape(BATCH, SEQ, SLAB_A)
        for g, (w, d) in enumerate(A_GROUPS):
            for j in range(2):
                c0 = (1 + j) * A_QKV + g * A_W
                st_p[2 * g + j].append(sa3[:, SEQ - w:, c0:c0 + A_W].reshape(BATCH, w, A_HEADS, HEAD_DIM))
        sb3 = sb.reshape(BATCH, SEQ, SLAB_B)
        for j in range(2):
            c0 = B_QW + j * B_KW
            st_p[6 + j].append(sb3[:, SEQ - B_WINDOW:, c0:c0 + B_KW].reshape(BATCH, B_WINDOW, B_KV_HEADS, HEAD_DIM))
        st_p[8].append(tail.reshape(BATCH, SEQ // tm_p, SUBLANES, C_WIDTH)[:, -1, SUBLANES - (C_CONV - 1):])
        st_p[9].append(s_re.reshape(BATCH, D_NGROUPS, D_STATE))
        st_p[10].append(s_im.reshape(BATCH, D_NGROUPS, D_STATE))

        s1 = jnp.broadcast_to(state_c_conv[i][:, None, 1, :], (DEC_BATCH, DEC_SEQ, C_WIDTH)).reshape(ms, C_WIDTH)
        s2 = jnp.broadcast_to(state_c_conv[i][:, None, 0, :], (DEC_BATCH, DEC_SEQ, C_WIDTH)).reshape(ms, C_WIDTH)
        sa, sb, sc, tail = _inproj(ys, i, g_mix_pre, w_in_b, cos_s, sin_s, conv_c_w, tm_s, 1, 1, row_state=(s1, s2))
        o_a, ob, new_caches = _sample_attn(i, sa, sb, sinks_s, caches, new_caches)
        x0_re = state_d_re[i].reshape(DEC_BATCH // SUBLANES, SUBLANES, D_NS)
        x0_im = state_d_im[i].reshape(DEC_BATCH // SUBLANES, SUBLANES, D_NS)
        od, s_re, s_im = _ssm(sc, DEC_BATCH, DEC_SEQ, i, *ssm_w, x0_re, x0_im, DEC_SEQ)
        x1 = _merge(ys, [o_a], ob, sc, od, i, *merge_w, tm_s)
        ys = _ffn(x1, p_sample.reshape(DEPTH, ms, PLE_DIM), i, *ffn_w, tm_s)
        conv_s.append(tail.reshape(DEC_BATCH, DEC_SEQ, C_WIDTH)[:, DEC_SEQ - (C_CONV - 1):])
        dre_s.append(s_re.reshape(DEC_BATCH, D_NGROUPS, D_STATE))
        dim_s.append(s_im.reshape(DEC_BATCH, D_NGROUPS, D_STATE))

    p_states = [jnp.stack(s) for s in st_p]
    s_states = [_row_major(c) for c in new_caches] + [jnp.stack(conv_s), jnp.stack(dre_s), jnp.stack(dim_s)]
    out = [yp.reshape(BATCH, SEQ, D_MODEL), ys.reshape(DEC_BATCH, DEC_SEQ, D_MODEL)]
    for a, b in zip(p_states, s_states):
        out += [a, b]
    return tuple(out)
```

```python
import functools

import jax
import jax.numpy as jnp
from jax import lax
from jax.experimental import pallas as pl
from jax.experimental.pallas import tpu as pltpu

F32 = jnp.float32
BF16 = jnp.bfloat16

D_MODEL = 1024
BATCH = 8
SEQ = 2048
DEPTH = 2
DEC_BATCH = 128
DEC_SEQ = 8
PAST_LEN = 16384
HEAD_DIM = 64
HALF = HEAD_DIM // 2
ROPE_THETA = 10000.0
BLOCK = 128
EPS = 1e-6
NEG_INF = -1e30
A_HEADS = 4
A_GROUPS = ((128, 1), (512, 4), (2048, 16))
A_NG = len(A_GROUPS)
A_W = A_HEADS * HEAD_DIM
A_QKV = A_NG * A_W
B_Q_HEADS = 8
B_KV_HEADS = 2
B_GROUP = B_Q_HEADS // B_KV_HEADS
B_WINDOW = 128
B_QW = B_Q_HEADS * HEAD_DIM
B_KW = B_KV_HEADS * HEAD_DIM
C_WIDTH = 256
C_CONV = 3
D_GROUP = 16
D_NGROUPS = 16
D_WIDTH = 256
D_STATE = 64
D_NS = D_NGROUPS * D_STATE
N_BRANCH = 4
D_FF = 2816
PLE_DIM = 256
MIX_W = 4096
SCALE = HEAD_DIM ** -0.5
N_CACHES = 2 * A_NG + 2

LANES = 128
SUBLANES = 8
VMEM_LIMIT = 56 * 1024 * 1024

SLAB_A = 3 * A_QKV
SLAB_B = B_QW + 2 * B_KW
SLAB_C = C_WIDTH + D_WIDTH
ROPE_A = 2 * A_QKV
ROPE_B = B_QW + B_KW


def _params(n_axes):
    return pltpu.CompilerParams(dimension_semantics=("arbitrary",) * n_axes, vmem_limit_bytes=VMEM_LIMIT)


def _resident(shape, layer, col=0):
    return pl.BlockSpec((None,) + tuple(shape), lambda *_: (layer, 0, col), pipeline_mode=pl.Buffered(1))


def _rmsnorm(x, g):
    return x * lax.rsqrt(jnp.mean(x * x, axis=-1, keepdims=True) + EPS) * g


def _mm(a, b):
    return jnp.dot(a, b, preferred_element_type=F32)


def _mm_nt(a, b):
    return lax.dot_general(a, b, (((1,), (1,)), ((), ())), preferred_element_type=F32)


def _low_lanes(rows):
    return lax.broadcasted_iota(jnp.int32, (rows, LANES), 1) < HEAD_DIM


def _inproj_body(*refs, seq_tiles, per_row_state, aliased):
    if per_row_state:
        x_ref, g_ref, w_ref, cos_ref, sin_ref, cw_ref, s1_ref, s2_ref, a_ref, b_ref, c_ref, tail_ref = refs
    else:
        x_ref, g_ref, w_ref, cos_ref, sin_ref, cw_ref = refs[:6]
        n_in = 6 + (N_CACHES if aliased else 0)
        a_ref, b_ref, c_ref, tail_ref = refs[n_in:n_in + 4]
        cache_refs = refs[n_in + 4:n_in + 4 + N_CACHES]
        carry_ref = refs[n_in + 4 + N_CACHES]
    tm = x_ref.shape[0]
    h = _rmsnorm(x_ref[...], g_ref[...]).astype(BF16)
    cos = cos_ref[...]
    sin = sin_ref[...]
    lane = lax.broadcasted_iota(jnp.int32, (tm, LANES), 1)
    first_half = (lane & (HEAD_DIM - 1)) < HALF

    def rope(z):
        partner = jnp.where(first_half, pltpu.roll(z, LANES - HALF, axis=1), pltpu.roll(z, HALF, axis=1))
        return z * cos + partner * sin

    za = _mm(h, w_ref[:, 0:SLAB_A])
    a_chunks = []
    for c in range(SLAB_A // LANES):
        blk = za[:, c * LANES:(c + 1) * LANES]
        a_chunks.append(rope(blk) if c * LANES < ROPE_A else blk)
        a_ref[:, c * LANES:(c + 1) * LANES] = a_chunks[-1]
    zb = _mm(h, w_ref[:, SLAB_A:SLAB_A + SLAB_B])
    b_chunks = []
    for c in range(SLAB_B // LANES):
        blk = zb[:, c * LANES:(c + 1) * LANES]
        b_chunks.append(rope(blk) if c * LANES < ROPE_B else blk)
        b_ref[:, c * LANES:(c + 1) * LANES] = b_chunks[-1]

    if not per_row_state:
        pairs = A_W // LANES
        chunk_of = lambda part, g, j: a_chunks[(part * A_QKV + g * A_W) // LANES + j]

        def write(ref, chunk, rows):
            ref[0, :, :] = jnp.concatenate([ch[tm - rows:tm, :].T for ch in chunk], axis=0)

        last_group = A_NG - 1
        assert A_GROUPS[last_group][0] == seq_tiles * tm and A_GROUPS[1][0] == tm and A_GROUPS[0][0] <= tm
        for part in (1, 2):
            write(cache_refs[2 * last_group + part - 1], [chunk_of(part, last_group, j) for j in range(pairs)], tm)

        @pl.when(pl.program_id(0) % seq_tiles == seq_tiles - 1)
        def _():
            for g in range(last_group):
                for part in (1, 2):
                    write(cache_refs[2 * g + part - 1], [chunk_of(part, g, j) for j in range(pairs)], A_GROUPS[g][0])
            write(cache_refs[6], [b_chunks[B_QW // LANES]], B_WINDOW)
            write(cache_refs[7], [b_chunks[B_QW // LANES + 1]], B_WINDOW)
    z = _mm(h, w_ref[:, SLAB_A + SLAB_B:MIX_W])
    zc = z[:, 2 * C_WIDTH:3 * C_WIDTH] * z[:, 0:C_WIDTH]

    row = lax.broadcasted_iota(jnp.int32, (tm, C_WIDTH), 0)
    z1 = pltpu.roll(zc, 1, axis=0)
    z2 = pltpu.roll(zc, 2, axis=0)
    if per_row_state:
        t = row & (DEC_SEQ - 1)
        z1 = jnp.where(t == 0, s1_ref[...], z1)
        z2 = jnp.where(t == 0, s2_ref[...], jnp.where(t == 1, s1_ref[...], z2))
        tail_ref[...] = zc
    else:
        @pl.when(pl.program_id(0) % seq_tiles == 0)
        def _():
            carry_ref[...] = jnp.zeros(carry_ref.shape, F32)
        z1 = jnp.where(row == 0, carry_ref[SUBLANES - 1:SUBLANES, :], z1)
        z2 = jnp.where(row == 0, carry_ref[SUBLANES - 2:SUBLANES - 1, :],
                       jnp.where(row == 1, carry_ref[SUBLANES - 1:SUBLANES, :], z2))
        carry_ref[...] = zc[tm - SUBLANES:tm, :]
        tail_ref[...] = zc[tm - SUBLANES:tm, :]
    conv = cw_ref[0:1, :] * z2 + cw_ref[1:2, :] * z1 + cw_ref[2:3, :] * zc
    c_ref[:, 0:C_WIDTH] = z[:, C_WIDTH:2 * C_WIDTH] * conv
    c_ref[:, C_WIDTH:SLAB_C] = z[:, 3 * C_WIDTH:4 * C_WIDTH]


def _inproj(x, layer, g, w_in, cos, sin, convw, tm, table_blocks, seq_tiles, row_state=None, prev_caches=None):
    m = x.shape[0]
    per_row_state = row_state is not None
    aliased = prev_caches is not None
    row = lambda i: (i, 0)
    tab = lambda i: (i % table_blocks, 0)
    in_specs = [pl.BlockSpec((tm, D_MODEL), row), _resident((1, D_MODEL), layer), _resident((D_MODEL, MIX_W), layer),
                pl.BlockSpec((tm, LANES), tab), pl.BlockSpec((tm, LANES), tab), _resident((C_CONV, C_WIDTH), layer)]
    args = [x, g, w_in, cos, sin, convw]
    out_specs = [pl.BlockSpec((tm, SLAB_A), row), pl.BlockSpec((tm, SLAB_B), row), pl.BlockSpec((tm, SLAB_C), row)]
    out_shape = [jax.ShapeDtypeStruct((m, SLAB_A), F32), jax.ShapeDtypeStruct((m, SLAB_B), F32),
                 jax.ShapeDtypeStruct((m, SLAB_C), F32)]
    scratch, aliases = [], {}
    if per_row_state:
        in_specs += [pl.BlockSpec((tm, C_WIDTH), row)] * 2
        args += list(row_state)
        out_specs.append(pl.BlockSpec((tm, C_WIDTH), row))
        out_shape.append(jax.ShapeDtypeStruct((m, C_WIDTH), F32))
    else:
        scratch = [pltpu.VMEM((SUBLANES, C_WIDTH), F32)]
        out_specs.append(pl.BlockSpec((SUBLANES, C_WIDTH), row))
        out_shape.append(jax.ShapeDtypeStruct((m // tm * SUBLANES, C_WIDTH), F32))
        n_seqs = m // (tm * seq_tiles)
        widths = [(A_W, w) for w, _ in A_GROUPS for _ in range(2)] + [(B_KW, B_WINDOW)] * 2
        for j, (rows_, window) in enumerate(widths):
            per_tile = window == tm * seq_tiles
            idx = ((lambda i: (layer, i // seq_tiles, 0, i % seq_tiles)) if per_tile
                   else (lambda i: (layer, i // seq_tiles, 0, 0)))
            out_specs.append(pl.BlockSpec((None, 1, rows_, tm if per_tile else window), idx))
            out_shape.append(jax.ShapeDtypeStruct((DEPTH, n_seqs, rows_, window), F32))
        if aliased:
            aliases = {len(args) + j: 4 + j for j in range(N_CACHES)}
            in_specs += [pl.BlockSpec(memory_space=pl.ANY)] * N_CACHES
            args += list(prev_caches)
    return pl.pallas_call(
        functools.partial(_inproj_body, seq_tiles=seq_tiles, per_row_state=per_row_state, aliased=aliased),
        grid=(m // tm,),
        in_specs=in_specs,
        out_specs=out_specs,
        out_shape=out_shape,
        input_output_aliases=aliases,
        scratch_shapes=scratch,
        compiler_params=_params(1),
        name="in_proj",
    )(*args)


def _band_mask_t(nk):
    kj = lax.broadcasted_iota(jnp.int32, (nk, BLOCK), 0)
    qi = lax.broadcasted_iota(jnp.int32, (nk, BLOCK), 1)
    dist = qi + (nk - BLOCK) - kj
    return (dist >= 0) & (dist <= BLOCK)


def _softmax_t(s, mask, sink=None):
    s = jnp.where(mask, s, NEG_INF)
    m = jnp.max(s, axis=0, keepdims=True)
    if sink is not None:
        m = jnp.maximum(m, sink)
    p = jnp.exp(s - m)
    den = jnp.sum(p, axis=0, keepdims=True)
    if sink is not None:
        den = den + jnp.exp(sink - m)
    return p, 1.0 / den, m + jnp.log(den)


def _attn_a_prompt_body(q_ref, k_ref, v_ref, o_ref, lse_ref, *, d):
    nb = SEQ // d // BLOCK
    first_head_rows = lax.broadcasted_iota(jnp.int32, (LANES, BLOCK), 0) < HEAD_DIM

    def rows(start, n):
        return pl.ds(start, n) if d == 1 else pl.ds(start, n, stride=d)

    def block(q0, k0, nk):
        mask = _band_mask_t(nk)
        lo = _low_lanes(nk)
        q = (q_ref[0, rows(q0, BLOCK), :] * SCALE).astype(BF16)
        k = k_ref[0, rows(k0, nk), :]
        v = v_ref[0, rows(k0, nk), :]
        k2 = jnp.concatenate([jnp.where(lo, k, 0.0), jnp.where(lo, 0.0, k)], axis=0).astype(BF16)
        st = _mm_nt(k2, q)
        p0, r0, l0 = _softmax_t(st[0:nk], mask)
        p1, r1, l1 = _softmax_t(st[nk:2 * nk], mask)
        ot = _mm(v.T.astype(BF16), jnp.concatenate([p0, p1], axis=1).astype(BF16))
        o_t = jnp.where(first_head_rows, ot[:, 0:BLOCK] * r0, ot[:, BLOCK:2 * BLOCK] * r1)
        l_t = jnp.where(first_head_rows, jnp.broadcast_to(l0, (LANES, BLOCK)), jnp.broadcast_to(l1, (LANES, BLOCK)))
        o_ref[0, rows(q0, BLOCK), :] = o_t.T
        lse_ref[0, rows(q0, BLOCK), :] = l_t.T

    for r in range(d):
        block(r, r, BLOCK)
        if nb > 1:
            def body(i, carry):
                q0 = r + i * (BLOCK * d)
                block(q0, q0 - BLOCK * d, 2 * BLOCK)
                return carry
            lax.fori_loop(1, nb, body, 0, unroll=min(nb - 1, 5))


def _attn_a_prompt(slab_a, g, d):
    view = slab_a.reshape(BATCH, SEQ, SLAB_A)
    pairs = A_W // LANES
    spec = lambda off: pl.BlockSpec((1, SEQ, LANES), lambda b, hp: (b, 0, off * pairs + hp))
    oshape = jax.ShapeDtypeStruct((BATCH, SEQ, A_W), F32)
    o, lse = pl.pallas_call(
        functools.partial(_attn_a_prompt_body, d=d),
        grid=(BATCH, pairs),
        in_specs=[spec(g), spec(A_NG + g), spec(2 * A_NG + g)],
        out_specs=[spec(0), spec(0)],
        out_shape=[oshape, oshape],
        compiler_params=_params(2),
        name=f"attn_a{g + 1}_prompt",
    )(view, view, view)
    return o.reshape(BATCH * SEQ, A_W), lse.reshape(BATCH * SEQ, A_W)


def _attn_b_prompt_body(q_ref, k_ref, v_ref, sink_ref, o_ref, *, nb):
    def block(q0, k0, nk):
        mask = _band_mask_t(nk)
        lo = _low_lanes(nk)
        k = k_ref[0, pl.ds(k0, nk), :]
        k_sw = pltpu.roll(k, HEAD_DIM, axis=1)
        vt = v_ref[0, pl.ds(k0, nk), :].T.astype(BF16)
        for kv in range(B_KV_HEADS):
            on_lo, on_hi = (k, k_sw) if kv == 0 else (k_sw, k)
            k2 = jnp.concatenate([jnp.where(lo, on_lo, 0.0), jnp.where(lo, 0.0, on_hi)], axis=0).astype(BF16)
            cols = slice(2 * kv * LANES, (2 * kv + 2) * LANES)
            q2 = q_ref[0, pl.ds(q0, BLOCK), cols] * SCALE
            q2 = jnp.concatenate([q2[:, 0:LANES], q2[:, LANES:2 * LANES]], axis=0).astype(BF16)
            st = _mm_nt(k2, q2)
            ps, rs = [], []
            for g in range(B_GROUP):
                half, chunk = g % 2, g // 2
                sink = sink_ref[kv * B_GROUP + g:kv * B_GROUP + g + 1, :]
                p, r, _ = _softmax_t(st[half * nk:(half + 1) * nk, chunk * BLOCK:(chunk + 1) * BLOCK], mask, sink)
                ps.append(p)
                rs.append(r)
            ot = _mm(vt, jnp.concatenate(ps, axis=1).astype(BF16))
            oj = ot[kv * HEAD_DIM:(kv + 1) * HEAD_DIM, :] * jnp.concatenate(rs, axis=1)
            for chunk in range(2):
                o_t = jnp.concatenate([oj[:, (2 * chunk) * BLOCK:(2 * chunk + 1) * BLOCK],
                                       oj[:, (2 * chunk + 1) * BLOCK:(2 * chunk + 2) * BLOCK]], axis=0)
                c = 2 * kv + chunk
                o_ref[0, pl.ds(q0, BLOCK), c * LANES:(c + 1) * LANES] = o_t.T

    block(0, 0, BLOCK)

    def body(i, carry):
        q0 = pl.multiple_of(i * BLOCK, BLOCK)
        block(q0, pl.multiple_of(q0 - BLOCK, BLOCK), 2 * BLOCK)
        return carry
    lax.fori_loop(1, nb, body, 0, unroll=5)


def _attn_b_prompt(slab_b, layer, sinks):
    view = slab_b.reshape(BATCH, SEQ, SLAB_B)
    nq = B_QW // LANES
    o = pl.pallas_call(
        functools.partial(_attn_b_prompt_body, nb=SEQ // BLOCK),
        grid=(BATCH,),
        in_specs=[pl.BlockSpec((1, SEQ, B_QW), lambda b: (b, 0, 0)),
                  pl.BlockSpec((1, SEQ, B_KW), lambda b: (b, 0, nq)),
                  pl.BlockSpec((1, SEQ, B_KW), lambda b: (b, 0, nq + 1)),
                  _resident((B_Q_HEADS, LANES), layer)],
        out_specs=pl.BlockSpec((1, SEQ, B_QW), lambda b: (b, 0, 0)),
        out_shape=jax.ShapeDtypeStruct((BATCH, SEQ, B_QW), F32),
        compiler_params=_params(1),
        name="attn_b_prompt",
    )(view, view, view, sinks)
    return o.reshape(BATCH * SEQ, B_QW)


def _sample_masks(rows, lb, window, dil):
    assert dil & (dil - 1) == 0 and DEC_SEQ & (DEC_SEQ - 1) == 0
    t_c = lax.broadcasted_iota(jnp.int32, (rows, lb), 0) & (DEC_SEQ - 1)
    dist_c = lb + t_c - lax.broadcasted_iota(jnp.int32, (rows, lb), 1)
    mask_c = (dist_c >= 0) & (dist_c <= window) & ((dist_c & (dil - 1)) == 0)
    t_n = lax.broadcasted_iota(jnp.int32, (rows, DEC_SEQ), 0) & (DEC_SEQ - 1)
    dist_n = t_n - lax.broadcasted_iota(jnp.int32, (rows, DEC_SEQ), 1)
    mask_n = (dist_n >= 0) & (dist_n <= window) & ((dist_n & (dil - 1)) == 0)
    return mask_c, mask_n


def _sample_attend(q, kt, vt, kn, vn, masks, sink=None):
    mask_c, mask_n = masks
    s_c = jnp.where(mask_c, _mm(q.astype(BF16), kt), NEG_INF)
    s_n = jnp.where(mask_n, _mm_nt(q, kn), NEG_INF)
    m = jnp.maximum(jnp.max(s_c, axis=-1, keepdims=True), jnp.max(s_n, axis=-1, keepdims=True))
    if sink is not None:
        m = jnp.maximum(m, sink)
    p_c = jnp.exp(s_c - m)
    p_n = jnp.exp(s_n - m)
    den = jnp.sum(p_c, axis=-1, keepdims=True) + jnp.sum(p_n, axis=-1, keepdims=True)
    if sink is not None:
        den = den + jnp.exp(sink - m)
    o = _mm_nt(p_c.astype(BF16), vt) + _mm(p_n, vn)
    return o * (1.0 / den), m + jnp.log(den)


def _write_shifted_cache(dst_ref, src_ref, s, heads, new_rows, lb):
    padded = jnp.concatenate([jnp.zeros((LANES - DEC_SEQ, LANES), F32), new_rows], axis=0)
    new_t = padded.T
    keep = lax.broadcasted_iota(jnp.int32, (LANES, LANES), 1) < LANES - DEC_SEQ
    rotated = lambda j: pltpu.roll(src_ref[0, s, heads, :, j * LANES:(j + 1) * LANES].reshape(LANES, LANES),
                                   LANES - DEC_SEQ, axis=1)
    n_tiles = lb // LANES
    cur = rotated(0)
    for j in range(n_tiles):
        nxt = rotated(j + 1) if j + 1 < n_tiles else new_t
        dst_ref[0, s, heads, :, j * LANES:(j + 1) * LANES] = jnp.where(keep, cur, nxt).reshape(2, HEAD_DIM, LANES)
        cur = nxt


SAMPLE_SEQS_PER_STEP = 2


def _sample_attn_body(*refs, aliased, n_seq):
    for s in range(n_seq):
        _sample_attn_one(refs, aliased, s)


def _sample_attn_one(refs, aliased, s):
    sa_ref, sb_ref, sink_ref = refs[:3]
    cache_refs = refs[3:11]
    pos = 11 + (8 if aliased else 0)
    oa_ref, ob_ref = refs[pos:pos + 2]
    out_refs = refs[pos + 2:pos + 10]
    lo8 = _low_lanes(DEC_SEQ)
    seq_rows = slice(s * DEC_SEQ, (s + 1) * DEC_SEQ)

    o_g, l_g = [], []
    for g, (window, dil) in enumerate(A_GROUPS):
        kc_ref, vc_ref, ko_ref, vo_ref = cache_refs[2 * g], cache_refs[2 * g + 1], out_refs[2 * g], out_refs[2 * g + 1]
        lb = kc_ref.shape[-1]
        masks = _sample_masks(2 * DEC_SEQ, lb, window, dil)
        o_pairs, l_pairs = [], []
        for c in range(A_W // LANES):
            col = lambda part: slice(part * A_QKV + g * A_W + c * LANES, part * A_QKV + g * A_W + (c + 1) * LANES)
            heads = slice(2 * c, 2 * c + 2)
            q = sa_ref[seq_rows, col(0)] * SCALE
            kn, vn = sa_ref[seq_rows, col(1)], sa_ref[seq_rows, col(2)]
            kt = kc_ref[0, s, heads].reshape(LANES, lb)
            vt = vc_ref[0, s, heads].reshape(LANES, lb)
            q2 = jnp.concatenate([jnp.where(lo8, q, 0.0), jnp.where(lo8, 0.0, q)], axis=0)
            o, lse = _sample_attend(q2, kt.astype(BF16), vt.astype(BF16), kn, vn, masks)
            o_pairs.append(jnp.where(lo8, o[0:DEC_SEQ], o[DEC_SEQ:]))
            l_pairs.append(jnp.where(lo8, lse[0:DEC_SEQ], lse[DEC_SEQ:]))
            _write_shifted_cache(ko_ref, kc_ref, s, heads, kn, lb)
            _write_shifted_cache(vo_ref, vc_ref, s, heads, vn, lb)
        o_g.append(o_pairs)
        l_g.append(l_pairs)
    for c in range(A_W // LANES):
        l1, l2, l3 = l_g[0][c], l_g[1][c], l_g[2][c]
        lm = jnp.maximum(jnp.maximum(l1, l2), l3)
        e1, e2, e3 = jnp.exp(l1 - lm), jnp.exp(l2 - lm), jnp.exp(l3 - lm)
        es = e1 + e2 + e3
        oa_ref[seq_rows, c * LANES:(c + 1) * LANES] = ((e1 / es) * o_g[0][c] + (e2 / es) * o_g[1][c]
                                                       + (e3 / es) * o_g[2][c])

    kc_ref, vc_ref, ko_ref, vo_ref = cache_refs[6], cache_refs[7], out_refs[6], out_refs[7]
    lb = kc_ref.shape[-1]
    rows = B_GROUP * DEC_SEQ
    masks = _sample_masks(rows, lb, B_WINDOW, 1)
    both = slice(0, B_KV_HEADS)
    ktb = kc_ref[0, s].reshape(LANES, lb).astype(BF16)
    vtb = vc_ref[0, s].reshape(LANES, lb).astype(BF16)
    kn, vn = sb_ref[seq_rows, B_QW:B_QW + B_KW], sb_ref[seq_rows, B_QW + B_KW:SLAB_B]
    halves = {}
    for kv in range(B_KV_HEADS):
        parts = []
        for gq in range(B_GROUP):
            head = kv * B_GROUP + gq
            q = sb_ref[seq_rows, (head // 2) * LANES:(head // 2 + 1) * LANES] * SCALE
            q = jnp.where(lo8, q, 0.0) if head % 2 == 0 else jnp.where(lo8, 0.0, q)
            parts.append(q if head % 2 == kv else pltpu.roll(q, HEAD_DIM, axis=1))
        q4 = jnp.concatenate(parts, axis=0)
        sink = sink_ref[kv * rows:(kv + 1) * rows, 0:1]
        o, _ = _sample_attend(q4, ktb, vtb, kn, vn, masks, sink)
        for gq in range(B_GROUP):
            head = kv * B_GROUP + gq
            oh = o[gq * DEC_SEQ:(gq + 1) * DEC_SEQ]
            halves[head] = oh if head % 2 == kv else pltpu.roll(oh, HEAD_DIM, axis=1)
    for c in range(B_QW // LANES):
        ob_ref[seq_rows, c * LANES:(c + 1) * LANES] = jnp.where(lo8, halves[2 * c], halves[2 * c + 1])
    _write_shifted_cache(ko_ref, kc_ref, s, both, kn, lb)
    _write_shifted_cache(vo_ref, vc_ref, s, both, vn, lb)


def _sample_attn(layer, slab_a, slab_b, sinks, caches, prev):
    aliased = prev is not None
    n_seq = SAMPLE_SEQS_PER_STEP
    rows = n_seq * DEC_SEQ
    row = lambda b: (b, 0)
    cspec = lambda c: pl.BlockSpec((1, n_seq) + c.shape[2:], lambda b: (layer, b, 0, 0, 0))
    in_specs = [pl.BlockSpec((rows, SLAB_A), row), pl.BlockSpec((rows, SLAB_B), row),
                _resident(sinks.shape[1:], layer)]
    in_specs += [cspec(c) for c in caches]
    args = [slab_a, slab_b, sinks] + list(caches)
    aliases = {}
    if aliased:
        aliases = {len(args) + j: 2 + j for j in range(8)}
        in_specs += [pl.BlockSpec(memory_space=pl.ANY)] * 8
        args += list(prev)
    m = slab_a.shape[0]
    res = pl.pallas_call(
        functools.partial(_sample_attn_body, aliased=aliased, n_seq=n_seq),
        grid=(DEC_BATCH // n_seq,),
        in_specs=in_specs,
        out_specs=[pl.BlockSpec((rows, A_W), row), pl.BlockSpec((rows, B_QW), row)] + [cspec(c) for c in caches],
        out_shape=[jax.ShapeDtypeStruct((m, A_W), F32), jax.ShapeDtypeStruct((m, B_QW), F32)]
                  + [jax.ShapeDtypeStruct(c.shape, F32) for c in caches],
        input_output_aliases=aliases,
        compiler_params=_params(1),
        name="sample_attn",
    )(*args)
    return res[0], res[1], list(res[2:])


def _ssm_body(u_ref, are_ref, aim_ref, b_ref, cbig_ref, dvec_ref, wglu_ref, x0re_ref, x0im_ref,
              od_ref, sre_ref, sim_ref, bu_ref, xs_ref, *, steps, pitch):
    @pl.when(pl.program_id(1) == 0)
    def _():
        sre_ref[0] = x0re_ref[0]
        sim_ref[0] = x0im_ref[0]

    n_re = D_NS // LANES
    tiles = lambda x: [x[:, j * LANES:(j + 1) * LANES] for j in range(x.shape[1] // LANES)]

    if pitch == steps:
        planes = [(slice(None), slice(0, SUBLANES * steps))]
    else:
        planes = [(s, slice(s * pitch, s * pitch + steps)) for s in range(SUBLANES)]
    load_u = lambda sel: u_ref[sel].reshape(-1, D_WIDTH)

    for sel, plane_rows in planes:
        bu = _mm(load_u(sel).astype(BF16), b_ref[...])
        for j, tile in enumerate(tiles(bu)):
            bu_ref[j, plane_rows, :] = tile
    a_re = are_ref[...]
    a_im = aim_ref[...]

    def step(t, carry):
        re, im = carry
        at_t = pl.ds(t, SUBLANES, stride=pitch)
        b_re = jnp.concatenate([bu_ref[j, at_t, :] for j in range(n_re)], axis=1)
        b_im = jnp.concatenate([bu_ref[n_re + j, at_t, :] for j in range(n_re)], axis=1)
        re, im = a_re * re - a_im * im + b_re, a_re * im + a_im * re + b_im
        for j, (tr, ti) in enumerate(zip(tiles(re), tiles(im))):
            xs_ref[j, at_t, :] = tr
            xs_ref[n_re + j, at_t, :] = ti
        return re, im

    re, im = lax.fori_loop(0, steps, step, (sre_ref[0], sim_ref[0]), unroll=min(steps, SUBLANES))
    sre_ref[0] = re
    sim_ref[0] = im
    for sel, plane_rows in planes:
        xs = jnp.concatenate([xs_ref[j, plane_rows, :] for j in range(2 * n_re)], axis=1)
        y = _mm(xs.astype(BF16), cbig_ref[...]) + dvec_ref[...] * load_u(sel)
        y = _mm(jax.nn.gelu(y).astype(BF16), wglu_ref[...])
        od = y[:, 0:D_WIDTH] * jax.nn.sigmoid(y[:, D_WIDTH:2 * D_WIDTH])
        od_ref[sel] = od.reshape((SUBLANES, steps, D_WIDTH) if pitch == steps else (steps, D_WIDTH))


def _ssm(slab_c, bsz, length, layer, a_re, a_im, bbig, cbig, dvec, wglu, x0_re, x0_im, steps):
    n = bsz // SUBLANES
    pitch = steps if steps <= SUBLANES else steps + SUBLANES
    view = slab_c.reshape(bsz, length, SLAB_C)
    u_spec = pl.BlockSpec((SUBLANES, steps, D_WIDTH), lambda i, c: (i, c, SLAB_C // D_WIDTH - 1))
    o_spec = pl.BlockSpec((SUBLANES, steps, D_WIDTH), lambda i, c: (i, c, 0))
    st_spec = pl.BlockSpec((1, SUBLANES, D_NS), lambda i, c: (i, 0, 0))
    st_shape = jax.ShapeDtypeStruct((n, SUBLANES, D_NS), F32)
    od, s_re, s_im = pl.pallas_call(
        functools.partial(_ssm_body, steps=steps, pitch=pitch),
        grid=(n, length // steps),
        in_specs=[u_spec, _resident((SUBLANES, D_NS), layer), _resident((SUBLANES, D_NS), layer),
                  _resident((D_WIDTH, 2 * D_NS), layer), _resident((2 * D_NS, D_WIDTH), layer),
                  _resident((1, D_WIDTH), layer), _resident((D_WIDTH, 2 * D_WIDTH), layer), st_spec, st_spec],
        out_specs=[o_spec, st_spec, st_spec],
        out_shape=[jax.ShapeDtypeStruct((bsz, length, D_WIDTH), F32), st_shape, st_shape],
        scratch_shapes=[pltpu.VMEM((2 * D_NS // LANES, SUBLANES * pitch, LANES), F32)] * 2,
        compiler_params=_params(2),
        name="ssm",
    )(view, a_re, a_im, bbig, cbig, dvec, wglu, x0_re, x0_im)
    return od.reshape(bsz * length, D_WIDTH), s_re, s_im


def _merge_body(*refs, n_groups):
    x_ref = refs[0]
    a_refs = refs[1:1 + (2 * n_groups if n_groups > 1 else 1)]
    ob_ref, oc_ref, od_ref, gpre_ref, wgl_ref, wa_ref, wb_ref, wc_ref, wd_ref, wout_ref, gpost_ref, y_ref = \
        refs[1 + len(a_refs):]
    x = x_ref[...]
    h = _rmsnorm(x, gpre_ref[...]).astype(BF16)
    if n_groups > 1:
        lses = [a_refs[2 * g + 1][...] for g in range(n_groups)]
        lm = functools.reduce(jnp.maximum, lses)
        es = [jnp.exp(l - lm) for l in lses]
        den = functools.reduce(lambda a, b: a + b, es)
        o_a = functools.reduce(lambda a, b: a + b, [(es[g] / den) * a_refs[2 * g][...] for g in range(n_groups)])
    else:
        o_a = a_refs[0][...]
    branches = ((o_a, wa_ref), (ob_ref[...], wb_ref), (oc_ref[...], wc_ref), (od_ref[...], wd_ref))
    merged = None
    for j, (o, w_ref) in enumerate(branches):
        gate = jax.nn.sigmoid(_mm(h, wgl_ref[:, j * D_MODEL:(j + 1) * D_MODEL]))
        term = gate * _mm(o.astype(BF16), w_ref[...])
        merged = term if merged is None else merged + term
    mix = _mm(merged.astype(BF16), wout_ref[...])
    y_ref[...] = x + _rmsnorm(mix, gpost_ref[...])


def _merge(x, a_parts, ob, slab_c, od, layer, gpre, w_in, wa, wb, wc, wd, wout, gpost, tm):
    m = x.shape[0]
    row = lambda i: (i, 0)
    r256 = pl.BlockSpec((tm, 256), row)
    n_groups = max(1, len(a_parts) // 2)
    return pl.pallas_call(
        functools.partial(_merge_body, n_groups=n_groups),
        grid=(m // tm,),
        in_specs=[pl.BlockSpec((tm, D_MODEL), row)] + [r256] * len(a_parts)
                 + [pl.BlockSpec((tm, B_QW), row), r256, r256,
                    _resident((1, D_MODEL), layer), _resident((D_MODEL, N_BRANCH * D_MODEL), layer, col=1),
                    _resident((A_W, D_MODEL), layer), _resident((B_QW, D_MODEL), layer),
                    _resident((C_WIDTH, D_MODEL), layer), _resident((D_WIDTH, D_MODEL), layer),
                    _resident((D_MODEL, D_MODEL), layer), _resident((1, D_MODEL), layer)],
        out_specs=pl.BlockSpec((tm, D_MODEL), row),
        out_shape=jax.ShapeDtypeStruct((m, D_MODEL), F32),
        compiler_params=_params(1),
        name="merge",
    )(x, *a_parts, ob, slab_c, od, gpre, w_in, wa, wb, wc, wd, wout, gpost)


def _ffn_body(x_ref, p_ref, gpre_ref, wg_ref, wu_ref, wdn_ref, gpost_ref, wple_ref, wpg_ref, y_ref):
    x = x_ref[...]
    h = _rmsnorm(x, gpre_ref[...]).astype(BF16)
    act = jax.nn.silu(_mm(h, wg_ref[...])) * _mm(h, wu_ref[...])
    f = _mm(act.astype(BF16), wdn_ref[...])
    x = x + _rmsnorm(f, gpost_ref[...])
    gate = jax.nn.sigmoid(_mm(x.astype(BF16), wpg_ref[...]))
    y_ref[...] = x + gate * _mm(p_ref[...].astype(BF16), wple_ref[...])


def _ffn(x, p_all, layer, gpre, wg, wu, wdn, gpost, wple, wpg, tm):
    m = x.shape[0]
    row = lambda i: (i, 0)
    return pl.pallas_call(
        _ffn_body,
        grid=(m // tm,),
        in_specs=[pl.BlockSpec((tm, D_MODEL), row), pl.BlockSpec((None, tm, PLE_DIM), lambda i: (layer, i, 0)),
                  _resident((1, D_MODEL), layer), _resident((D_MODEL, D_FF), layer), _resident((D_MODEL, D_FF), layer),
                  _resident((D_FF, D_MODEL), layer), _resident((1, D_MODEL), layer),
                  _resident((PLE_DIM, D_MODEL), layer), _resident((D_MODEL, D_MODEL), layer)],
        out_specs=pl.BlockSpec((tm, D_MODEL), row),
        out_shape=jax.ShapeDtypeStruct((m, D_MODEL), F32),
        compiler_params=_params(1),
        name="ffn_ple",
    )(x, p_all, gpre, wg, wu, wdn, gpost, wple, wpg)


def _rope_tables(pos):
    inv = ROPE_THETA ** (-jnp.arange(HALF, dtype=F32) / HALF)
    ang = pos.astype(F32)[:, None] * inv[None, :]
    cos, sin = jnp.cos(ang), jnp.sin(ang)
    cos_t = jnp.tile(cos, (1, LANES // HALF))
    sin_t = jnp.tile(jnp.concatenate([-sin, sin], axis=1), (1, LANES // HEAD_DIM))
    return cos_t, sin_t


def _ssm_weights(lam_re, lam_im, log_dt, b_re, b_im, c_re, c_im):
    lam = lax.complex(lam_re, lam_im)
    dt = jnp.exp(log_dt)[..., None]
    a_bar = jnp.exp(lam * dt)
    b_bar = ((a_bar - 1.0) / lam)[..., None] * lax.complex(b_re, b_im)
    eye = jnp.eye(D_NGROUPS, dtype=F32)
    pack_b = lambda b: jnp.einsum('lgni,gh->lgihn', b, eye).reshape(DEPTH, D_WIDTH, D_NS)
    pack_c = lambda c: jnp.einsum('lgin,gh->lgnhi', c, eye).reshape(DEPTH, D_NS, D_WIDTH)
    bbig = jnp.concatenate([pack_b(jnp.real(b_bar)), pack_b(jnp.imag(b_bar))], axis=2)
    cbig = jnp.concatenate([pack_c(c_re), -pack_c(c_im)], axis=1)
    tile8 = lambda a: jnp.broadcast_to(a.reshape(DEPTH, 1, D_NS), (DEPTH, SUBLANES, D_NS))
    return tile8(jnp.real(a_bar)), tile8(jnp.imag(a_bar)), bbig.astype(BF16), cbig.astype(BF16)


def _key_major(c):
    return jnp.transpose(c, (0, 1, 3, 4, 2))


def _row_major(c):
    return jnp.transpose(c, (0, 1, 4, 2, 3))


def kernel(x_prompt, x_sample, p_prompt, p_sample, cache_a1_k, cache_a1_v, cache_a2_k, cache_a2_v, cache_a3_k, cache_a3_v, cache_b_k, cache_b_v, state_c_conv, state_d_re, state_d_im, norm_mix_pre, norm_mix_post, norm_ffn_pre, norm_ffn_post, w_in, attn_sinks, conv_c_w, ssm_lam_re, ssm_lam_im, ssm_log_dt, ssm_b_re, ssm_b_im, ssm_c_re, ssm_c_im, ssm_d, w_d_glu, w_br_a, w_br_b, w_br_c, w_br_d, w_out, w_ffn_gate, w_ffn_up, w_ffn_down, w_ple, w_ple_gate):
    assert x_prompt.shape == (BATCH, SEQ, D_MODEL) and x_sample.shape == (DEC_BATCH, DEC_SEQ, D_MODEL)
    assert w_in.shape == (DEPTH, D_MODEL, 2 * MIX_W)
    assert all(min(w, PAST_LEN) == w for w, _ in A_GROUPS) and min(B_WINDOW, PAST_LEN) == B_WINDOW
    mp, ms = BATCH * SEQ, DEC_BATCH * DEC_SEQ
    tm_p, tm_s = 512, 256

    cos_p, sin_p = _rope_tables(jnp.arange(SEQ, dtype=jnp.int32))
    cos_s, sin_s = _rope_tables(PAST_LEN + jnp.arange(DEC_SEQ, dtype=jnp.int32))
    cos_s, sin_s = jnp.tile(cos_s, (tm_s // DEC_SEQ, 1)), jnp.tile(sin_s, (tm_s // DEC_SEQ, 1))

    caches = [_key_major(c) for c in (cache_a1_k, cache_a1_v, cache_a2_k, cache_a2_v, cache_a3_k, cache_a3_v,
                                      cache_b_k, cache_b_v)]
    yp = x_prompt.reshape(mp, D_MODEL)
    ys = x_sample.reshape(ms, D_MODEL)
    new_caches = None
    prompt_caches = None
    st_p = [[] for _ in range(3)]
    conv_s, dre_s, dim_s = [], [], []

    rows3 = lambda a: a.reshape(DEPTH, 1, -1)
    w_in_b = w_in.astype(BF16)
    g_mix_pre = rows3(norm_mix_pre)
    ssm_w = _ssm_weights(ssm_lam_re, ssm_lam_im, ssm_log_dt, ssm_b_re, ssm_b_im, ssm_c_re, ssm_c_im) + (
        rows3(ssm_d), w_d_glu.astype(BF16))
    sinks_p = jnp.broadcast_to(attn_sinks.reshape(DEPTH, B_Q_HEADS, 1), (DEPTH, B_Q_HEADS, LANES))
    sinks_s = jnp.broadcast_to(jnp.repeat(attn_sinks.reshape(DEPTH, B_Q_HEADS), DEC_SEQ, axis=1)[..., None],
                               (DEPTH, B_Q_HEADS * DEC_SEQ, LANES))
    merge_w = (g_mix_pre, w_in_b, w_br_a.astype(BF16), w_br_b.astype(BF16), w_br_c.astype(BF16),
               w_br_d.astype(BF16), w_out.astype(BF16), rows3(norm_mix_post))
    ffn_w = (rows3(norm_ffn_pre), w_ffn_gate.astype(BF16), w_ffn_up.astype(BF16), w_ffn_down.astype(BF16),
             rows3(norm_ffn_post), w_ple.astype(BF16), w_ple_gate.astype(BF16))
    zero_state = jnp.zeros((BATCH // SUBLANES, SUBLANES, D_NS), F32)

    for i in range(DEPTH):
        sa, sb, sc, tail, *prompt_caches = _inproj(yp, i, g_mix_pre, w_in_b, cos_p, sin_p, conv_c_w, tm_p,
                                                   SEQ // tm_p, SEQ // tm_p, prev_caches=prompt_caches)
        a_parts = []
        for g, (w, d) in enumerate(A_GROUPS):
            a_parts += list(_attn_a_prompt(sa, g, d))
        ob = _attn_b_prompt(sb, i, sinks_p)
        od, s_re, s_im = _ssm(sc, BATCH, SEQ, i, *ssm_w, zero_state, zero_state, 128)
        x1 = _merge(yp, a_parts, ob, sc, od, i, *merge_w, tm_p)
        yp = _ffn(x1, p_prompt.reshape(DEPTH, mp, PLE_DIM), i, *ffn_w, tm_p)
        st_p[0].append(tail.reshape(BATCH, SEQ // tm_p, SUBLANES, C_WIDTH)[:, -1, SUBLANES - (C_CONV - 1):])
        st_p[1].append(s_re.reshape(BATCH, D_NGROUPS, D_STATE))
        st_p[2].append(s_im.reshape(BATCH, D_NGROUPS, D_STATE))

        s1 = jnp.broadcast_to(state_c_conv[i][:, None, 1, :], (DEC_BATCH, DEC_SEQ, C_WIDTH)).reshape(ms, C_WIDTH)
        s2 = jnp.broadcast_to(state_c_conv[i][:, None, 0, :], (DEC_BATCH, DEC_SEQ, C_WIDTH)).reshape(ms, C_WIDTH)
        sa, sb, sc, tail = _inproj(ys, i, g_mix_pre, w_in_b, cos_s, sin_s, conv_c_w, tm_s, 1, 1, row_state=(s1, s2))
        o_a, ob, new_caches = _sample_attn(i, sa, sb, sinks_s, caches, new_caches)
        x0_re = state_d_re[i].reshape(DEC_BATCH // SUBLANES, SUBLANES, D_NS)
        x0_im = state_d_im[i].reshape(DEC_BATCH // SUBLANES, SUBLANES, D_NS)
        od, s_re, s_im = _ssm(sc, DEC_BATCH, DEC_SEQ, i, *ssm_w, x0_re, x0_im, DEC_SEQ)
        x1 = _merge(ys, [o_a], ob, sc, od, i, *merge_w, tm_s)
        ys = _ffn(x1, p_sample.reshape(DEPTH, ms, PLE_DIM), i, *ffn_w, tm_s)
        conv_s.append(tail.reshape(DEC_BATCH, DEC_SEQ, C_WIDTH)[:, DEC_SEQ - (C_CONV - 1):])
        dre_s.append(s_re.reshape(DEC_BATCH, D_NGROUPS, D_STATE))
        dim_s.append(s_im.reshape(DEC_BATCH, D_NGROUPS, D_STATE))

    heads_of = lambda c: c.reshape(c.shape[:2] + (c.shape[2] // HEAD_DIM, HEAD_DIM, c.shape[3]))
    p_states = [_row_major(heads_of(c)) for c in prompt_caches] + [jnp.stack(s) for s in st_p]
    s_states = [_row_major(c) for c in new_caches] + [jnp.stack(conv_s), jnp.stack(dre_s), jnp.stack(dim_s)]
    out = [yp.reshape(BATCH, SEQ, D_MODEL), ys.reshape(DEC_BATCH, DEC_SEQ, D_MODEL)]
    for a, b in zip(p_states, s_states):
        out += [a, b]
    return tuple(out)
```

```python
import functools

import jax
import jax.numpy as jnp
from jax import lax
from jax.experimental import pallas as pl
from jax.experimental.pallas import tpu as pltpu

F32 = jnp.float32
BF16 = jnp.bfloat16

D_MODEL = 1024
BATCH = 8
SEQ = 2048
DEPTH = 2
DEC_BATCH = 128
DEC_SEQ = 8
PAST_LEN = 16384
HEAD_DIM = 64
HALF = HEAD_DIM // 2
ROPE_THETA = 10000.0
BLOCK = 128
EPS = 1e-6
NEG_INF = -1e30
A_HEADS = 4
A_GROUPS = ((128, 1), (512, 4), (2048, 16))
A_NG = len(A_GROUPS)
A_W = A_HEADS * HEAD_DIM
A_QKV = A_NG * A_W
B_Q_HEADS = 8
B_KV_HEADS = 2
B_GROUP = B_Q_HEADS // B_KV_HEADS
B_WINDOW = 128
B_QW = B_Q_HEADS * HEAD_DIM
B_KW = B_KV_HEADS * HEAD_DIM
C_WIDTH = 256
C_CONV = 3
D_GROUP = 16
D_NGROUPS = 16
D_WIDTH = 256
D_STATE = 64
D_NS = D_NGROUPS * D_STATE
N_BRANCH = 4
D_FF = 2816
PLE_DIM = 256
MIX_W = 4096
SCALE = HEAD_DIM ** -0.5
N_CACHES = 2 * A_NG + 2

LANES = 128
SUBLANES = 8
VMEM_LIMIT = 56 * 1024 * 1024

SLAB_A = 3 * A_QKV
SLAB_B = B_QW + 2 * B_KW
SLAB_C = C_WIDTH + D_WIDTH
ROPE_A = 2 * A_QKV
ROPE_B = B_QW + B_KW


def _params(n_axes):
    return pltpu.CompilerParams(dimension_semantics=("arbitrary",) * n_axes, vmem_limit_bytes=VMEM_LIMIT)


def _resident(shape, layer, col=0):
    return pl.BlockSpec((None,) + tuple(shape), lambda *_: (layer, 0, col), pipeline_mode=pl.Buffered(1))


def _rmsnorm(x, g):
    return x * lax.rsqrt(jnp.mean(x * x, axis=-1, keepdims=True) + EPS) * g


def _mm(a, b):
    return jnp.dot(a, b, preferred_element_type=F32)


def _mm_nt(a, b):
    return lax.dot_general(a, b, (((1,), (1,)), ((), ())), preferred_element_type=F32)


def _row_halves(tm):
    if tm >= 512:
        return (slice(0, tm // 2), slice(tm // 2, tm))
    return (slice(0, tm),)


def _low_lanes(rows):
    return lax.broadcasted_iota(jnp.int32, (rows, LANES), 1) < HEAD_DIM


def _inproj_body(*refs, seq_tiles, per_row_state, aliased):
    if per_row_state:
        x_ref, g_ref, w_ref, cos_ref, sin_ref, cw_ref, s1_ref, s2_ref, a_ref, b_ref, c_ref, tail_ref = refs
    else:
        x_ref, g_ref, w_ref, cos_ref, sin_ref, cw_ref = refs[:6]
        n_in = 6 + (N_CACHES if aliased else 0)
        a_ref, b_ref, c_ref, tail_ref = refs[n_in:n_in + 4]
        cache_refs = refs[n_in + 4:n_in + 4 + N_CACHES]
        carry_ref = refs[n_in + 4 + N_CACHES]
    tm = x_ref.shape[0]
    a_halves, b_halves, z_halves = [], [], []
    for rows in _row_halves(tm):
        h = _rmsnorm(x_ref[rows, :], g_ref[...]).astype(BF16)
        cos = cos_ref[rows, :]
        sin = sin_ref[rows, :]
        lane = lax.broadcasted_iota(jnp.int32, cos.shape, 1)
        first_half = (lane & (HEAD_DIM - 1)) < HALF

        def rope(z):
            partner = jnp.where(first_half, pltpu.roll(z, LANES - HALF, axis=1), pltpu.roll(z, HALF, axis=1))
            return z * cos + partner * sin

        za = _mm(h, w_ref[:, 0:SLAB_A])
        chunks = []
        for c in range(SLAB_A // LANES):
            blk = za[:, c * LANES:(c + 1) * LANES]
            chunks.append(rope(blk) if c * LANES < ROPE_A else blk)
            a_ref[rows, c * LANES:(c + 1) * LANES] = chunks[-1]
        a_halves.append(chunks)
        zb = _mm(h, w_ref[:, SLAB_A:SLAB_A + SLAB_B])
        chunks = []
        for c in range(SLAB_B // LANES):
            blk = zb[:, c * LANES:(c + 1) * LANES]
            chunks.append(rope(blk) if c * LANES < ROPE_B else blk)
            b_ref[rows, c * LANES:(c + 1) * LANES] = chunks[-1]
        b_halves.append(chunks)
        z_halves.append(_mm(h, w_ref[:, SLAB_A + SLAB_B:MIX_W]))
    whole = lambda halves, c: jnp.concatenate([chunks[c] for chunks in halves], axis=0)

    if not per_row_state:
        pairs = A_W // LANES
        chunk_of = lambda part, g, j: whole(a_halves, (part * A_QKV + g * A_W) // LANES + j)

        def write(ref, chunk, rows):
            ref[0, :, :] = jnp.concatenate([ch[tm - rows:tm, :].T for ch in chunk], axis=0)

        last_group = A_NG - 1
        assert A_GROUPS[last_group][0] == seq_tiles * tm and A_GROUPS[1][0] == tm and A_GROUPS[0][0] <= tm
        for part in (1, 2):
            write(cache_refs[2 * last_group + part - 1], [chunk_of(part, last_group, j) for j in range(pairs)], tm)

        @pl.when(pl.program_id(0) % seq_tiles == seq_tiles - 1)
        def _():
            for g in range(last_group):
                for part in (1, 2):
                    write(cache_refs[2 * g + part - 1], [chunk_of(part, g, j) for j in range(pairs)], A_GROUPS[g][0])
            write(cache_refs[6], [whole(b_halves, B_QW // LANES)], B_WINDOW)
            write(cache_refs[7], [whole(b_halves, B_QW // LANES + 1)], B_WINDOW)
    z = jnp.concatenate(z_halves, axis=0)
    zc = z[:, 2 * C_WIDTH:3 * C_WIDTH] * z[:, 0:C_WIDTH]

    row = lax.broadcasted_iota(jnp.int32, (tm, C_WIDTH), 0)
    z1 = pltpu.roll(zc, 1, axis=0)
    z2 = pltpu.roll(zc, 2, axis=0)
    if per_row_state:
        t = row & (DEC_SEQ - 1)
        z1 = jnp.where(t == 0, s1_ref[...], z1)
        z2 = jnp.where(t == 0, s2_ref[...], jnp.where(t == 1, s1_ref[...], z2))
        tail_ref[...] = zc
    else:
        @pl.when(pl.program_id(0) % seq_tiles == 0)
        def _():
            carry_ref[...] = jnp.zeros(carry_ref.shape, F32)
        z1 = jnp.where(row == 0, carry_ref[SUBLANES - 1:SUBLANES, :], z1)
        z2 = jnp.where(row == 0, carry_ref[SUBLANES - 2:SUBLANES - 1, :],
                       jnp.where(row == 1, carry_ref[SUBLANES - 1:SUBLANES, :], z2))
        carry_ref[...] = zc[tm - SUBLANES:tm, :]
        tail_ref[...] = zc[tm - SUBLANES:tm, :]
    conv = cw_ref[0:1, :] * z2 + cw_ref[1:2, :] * z1 + cw_ref[2:3, :] * zc
    c_ref[:, 0:C_WIDTH] = z[:, C_WIDTH:2 * C_WIDTH] * conv
    c_ref[:, C_WIDTH:SLAB_C] = z[:, 3 * C_WIDTH:4 * C_WIDTH]


def _inproj(x, layer, g, w_in, cos, sin, convw, tm, table_blocks, seq_tiles, row_state=None, prev_caches=None):
    m = x.shape[0]
    per_row_state = row_state is not None
    aliased = prev_caches is not None
    row = lambda i: (i, 0)
    tab = lambda i: (i % table_blocks, 0)
    in_specs = [pl.BlockSpec((tm, D_MODEL), row), _resident((1, D_MODEL), layer), _resident((D_MODEL, MIX_W), layer),
                pl.BlockSpec((tm, LANES), tab), pl.BlockSpec((tm, LANES), tab), _resident((C_CONV, C_WIDTH), layer)]
    args = [x, g, w_in, cos, sin, convw]
    out_specs = [pl.BlockSpec((tm, SLAB_A), row), pl.BlockSpec((tm, SLAB_B), row), pl.BlockSpec((tm, SLAB_C), row)]
    out_shape = [jax.ShapeDtypeStruct((m, SLAB_A), F32), jax.ShapeDtypeStruct((m, SLAB_B), F32),
                 jax.ShapeDtypeStruct((m, SLAB_C), F32)]
    scratch, aliases = [], {}
    if per_row_state:
        in_specs += [pl.BlockSpec((tm, C_WIDTH), row)] * 2
        args += list(row_state)
        out_specs.append(pl.BlockSpec((tm, C_WIDTH), row))
        out_shape.append(jax.ShapeDtypeStruct((m, C_WIDTH), F32))
    else:
        scratch = [pltpu.VMEM((SUBLANES, C_WIDTH), F32)]
        out_specs.append(pl.BlockSpec((SUBLANES, C_WIDTH), row))
        out_shape.append(jax.ShapeDtypeStruct((m // tm * SUBLANES, C_WIDTH), F32))
        n_seqs = m // (tm * seq_tiles)
        widths = [(A_W, w) for w, _ in A_GROUPS for _ in range(2)] + [(B_KW, B_WINDOW)] * 2
        for j, (rows_, window) in enumerate(widths):
            per_tile = window == tm * seq_tiles
            idx = ((lambda i: (layer, i // seq_tiles, 0, i % seq_tiles)) if per_tile
                   else (lambda i: (layer, i // seq_tiles, 0, 0)))
            out_specs.append(pl.BlockSpec((None, 1, rows_, tm if per_tile else window), idx))
            out_shape.append(jax.ShapeDtypeStruct((DEPTH, n_seqs, rows_, window), F32))
        if aliased:
            aliases = {len(args) + j: 4 + j for j in range(N_CACHES)}
            in_specs += [pl.BlockSpec(memory_space=pl.ANY)] * N_CACHES
            args += list(prev_caches)
    return pl.pallas_call(
        functools.partial(_inproj_body, seq_tiles=seq_tiles, per_row_state=per_row_state, aliased=aliased),
        grid=(m // tm,),
        in_specs=in_specs,
        out_specs=out_specs,
        out_shape=out_shape,
        input_output_aliases=aliases,
        scratch_shapes=scratch,
        compiler_params=_params(1),
        name="in_proj",
    )(*args)


def _band_mask_t(nk):
    kj = lax.broadcasted_iota(jnp.int32, (nk, BLOCK), 0)
    qi = lax.broadcasted_iota(jnp.int32, (nk, BLOCK), 1)
    dist = qi + (nk - BLOCK) - kj
    return (dist >= 0) & (dist <= BLOCK)


def _softmax_t(s, mask, sink=None):
    s = jnp.where(mask, s, NEG_INF)
    m = jnp.max(s, axis=0, keepdims=True)
    if sink is not None:
        m = jnp.maximum(m, sink)
    p = jnp.exp(s - m)
    den = jnp.sum(p, axis=0, keepdims=True)
    if sink is not None:
        den = den + jnp.exp(sink - m)
    return p, 1.0 / den, m + jnp.log(den)


def _attn_a_prompt_body(q_ref, k_ref, v_ref, o_ref, lse_ref, *, d):
    nb = SEQ // d // BLOCK
    pairs = q_ref.shape[-1] // LANES
    first_head_rows = lax.broadcasted_iota(jnp.int32, (LANES, BLOCK), 0) < HEAD_DIM

    def rows(start, n):
        return pl.ds(start, n) if d == 1 else pl.ds(start, n, stride=d)

    def block(q0, k0, nk):
        mask = _band_mask_t(nk)
        lo = _low_lanes(nk)
        for hp in range(pairs):
            cols = slice(hp * LANES, (hp + 1) * LANES)
            q = (q_ref[0, rows(q0, BLOCK), cols] * SCALE).astype(BF16)
            k = k_ref[0, rows(k0, nk), cols]
            v = v_ref[0, rows(k0, nk), cols]
            k2 = jnp.concatenate([jnp.where(lo, k, 0.0), jnp.where(lo, 0.0, k)], axis=0).astype(BF16)
            st = _mm_nt(k2, q)
            p0, r0, l0 = _softmax_t(st[0:nk], mask)
            p1, r1, l1 = _softmax_t(st[nk:2 * nk], mask)
            ot = _mm(v.T.astype(BF16), jnp.concatenate([p0, p1], axis=1).astype(BF16))
            o_t = jnp.where(first_head_rows, ot[:, 0:BLOCK] * r0, ot[:, BLOCK:2 * BLOCK] * r1)
            l_t = jnp.where(first_head_rows, jnp.broadcast_to(l0, (LANES, BLOCK)),
                            jnp.broadcast_to(l1, (LANES, BLOCK)))
            o_ref[0, rows(q0, BLOCK), cols] = o_t.T
            lse_ref[0, rows(q0, BLOCK), cols] = l_t.T

    for r in range(d):
        block(r, r, BLOCK)
        if nb > 1:
            def body(i, carry):
                q0 = r + i * (BLOCK * d)
                block(q0, q0 - BLOCK * d, 2 * BLOCK)
                return carry
            lax.fori_loop(1, nb, body, 0, unroll=min(nb - 1, 5))


def _attn_a_prompt(slab_a, g, d):
    view = slab_a.reshape(BATCH, SEQ, SLAB_A)
    width = A_W if d == 1 else LANES
    steps = A_W // width
    spec = lambda off: pl.BlockSpec((1, SEQ, width), lambda b, hp: (b, 0, off * steps + hp))
    oshape = jax.ShapeDtypeStruct((BATCH, SEQ, A_W), F32)
    o, lse = pl.pallas_call(
        functools.partial(_attn_a_prompt_body, d=d),
        grid=(BATCH, steps),
        in_specs=[spec(g), spec(A_NG + g), spec(2 * A_NG + g)],
        out_specs=[spec(0), spec(0)],
        out_shape=[oshape, oshape],
        compiler_params=_params(2),
        name=f"attn_a{g + 1}_prompt",
    )(view, view, view)
    return o.reshape(BATCH * SEQ, A_W), lse.reshape(BATCH * SEQ, A_W)


def _attn_b_prompt_body(q_ref, k_ref, v_ref, sink_ref, o_ref, *, nb):
    def block(q0, k0, nk):
        mask = _band_mask_t(nk)
        lo = _low_lanes(nk)
        k = k_ref[0, pl.ds(k0, nk), :]
        k_sw = pltpu.roll(k, HEAD_DIM, axis=1)
        vt = v_ref[0, pl.ds(k0, nk), :].T.astype(BF16)
        for kv in range(B_KV_HEADS):
            on_lo, on_hi = (k, k_sw) if kv == 0 else (k_sw, k)
            k2 = jnp.concatenate([jnp.where(lo, on_lo, 0.0), jnp.where(lo, 0.0, on_hi)], axis=0).astype(BF16)
            cols = slice(2 * kv * LANES, (2 * kv + 2) * LANES)
            q2 = q_ref[0, pl.ds(q0, BLOCK), cols] * SCALE
            q2 = jnp.concatenate([q2[:, 0:LANES], q2[:, LANES:2 * LANES]], axis=0).astype(BF16)
            st = _mm_nt(k2, q2)
            ps, rs = [], []
            for g in range(B_GROUP):
                half, chunk = g % 2, g // 2
                sink = sink_ref[kv * B_GROUP + g:kv * B_GROUP + g + 1, :]
                p, r, _ = _softmax_t(st[half * nk:(half + 1) * nk, chunk * BLOCK:(chunk + 1) * BLOCK], mask, sink)
                ps.append(p)
                rs.append(r)
            ot = _mm(vt, jnp.concatenate(ps, axis=1).astype(BF16))
            oj = ot[kv * HEAD_DIM:(kv + 1) * HEAD_DIM, :] * jnp.concatenate(rs, axis=1)
            for chunk in range(2):
                o_t = jnp.concatenate([oj[:, (2 * chunk) * BLOCK:(2 * chunk + 1) * BLOCK],
                                       oj[:, (2 * chunk + 1) * BLOCK:(2 * chunk + 2) * BLOCK]], axis=0)
                c = 2 * kv + chunk
                o_ref[0, pl.ds(q0, BLOCK), c * LANES:(c + 1) * LANES] = o_t.T

    block(0, 0, BLOCK)

    def body(i, carry):
        q0 = pl.multiple_of(i * BLOCK, BLOCK)
        block(q0, pl.multiple_of(q0 - BLOCK, BLOCK), 2 * BLOCK)
        return carry
    lax.fori_loop(1, nb, body, 0, unroll=5)


def _attn_b_prompt(slab_b, layer, sinks):
    view = slab_b.reshape(BATCH, SEQ, SLAB_B)
    nq = B_QW // LANES
    o = pl.pallas_call(
        functools.partial(_attn_b_prompt_body, nb=SEQ // BLOCK),
        grid=(BATCH,),
        in_specs=[pl.BlockSpec((1, SEQ, B_QW), lambda b: (b, 0, 0)),
                  pl.BlockSpec((1, SEQ, B_KW), lambda b: (b, 0, nq)),
                  pl.BlockSpec((1, SEQ, B_KW), lambda b: (b, 0, nq + 1)),
                  _resident((B_Q_HEADS, LANES), layer)],
        out_specs=pl.BlockSpec((1, SEQ, B_QW), lambda b: (b, 0, 0)),
        out_shape=jax.ShapeDtypeStruct((BATCH, SEQ, B_QW), F32),
        compiler_params=_params(1),
        name="attn_b_prompt",
    )(view, view, view, sinks)
    return o.reshape(BATCH * SEQ, B_QW)


def _sample_masks(rows, lb, window, dil):
    assert dil & (dil - 1) == 0 and DEC_SEQ & (DEC_SEQ - 1) == 0
    t_c = lax.broadcasted_iota(jnp.int32, (rows, lb), 0) & (DEC_SEQ - 1)
    dist_c = lb + t_c - lax.broadcasted_iota(jnp.int32, (rows, lb), 1)
    mask_c = (dist_c >= 0) & (dist_c <= window) & ((dist_c & (dil - 1)) == 0)
    t_n = lax.broadcasted_iota(jnp.int32, (rows, DEC_SEQ), 0) & (DEC_SEQ - 1)
    dist_n = t_n - lax.broadcasted_iota(jnp.int32, (rows, DEC_SEQ), 1)
    mask_n = (dist_n >= 0) & (dist_n <= window) & ((dist_n & (dil - 1)) == 0)
    return mask_c, mask_n


def _sample_attend(q, kt, vt, kn, vn, masks, sink=None):
    mask_c, mask_n = masks
    s_c = jnp.where(mask_c, _mm(q.astype(BF16), kt), NEG_INF)
    s_n = jnp.where(mask_n, _mm_nt(q, kn), NEG_INF)
    m = jnp.maximum(jnp.max(s_c, axis=-1, keepdims=True), jnp.max(s_n, axis=-1, keepdims=True))
    if sink is not None:
        m = jnp.maximum(m, sink)
    p_c = jnp.exp(s_c - m)
    p_n = jnp.exp(s_n - m)
    den = jnp.sum(p_c, axis=-1, keepdims=True) + jnp.sum(p_n, axis=-1, keepdims=True)
    if sink is not None:
        den = den + jnp.exp(sink - m)
    o = _mm_nt(p_c.astype(BF16), vt) + _mm(p_n, vn)
    return o * (1.0 / den), m + jnp.log(den)


def _write_shifted_cache(dst_ref, src_ref, s, heads, new_rows, lb):
    padded = jnp.concatenate([jnp.zeros((LANES - DEC_SEQ, LANES), F32), new_rows], axis=0)
    new_t = padded.T
    keep = lax.broadcasted_iota(jnp.int32, (LANES, LANES), 1) < LANES - DEC_SEQ
    rotated = lambda j: pltpu.roll(src_ref[0, s, heads, :, j * LANES:(j + 1) * LANES].reshape(LANES, LANES),
                                   LANES - DEC_SEQ, axis=1)
    n_tiles = lb // LANES
    cur = rotated(0)
    for j in range(n_tiles):
        nxt = rotated(j + 1) if j + 1 < n_tiles else new_t
        dst_ref[0, s, heads, :, j * LANES:(j + 1) * LANES] = jnp.where(keep, cur, nxt).reshape(2, HEAD_DIM, LANES)
        cur = nxt


SAMPLE_SEQS_PER_STEP = 2


def _sample_attn_body(*refs, aliased, n_seq):
    for s in range(n_seq):
        _sample_attn_one(refs, aliased, s)


def _sample_attn_one(refs, aliased, s):
    sa_ref, sb_ref, sink_ref = refs[:3]
    cache_refs = refs[3:11]
    pos = 11 + (8 if aliased else 0)
    oa_ref, ob_ref = refs[pos:pos + 2]
    out_refs = refs[pos + 2:pos + 10]
    lo8 = _low_lanes(DEC_SEQ)
    seq_rows = slice(s * DEC_SEQ, (s + 1) * DEC_SEQ)

    o_g, l_g = [], []
    for g, (window, dil) in enumerate(A_GROUPS):
        kc_ref, vc_ref, ko_ref, vo_ref = cache_refs[2 * g], cache_refs[2 * g + 1], out_refs[2 * g], out_refs[2 * g + 1]
        lb = kc_ref.shape[-1]
        masks = _sample_masks(2 * DEC_SEQ, lb, window, dil)
        o_pairs, l_pairs = [], []
        for c in range(A_W // LANES):
            col = lambda part: slice(part * A_QKV + g * A_W + c * LANES, part * A_QKV + g * A_W + (c + 1) * LANES)
            heads = slice(2 * c, 2 * c + 2)
            q = sa_ref[seq_rows, col(0)] * SCALE
            kn, vn = sa_ref[seq_rows, col(1)], sa_ref[seq_rows, col(2)]
            kt = kc_ref[0, s, heads].reshape(LANES, lb)
            vt = vc_ref[0, s, heads].reshape(LANES, lb)
            q2 = jnp.concatenate([jnp.where(lo8, q, 0.0), jnp.where(lo8, 0.0, q)], axis=0)
            o, lse = _sample_attend(q2, kt.astype(BF16), vt.astype(BF16), kn, vn, masks)
            o_pairs.append(jnp.where(lo8, o[0:DEC_SEQ], o[DEC_SEQ:]))
            l_pairs.append(jnp.where(lo8, lse[0:DEC_SEQ], lse[DEC_SEQ:]))
            _write_shifted_cache(ko_ref, kc_ref, s, heads, kn, lb)
            _write_shifted_cache(vo_ref, vc_ref, s, heads, vn, lb)
        o_g.append(o_pairs)
        l_g.append(l_pairs)
    for c in range(A_W // LANES):
        l1, l2, l3 = l_g[0][c], l_g[1][c], l_g[2][c]
        lm = jnp.maximum(jnp.maximum(l1, l2), l3)
        e1, e2, e3 = jnp.exp(l1 - lm), jnp.exp(l2 - lm), jnp.exp(l3 - lm)
        es = e1 + e2 + e3
        oa_ref[seq_rows, c * LANES:(c + 1) * LANES] = ((e1 / es) * o_g[0][c] + (e2 / es) * o_g[1][c]
                                                       + (e3 / es) * o_g[2][c])

    kc_ref, vc_ref, ko_ref, vo_ref = cache_refs[6], cache_refs[7], out_refs[6], out_refs[7]
    lb = kc_ref.shape[-1]
    rows = B_GROUP * DEC_SEQ
    masks = _sample_masks(rows, lb, B_WINDOW, 1)
    both = slice(0, B_KV_HEADS)
    ktb = kc_ref[0, s].reshape(LANES, lb).astype(BF16)
    vtb = vc_ref[0, s].reshape(LANES, lb).astype(BF16)
    kn, vn = sb_ref[seq_rows, B_QW:B_QW + B_KW], sb_ref[seq_rows, B_QW + B_KW:SLAB_B]
    halves = {}
    for kv in range(B_KV_HEADS):
        parts = []
        for gq in range(B_GROUP):
            head = kv * B_GROUP + gq
            q = sb_ref[seq_rows, (head // 2) * LANES:(head // 2 + 1) * LANES] * SCALE
            q = jnp.where(lo8, q, 0.0) if head % 2 == 0 else jnp.where(lo8, 0.0, q)
            parts.append(q if head % 2 == kv else pltpu.roll(q, HEAD_DIM, axis=1))
        q4 = jnp.concatenate(parts, axis=0)
        sink = sink_ref[kv * rows:(kv + 1) * rows, 0:1]
        o, _ = _sample_attend(q4, ktb, vtb, kn, vn, masks, sink)
        for gq in range(B_GROUP):
            head = kv * B_GROUP + gq
            oh = o[gq * DEC_SEQ:(gq + 1) * DEC_SEQ]
            halves[head] = oh if head % 2 == kv else pltpu.roll(oh, HEAD_DIM, axis=1)
    for c in range(B_QW // LANES):
        ob_ref[seq_rows, c * LANES:(c + 1) * LANES] = jnp.where(lo8, halves[2 * c], halves[2 * c + 1])
    _write_shifted_cache(ko_ref, kc_ref, s, both, kn, lb)
    _write_shifted_cache(vo_ref, vc_ref, s, both, vn, lb)


def _sample_attn(layer, slab_a, slab_b, sinks, caches, prev):
    aliased = prev is not None
    n_seq = SAMPLE_SEQS_PER_STEP
    rows = n_seq * DEC_SEQ
    row = lambda b: (b, 0)
    cspec = lambda c: pl.BlockSpec((1, n_seq) + c.shape[2:], lambda b: (layer, b, 0, 0, 0))
    in_specs = [pl.BlockSpec((rows, SLAB_A), row), pl.BlockSpec((rows, SLAB_B), row),
                _resident(sinks.shape[1:], layer)]
    in_specs += [cspec(c) for c in caches]
    args = [slab_a, slab_b, sinks] + list(caches)
    aliases = {}
    if aliased:
        aliases = {len(args) + j: 2 + j for j in range(8)}
        in_specs += [pl.BlockSpec(memory_space=pl.ANY)] * 8
        args += list(prev)
    m = slab_a.shape[0]
    res = pl.pallas_call(
        functools.partial(_sample_attn_body, aliased=aliased, n_seq=n_seq),
        grid=(DEC_BATCH // n_seq,),
        in_specs=in_specs,
        out_specs=[pl.BlockSpec((rows, A_W), row), pl.BlockSpec((rows, B_QW), row)] + [cspec(c) for c in caches],
        out_shape=[jax.ShapeDtypeStruct((m, A_W), F32), jax.ShapeDtypeStruct((m, B_QW), F32)]
                  + [jax.ShapeDtypeStruct(c.shape, F32) for c in caches],
        input_output_aliases=aliases,
        compiler_params=_params(1),
        name="sample_attn",
    )(*args)
    return res[0], res[1], list(res[2:])


def _ssm_body(u_ref, are_ref, aim_ref, b_ref, cbig_ref, dvec_ref, wglu_ref, x0re_ref, x0im_ref,
              od_ref, sre_ref, sim_ref, bu_ref, xs_ref, *, steps, pitch):
    @pl.when(pl.program_id(1) == 0)
    def _():
        sre_ref[0] = x0re_ref[0]
        sim_ref[0] = x0im_ref[0]

    n_re = D_NS // LANES
    tiles = lambda x: [x[:, j * LANES:(j + 1) * LANES] for j in range(x.shape[1] // LANES)]

    plane = lambda s: slice(s * pitch, s * pitch + steps)
    u = u_ref[...].reshape(SUBLANES * steps, D_WIDTH)
    bu = _mm(u.astype(BF16), b_ref[...])
    for s in range(SUBLANES):
        for j, tile in enumerate(tiles(bu[s * steps:(s + 1) * steps])):
            bu_ref[j, plane(s), :] = tile
    a_re = are_ref[...]
    a_im = aim_ref[...]

    def step(t, carry):
        re, im = carry
        at_t = pl.ds(t, SUBLANES, stride=pitch)
        b_re = jnp.concatenate([bu_ref[j, at_t, :] for j in range(n_re)], axis=1)
        b_im = jnp.concatenate([bu_ref[n_re + j, at_t, :] for j in range(n_re)], axis=1)
        re, im = a_re * re - a_im * im + b_re, a_re * im + a_im * re + b_im
        for j, (tr, ti) in enumerate(zip(tiles(re), tiles(im))):
            xs_ref[j, at_t, :] = tr
            xs_ref[n_re + j, at_t, :] = ti
        return re, im

    re, im = lax.fori_loop(0, steps, step, (sre_ref[0], sim_ref[0]), unroll=min(steps, SUBLANES))
    sre_ref[0] = re
    sim_ref[0] = im
    xs = jnp.concatenate([jnp.concatenate([xs_ref[j, plane(s), :] for j in range(2 * n_re)], axis=1)
                          for s in range(SUBLANES)], axis=0)
    y = _mm(xs.astype(BF16), cbig_ref[...]) + dvec_ref[...] * u
    y = _mm(jax.nn.gelu(y).astype(BF16), wglu_ref[...])
    od = y[:, 0:D_WIDTH] * jax.nn.sigmoid(y[:, D_WIDTH:2 * D_WIDTH])
    od_ref[...] = od.reshape(SUBLANES, steps, D_WIDTH)


def _ssm(slab_c, bsz, length, layer, a_re, a_im, bbig, cbig, dvec, wglu, x0_re, x0_im, steps):
    n = bsz // SUBLANES
    pitch = steps if steps <= SUBLANES else steps + SUBLANES
    view = slab_c.reshape(bsz, length, SLAB_C)
    u_spec = pl.BlockSpec((SUBLANES, steps, D_WIDTH), lambda i, c: (i, c, SLAB_C // D_WIDTH - 1))
    o_spec = pl.BlockSpec((SUBLANES, steps, D_WIDTH), lambda i, c: (i, c, 0))
    st_spec = pl.BlockSpec((1, SUBLANES, D_NS), lambda i, c: (i, 0, 0))
    st_shape = jax.ShapeDtypeStruct((n, SUBLANES, D_NS), F32)
    od, s_re, s_im = pl.pallas_call(
        functools.partial(_ssm_body, steps=steps, pitch=pitch),
        grid=(n, length // steps),
        in_specs=[u_spec, _resident((SUBLANES, D_NS), layer), _resident((SUBLANES, D_NS), layer),
                  _resident((D_WIDTH, 2 * D_NS), layer), _resident((2 * D_NS, D_WIDTH), layer),
                  _resident((1, D_WIDTH), layer), _resident((D_WIDTH, 2 * D_WIDTH), layer), st_spec, st_spec],
        out_specs=[o_spec, st_spec, st_spec],
        out_shape=[jax.ShapeDtypeStruct((bsz, length, D_WIDTH), F32), st_shape, st_shape],
        scratch_shapes=[pltpu.VMEM((2 * D_NS // LANES, SUBLANES * pitch, LANES), F32)] * 2,
        compiler_params=_params(2),
        name="ssm",
    )(view, a_re, a_im, bbig, cbig, dvec, wglu, x0_re, x0_im)
    return od.reshape(bsz * length, D_WIDTH), s_re, s_im


def _merge_body(*refs, n_groups):
    x_ref = refs[0]
    a_refs = refs[1:1 + (2 * n_groups if n_groups > 1 else 1)]
    ob_ref, oc_ref, od_ref, gpre_ref, wgl_ref, wa_ref, wb_ref, wc_ref, wd_ref, wout_ref, gpost_ref, y_ref = \
        refs[1 + len(a_refs):]
    tm = x_ref.shape[0]
    for rows in _row_halves(tm):
        x = x_ref[rows, :]
        h = _rmsnorm(x, gpre_ref[...]).astype(BF16)
        if n_groups > 1:
            lses = [a_refs[2 * g + 1][rows, :] for g in range(n_groups)]
            lm = functools.reduce(jnp.maximum, lses)
            es = [jnp.exp(l - lm) for l in lses]
            den = functools.reduce(lambda a, b: a + b, es)
            o_a = functools.reduce(lambda a, b: a + b,
                                   [(es[g] / den) * a_refs[2 * g][rows, :] for g in range(n_groups)])
        else:
            o_a = a_refs[0][rows, :]
        branches = ((o_a, wa_ref), (ob_ref[rows, :], wb_ref), (oc_ref[rows, :], wc_ref), (od_ref[rows, :], wd_ref))
        merged = None
        for j, (o, w_ref) in enumerate(branches):
            gate = jax.nn.sigmoid(_mm(h, wgl_ref[:, j * D_MODEL:(j + 1) * D_MODEL]))
            term = gate * _mm(o.astype(BF16), w_ref[...])
            merged = term if merged is None else merged + term
        mix = _mm(merged.astype(BF16), wout_ref[...])
        y_ref[rows, :] = x + _rmsnorm(mix, gpost_ref[...])


def _merge(x, a_parts, ob, slab_c, od, layer, gpre, w_in, wa, wb, wc, wd, wout, gpost, tm):
    m = x.shape[0]
    row = lambda i: (i, 0)
    r256 = pl.BlockSpec((tm, 256), row)
    n_groups = max(1, len(a_parts) // 2)
    return pl.pallas_call(
        functools.partial(_merge_body, n_groups=n_groups),
        grid=(m // tm,),
        in_specs=[pl.BlockSpec((tm, D_MODEL), row)] + [r256] * len(a_parts)
                 + [pl.BlockSpec((tm, B_QW), row), r256, r256,
                    _resident((1, D_MODEL), layer), _resident((D_MODEL, N_BRANCH * D_MODEL), layer, col=1),
                    _resident((A_W, D_MODEL), layer), _resident((B_QW, D_MODEL), layer),
                    _resident((C_WIDTH, D_MODEL), layer), _resident((D_WIDTH, D_MODEL), layer),
                    _resident((D_MODEL, D_MODEL), layer), _resident((1, D_MODEL), layer)],
        out_specs=pl.BlockSpec((tm, D_MODEL), row),
        out_shape=jax.ShapeDtypeStruct((m, D_MODEL), F32),
        compiler_params=_params(1),
        name="merge",
    )(x, *a_parts, ob, slab_c, od, gpre, w_in, wa, wb, wc, wd, wout, gpost)


def _ffn_body(x_ref, p_ref, gpre_ref, wg_ref, wu_ref, wdn_ref, gpost_ref, wple_ref, wpg_ref, y_ref):
    x = x_ref[...]
    h = _rmsnorm(x, gpre_ref[...]).astype(BF16)
    act = jax.nn.silu(_mm(h, wg_ref[...])) * _mm(h, wu_ref[...])
    f = _mm(act.astype(BF16), wdn_ref[...])
    x = x + _rmsnorm(f, gpost_ref[...])
    gate = jax.nn.sigmoid(_mm(x.astype(BF16), wpg_ref[...]))
    y_ref[...] = x + gate * _mm(p_ref[...].astype(BF16), wple_ref[...])


def _ffn(x, p_all, layer, gpre, wg, wu, wdn, gpost, wple, wpg, tm):
    m = x.shape[0]
    row = lambda i: (i, 0)
    return pl.pallas_call(
        _ffn_body,
        grid=(m // tm,),
        in_specs=[pl.BlockSpec((tm, D_MODEL), row), pl.BlockSpec((None, tm, PLE_DIM), lambda i: (layer, i, 0)),
                  _resident((1, D_MODEL), layer), _resident((D_MODEL, D_FF), layer), _resident((D_MODEL, D_FF), layer),
                  _resident((D_FF, D_MODEL), layer), _resident((1, D_MODEL), layer),
                  _resident((PLE_DIM, D_MODEL), layer), _resident((D_MODEL, D_MODEL), layer)],
        out_specs=pl.BlockSpec((tm, D_MODEL), row),
        out_shape=jax.ShapeDtypeStruct((m, D_MODEL), F32),
        compiler_params=_params(1),
        name="ffn_ple",
    )(x, p_all, gpre, wg, wu, wdn, gpost, wple, wpg)


def _rope_tables(pos):
    inv = ROPE_THETA ** (-jnp.arange(HALF, dtype=F32) / HALF)
    ang = pos.astype(F32)[:, None] * inv[None, :]
    cos, sin = jnp.cos(ang), jnp.sin(ang)
    cos_t = jnp.tile(cos, (1, LANES // HALF))
    sin_t = jnp.tile(jnp.concatenate([-sin, sin], axis=1), (1, LANES // HEAD_DIM))
    return cos_t, sin_t


def _ssm_weights(lam_re, lam_im, log_dt, b_re, b_im, c_re, c_im):
    lam = lax.complex(lam_re, lam_im)
    dt = jnp.exp(log_dt)[..., None]
    a_bar = jnp.exp(lam * dt)
    b_bar = ((a_bar - 1.0) / lam)[..., None] * lax.complex(b_re, b_im)
    eye = jnp.eye(D_NGROUPS, dtype=F32)
    pack_b = lambda b: jnp.einsum('lgni,gh->lgihn', b, eye).reshape(DEPTH, D_WIDTH, D_NS)
    pack_c = lambda c: jnp.einsum('lgin,gh->lgnhi', c, eye).reshape(DEPTH, D_NS, D_WIDTH)
    bbig = jnp.concatenate([pack_b(jnp.real(b_bar)), pack_b(jnp.imag(b_bar))], axis=2)
    cbig = jnp.concatenate([pack_c(c_re), -pack_c(c_im)], axis=1)
    tile8 = lambda a: jnp.broadcast_to(a.reshape(DEPTH, 1, D_NS), (DEPTH, SUBLANES, D_NS))
    return tile8(jnp.real(a_bar)), tile8(jnp.imag(a_bar)), bbig.astype(BF16), cbig.astype(BF16)


def _key_major(c):
    return jnp.transpose(c, (0, 1, 3, 4, 2))


def _row_major(c):
    return jnp.transpose(c, (0, 1, 4, 2, 3))


def kernel(x_prompt, x_sample, p_prompt, p_sample, cache_a1_k, cache_a1_v, cache_a2_k, cache_a2_v, cache_a3_k, cache_a3_v, cache_b_k, cache_b_v, state_c_conv, state_d_re, state_d_im, norm_mix_pre, norm_mix_post, norm_ffn_pre, norm_ffn_post, w_in, attn_sinks, conv_c_w, ssm_lam_re, ssm_lam_im, ssm_log_dt, ssm_b_re, ssm_b_im, ssm_c_re, ssm_c_im, ssm_d, w_d_glu, w_br_a, w_br_b, w_br_c, w_br_d, w_out, w_ffn_gate, w_ffn_up, w_ffn_down, w_ple, w_ple_gate):
    assert x_prompt.shape == (BATCH, SEQ, D_MODEL) and x_sample.shape == (DEC_BATCH, DEC_SEQ, D_MODEL)
    assert w_in.shape == (DEPTH, D_MODEL, 2 * MIX_W)
    assert all(min(w, PAST_LEN) == w for w, _ in A_GROUPS) and min(B_WINDOW, PAST_LEN) == B_WINDOW
    mp, ms = BATCH * SEQ, DEC_BATCH * DEC_SEQ
    tm_p, tm_s = 512, 256

    cos_p, sin_p = _rope_tables(jnp.arange(SEQ, dtype=jnp.int32))
    cos_s, sin_s = _rope_tables(PAST_LEN + jnp.arange(DEC_SEQ, dtype=jnp.int32))
    cos_s, sin_s = jnp.tile(cos_s, (tm_s // DEC_SEQ, 1)), jnp.tile(sin_s, (tm_s // DEC_SEQ, 1))

    caches = [_key_major(c) for c in (cache_a1_k, cache_a1_v, cache_a2_k, cache_a2_v, cache_a3_k, cache_a3_v,
                                      cache_b_k, cache_b_v)]
    yp = x_prompt.reshape(mp, D_MODEL)
    ys = x_sample.reshape(ms, D_MODEL)
    new_caches = None
    prompt_caches = None
    st_p = [[] for _ in range(3)]
    conv_s, dre_s, dim_s = [], [], []

    rows3 = lambda a: a.reshape(DEPTH, 1, -1)
    w_in_b = w_in.astype(BF16)
    g_mix_pre = rows3(norm_mix_pre)
    ssm_w = _ssm_weights(ssm_lam_re, ssm_lam_im, ssm_log_dt, ssm_b_re, ssm_b_im, ssm_c_re, ssm_c_im) + (
        rows3(ssm_d), w_d_glu.astype(BF16))
    sinks_p = jnp.broadcast_to(attn_sinks.reshape(DEPTH, B_Q_HEADS, 1), (DEPTH, B_Q_HEADS, LANES))
    sinks_s = jnp.broadcast_to(jnp.repeat(attn_sinks.reshape(DEPTH, B_Q_HEADS), DEC_SEQ, axis=1)[..., None],
                               (DEPTH, B_Q_HEADS * DEC_SEQ, LANES))
    merge_w = (g_mix_pre, w_in_b, w_br_a.astype(BF16), w_br_b.astype(BF16), w_br_c.astype(BF16),
               w_br_d.astype(BF16), w_out.astype(BF16), rows3(norm_mix_post))
    ffn_w = (rows3(norm_ffn_pre), w_ffn_gate.astype(BF16), w_ffn_up.astype(BF16), w_ffn_down.astype(BF16),
             rows3(norm_ffn_post), w_ple.astype(BF16), w_ple_gate.astype(BF16))
    zero_state = jnp.zeros((BATCH // SUBLANES, SUBLANES, D_NS), F32)

    for i in range(DEPTH):
        sa, sb, sc, tail, *prompt_caches = _inproj(yp, i, g_mix_pre, w_in_b, cos_p, sin_p, conv_c_w, tm_p,
                                                   SEQ // tm_p, SEQ // tm_p, prev_caches=prompt_caches)
        a_parts = []
        for g, (w, d) in enumerate(A_GROUPS):
            a_parts += list(_attn_a_prompt(sa, g, d))
        ob = _attn_b_prompt(sb, i, sinks_p)
        od, s_re, s_im = _ssm(sc, BATCH, SEQ, i, *ssm_w, zero_state, zero_state, 128)
        x1 = _merge(yp, a_parts, ob, sc, od, i, *merge_w, tm_p)
        yp = _ffn(x1, p_prompt.reshape(DEPTH, mp, PLE_DIM), i, *ffn_w, tm_p)
        st_p[0].append(tail.reshape(BATCH, SEQ // tm_p, SUBLANES, C_WIDTH)[:, -1, SUBLANES - (C_CONV - 1):])
        st_p[1].append(s_re.reshape(BATCH, D_NGROUPS, D_STATE))
        st_p[2].append(s_im.reshape(BATCH, D_NGROUPS, D_STATE))

        s1 = jnp.broadcast_to(state_c_conv[i][:, None, 1, :], (DEC_BATCH, DEC_SEQ, C_WIDTH)).reshape(ms, C_WIDTH)
        s2 = jnp.broadcast_to(state_c_conv[i][:, None, 0, :], (DEC_BATCH, DEC_SEQ, C_WIDTH)).reshape(ms, C_WIDTH)
        sa, sb, sc, tail = _inproj(ys, i, g_mix_pre, w_in_b, cos_s, sin_s, conv_c_w, tm_s, 1, 1, row_state=(s1, s2))
        o_a, ob, new_caches = _sample_attn(i, sa, sb, sinks_s, caches, new_caches)
        x0_re = state_d_re[i].reshape(DEC_BATCH // SUBLANES, SUBLANES, D_NS)
        x0_im = state_d_im[i].reshape(DEC_BATCH // SUBLANES, SUBLANES, D_NS)
        od, s_re, s_im = _ssm(sc, DEC_BATCH, DEC_SEQ, i, *ssm_w, x0_re, x0_im, DEC_SEQ)
        x1 = _merge(ys, [o_a], ob, sc, od, i, *merge_w, tm_s)
        ys = _ffn(x1, p_sample.reshape(DEPTH, ms, PLE_DIM), i, *ffn_w, tm_s)
        conv_s.append(tail.reshape(DEC_BATCH, DEC_SEQ, C_WIDTH)[:, DEC_SEQ - (C_CONV - 1):])
        dre_s.append(s_re.reshape(DEC_BATCH, D_NGROUPS, D_STATE))
        dim_s.append(s_im.reshape(DEC_BATCH, D_NGROUPS, D_STATE))

    heads_of = lambda c: c.reshape(c.shape[:2] + (c.shape[2] // HEAD_DIM, HEAD_DIM, c.shape[3]))
    p_states = [_row_major(heads_of(c)) for c in prompt_caches] + [jnp.stack(s) for s in st_p]
    s_states = [_row_major(c) for c in new_caches] + [jnp.stack(conv_s), jnp.stack(dre_s), jnp.stack(dim_s)]
    out = [yp.reshape(BATCH, SEQ, D_MODEL), ys.reshape(DEC_BATCH, DEC_SEQ, D_MODEL)]
    for a, b in zip(p_states, s_states):
        out += [a, b]
    return tuple(out)
```

```python
import functools

import jax
import jax.numpy as jnp
from jax import lax
from jax.experimental import pallas as pl
from jax.experimental.pallas import tpu as pltpu

F32 = jnp.float32
BF16 = jnp.bfloat16

D_MODEL = 1024
BATCH = 8
SEQ = 2048
DEPTH = 2
DEC_BATCH = 128
DEC_SEQ = 8
PAST_LEN = 16384
HEAD_DIM = 64
HALF = HEAD_DIM // 2
ROPE_THETA = 10000.0
BLOCK = 128
EPS = 1e-6
NEG_INF = -1e30
A_HEADS = 4
A_GROUPS = ((128, 1), (512, 4), (2048, 16))
A_NG = len(A_GROUPS)
A_W = A_HEADS * HEAD_DIM
A_QKV = A_NG * A_W
B_Q_HEADS = 8
B_KV_HEADS = 2
B_GROUP = B_Q_HEADS // B_KV_HEADS
B_WINDOW = 128
B_QW = B_Q_HEADS * HEAD_DIM
B_KW = B_KV_HEADS * HEAD_DIM
C_WIDTH = 256
C_CONV = 3
D_GROUP = 16
D_NGROUPS = 16
D_WIDTH = 256
D_STATE = 64
D_NS = D_NGROUPS * D_STATE
N_BRANCH = 4
D_FF = 2816
PLE_DIM = 256
MIX_W = 4096
SCALE = HEAD_DIM ** -0.5
N_CACHES = 2 * A_NG + 2

LANES = 128
SUBLANES = 8
VMEM_LIMIT = 56 * 1024 * 1024

SLAB_A = 3 * A_QKV
SLAB_B = B_QW + 2 * B_KW
SLAB_C = C_WIDTH + D_WIDTH
ROPE_A = 2 * A_QKV
ROPE_B = B_QW + B_KW


def _params(n_axes):
    return pltpu.CompilerParams(dimension_semantics=("arbitrary",) * n_axes, vmem_limit_bytes=VMEM_LIMIT)


def _resident(shape, layer, col=0):
    return pl.BlockSpec((None,) + tuple(shape), lambda *_: (layer, 0, col), pipeline_mode=pl.Buffered(1))


def _rmsnorm(x, g):
    return x * lax.rsqrt(jnp.mean(x * x, axis=-1, keepdims=True) + EPS) * g


def _mm(a, b):
    return jnp.dot(a, b, preferred_element_type=F32)


def _mm_nt(a, b):
    return lax.dot_general(a, b, (((1,), (1,)), ((), ())), preferred_element_type=F32)


def _row_halves(tm):
    if tm >= 512:
        return (slice(0, tm // 2), slice(tm // 2, tm))
    return (slice(0, tm),)


def _low_lanes(rows):
    return lax.broadcasted_iota(jnp.int32, (rows, LANES), 1) < HEAD_DIM


def _inproj_body(*refs, seq_tiles, per_row_state, aliased):
    if per_row_state:
        x_ref, g_ref, w_ref, cos_ref, sin_ref, cw_ref, s1_ref, s2_ref, a_ref, b_ref, c_ref, tail_ref = refs
    else:
        x_ref, g_ref, w_ref, cos_ref, sin_ref, cw_ref = refs[:6]
        n_in = 6 + (N_CACHES if aliased else 0)
        a_ref, b_ref, c_ref, tail_ref = refs[n_in:n_in + 4]
        cache_refs = refs[n_in + 4:n_in + 4 + N_CACHES]
        carry_ref = refs[n_in + 4 + N_CACHES]
    tm = x_ref.shape[0]
    a_halves, b_halves, z_halves = [], [], []
    for rows in _row_halves(tm):
        h = _rmsnorm(x_ref[rows, :], g_ref[...]).astype(BF16)
        cos = cos_ref[rows, :]
        sin = sin_ref[rows, :]
        lane = lax.broadcasted_iota(jnp.int32, cos.shape, 1)
        first_half = (lane & (HEAD_DIM - 1)) < HALF

        def rope(z):
            partner = jnp.where(first_half, pltpu.roll(z, LANES - HALF, axis=1), pltpu.roll(z, HALF, axis=1))
            return z * cos + partner * sin

        za = _mm(h, w_ref[:, 0:SLAB_A])
        chunks = []
        for c in range(SLAB_A // LANES):
            blk = za[:, c * LANES:(c + 1) * LANES]
            chunks.append(rope(blk) if c * LANES < ROPE_A else blk)
            a_ref[rows, c * LANES:(c + 1) * LANES] = chunks[-1]
        a_halves.append(chunks)
        zb = _mm(h, w_ref[:, SLAB_A:SLAB_A + SLAB_B])
        chunks = []
        for c in range(SLAB_B // LANES):
            blk = zb[:, c * LANES:(c + 1) * LANES]
            chunks.append(rope(blk) if c * LANES < ROPE_B else blk)
            b_ref[rows, c * LANES:(c + 1) * LANES] = chunks[-1]
        b_halves.append(chunks)
        z_halves.append(_mm(h, w_ref[:, SLAB_A + SLAB_B:MIX_W]))
    whole = lambda halves, c: jnp.concatenate([chunks[c] for chunks in halves], axis=0)

    if not per_row_state:
        pairs = A_W // LANES
        chunk_of = lambda part, g, j: whole(a_halves, (part * A_QKV + g * A_W) // LANES + j)

        def write(ref, chunk, rows):
            ref[0, :, :] = jnp.concatenate([ch[tm - rows:tm, :].T for ch in chunk], axis=0)

        last_group = A_NG - 1
        assert A_GROUPS[last_group][0] == seq_tiles * tm and A_GROUPS[1][0] == tm and A_GROUPS[0][0] <= tm
        for part in (1, 2):
            write(cache_refs[2 * last_group + part - 1], [chunk_of(part, last_group, j) for j in range(pairs)], tm)

        @pl.when(pl.program_id(0) % seq_tiles == seq_tiles - 1)
        def _():
            for g in range(last_group):
                for part in (1, 2):
                    write(cache_refs[2 * g + part - 1], [chunk_of(part, g, j) for j in range(pairs)], A_GROUPS[g][0])
            write(cache_refs[6], [whole(b_halves, B_QW // LANES)], B_WINDOW)
            write(cache_refs[7], [whole(b_halves, B_QW // LANES + 1)], B_WINDOW)
    z = jnp.concatenate(z_halves, axis=0)
    zc = z[:, 2 * C_WIDTH:3 * C_WIDTH] * z[:, 0:C_WIDTH]

    row = lax.broadcasted_iota(jnp.int32, (tm, C_WIDTH), 0)
    z1 = pltpu.roll(zc, 1, axis=0)
    z2 = pltpu.roll(zc, 2, axis=0)
    if per_row_state:
        t = row & (DEC_SEQ - 1)
        z1 = jnp.where(t == 0, s1_ref[...], z1)
        z2 = jnp.where(t == 0, s2_ref[...], jnp.where(t == 1, s1_ref[...], z2))
        tail_ref[...] = zc
    else:
        @pl.when(pl.program_id(0) % seq_tiles == 0)
        def _():
            carry_ref[...] = jnp.zeros(carry_ref.shape, F32)
        z1 = jnp.where(row == 0, carry_ref[SUBLANES - 1:SUBLANES, :], z1)
        z2 = jnp.where(row == 0, carry_ref[SUBLANES - 2:SUBLANES - 1, :],
                       jnp.where(row == 1, carry_ref[SUBLANES - 1:SUBLANES, :], z2))
        carry_ref[...] = zc[tm - SUBLANES:tm, :]
        tail_ref[...] = zc[tm - SUBLANES:tm, :]
    conv = cw_ref[0:1, :] * z2 + cw_ref[1:2, :] * z1 + cw_ref[2:3, :] * zc
    c_ref[:, 0:C_WIDTH] = z[:, C_WIDTH:2 * C_WIDTH] * conv
    c_ref[:, C_WIDTH:SLAB_C] = z[:, 3 * C_WIDTH:4 * C_WIDTH]


def _inproj(x, layer, g, w_in, cos, sin, convw, tm, table_blocks, seq_tiles, row_state=None, prev_caches=None):
    m = x.shape[0]
    per_row_state = row_state is not None
    aliased = prev_caches is not None
    row = lambda i: (i, 0)
    tab = lambda i: (i % table_blocks, 0)
    in_specs = [pl.BlockSpec((tm, D_MODEL), row), _resident((1, D_MODEL), layer), _resident((D_MODEL, MIX_W), layer),
                pl.BlockSpec((tm, LANES), tab), pl.BlockSpec((tm, LANES), tab), _resident((C_CONV, C_WIDTH), layer)]
    args = [x, g, w_in, cos, sin, convw]
    out_specs = [pl.BlockSpec((tm, SLAB_A), row), pl.BlockSpec((tm, SLAB_B), row), pl.BlockSpec((tm, SLAB_C), row)]
    out_shape = [jax.ShapeDtypeStruct((m, SLAB_A), F32), jax.ShapeDtypeStruct((m, SLAB_B), F32),
                 jax.ShapeDtypeStruct((m, SLAB_C), F32)]
    scratch, aliases = [], {}
    if per_row_state:
        in_specs += [pl.BlockSpec((tm, C_WIDTH), row)] * 2
        args += list(row_state)
        out_specs.append(pl.BlockSpec((tm, C_WIDTH), row))
        out_shape.append(jax.ShapeDtypeStruct((m, C_WIDTH), F32))
    else:
        scratch = [pltpu.VMEM((SUBLANES, C_WIDTH), F32)]
        out_specs.append(pl.BlockSpec((SUBLANES, C_WIDTH), row))
        out_shape.append(jax.ShapeDtypeStruct((m // tm * SUBLANES, C_WIDTH), F32))
        n_seqs = m // (tm * seq_tiles)
        widths = [(A_W, w) for w, _ in A_GROUPS for _ in range(2)] + [(B_KW, B_WINDOW)] * 2
        for j, (rows_, window) in enumerate(widths):
            per_tile = window == tm * seq_tiles
            idx = ((lambda i: (layer, i // seq_tiles, 0, i % seq_tiles)) if per_tile
                   else (lambda i: (layer, i // seq_tiles, 0, 0)))
            out_specs.append(pl.BlockSpec((None, 1, rows_, tm if per_tile else window), idx))
            out_shape.append(jax.ShapeDtypeStruct((DEPTH, n_seqs, rows_, window), F32))
        if aliased:
            aliases = {len(args) + j: 4 + j for j in range(N_CACHES)}
            in_specs += [pl.BlockSpec(memory_space=pl.ANY)] * N_CACHES
            args += list(prev_caches)
    return pl.pallas_call(
        functools.partial(_inproj_body, seq_tiles=seq_tiles, per_row_state=per_row_state, aliased=aliased),
        grid=(m // tm,),
        in_specs=in_specs,
        out_specs=out_specs,
        out_shape=out_shape,
        input_output_aliases=aliases,
        scratch_shapes=scratch,
        compiler_params=_params(1),
        name="in_proj",
    )(*args)


def _band_mask_t(nk):
    kj = lax.broadcasted_iota(jnp.int32, (nk, BLOCK), 0)
    qi = lax.broadcasted_iota(jnp.int32, (nk, BLOCK), 1)
    dist = qi + (nk - BLOCK) - kj
    return (dist >= 0) & (dist <= BLOCK)


def _softmax_t(s, mask, sink=None):
    s = jnp.where(mask, s, NEG_INF)
    m = jnp.max(s, axis=0, keepdims=True)
    if sink is not None:
        m = jnp.maximum(m, sink)
    p = jnp.exp(s - m)
    den = jnp.sum(p, axis=0, keepdims=True)
    if sink is not None:
        den = den + jnp.exp(sink - m)
    return p, 1.0 / den, m + jnp.log(den)


def _band_softmax_t(s, sink=None):
    j = lax.broadcasted_iota(jnp.int32, (BLOCK, BLOCK), 0)
    qi = lax.broadcasted_iota(jnp.int32, (BLOCK, BLOCK), 1)
    above = j > qi
    s_prev, s_own = s[0:BLOCK], s[BLOCK:2 * BLOCK]
    folded = jnp.where(above, s_prev, s_own)
    diag = jnp.sum(jnp.where(j == qi, s_prev, 0.0), axis=0, keepdims=True)
    m = jnp.maximum(jnp.max(folded, axis=0, keepdims=True), diag)
    if sink is not None:
        m = jnp.maximum(m, sink)
    p = jnp.exp(folded - m)
    p_diag = jnp.exp(diag - m)
    den = jnp.sum(p, axis=0, keepdims=True) + p_diag
    if sink is not None:
        den = den + jnp.exp(sink - m)
    unfolded = jnp.concatenate([jnp.where(above, p, 0.0), jnp.where(above, 0.0, p)], axis=0)
    return unfolded, p_diag, 1.0 / den, m + jnp.log(den)


def _attn_a_prompt_body(q_ref, k_ref, v_ref, o_ref, lse_ref, *, d):
    nb = SEQ // d // BLOCK
    pairs = q_ref.shape[-1] // LANES
    first_head_rows = lax.broadcasted_iota(jnp.int32, (LANES, BLOCK), 0) < HEAD_DIM

    def rows(start, n):
        return pl.ds(start, n) if d == 1 else pl.ds(start, n, stride=d)

    def block(q0, k0, nk):
        mask = _band_mask_t(nk)
        lo = _low_lanes(nk)
        for hp in range(pairs):
            cols = slice(hp * LANES, (hp + 1) * LANES)
            q = (q_ref[0, rows(q0, BLOCK), cols] * SCALE).astype(BF16)
            k = k_ref[0, rows(k0, nk), cols]
            v = v_ref[0, rows(k0, nk), cols]
            k2 = jnp.concatenate([jnp.where(lo, k, 0.0), jnp.where(lo, 0.0, k)], axis=0).astype(BF16)
            st = _mm_nt(k2, q)
            vt32 = v.T
            if nk == BLOCK:
                p0, r0, l0 = _softmax_t(st[0:nk], mask)
                p1, r1, l1 = _softmax_t(st[nk:2 * nk], mask)
                d0 = d1 = 0.0
            else:
                p0, pd0, r0, l0 = _band_softmax_t(st[0:nk])
                p1, pd1, r1, l1 = _band_softmax_t(st[nk:2 * nk])
                d0, d1 = vt32[:, 0:BLOCK] * pd0, vt32[:, 0:BLOCK] * pd1
            ot = _mm(vt32.astype(BF16), jnp.concatenate([p0, p1], axis=1).astype(BF16))
            o_t = jnp.where(first_head_rows, (ot[:, 0:BLOCK] + d0) * r0, (ot[:, BLOCK:2 * BLOCK] + d1) * r1)
            l_t = jnp.where(first_head_rows, jnp.broadcast_to(l0, (LANES, BLOCK)),
                            jnp.broadcast_to(l1, (LANES, BLOCK)))
            o_ref[0, rows(q0, BLOCK), cols] = o_t.T
            lse_ref[0, rows(q0, BLOCK), cols] = l_t.T

    for r in range(d):
        block(r, r, BLOCK)
        if nb > 1:
            def body(i, carry):
                q0 = r + i * (BLOCK * d)
                block(q0, q0 - BLOCK * d, 2 * BLOCK)
                return carry
            lax.fori_loop(1, nb, body, 0, unroll=nb - 1)


def _attn_a_prompt(slab_a, g, d):
    view = slab_a.reshape(BATCH, SEQ, SLAB_A)
    width = A_W if d == 1 else LANES
    steps = A_W // width
    spec = lambda off: pl.BlockSpec((1, SEQ, width), lambda b, hp: (b, 0, off * steps + hp))
    oshape = jax.ShapeDtypeStruct((BATCH, SEQ, A_W), F32)
    o, lse = pl.pallas_call(
        functools.partial(_attn_a_prompt_body, d=d),
        grid=(BATCH, steps),
        in_specs=[spec(g), spec(A_NG + g), spec(2 * A_NG + g)],
        out_specs=[spec(0), spec(0)],
        out_shape=[oshape, oshape],
        compiler_params=_params(2),
        name=f"attn_a{g + 1}_prompt",
    )(view, view, view)
    return o.reshape(BATCH * SEQ, A_W), lse.reshape(BATCH * SEQ, A_W)


def _attn_b_prompt_body(q_ref, k_ref, v_ref, sink_ref, o_ref, *, nb):
    def block(q0, k0, nk):
        mask = _band_mask_t(nk)
        lo = _low_lanes(nk)
        k = k_ref[0, pl.ds(k0, nk), :]
        k_sw = pltpu.roll(k, HEAD_DIM, axis=1)
        vt32 = v_ref[0, pl.ds(k0, nk), :].T
        vt = vt32.astype(BF16)
        for kv in range(B_KV_HEADS):
            on_lo, on_hi = (k, k_sw) if kv == 0 else (k_sw, k)
            k2 = jnp.concatenate([jnp.where(lo, on_lo, 0.0), jnp.where(lo, 0.0, on_hi)], axis=0).astype(BF16)
            cols = slice(2 * kv * LANES, (2 * kv + 2) * LANES)
            q2 = q_ref[0, pl.ds(q0, BLOCK), cols] * SCALE
            q2 = jnp.concatenate([q2[:, 0:LANES], q2[:, LANES:2 * LANES]], axis=0).astype(BF16)
            st = _mm_nt(k2, q2)
            kv_rows = slice(kv * HEAD_DIM, (kv + 1) * HEAD_DIM)
            ps, rs, diag_terms = [], [], []
            for g in range(B_GROUP):
                half, chunk = g % 2, g // 2
                sink = sink_ref[kv * B_GROUP + g:kv * B_GROUP + g + 1, :]
                s = st[half * nk:(half + 1) * nk, chunk * BLOCK:(chunk + 1) * BLOCK]
                if nk == BLOCK:
                    p, r, _ = _softmax_t(s, mask, sink)
                else:
                    p, p_diag, r, _ = _band_softmax_t(s, sink)
                    diag_terms.append(vt32[kv_rows, 0:BLOCK] * p_diag)
                ps.append(p)
                rs.append(r)
            ot = _mm(vt, jnp.concatenate(ps, axis=1).astype(BF16))[kv_rows, :]
            if diag_terms:
                ot = ot + jnp.concatenate(diag_terms, axis=1)
            oj = ot * jnp.concatenate(rs, axis=1)
            for chunk in range(2):
                o_t = jnp.concatenate([oj[:, (2 * chunk) * BLOCK:(2 * chunk + 1) * BLOCK],
                                       oj[:, (2 * chunk + 1) * BLOCK:(2 * chunk + 2) * BLOCK]], axis=0)
                c = 2 * kv + chunk
                o_ref[0, pl.ds(q0, BLOCK), c * LANES:(c + 1) * LANES] = o_t.T

    block(0, 0, BLOCK)

    def body(i, carry):
        q0 = pl.multiple_of(i * BLOCK, BLOCK)
        block(q0, pl.multiple_of(q0 - BLOCK, BLOCK), 2 * BLOCK)
        return carry
    lax.fori_loop(1, nb, body, 0, unroll=15)


def _attn_b_prompt(slab_b, layer, sinks):
    view = slab_b.reshape(BATCH, SEQ, SLAB_B)
    nq = B_QW // LANES
    o = pl.pallas_call(
        functools.partial(_attn_b_prompt_body, nb=SEQ // BLOCK),
        grid=(BATCH,),
        in_specs=[pl.BlockSpec((1, SEQ, B_QW), lambda b: (b, 0, 0)),
                  pl.BlockSpec((1, SEQ, B_KW), lambda b: (b, 0, nq)),
                  pl.BlockSpec((1, SEQ, B_KW), lambda b: (b, 0, nq + 1)),
                  _resident((B_Q_HEADS, LANES), layer)],
        out_specs=pl.BlockSpec((1, SEQ, B_QW), lambda b: (b, 0, 0)),
        out_shape=jax.ShapeDtypeStruct((BATCH, SEQ, B_QW), F32),
        compiler_params=_params(1),
        name="attn_b_prompt",
    )(view, view, view, sinks)
    return o.reshape(BATCH * SEQ, B_QW)


def _sample_masks(rows, lb, window, dil):
    assert dil & (dil - 1) == 0 and DEC_SEQ & (DEC_SEQ - 1) == 0
    t_c = lax.broadcasted_iota(jnp.int32, (rows, lb), 0) & (DEC_SEQ - 1)
    dist_c = lb + t_c - lax.broadcasted_iota(jnp.int32, (rows, lb), 1)
    mask_c = (dist_c >= 0) & (dist_c <= window) & ((dist_c & (dil - 1)) == 0)
    t_n = lax.broadcasted_iota(jnp.int32, (rows, DEC_SEQ), 0) & (DEC_SEQ - 1)
    dist_n = t_n - lax.broadcasted_iota(jnp.int32, (rows, DEC_SEQ), 1)
    mask_n = (dist_n >= 0) & (dist_n <= window) & ((dist_n & (dil - 1)) == 0)
    return mask_c, mask_n


def _sample_attend(q, kt, vt, kn, vn, masks, sink=None):
    mask_c, mask_n = masks
    s_c = jnp.where(mask_c, _mm(q.astype(BF16), kt), NEG_INF)
    s_n = jnp.where(mask_n, _mm_nt(q, kn), NEG_INF)
    m = jnp.maximum(jnp.max(s_c, axis=-1, keepdims=True), jnp.max(s_n, axis=-1, keepdims=True))
    if sink is not None:
        m = jnp.maximum(m, sink)
    p_c = jnp.exp(s_c - m)
    p_n = jnp.exp(s_n - m)
    den = jnp.sum(p_c, axis=-1, keepdims=True) + jnp.sum(p_n, axis=-1, keepdims=True)
    if sink is not None:
        den = den + jnp.exp(sink - m)
    o = _mm_nt(p_c.astype(BF16), vt) + _mm(p_n, vn)
    return o * (1.0 / den), m + jnp.log(den)


def _write_shifted_cache(dst_ref, src_ref, s, heads, new_rows, lb):
    padded = jnp.concatenate([jnp.zeros((LANES - DEC_SEQ, LANES), F32), new_rows], axis=0)
    new_t = padded.T
    keep = lax.broadcasted_iota(jnp.int32, (LANES, LANES), 1) < LANES - DEC_SEQ
    rotated = lambda j: pltpu.roll(src_ref[0, s, heads, :, j * LANES:(j + 1) * LANES].reshape(LANES, LANES),
                                   LANES - DEC_SEQ, axis=1)
    n_tiles = lb // LANES
    cur = rotated(0)
    for j in range(n_tiles):
        nxt = rotated(j + 1) if j + 1 < n_tiles else new_t
        dst_ref[0, s, heads, :, j * LANES:(j + 1) * LANES] = jnp.where(keep, cur, nxt).reshape(2, HEAD_DIM, LANES)
        cur = nxt


SAMPLE_SEQS_PER_STEP = 2


def _sample_attn_body(*refs, aliased, n_seq):
    for s in range(n_seq):
        _sample_attn_one(refs, aliased, s)


def _sample_attn_one(refs, aliased, s):
    sa_ref, sb_ref, sink_ref = refs[:3]
    cache_refs = refs[3:11]
    pos = 11 + (8 if aliased else 0)
    oa_ref, ob_ref = refs[pos:pos + 2]
    out_refs = refs[pos + 2:pos + 10]
    lo8 = _low_lanes(DEC_SEQ)
    seq_rows = slice(s * DEC_SEQ, (s + 1) * DEC_SEQ)

    o_g, l_g = [], []
    for g, (window, dil) in enumerate(A_GROUPS):
        kc_ref, vc_ref, ko_ref, vo_ref = cache_refs[2 * g], cache_refs[2 * g + 1], out_refs[2 * g], out_refs[2 * g + 1]
        lb = kc_ref.shape[-1]
        masks = _sample_masks(2 * DEC_SEQ, lb, window, dil)
        o_pairs, l_pairs = [], []
        for c in range(A_W // LANES):
            col = lambda part: slice(part * A_QKV + g * A_W + c * LANES, part * A_QKV + g * A_W + (c + 1) * LANES)
            heads = slice(2 * c, 2 * c + 2)
            q = sa_ref[seq_rows, col(0)] * SCALE
            kn, vn = sa_ref[seq_rows, col(1)], sa_ref[seq_rows, col(2)]
            kt = kc_ref[0, s, heads].reshape(LANES, lb)
            vt = vc_ref[0, s, heads].reshape(LANES, lb)
            q2 = jnp.concatenate([jnp.where(lo8, q, 0.0), jnp.where(lo8, 0.0, q)], axis=0)
            o, lse = _sample_attend(q2, kt.astype(BF16), vt.astype(BF16), kn, vn, masks)
            o_pairs.append(jnp.where(lo8, o[0:DEC_SEQ], o[DEC_SEQ:]))
            l_pairs.append(jnp.where(lo8, lse[0:DEC_SEQ], lse[DEC_SEQ:]))
            _write_shifted_cache(ko_ref, kc_ref, s, heads, kn, lb)
            _write_shifted_cache(vo_ref, vc_ref, s, heads, vn, lb)
        o_g.append(o_pairs)
        l_g.append(l_pairs)
    for c in range(A_W // LANES):
        l1, l2, l3 = l_g[0][c], l_g[1][c], l_g[2][c]
        lm = jnp.maximum(jnp.maximum(l1, l2), l3)
        e1, e2, e3 = jnp.exp(l1 - lm), jnp.exp(l2 - lm), jnp.exp(l3 - lm)
        es = e1 + e2 + e3
        oa_ref[seq_rows, c * LANES:(c + 1) * LANES] = ((e1 / es) * o_g[0][c] + (e2 / es) * o_g[1][c]
                                                       + (e3 / es) * o_g[2][c])

    kc_ref, vc_ref, ko_ref, vo_ref = cache_refs[6], cache_refs[7], out_refs[6], out_refs[7]
    lb = kc_ref.shape[-1]
    rows = B_GROUP * DEC_SEQ
    masks = _sample_masks(rows, lb, B_WINDOW, 1)
    both = slice(0, B_KV_HEADS)
    ktb = kc_ref[0, s].reshape(LANES, lb).astype(BF16)
    vtb = vc_ref[0, s].reshape(LANES, lb).astype(BF16)
    kn, vn = sb_ref[seq_rows, B_QW:B_QW + B_KW], sb_ref[seq_rows, B_QW + B_KW:SLAB_B]
    halves = {}
    for kv in range(B_KV_HEADS):
        parts = []
        for gq in range(B_GROUP):
            head = kv * B_GROUP + gq
            q = sb_ref[seq_rows, (head // 2) * LANES:(head // 2 + 1) * LANES] * SCALE
            q = jnp.where(lo8, q, 0.0) if head % 2 == 0 else jnp.where(lo8, 0.0, q)
            parts.append(q if head % 2 == kv else pltpu.roll(q, HEAD_DIM, axis=1))
        q4 = jnp.concatenate(parts, axis=0)
        sink = sink_ref[kv * rows:(kv + 1) * rows, 0:1]
        o, _ = _sample_attend(q4, ktb, vtb, kn, vn, masks, sink)
        for gq in range(B_GROUP):
            head = kv * B_GROUP + gq
            oh = o[gq * DEC_SEQ:(gq + 1) * DEC_SEQ]
            halves[head] = oh if head % 2 == kv else pltpu.roll(oh, HEAD_DIM, axis=1)
    for c in range(B_QW // LANES):
        ob_ref[seq_rows, c * LANES:(c + 1) * LANES] = jnp.where(lo8, halves[2 * c], halves[2 * c + 1])
    _write_shifted_cache(ko_ref, kc_ref, s, both, kn, lb)
    _write_shifted_cache(vo_ref, vc_ref, s, both, vn, lb)


def _sample_attn(layer, slab_a, slab_b, sinks, caches, prev):
    aliased = prev is not None
    n_seq = SAMPLE_SEQS_PER_STEP
    rows = n_seq * DEC_SEQ
    row = lambda b: (b, 0)
    cspec = lambda c: pl.BlockSpec((1, n_seq) + c.shape[2:], lambda b: (layer, b, 0, 0, 0))
    in_specs = [pl.BlockSpec((rows, SLAB_A), row), pl.BlockSpec((rows, SLAB_B), row),
                _resident(sinks.shape[1:], layer)]
    in_specs += [cspec(c) for c in caches]
    args = [slab_a, slab_b, sinks] + list(caches)
    aliases = {}
    if aliased:
        aliases = {len(args) + j: 2 + j for j in range(8)}
        in_specs += [pl.BlockSpec(memory_space=pl.ANY)] * 8
        args += list(prev)
    m = slab_a.shape[0]
    res = pl.pallas_call(
        functools.partial(_sample_attn_body, aliased=aliased, n_seq=n_seq),
        grid=(DEC_BATCH // n_seq,),
        in_specs=in_specs,
        out_specs=[pl.BlockSpec((rows, A_W), row), pl.BlockSpec((rows, B_QW), row)] + [cspec(c) for c in caches],
        out_shape=[jax.ShapeDtypeStruct((m, A_W), F32), jax.ShapeDtypeStruct((m, B_QW), F32)]
                  + [jax.ShapeDtypeStruct(c.shape, F32) for c in caches],
        input_output_aliases=aliases,
        compiler_params=_params(1),
        name="sample_attn",
    )(*args)
    return res[0], res[1], list(res[2:])


def _ssm_body(u_ref, are_ref, aim_ref, b_ref, cbig_ref, dvec_ref, wglu_ref, x0re_ref, x0im_ref,
              od_ref, sre_ref, sim_ref, bu_ref, xs_ref, *, steps, pitch):
    @pl.when(pl.program_id(1) == 0)
    def _():
        sre_ref[0] = x0re_ref[0]
        sim_ref[0] = x0im_ref[0]

    n_re = D_NS // LANES
    tiles = lambda x: [x[:, j * LANES:(j + 1) * LANES] for j in range(x.shape[1] // LANES)]

    plane = lambda s: slice(s * pitch, s * pitch + steps)
    u = u_ref[...].reshape(SUBLANES * steps, D_WIDTH)
    bu = _mm(u.astype(BF16), b_ref[...])
    for s in range(SUBLANES):
        for j, tile in enumerate(tiles(bu[s * steps:(s + 1) * steps])):
            bu_ref[j, plane(s), :] = tile
    a_re = are_ref[...]
    a_im = aim_ref[...]

    def step(t, carry):
        re, im = carry
        at_t = pl.ds(t, SUBLANES, stride=pitch)
        b_re = jnp.concatenate([bu_ref[j, at_t, :] for j in range(n_re)], axis=1)
        b_im = jnp.concatenate([bu_ref[n_re + j, at_t, :] for j in range(n_re)], axis=1)
        re, im = a_re * re - a_im * im + b_re, a_re * im + a_im * re + b_im
        for j, (tr, ti) in enumerate(zip(tiles(re), tiles(im))):
            xs_ref[j, at_t, :] = tr
            xs_ref[n_re + j, at_t, :] = ti
        return re, im

    re, im = lax.fori_loop(0, steps, step, (sre_ref[0], sim_ref[0]), unroll=min(steps, SUBLANES))
    sre_ref[0] = re
    sim_ref[0] = im
    xs = jnp.concatenate([jnp.concatenate([xs_ref[j, plane(s), :] for j in range(2 * n_re)], axis=1)
                          for s in range(SUBLANES)], axis=0)
    y = _mm(xs.astype(BF16), cbig_ref[...]) + dvec_ref[...] * u
    y = _mm(jax.nn.gelu(y).astype(BF16), wglu_ref[...])
    od = y[:, 0:D_WIDTH] * jax.nn.sigmoid(y[:, D_WIDTH:2 * D_WIDTH])
    od_ref[...] = od.reshape(SUBLANES, steps, D_WIDTH)


def _ssm(slab_c, bsz, length, layer, a_re, a_im, bbig, cbig, dvec, wglu, x0_re, x0_im, steps):
    n = bsz // SUBLANES
    pitch = steps if steps <= SUBLANES else steps + SUBLANES
    view = slab_c.reshape(bsz, length, SLAB_C)
    u_spec = pl.BlockSpec((SUBLANES, steps, D_WIDTH), lambda i, c: (i, c, SLAB_C // D_WIDTH - 1))
    o_spec = pl.BlockSpec((SUBLANES, steps, D_WIDTH), lambda i, c: (i, c, 0))
    st_spec = pl.BlockSpec((1, SUBLANES, D_NS), lambda i, c: (i, 0, 0))
    st_shape = jax.ShapeDtypeStruct((n, SUBLANES, D_NS), F32)
    od, s_re, s_im = pl.pallas_call(
        functools.partial(_ssm_body, steps=steps, pitch=pitch),
        grid=(n, length // steps),
        in_specs=[u_spec, _resident((SUBLANES, D_NS), layer), _resident((SUBLANES, D_NS), layer),
                  _resident((D_WIDTH, 2 * D_NS), layer), _resident((2 * D_NS, D_WIDTH), layer),
                  _resident((1, D_WIDTH), layer), _resident((D_WIDTH, 2 * D_WIDTH), layer), st_spec, st_spec],
        out_specs=[o_spec, st_spec, st_spec],
        out_shape=[jax.ShapeDtypeStruct((bsz, length, D_WIDTH), F32), st_shape, st_shape],
        scratch_shapes=[pltpu.VMEM((2 * D_NS // LANES, SUBLANES * pitch, LANES), F32)] * 2,
        compiler_params=_params(2),
        name="ssm",
    )(view, a_re, a_im, bbig, cbig, dvec, wglu, x0_re, x0_im)
    return od.reshape(bsz * length, D_WIDTH), s_re, s_im


def _merge_body(*refs, n_groups):
    x_ref = refs[0]
    a_refs = refs[1:1 + (2 * n_groups if n_groups > 1 else 1)]
    ob_ref, oc_ref, od_ref, gpre_ref, wgl_ref, wa_ref, wb_ref, wc_ref, wd_ref, wout_ref, gpost_ref, y_ref = \
        refs[1 + len(a_refs):]
    tm = x_ref.shape[0]
    for rows in _row_halves(tm):
        x = x_ref[rows, :]
        h = _rmsnorm(x, gpre_ref[...]).astype(BF16)
        if n_groups > 1:
            lses = [a_refs[2 * g + 1][rows, :] for g in range(n_groups)]
            lm = functools.reduce(jnp.maximum, lses)
            es = [jnp.exp(l - lm) for l in lses]
            den = functools.reduce(lambda a, b: a + b, es)
            o_a = functools.reduce(lambda a, b: a + b,
                                   [(es[g] / den) * a_refs[2 * g][rows, :] for g in range(n_groups)])
        else:
            o_a = a_refs[0][rows, :]
        branches = ((o_a, wa_ref), (ob_ref[rows, :], wb_ref), (oc_ref[rows, :], wc_ref), (od_ref[rows, :], wd_ref))
        merged = None
        for j, (o, w_ref) in enumerate(branches):
            gate = jax.nn.sigmoid(_mm(h, wgl_ref[:, j * D_MODEL:(j + 1) * D_MODEL]))
            term = gate * _mm(o.astype(BF16), w_ref[...])
            merged = term if merged is None else merged + term
        mix = _mm(merged.astype(BF16), wout_ref[...])
        y_ref[rows, :] = x + _rmsnorm(mix, gpost_ref[...])


def _merge(x, a_parts, ob, slab_c, od, layer, gpre, w_in, wa, wb, wc, wd, wout, gpost, tm):
    m = x.shape[0]
    row = lambda i: (i, 0)
    r256 = pl.BlockSpec((tm, 256), row)
    n_groups = max(1, len(a_parts) // 2)
    return pl.pallas_call(
        functools.partial(_merge_body, n_groups=n_groups),
        grid=(m // tm,),
        in_specs=[pl.BlockSpec((tm, D_MODEL), row)] + [r256] * len(a_parts)
                 + [pl.BlockSpec((tm, B_QW), row), r256, r256,
                    _resident((1, D_MODEL), layer), _resident((D_MODEL, N_BRANCH * D_MODEL), layer, col=1),
                    _resident((A_W, D_MODEL), layer), _resident((B_QW, D_MODEL), layer),
                    _resident((C_WIDTH, D_MODEL), layer), _resident((D_WIDTH, D_MODEL), layer),
                    _resident((D_MODEL, D_MODEL), layer), _resident((1, D_MODEL), layer)],
        out_specs=pl.BlockSpec((tm, D_MODEL), row),
        out_shape=jax.ShapeDtypeStruct((m, D_MODEL), F32),
        compiler_params=_params(1),
        name="merge",
    )(x, *a_parts, ob, slab_c, od, gpre, w_in, wa, wb, wc, wd, wout, gpost)


def _ffn_body(x_ref, p_ref, gpre_ref, wg_ref, wu_ref, wdn_ref, gpost_ref, wple_ref, wpg_ref, y_ref):
    x = x_ref[...]
    h = _rmsnorm(x, gpre_ref[...]).astype(BF16)
    act = jax.nn.silu(_mm(h, wg_ref[...])) * _mm(h, wu_ref[...])
    f = _mm(act.astype(BF16), wdn_ref[...])
    x = x + _rmsnorm(f, gpost_ref[...])
    gate = jax.nn.sigmoid(_mm(x.astype(BF16), wpg_ref[...]))
    y_ref[...] = x + gate * _mm(p_ref[...].astype(BF16), wple_ref[...])


def _ffn(x, p_all, layer, gpre, wg, wu, wdn, gpost, wple, wpg, tm):
    m = x.shape[0]
    row = lambda i: (i, 0)
    return pl.pallas_call(
        _ffn_body,
        grid=(m // tm,),
        in_specs=[pl.BlockSpec((tm, D_MODEL), row), pl.BlockSpec((None, tm, PLE_DIM), lambda i: (layer, i, 0)),
                  _resident((1, D_MODEL), layer), _resident((D_MODEL, D_FF), layer), _resident((D_MODEL, D_FF), layer),
                  _resident((D_FF, D_MODEL), layer), _resident((1, D_MODEL), layer),
                  _resident((PLE_DIM, D_MODEL), layer), _resident((D_MODEL, D_MODEL), layer)],
        out_specs=pl.BlockSpec((tm, D_MODEL), row),
        out_shape=jax.ShapeDtypeStruct((m, D_MODEL), F32),
        compiler_params=_params(1),
        name="ffn_ple",
    )(x, p_all, gpre, wg, wu, wdn, gpost, wple, wpg)


def _rope_tables(pos):
    inv = ROPE_THETA ** (-jnp.arange(HALF, dtype=F32) / HALF)
    ang = pos.astype(F32)[:, None] * inv[None, :]
    cos, sin = jnp.cos(ang), jnp.sin(ang)
    cos_t = jnp.tile(cos, (1, LANES // HALF))
    sin_t = jnp.tile(jnp.concatenate([-sin, sin], axis=1), (1, LANES // HEAD_DIM))
    return cos_t, sin_t


def _ssm_weights(lam_re, lam_im, log_dt, b_re, b_im, c_re, c_im):
    lam = lax.complex(lam_re, lam_im)
    dt = jnp.exp(log_dt)[..., None]
    a_bar = jnp.exp(lam * dt)
    b_bar = ((a_bar - 1.0) / lam)[..., None] * lax.complex(b_re, b_im)
    eye = jnp.eye(D_NGROUPS, dtype=F32)
    pack_b = lambda b: jnp.einsum('lgni,gh->lgihn', b, eye).reshape(DEPTH, D_WIDTH, D_NS)
    pack_c = lambda c: jnp.einsum('lgin,gh->lgnhi', c, eye).reshape(DEPTH, D_NS, D_WIDTH)
    bbig = jnp.concatenate([pack_b(jnp.real(b_bar)), pack_b(jnp.imag(b_bar))], axis=2)
    cbig = jnp.concatenate([pack_c(c_re), -pack_c(c_im)], axis=1)
    tile8 = lambda a: jnp.broadcast_to(a.reshape(DEPTH, 1, D_NS), (DEPTH, SUBLANES, D_NS))
    return tile8(jnp.real(a_bar)), tile8(jnp.imag(a_bar)), bbig.astype(BF16), cbig.astype(BF16)


def _key_major(c):
    return jnp.transpose(c, (0, 1, 3, 4, 2))


def _row_major(c):
    return jnp.transpose(c, (0, 1, 4, 2, 3))


def kernel(x_prompt, x_sample, p_prompt, p_sample, cache_a1_k, cache_a1_v, cache_a2_k, cache_a2_v, cache_a3_k, cache_a3_v, cache_b_k, cache_b_v, state_c_conv, state_d_re, state_d_im, norm_mix_pre, norm_mix_post, norm_ffn_pre, norm_ffn_post, w_in, attn_sinks, conv_c_w, ssm_lam_re, ssm_lam_im, ssm_log_dt, ssm_b_re, ssm_b_im, ssm_c_re, ssm_c_im, ssm_d, w_d_glu, w_br_a, w_br_b, w_br_c, w_br_d, w_out, w_ffn_gate, w_ffn_up, w_ffn_down, w_ple, w_ple_gate):
    assert x_prompt.shape == (BATCH, SEQ, D_MODEL) and x_sample.shape == (DEC_BATCH, DEC_SEQ, D_MODEL)
    assert w_in.shape == (DEPTH, D_MODEL, 2 * MIX_W)
    assert all(min(w, PAST_LEN) == w for w, _ in A_GROUPS) and min(B_WINDOW, PAST_LEN) == B_WINDOW
    mp, ms = BATCH * SEQ, DEC_BATCH * DEC_SEQ
    tm_p, tm_s = 512, 256

    cos_p, sin_p = _rope_tables(jnp.arange(SEQ, dtype=jnp.int32))
    cos_s, sin_s = _rope_tables(PAST_LEN + jnp.arange(DEC_SEQ, dtype=jnp.int32))
    cos_s, sin_s = jnp.tile(cos_s, (tm_s // DEC_SEQ, 1)), jnp.tile(sin_s, (tm_s // DEC_SEQ, 1))

    caches = [_key_major(c) for c in (cache_a1_k, cache_a1_v, cache_a2_k, cache_a2_v, cache_a3_k, cache_a3_v,
                                      cache_b_k, cache_b_v)]
    yp = x_prompt.reshape(mp, D_MODEL)
    ys = x_sample.reshape(ms, D_MODEL)
    new_caches = None
    prompt_caches = None
    st_p = [[] for _ in range(3)]
    conv_s, dre_s, dim_s = [], [], []

    rows3 = lambda a: a.reshape(DEPTH, 1, -1)
    w_in_b = w_in.astype(BF16)
    g_mix_pre = rows3(norm_mix_pre)
    ssm_w = _ssm_weights(ssm_lam_re, ssm_lam_im, ssm_log_dt, ssm_b_re, ssm_b_im, ssm_c_re, ssm_c_im) + (
        rows3(ssm_d), w_d_glu.astype(BF16))
    sinks_p = jnp.broadcast_to(attn_sinks.reshape(DEPTH, B_Q_HEADS, 1), (DEPTH, B_Q_HEADS, LANES))
    sinks_s = jnp.broadcast_to(jnp.repeat(attn_sinks.reshape(DEPTH, B_Q_HEADS), DEC_SEQ, axis=1)[..., None],
                               (DEPTH, B_Q_HEADS * DEC_SEQ, LANES))
    merge_w = (g_mix_pre, w_in_b, w_br_a.astype(BF16), w_br_b.astype(BF16), w_br_c.astype(BF16),
               w_br_d.astype(BF16), w_out.astype(BF16), rows3(norm_mix_post))
    ffn_w = (rows3(norm_ffn_pre), w_ffn_gate.astype(BF16), w_ffn_up.astype(BF16), w_ffn_down.astype(BF16),
             rows3(norm_ffn_post), w_ple.astype(BF16), w_ple_gate.astype(BF16))
    zero_state = jnp.zeros((BATCH // SUBLANES, SUBLANES, D_NS), F32)

    for i in range(DEPTH):
        sa, sb, sc, tail, *prompt_caches = _inproj(yp, i, g_mix_pre, w_in_b, cos_p, sin_p, conv_c_w, tm_p,
                                                   SEQ // tm_p, SEQ // tm_p, prev_caches=prompt_caches)
        a_parts = []
        for g, (w, d) in enumerate(A_GROUPS):
            a_parts += list(_attn_a_prompt(sa, g, d))
        ob = _attn_b_prompt(sb, i, sinks_p)
        od, s_re, s_im = _ssm(sc, BATCH, SEQ, i, *ssm_w, zero_state, zero_state, 128)
        x1 = _merge(yp, a_parts, ob, sc, od, i, *merge_w, tm_p)
        yp = _ffn(x1, p_prompt.reshape(DEPTH, mp, PLE_DIM), i, *ffn_w, tm_p)
        st_p[0].append(tail.reshape(BATCH, SEQ // tm_p, SUBLANES, C_WIDTH)[:, -1, SUBLANES - (C_CONV - 1):])
        st_p[1].append(s_re.reshape(BATCH, D_NGROUPS, D_STATE))
        st_p[2].append(s_im.reshape(BATCH, D_NGROUPS, D_STATE))

        s1 = jnp.broadcast_to(state_c_conv[i][:, None, 1, :], (DEC_BATCH, DEC_SEQ, C_WIDTH)).reshape(ms, C_WIDTH)
        s2 = jnp.broadcast_to(state_c_conv[i][:, None, 0, :], (DEC_BATCH, DEC_SEQ, C_WIDTH)).reshape(ms, C_WIDTH)
        sa, sb, sc, tail = _inproj(ys, i, g_mix_pre, w_in_b, cos_s, sin_s, conv_c_w, tm_s, 1, 1, row_state=(s1, s2))
        o_a, ob, new_caches = _sample_attn(i, sa, sb, sinks_s, caches, new_caches)
        x0_re = state_d_re[i].reshape(DEC_BATCH // SUBLANES, SUBLANES, D_NS)
        x0_im = state_d_im[i].reshape(DEC_BATCH // SUBLANES, SUBLANES, D_NS)
        od, s_re, s_im = _ssm(sc, DEC_BATCH, DEC_SEQ, i, *ssm_w, x0_re, x0_im, DEC_SEQ)
        x1 = _merge(ys, [o_a], ob, sc, od, i, *merge_w, tm_s)
        ys = _ffn(x1, p_sample.reshape(DEPTH, ms, PLE_DIM), i, *ffn_w, tm_s)
        conv_s.append(tail.reshape(DEC_BATCH, DEC_SEQ, C_WIDTH)[:, DEC_SEQ - (C_CONV - 1):])
        dre_s.append(s_re.reshape(DEC_BATCH, D_NGROUPS, D_STATE))
        dim_s.append(s_im.reshape(DEC_BATCH, D_NGROUPS, D_STATE))

    heads_of = lambda c: c.reshape(c.shape[:2] + (c.shape[2] // HEAD_DIM, HEAD_DIM, c.shape[3]))
    p_states = [_row_major(heads_of(c)) for c in prompt_caches] + [jnp.stack(s) for s in st_p]
    s_states = [_row_major(c) for c in new_caches] + [jnp.stack(conv_s), jnp.stack(dre_s), jnp.stack(dim_s)]
    out = [yp.reshape(BATCH, SEQ, D_MODEL), ys.reshape(DEC_BATCH, DEC_SEQ, D_MODEL)]
    for a, b in zip(p_states, s_states):
        out += [a, b]
    return tuple(out)
```

```python
import functools

import jax
import jax.numpy as jnp
from jax import lax
from jax.experimental import pallas as pl
from jax.experimental.pallas import tpu as pltpu

F32 = jnp.float32
BF16 = jnp.bfloat16

D_MODEL = 1024
BATCH = 8
SEQ = 2048
DEPTH = 2
DEC_BATCH = 128
DEC_SEQ = 8
PAST_LEN = 16384
HEAD_DIM = 64
HALF = HEAD_DIM // 2
ROPE_THETA = 10000.0
BLOCK = 128
EPS = 1e-6
NEG_INF = -1e30
A_HEADS = 4
A_GROUPS = ((128, 1), (512, 4), (2048, 16))
A_NG = len(A_GROUPS)
A_W = A_HEADS * HEAD_DIM
A_QKV = A_NG * A_W
B_Q_HEADS = 8
B_KV_HEADS = 2
B_GROUP = B_Q_HEADS // B_KV_HEADS
B_WINDOW = 128
B_QW = B_Q_HEADS * HEAD_DIM
B_KW = B_KV_HEADS * HEAD_DIM
C_WIDTH = 256
C_CONV = 3
D_GROUP = 16
D_NGROUPS = 16
D_WIDTH = 256
D_STATE = 64
D_NS = D_NGROUPS * D_STATE
N_BRANCH = 4
D_FF = 2816
PLE_DIM = 256
MIX_W = 4096
SCALE = HEAD_DIM ** -0.5
N_CACHES = 2 * A_NG + 2

LANES = 128
SUBLANES = 8
VMEM_LIMIT = 56 * 1024 * 1024
PROMPT_TILE = 512
SAMPLE_TILE = 256
SSM_CHUNK = 128

SLAB_A = 3 * A_QKV
SLAB_B = B_QW + 2 * B_KW
SLAB_C = C_WIDTH + D_WIDTH
ROPE_A = 2 * A_QKV
ROPE_B = B_QW + B_KW


def _params(n_axes):
    return pltpu.CompilerParams(dimension_semantics=("arbitrary",) * n_axes, vmem_limit_bytes=VMEM_LIMIT)


def _resident(shape, layer, col=0):
    return pl.BlockSpec((None,) + tuple(shape), lambda *_: (layer, 0, col), pipeline_mode=pl.Buffered(1))


def _rmsnorm(x, g):
    return x * lax.rsqrt(jnp.mean(x * x, axis=-1, keepdims=True) + EPS) * g


def _mm(a, b):
    return jnp.dot(a, b, preferred_element_type=F32)


def _mm_nt(a, b):
    return lax.dot_general(a, b, (((1,), (1,)), ((), ())), preferred_element_type=F32)


def _row_halves(tm):
    if tm >= 2 * SAMPLE_TILE:
        return (slice(0, tm // 2), slice(tm // 2, tm))
    return (slice(0, tm),)


def _low_lanes(rows):
    return lax.broadcasted_iota(jnp.int32, (rows, LANES), 1) < HEAD_DIM


def _inproj_body(*refs, seq_tiles, per_row_state, aliased):
    if per_row_state:
        x_ref, g_ref, w_ref, cos_ref, sin_ref, cw_ref, s1_ref, s2_ref, a_ref, b_ref, c_ref, tail_ref = refs
    else:
        x_ref, g_ref, w_ref, cos_ref, sin_ref, cw_ref = refs[:6]
        n_in = 6 + (N_CACHES if aliased else 0)
        a_ref, b_ref, c_ref, tail_ref = refs[n_in:n_in + 4]
        cache_refs = refs[n_in + 4:n_in + 4 + N_CACHES]
        carry_ref = refs[n_in + 4 + N_CACHES]
    tm = x_ref.shape[0]
    a_halves, b_halves, z_halves = [], [], []
    for rows in _row_halves(tm):
        h = _rmsnorm(x_ref[rows, :], g_ref[...]).astype(BF16)
        cos = cos_ref[rows, :]
        sin = sin_ref[rows, :]
        lane = lax.broadcasted_iota(jnp.int32, cos.shape, 1)
        first_half = (lane & (HEAD_DIM - 1)) < HALF

        def rope(z):
            partner = jnp.where(first_half, pltpu.roll(z, LANES - HALF, axis=1), pltpu.roll(z, HALF, axis=1))
            return z * cos + partner * sin

        za = _mm(h, w_ref[:, 0:SLAB_A])
        chunks = []
        for c in range(SLAB_A // LANES):
            blk = za[:, c * LANES:(c + 1) * LANES]
            chunks.append(rope(blk) if c * LANES < ROPE_A else blk)
            a_ref[rows, c * LANES:(c + 1) * LANES] = chunks[-1]
        a_halves.append(chunks)
        zb = _mm(h, w_ref[:, SLAB_A:SLAB_A + SLAB_B])
        chunks = []
        for c in range(SLAB_B // LANES):
            blk = zb[:, c * LANES:(c + 1) * LANES]
            chunks.append(rope(blk) if c * LANES < ROPE_B else blk)
            b_ref[rows, c * LANES:(c + 1) * LANES] = chunks[-1]
        b_halves.append(chunks)
        z_halves.append(_mm(h, w_ref[:, SLAB_A + SLAB_B:MIX_W]))
    whole = lambda halves, c: jnp.concatenate([chunks[c] for chunks in halves], axis=0)

    if not per_row_state:
        pairs = A_W // LANES
        chunk_of = lambda part, g, j: whole(a_halves, (part * A_QKV + g * A_W) // LANES + j)

        def write(ref, chunk, rows):
            ref[0, :, :] = jnp.concatenate([ch[tm - rows:tm, :].T for ch in chunk], axis=0)

        last_group = A_NG - 1
        assert A_GROUPS[last_group][0] == seq_tiles * tm and A_GROUPS[1][0] == tm and A_GROUPS[0][0] <= tm
        for part in (1, 2):
            write(cache_refs[2 * last_group + part - 1], [chunk_of(part, last_group, j) for j in range(pairs)], tm)

        @pl.when(pl.program_id(0) % seq_tiles == seq_tiles - 1)
        def _():
            for g in range(last_group):
                for part in (1, 2):
                    write(cache_refs[2 * g + part - 1], [chunk_of(part, g, j) for j in range(pairs)], A_GROUPS[g][0])
            write(cache_refs[6], [whole(b_halves, B_QW // LANES)], B_WINDOW)
            write(cache_refs[7], [whole(b_halves, B_QW // LANES + 1)], B_WINDOW)
    z = jnp.concatenate(z_halves, axis=0)
    zc = z[:, 2 * C_WIDTH:3 * C_WIDTH] * z[:, 0:C_WIDTH]

    row = lax.broadcasted_iota(jnp.int32, (tm, C_WIDTH), 0)
    z1 = pltpu.roll(zc, 1, axis=0)
    z2 = pltpu.roll(zc, 2, axis=0)
    if per_row_state:
        t = row & (DEC_SEQ - 1)
        z1 = jnp.where(t == 0, s1_ref[...], z1)
        z2 = jnp.where(t == 0, s2_ref[...], jnp.where(t == 1, s1_ref[...], z2))
        tail_ref[...] = zc
    else:
        @pl.when(pl.program_id(0) % seq_tiles == 0)
        def _():
            carry_ref[...] = jnp.zeros(carry_ref.shape, F32)
        z1 = jnp.where(row == 0, carry_ref[SUBLANES - 1:SUBLANES, :], z1)
        z2 = jnp.where(row == 0, carry_ref[SUBLANES - 2:SUBLANES - 1, :],
                       jnp.where(row == 1, carry_ref[SUBLANES - 1:SUBLANES, :], z2))
        carry_ref[...] = zc[tm - SUBLANES:tm, :]
        tail_ref[...] = zc[tm - SUBLANES:tm, :]
    conv = cw_ref[0:1, :] * z2 + cw_ref[1:2, :] * z1 + cw_ref[2:3, :] * zc
    c_ref[:, 0:C_WIDTH] = z[:, C_WIDTH:2 * C_WIDTH] * conv
    c_ref[:, C_WIDTH:SLAB_C] = z[:, 3 * C_WIDTH:4 * C_WIDTH]


def _inproj(x, layer, g, w_in, cos, sin, convw, tm, table_blocks, seq_tiles, row_state=None, prev_caches=None):
    m = x.shape[0]
    per_row_state = row_state is not None
    aliased = prev_caches is not None
    row = lambda i: (i, 0)
    tab = lambda i: (i % table_blocks, 0)
    in_specs = [pl.BlockSpec((tm, D_MODEL), row), _resident((1, D_MODEL), layer), _resident((D_MODEL, MIX_W), layer),
                pl.BlockSpec((tm, LANES), tab), pl.BlockSpec((tm, LANES), tab), _resident((C_CONV, C_WIDTH), layer)]
    args = [x, g, w_in, cos, sin, convw]
    out_specs = [pl.BlockSpec((tm, SLAB_A), row), pl.BlockSpec((tm, SLAB_B), row), pl.BlockSpec((tm, SLAB_C), row)]
    out_shape = [jax.ShapeDtypeStruct((m, SLAB_A), F32), jax.ShapeDtypeStruct((m, SLAB_B), F32),
                 jax.ShapeDtypeStruct((m, SLAB_C), F32)]
    scratch, aliases = [], {}
    if per_row_state:
        in_specs += [pl.BlockSpec((tm, C_WIDTH), row)] * 2
        args += list(row_state)
        out_specs.append(pl.BlockSpec((tm, C_WIDTH), row))
        out_shape.append(jax.ShapeDtypeStruct((m, C_WIDTH), F32))
    else:
        scratch = [pltpu.VMEM((SUBLANES, C_WIDTH), F32)]
        out_specs.append(pl.BlockSpec((SUBLANES, C_WIDTH), row))
        out_shape.append(jax.ShapeDtypeStruct((m // tm * SUBLANES, C_WIDTH), F32))
        n_seqs = m // (tm * seq_tiles)
        widths = [(A_W, w) for w, _ in A_GROUPS for _ in range(2)] + [(B_KW, B_WINDOW)] * 2
        for j, (rows_, window) in enumerate(widths):
            per_tile = window == tm * seq_tiles
            idx = ((lambda i: (layer, i // seq_tiles, 0, i % seq_tiles)) if per_tile
                   else (lambda i: (layer, i // seq_tiles, 0, 0)))
            out_specs.append(pl.BlockSpec((None, 1, rows_, tm if per_tile else window), idx))
            out_shape.append(jax.ShapeDtypeStruct((DEPTH, n_seqs, rows_, window), F32))
        if aliased:
            aliases = {len(args) + j: 4 + j for j in range(N_CACHES)}
            in_specs += [pl.BlockSpec(memory_space=pl.ANY)] * N_CACHES
            args += list(prev_caches)
    return pl.pallas_call(
        functools.partial(_inproj_body, seq_tiles=seq_tiles, per_row_state=per_row_state, aliased=aliased),
        grid=(m // tm,),
        in_specs=in_specs,
        out_specs=out_specs,
        out_shape=out_shape,
        input_output_aliases=aliases,
        scratch_shapes=scratch,
        compiler_params=_params(1),
        name="in_proj",
    )(*args)


def _band_mask_t(nk):
    kj = lax.broadcasted_iota(jnp.int32, (nk, BLOCK), 0)
    qi = lax.broadcasted_iota(jnp.int32, (nk, BLOCK), 1)
    dist = qi + (nk - BLOCK) - kj
    return (dist >= 0) & (dist <= BLOCK)


def _softmax_t(s, mask, sink=None):
    s = jnp.where(mask, s, NEG_INF)
    m = jnp.max(s, axis=0, keepdims=True)
    if sink is not None:
        m = jnp.maximum(m, sink)
    p = jnp.exp(s - m)
    den = jnp.sum(p, axis=0, keepdims=True)
    if sink is not None:
        den = den + jnp.exp(sink - m)
    return p, 1.0 / den, m + jnp.log(den)


def _band_softmax_t(s, sink=None):
    j = lax.broadcasted_iota(jnp.int32, (BLOCK, BLOCK), 0)
    qi = lax.broadcasted_iota(jnp.int32, (BLOCK, BLOCK), 1)
    above = j > qi
    s_prev, s_own = s[0:BLOCK], s[BLOCK:2 * BLOCK]
    folded = jnp.where(above, s_prev, s_own)
    diag = jnp.sum(jnp.where(j == qi, s_prev, 0.0), axis=0, keepdims=True)
    m = jnp.maximum(jnp.max(folded, axis=0, keepdims=True), diag)
    if sink is not None:
        m = jnp.maximum(m, sink)
    p = jnp.exp(folded - m)
    p_diag = jnp.exp(diag - m)
    den = jnp.sum(p, axis=0, keepdims=True) + p_diag
    if sink is not None:
        den = den + jnp.exp(sink - m)
    unfolded = jnp.concatenate([jnp.where(above, p, 0.0), jnp.where(above, 0.0, p)], axis=0)
    return unfolded, p_diag, 1.0 / den, m + jnp.log(den)


def _attn_a_prompt_body(q_ref, k_ref, v_ref, o_ref, lse_ref, *, d):
    nb = SEQ // d // BLOCK
    pairs = q_ref.shape[-1] // LANES
    first_head_rows = lax.broadcasted_iota(jnp.int32, (LANES, BLOCK), 0) < HEAD_DIM

    def rows(start, n):
        return pl.ds(start, n) if d == 1 else pl.ds(start, n, stride=d)

    def block(q0, k0, nk):
        mask = _band_mask_t(nk)
        lo = _low_lanes(nk)
        for hp in range(pairs):
            cols = slice(hp * LANES, (hp + 1) * LANES)
            q = (q_ref[0, rows(q0, BLOCK), cols] * SCALE).astype(BF16)
            k = k_ref[0, rows(k0, nk), cols]
            v = v_ref[0, rows(k0, nk), cols]
            k2 = jnp.concatenate([jnp.where(lo, k, 0.0), jnp.where(lo, 0.0, k)], axis=0).astype(BF16)
            st = _mm_nt(k2, q)
            vt32 = v.T
            if nk == BLOCK:
                p0, r0, l0 = _softmax_t(st[0:nk], mask)
                p1, r1, l1 = _softmax_t(st[nk:2 * nk], mask)
                d0 = d1 = 0.0
            else:
                p0, pd0, r0, l0 = _band_softmax_t(st[0:nk])
                p1, pd1, r1, l1 = _band_softmax_t(st[nk:2 * nk])
                d0, d1 = vt32[:, 0:BLOCK] * pd0, vt32[:, 0:BLOCK] * pd1
            ot = _mm(vt32.astype(BF16), jnp.concatenate([p0, p1], axis=1).astype(BF16))
            o_t = jnp.where(first_head_rows, (ot[:, 0:BLOCK] + d0) * r0, (ot[:, BLOCK:2 * BLOCK] + d1) * r1)
            l_t = jnp.where(first_head_rows, jnp.broadcast_to(l0, (LANES, BLOCK)),
                            jnp.broadcast_to(l1, (LANES, BLOCK)))
            o_ref[0, rows(q0, BLOCK), cols] = o_t.T
            lse_ref[0, rows(q0, BLOCK), cols] = l_t.T

    for r in range(d):
        block(r, r, BLOCK)
        if nb > 1:
            def body(i, carry):
                q0 = r + i * (BLOCK * d)
                block(q0, q0 - BLOCK * d, 2 * BLOCK)
                return carry
            lax.fori_loop(1, nb, body, 0, unroll=nb - 1)


def _attn_a_prompt(slab_a, g, d):
    view = slab_a.reshape(BATCH, SEQ, SLAB_A)
    width = A_W if d == 1 else LANES
    steps = A_W // width
    spec = lambda off: pl.BlockSpec((1, SEQ, width), lambda b, hp: (b, 0, off * steps + hp))
    oshape = jax.ShapeDtypeStruct((BATCH, SEQ, A_W), F32)
    o, lse = pl.pallas_call(
        functools.partial(_attn_a_prompt_body, d=d),
        grid=(BATCH, steps),
        in_specs=[spec(g), spec(A_NG + g), spec(2 * A_NG + g)],
        out_specs=[spec(0), spec(0)],
        out_shape=[oshape, oshape],
        compiler_params=_params(2),
        name=f"attn_a{g + 1}_prompt",
    )(view, view, view)
    return o.reshape(BATCH * SEQ, A_W), lse.reshape(BATCH * SEQ, A_W)


def _attn_b_prompt_body(q_ref, k_ref, v_ref, sink_ref, o_ref, *, nb):
    def block(q0, k0, nk):
        mask = _band_mask_t(nk)
        lo = _low_lanes(nk)
        k = k_ref[0, pl.ds(k0, nk), :]
        k_sw = pltpu.roll(k, HEAD_DIM, axis=1)
        vt32 = v_ref[0, pl.ds(k0, nk), :].T
        vt = vt32.astype(BF16)
        for kv in range(B_KV_HEADS):
            on_lo, on_hi = (k, k_sw) if kv == 0 else (k_sw, k)
            k2 = jnp.concatenate([jnp.where(lo, on_lo, 0.0), jnp.where(lo, 0.0, on_hi)], axis=0).astype(BF16)
            cols = slice(2 * kv * LANES, (2 * kv + 2) * LANES)
            q2 = q_ref[0, pl.ds(q0, BLOCK), cols] * SCALE
            q2 = jnp.concatenate([q2[:, 0:LANES], q2[:, LANES:2 * LANES]], axis=0).astype(BF16)
            st = _mm_nt(k2, q2)
            kv_rows = slice(kv * HEAD_DIM, (kv + 1) * HEAD_DIM)
            ps, rs, diag_terms = [], [], []
            for g in range(B_GROUP):
                half, chunk = g % 2, g // 2
                sink = sink_ref[kv * B_GROUP + g:kv * B_GROUP + g + 1, :]
                s = st[half * nk:(half + 1) * nk, chunk * BLOCK:(chunk + 1) * BLOCK]
                if nk == BLOCK:
                    p, r, _ = _softmax_t(s, mask, sink)
                else:
                    p, p_diag, r, _ = _band_softmax_t(s, sink)
                    diag_terms.append(vt32[kv_rows, 0:BLOCK] * p_diag)
                ps.append(p)
                rs.append(r)
            ot = _mm(vt, jnp.concatenate(ps, axis=1).astype(BF16))[kv_rows, :]
            if diag_terms:
                ot = ot + jnp.concatenate(diag_terms, axis=1)
            oj = ot * jnp.concatenate(rs, axis=1)
            for chunk in range(2):
                o_t = jnp.concatenate([oj[:, (2 * chunk) * BLOCK:(2 * chunk + 1) * BLOCK],
                                       oj[:, (2 * chunk + 1) * BLOCK:(2 * chunk + 2) * BLOCK]], axis=0)
                c = 2 * kv + chunk
                o_ref[0, pl.ds(q0, BLOCK), c * LANES:(c + 1) * LANES] = o_t.T

    block(0, 0, BLOCK)

    def body(i, carry):
        q0 = pl.multiple_of(i * BLOCK, BLOCK)
        block(q0, pl.multiple_of(q0 - BLOCK, BLOCK), 2 * BLOCK)
        return carry
    lax.fori_loop(1, nb, body, 0, unroll=15)


def _attn_b_prompt(slab_b, layer, sinks):
    view = slab_b.reshape(BATCH, SEQ, SLAB_B)
    nq = B_QW // LANES
    o = pl.pallas_call(
        functools.partial(_attn_b_prompt_body, nb=SEQ // BLOCK),
        grid=(BATCH,),
        in_specs=[pl.BlockSpec((1, SEQ, B_QW), lambda b: (b, 0, 0)),
                  pl.BlockSpec((1, SEQ, B_KW), lambda b: (b, 0, nq)),
                  pl.BlockSpec((1, SEQ, B_KW), lambda b: (b, 0, nq + 1)),
                  _resident((B_Q_HEADS, LANES), layer)],
        out_specs=pl.BlockSpec((1, SEQ, B_QW), lambda b: (b, 0, 0)),
        out_shape=jax.ShapeDtypeStruct((BATCH, SEQ, B_QW), F32),
        compiler_params=_params(1),
        name="attn_b_prompt",
    )(view, view, view, sinks)
    return o.reshape(BATCH * SEQ, B_QW)


def _sample_masks(rows, lb, window, dil):
    assert dil & (dil - 1) == 0 and DEC_SEQ & (DEC_SEQ - 1) == 0
    t_c = lax.broadcasted_iota(jnp.int32, (rows, lb), 0) & (DEC_SEQ - 1)
    dist_c = lb + t_c - lax.broadcasted_iota(jnp.int32, (rows, lb), 1)
    mask_c = (dist_c >= 0) & (dist_c <= window) & ((dist_c & (dil - 1)) == 0)
    t_n = lax.broadcasted_iota(jnp.int32, (rows, DEC_SEQ), 0) & (DEC_SEQ - 1)
    dist_n = t_n - lax.broadcasted_iota(jnp.int32, (rows, DEC_SEQ), 1)
    mask_n = (dist_n >= 0) & (dist_n <= window) & ((dist_n & (dil - 1)) == 0)
    return mask_c, mask_n


def _sample_attend(q, kt, vt, kn, vn, masks, sink=None):
    mask_c, mask_n = masks
    s_c = jnp.where(mask_c, _mm(q.astype(BF16), kt), NEG_INF)
    s_n = jnp.where(mask_n, _mm_nt(q, kn), NEG_INF)
    m = jnp.maximum(jnp.max(s_c, axis=-1, keepdims=True), jnp.max(s_n, axis=-1, keepdims=True))
    if sink is not None:
        m = jnp.maximum(m, sink)
    p_c = jnp.exp(s_c - m)
    p_n = jnp.exp(s_n - m)
    den = jnp.sum(p_c, axis=-1, keepdims=True) + jnp.sum(p_n, axis=-1, keepdims=True)
    if sink is not None:
        den = den + jnp.exp(sink - m)
    o = _mm_nt(p_c.astype(BF16), vt) + _mm(p_n, vn)
    return o * (1.0 / den), m + jnp.log(den)


def _write_shifted_cache(dst_ref, src_ref, s, heads, new_rows, lb):
    padded = jnp.concatenate([jnp.zeros((LANES - DEC_SEQ, LANES), F32), new_rows], axis=0)
    new_t = padded.T
    keep = lax.broadcasted_iota(jnp.int32, (LANES, LANES), 1) < LANES - DEC_SEQ
    rotated = lambda j: pltpu.roll(src_ref[0, s, heads, :, j * LANES:(j + 1) * LANES].reshape(LANES, LANES),
                                   LANES - DEC_SEQ, axis=1)
    n_tiles = lb // LANES
    cur = rotated(0)
    for j in range(n_tiles):
        nxt = rotated(j + 1) if j + 1 < n_tiles else new_t
        dst_ref[0, s, heads, :, j * LANES:(j + 1) * LANES] = jnp.where(keep, cur, nxt).reshape(2, HEAD_DIM, LANES)
        cur = nxt


SAMPLE_SEQS_PER_STEP = 2


def _sample_attn_body(*refs, aliased, n_seq):
    for s in range(n_seq):
        _sample_attn_one(refs, aliased, s)


def _sample_attn_one(refs, aliased, s):
    sa_ref, sb_ref, sink_ref = refs[:3]
    cache_refs = refs[3:11]
    pos = 11 + (8 if aliased else 0)
    oa_ref, ob_ref = refs[pos:pos + 2]
    out_refs = refs[pos + 2:pos + 10]
    lo8 = _low_lanes(DEC_SEQ)
    seq_rows = slice(s * DEC_SEQ, (s + 1) * DEC_SEQ)

    o_g, l_g = [], []
    for g, (window, dil) in enumerate(A_GROUPS):
        kc_ref, vc_ref, ko_ref, vo_ref = cache_refs[2 * g], cache_refs[2 * g + 1], out_refs[2 * g], out_refs[2 * g + 1]
        lb = kc_ref.shape[-1]
        masks = _sample_masks(2 * DEC_SEQ, lb, window, dil)
        o_pairs, l_pairs = [], []
        for c in range(A_W // LANES):
            col = lambda part: slice(part * A_QKV + g * A_W + c * LANES, part * A_QKV + g * A_W + (c + 1) * LANES)
            heads = slice(2 * c, 2 * c + 2)
            q = sa_ref[seq_rows, col(0)] * SCALE
            kn, vn = sa_ref[seq_rows, col(1)], sa_ref[seq_rows, col(2)]
            kt = kc_ref[0, s, heads].reshape(LANES, lb)
            vt = vc_ref[0, s, heads].reshape(LANES, lb)
            q2 = jnp.concatenate([jnp.where(lo8, q, 0.0), jnp.where(lo8, 0.0, q)], axis=0)
            o, lse = _sample_attend(q2, kt.astype(BF16), vt.astype(BF16), kn, vn, masks)
            o_pairs.append(jnp.where(lo8, o[0:DEC_SEQ], o[DEC_SEQ:]))
            l_pairs.append(jnp.where(lo8, lse[0:DEC_SEQ], lse[DEC_SEQ:]))
            _write_shifted_cache(ko_ref, kc_ref, s, heads, kn, lb)
            _write_shifted_cache(vo_ref, vc_ref, s, heads, vn, lb)
        o_g.append(o_pairs)
        l_g.append(l_pairs)
    for c in range(A_W // LANES):
        l1, l2, l3 = l_g[0][c], l_g[1][c], l_g[2][c]
        lm = jnp.maximum(jnp.maximum(l1, l2), l3)
        e1, e2, e3 = jnp.exp(l1 - lm), jnp.exp(l2 - lm), jnp.exp(l3 - lm)
        es = e1 + e2 + e3
        oa_ref[seq_rows, c * LANES:(c + 1) * LANES] = ((e1 / es) * o_g[0][c] + (e2 / es) * o_g[1][c]
                                                       + (e3 / es) * o_g[2][c])

    kc_ref, vc_ref, ko_ref, vo_ref = cache_refs[6], cache_refs[7], out_refs[6], out_refs[7]
    lb = kc_ref.shape[-1]
    rows = B_GROUP * DEC_SEQ
    masks = _sample_masks(rows, lb, B_WINDOW, 1)
    both = slice(0, B_KV_HEADS)
    ktb = kc_ref[0, s].reshape(LANES, lb).astype(BF16)
    vtb = vc_ref[0, s].reshape(LANES, lb).astype(BF16)
    kn, vn = sb_ref[seq_rows, B_QW:B_QW + B_KW], sb_ref[seq_rows, B_QW + B_KW:SLAB_B]
    halves = {}
    for kv in range(B_KV_HEADS):
        parts = []
        for gq in range(B_GROUP):
            head = kv * B_GROUP + gq
            q = sb_ref[seq_rows, (head // 2) * LANES:(head // 2 + 1) * LANES] * SCALE
            q = jnp.where(lo8, q, 0.0) if head % 2 == 0 else jnp.where(lo8, 0.0, q)
            parts.append(q if head % 2 == kv else pltpu.roll(q, HEAD_DIM, axis=1))
        q4 = jnp.concatenate(parts, axis=0)
        sink = sink_ref[kv * rows:(kv + 1) * rows, 0:1]
        o, _ = _sample_attend(q4, ktb, vtb, kn, vn, masks, sink)
        for gq in range(B_GROUP):
            head = kv * B_GROUP + gq
            oh = o[gq * DEC_SEQ:(gq + 1) * DEC_SEQ]
            halves[head] = oh if head % 2 == kv else pltpu.roll(oh, HEAD_DIM, axis=1)
    for c in range(B_QW // LANES):
        ob_ref[seq_rows, c * LANES:(c + 1) * LANES] = jnp.where(lo8, halves[2 * c], halves[2 * c + 1])
    _write_shifted_cache(ko_ref, kc_ref, s, both, kn, lb)
    _write_shifted_cache(vo_ref, vc_ref, s, both, vn, lb)


def _sample_attn(layer, slab_a, slab_b, sinks, caches, prev):
    aliased = prev is not None
    n_seq = SAMPLE_SEQS_PER_STEP
    rows = n_seq * DEC_SEQ
    row = lambda b: (b, 0)
    cspec = lambda c: pl.BlockSpec((1, n_seq) + c.shape[2:], lambda b: (layer, b, 0, 0, 0))
    in_specs = [pl.BlockSpec((rows, SLAB_A), row), pl.BlockSpec((rows, SLAB_B), row),
                _resident(sinks.shape[1:], layer)]
    in_specs += [cspec(c) for c in caches]
    args = [slab_a, slab_b, sinks] + list(caches)
    aliases = {}
    if aliased:
        aliases = {len(args) + j: 2 + j for j in range(8)}
        in_specs += [pl.BlockSpec(memory_space=pl.ANY)] * 8
        args += list(prev)
    m = slab_a.shape[0]
    res = pl.pallas_call(
        functools.partial(_sample_attn_body, aliased=aliased, n_seq=n_seq),
        grid=(DEC_BATCH // n_seq,),
        in_specs=in_specs,
        out_specs=[pl.BlockSpec((rows, A_W), row), pl.BlockSpec((rows, B_QW), row)] + [cspec(c) for c in caches],
        out_shape=[jax.ShapeDtypeStruct((m, A_W), F32), jax.ShapeDtypeStruct((m, B_QW), F32)]
                  + [jax.ShapeDtypeStruct(c.shape, F32) for c in caches],
        input_output_aliases=aliases,
        compiler_params=_params(1),
        name="sample_attn",
    )(*args)
    return res[0], res[1], list(res[2:])


def _ssm_body(u_ref, are_ref, aim_ref, b_ref, cbig_ref, dvec_ref, wglu_ref, x0re_ref, x0im_ref,
              od_ref, sre_ref, sim_ref, bu_ref, xs_ref, *, steps, pitch):
    @pl.when(pl.program_id(1) == 0)
    def _():
        sre_ref[0] = x0re_ref[0]
        sim_ref[0] = x0im_ref[0]

    n_re = D_NS // LANES
    tiles = lambda x: [x[:, j * LANES:(j + 1) * LANES] for j in range(x.shape[1] // LANES)]

    plane = lambda s: slice(s * pitch, s * pitch + steps)
    u = u_ref[...].reshape(SUBLANES * steps, D_WIDTH)
    bu = _mm(u.astype(BF16), b_ref[...])
    for s in range(SUBLANES):
        for j, tile in enumerate(tiles(bu[s * steps:(s + 1) * steps])):
            bu_ref[j, plane(s), :] = tile
    a_re = are_ref[...]
    a_im = aim_ref[...]

    def step(t, carry):
        re, im = carry
        at_t = pl.ds(t, SUBLANES, stride=pitch)
        b_re = jnp.concatenate([bu_ref[j, at_t, :] for j in range(n_re)], axis=1)
        b_im = jnp.concatenate([bu_ref[n_re + j, at_t, :] for j in range(n_re)], axis=1)
        re, im = a_re * re - a_im * im + b_re, a_re * im + a_im * re + b_im
        for j, (tr, ti) in enumerate(zip(tiles(re), tiles(im))):
            xs_ref[j, at_t, :] = tr
            xs_ref[n_re + j, at_t, :] = ti
        return re, im

    re, im = lax.fori_loop(0, steps, step, (sre_ref[0], sim_ref[0]), unroll=min(steps, SUBLANES))
    sre_ref[0] = re
    sim_ref[0] = im
    group = SUBLANES // 2 if SUBLANES * steps >= 2 * SAMPLE_TILE else SUBLANES
    for s0 in range(0, SUBLANES, group):
        seqs = range(s0, s0 + group)
        xs = jnp.concatenate([jnp.concatenate([xs_ref[j, plane(s), :] for j in range(2 * n_re)], axis=1)
                              for s in seqs], axis=0)
        y = _mm(xs.astype(BF16), cbig_ref[...]) + dvec_ref[...] * u[s0 * steps:(s0 + group) * steps]
        y = _mm(jax.nn.gelu(y).astype(BF16), wglu_ref[...])
        od = y[:, 0:D_WIDTH] * jax.nn.sigmoid(y[:, D_WIDTH:2 * D_WIDTH])
        od_ref[s0:s0 + group] = od.reshape(group, steps, D_WIDTH)


def _ssm(slab_c, bsz, length, layer, a_re, a_im, bbig, cbig, dvec, wglu, x0_re, x0_im, steps):
    n = bsz // SUBLANES
    pitch = steps if steps <= SUBLANES else steps + SUBLANES
    view = slab_c.reshape(bsz, length, SLAB_C)
    u_spec = pl.BlockSpec((SUBLANES, steps, D_WIDTH), lambda i, c: (i, c, SLAB_C // D_WIDTH - 1))
    o_spec = pl.BlockSpec((SUBLANES, steps, D_WIDTH), lambda i, c: (i, c, 0))
    st_spec = pl.BlockSpec((1, SUBLANES, D_NS), lambda i, c: (i, 0, 0))
    st_shape = jax.ShapeDtypeStruct((n, SUBLANES, D_NS), F32)
    od, s_re, s_im = pl.pallas_call(
        functools.partial(_ssm_body, steps=steps, pitch=pitch),
        grid=(n, length // steps),
        in_specs=[u_spec, _resident((SUBLANES, D_NS), layer), _resident((SUBLANES, D_NS), layer),
                  _resident((D_WIDTH, 2 * D_NS), layer), _resident((2 * D_NS, D_WIDTH), layer),
                  _resident((1, D_WIDTH), layer), _resident((D_WIDTH, 2 * D_WIDTH), layer), st_spec, st_spec],
        out_specs=[o_spec, st_spec, st_spec],
        out_shape=[jax.ShapeDtypeStruct((bsz, length, D_WIDTH), F32), st_shape, st_shape],
        scratch_shapes=[pltpu.VMEM((2 * D_NS // LANES, SUBLANES * pitch, LANES), F32)] * 2,
        compiler_params=_params(2),
        name="ssm",
    )(view, a_re, a_im, bbig, cbig, dvec, wglu, x0_re, x0_im)
    return od.reshape(bsz * length, D_WIDTH), s_re, s_im


def _merge_body(*refs, n_groups):
    x_ref = refs[0]
    a_refs = refs[1:1 + (2 * n_groups if n_groups > 1 else 1)]
    ob_ref, oc_ref, od_ref, gpre_ref, wgl_ref, wa_ref, wb_ref, wc_ref, wd_ref, wout_ref, gpost_ref, y_ref = \
        refs[1 + len(a_refs):]
    tm = x_ref.shape[0]
    for rows in _row_halves(tm):
        x = x_ref[rows, :]
        h = _rmsnorm(x, gpre_ref[...]).astype(BF16)
        if n_groups > 1:
            lses = [a_refs[2 * g + 1][rows, :] for g in range(n_groups)]
            lm = functools.reduce(jnp.maximum, lses)
            es = [jnp.exp(l - lm) for l in lses]
            den = functools.reduce(lambda a, b: a + b, es)
            o_a = functools.reduce(lambda a, b: a + b,
                                   [(es[g] / den) * a_refs[2 * g][rows, :] for g in range(n_groups)])
        else:
            o_a = a_refs[0][rows, :]
        branches = ((o_a, wa_ref), (ob_ref[rows, :], wb_ref), (oc_ref[rows, :], wc_ref), (od_ref[rows, :], wd_ref))
        merged = None
        for j, (o, w_ref) in enumerate(branches):
            gate = jax.nn.sigmoid(_mm(h, wgl_ref[:, j * D_MODEL:(j + 1) * D_MODEL]))
            term = gate * _mm(o.astype(BF16), w_ref[...])
            merged = term if merged is None else merged + term
        mix = _mm(merged.astype(BF16), wout_ref[...])
        y_ref[rows, :] = x + _rmsnorm(mix, gpost_ref[...])


def _merge(x, a_parts, ob, slab_c, od, layer, gpre, w_in, wa, wb, wc, wd, wout, gpost, tm):
    m = x.shape[0]
    row = lambda i: (i, 0)
    r256 = pl.BlockSpec((tm, 256), row)
    n_groups = max(1, len(a_parts) // 2)
    return pl.pallas_call(
        functools.partial(_merge_body, n_groups=n_groups),
        grid=(m // tm,),
        in_specs=[pl.BlockSpec((tm, D_MODEL), row)] + [r256] * len(a_parts)
                 + [pl.BlockSpec((tm, B_QW), row), r256, r256,
                    _resident((1, D_MODEL), layer), _resident((D_MODEL, N_BRANCH * D_MODEL), layer, col=1),
                    _resident((A_W, D_MODEL), layer), _resident((B_QW, D_MODEL), layer),
                    _resident((C_WIDTH, D_MODEL), layer), _resident((D_WIDTH, D_MODEL), layer),
                    _resident((D_MODEL, D_MODEL), layer), _resident((1, D_MODEL), layer)],
        out_specs=pl.BlockSpec((tm, D_MODEL), row),
        out_shape=jax.ShapeDtypeStruct((m, D_MODEL), F32),
        compiler_params=_params(1),
        name="merge",
    )(x, *a_parts, ob, slab_c, od, gpre, w_in, wa, wb, wc, wd, wout, gpost)


def _ffn_body(x_ref, p_ref, gpre_ref, wg_ref, wu_ref, wdn_ref, gpost_ref, wple_ref, wpg_ref, y_ref):
    x = x_ref[...]
    h = _rmsnorm(x, gpre_ref[...]).astype(BF16)
    act = jax.nn.silu(_mm(h, wg_ref[...])) * _mm(h, wu_ref[...])
    f = _mm(act.astype(BF16), wdn_ref[...])
    x = x + _rmsnorm(f, gpost_ref[...])
    gate = jax.nn.sigmoid(_mm(x.astype(BF16), wpg_ref[...]))
    y_ref[...] = x + gate * _mm(p_ref[...].astype(BF16), wple_ref[...])


def _ffn(x, p_all, layer, gpre, wg, wu, wdn, gpost, wple, wpg, tm):
    m = x.shape[0]
    row = lambda i: (i, 0)
    return pl.pallas_call(
        _ffn_body,
        grid=(m // tm,),
        in_specs=[pl.BlockSpec((tm, D_MODEL), row), pl.BlockSpec((None, tm, PLE_DIM), lambda i: (layer, i, 0)),
                  _resident((1, D_MODEL), layer), _resident((D_MODEL, D_FF), layer), _resident((D_MODEL, D_FF), layer),
                  _resident((D_FF, D_MODEL), layer), _resident((1, D_MODEL), layer),
                  _resident((PLE_DIM, D_MODEL), layer), _resident((D_MODEL, D_MODEL), layer)],
        out_specs=pl.BlockSpec((tm, D_MODEL), row),
        out_shape=jax.ShapeDtypeStruct((m, D_MODEL), F32),
        compiler_params=_params(1),
        name="ffn_ple",
    )(x, p_all, gpre, wg, wu, wdn, gpost, wple, wpg)


def _rope_tables(pos):
    inv = ROPE_THETA ** (-jnp.arange(HALF, dtype=F32) / HALF)
    ang = pos.astype(F32)[:, None] * inv[None, :]
    cos, sin = jnp.cos(ang), jnp.sin(ang)
    cos_t = jnp.tile(cos, (1, LANES // HALF))
    sin_t = jnp.tile(jnp.concatenate([-sin, sin], axis=1), (1, LANES // HEAD_DIM))
    return cos_t, sin_t


def _ssm_weights(lam_re, lam_im, log_dt, b_re, b_im, c_re, c_im):
    lam = lax.complex(lam_re, lam_im)
    dt = jnp.exp(log_dt)[..., None]
    a_bar = jnp.exp(lam * dt)
    b_bar = ((a_bar - 1.0) / lam)[..., None] * lax.complex(b_re, b_im)
    eye = jnp.eye(D_NGROUPS, dtype=F32)
    pack_b = lambda b: jnp.einsum('lgni,gh->lgihn', b, eye).reshape(DEPTH, D_WIDTH, D_NS)
    pack_c = lambda c: jnp.einsum('lgin,gh->lgnhi', c, eye).reshape(DEPTH, D_NS, D_WIDTH)
    bbig = jnp.concatenate([pack_b(jnp.real(b_bar)), pack_b(jnp.imag(b_bar))], axis=2)
    cbig = jnp.concatenate([pack_c(c_re), -pack_c(c_im)], axis=1)
    tile8 = lambda a: jnp.broadcast_to(a.reshape(DEPTH, 1, D_NS), (DEPTH, SUBLANES, D_NS))
    return tile8(jnp.real(a_bar)), tile8(jnp.imag(a_bar)), bbig.astype(BF16), cbig.astype(BF16)


def _key_major(c):
    return jnp.transpose(c, (0, 1, 3, 4, 2))


def _row_major(c):
    return jnp.transpose(c, (0, 1, 4, 2, 3))


def kernel(x_prompt, x_sample, p_prompt, p_sample, cache_a1_k, cache_a1_v, cache_a2_k, cache_a2_v, cache_a3_k, cache_a3_v, cache_b_k, cache_b_v, state_c_conv, state_d_re, state_d_im, norm_mix_pre, norm_mix_post, norm_ffn_pre, norm_ffn_post, w_in, attn_sinks, conv_c_w, ssm_lam_re, ssm_lam_im, ssm_log_dt, ssm_b_re, ssm_b_im, ssm_c_re, ssm_c_im, ssm_d, w_d_glu, w_br_a, w_br_b, w_br_c, w_br_d, w_out, w_ffn_gate, w_ffn_up, w_ffn_down, w_ple, w_ple_gate):
    assert x_prompt.shape == (BATCH, SEQ, D_MODEL) and x_sample.shape == (DEC_BATCH, DEC_SEQ, D_MODEL)
    assert w_in.shape == (DEPTH, D_MODEL, 2 * MIX_W)
    assert all(min(w, PAST_LEN) == w for w, _ in A_GROUPS) and min(B_WINDOW, PAST_LEN) == B_WINDOW
    mp, ms = BATCH * SEQ, DEC_BATCH * DEC_SEQ
    tm_p, tm_s = PROMPT_TILE, SAMPLE_TILE

    cos_p, sin_p = _rope_tables(jnp.arange(SEQ, dtype=jnp.int32))
    cos_s, sin_s = _rope_tables(PAST_LEN + jnp.arange(DEC_SEQ, dtype=jnp.int32))
    cos_s, sin_s = jnp.tile(cos_s, (tm_s // DEC_SEQ, 1)), jnp.tile(sin_s, (tm_s // DEC_SEQ, 1))

    caches = [_key_major(c) for c in (cache_a1_k, cache_a1_v, cache_a2_k, cache_a2_v, cache_a3_k, cache_a3_v,
                                      cache_b_k, cache_b_v)]
    yp = x_prompt.reshape(mp, D_MODEL)
    ys = x_sample.reshape(ms, D_MODEL)
    new_caches = None
    prompt_caches = None
    st_p = [[] for _ in range(3)]
    conv_s, dre_s, dim_s = [], [], []

    rows3 = lambda a: a.reshape(DEPTH, 1, -1)
    w_in_b = w_in.astype(BF16)
    g_mix_pre = rows3(norm_mix_pre)
    ssm_w = _ssm_weights(ssm_lam_re, ssm_lam_im, ssm_log_dt, ssm_b_re, ssm_b_im, ssm_c_re, ssm_c_im) + (
        rows3(ssm_d), w_d_glu.astype(BF16))
    sinks_p = jnp.broadcast_to(attn_sinks.reshape(DEPTH, B_Q_HEADS, 1), (DEPTH, B_Q_HEADS, LANES))
    sinks_s = jnp.broadcast_to(jnp.repeat(attn_sinks.reshape(DEPTH, B_Q_HEADS), DEC_SEQ, axis=1)[..., None],
                               (DEPTH, B_Q_HEADS * DEC_SEQ, LANES))
    merge_w = (g_mix_pre, w_in_b, w_br_a.astype(BF16), w_br_b.astype(BF16), w_br_c.astype(BF16),
               w_br_d.astype(BF16), w_out.astype(BF16), rows3(norm_mix_post))
    ffn_w = (rows3(norm_ffn_pre), w_ffn_gate.astype(BF16), w_ffn_up.astype(BF16), w_ffn_down.astype(BF16),
             rows3(norm_ffn_post), w_ple.astype(BF16), w_ple_gate.astype(BF16))
    zero_state = jnp.zeros((BATCH // SUBLANES, SUBLANES, D_NS), F32)

    for i in range(DEPTH):
        sa, sb, sc, tail, *prompt_caches = _inproj(yp, i, g_mix_pre, w_in_b, cos_p, sin_p, conv_c_w, tm_p,
                                                   SEQ // tm_p, SEQ // tm_p, prev_caches=prompt_caches)
        a_parts = []
        for g, (w, d) in enumerate(A_GROUPS):
            a_parts += list(_attn_a_prompt(sa, g, d))
        ob = _attn_b_prompt(sb, i, sinks_p)
        od, s_re, s_im = _ssm(sc, BATCH, SEQ, i, *ssm_w, zero_state, zero_state, SSM_CHUNK)
        x1 = _merge(yp, a_parts, ob, sc, od, i, *merge_w, tm_p)
        yp = _ffn(x1, p_prompt.reshape(DEPTH, mp, PLE_DIM), i, *ffn_w, tm_p)
        st_p[0].append(tail.reshape(BATCH, SEQ // tm_p, SUBLANES, C_WIDTH)[:, -1, SUBLANES - (C_CONV - 1):])
        st_p[1].append(s_re.reshape(BATCH, D_NGROUPS, D_STATE))
        st_p[2].append(s_im.reshape(BATCH, D_NGROUPS, D_STATE))

        s1 = jnp.broadcast_to(state_c_conv[i][:, None, 1, :], (DEC_BATCH, DEC_SEQ, C_WIDTH)).reshape(ms, C_WIDTH)
        s2 = jnp.broadcast_to(state_c_conv[i][:, None, 0, :], (DEC_BATCH, DEC_SEQ, C_WIDTH)).reshape(ms, C_WIDTH)
        sa, sb, sc, tail = _inproj(ys, i, g_mix_pre, w_in_b, cos_s, sin_s, conv_c_w, tm_s, 1, 1, row_state=(s1, s2))
        o_a, ob, new_caches = _sample_attn(i, sa, sb, sinks_s, caches, new_caches)
        x0_re = state_d_re[i].reshape(DEC_BATCH // SUBLANES, SUBLANES, D_NS)
        x0_im = state_d_im[i].reshape(DEC_BATCH // SUBLANES, SUBLANES, D_NS)
        od, s_re, s_im = _ssm(sc, DEC_BATCH, DEC_SEQ, i, *ssm_w, x0_re, x0_im, DEC_SEQ)
        x1 = _merge(ys, [o_a], ob, sc, od, i, *merge_w, tm_s)
        ys = _ffn(x1, p_sample.reshape(DEPTH, ms, PLE_DIM), i, *ffn_w, tm_s)
        conv_s.append(tail.reshape(DEC_BATCH, DEC_SEQ, C_WIDTH)[:, DEC_SEQ - (C_CONV - 1):])
        dre_s.append(s_re.reshape(DEC_BATCH, D_NGROUPS, D_STATE))
        dim_s.append(s_im.reshape(DEC_BATCH, D_NGROUPS, D_STATE))

    heads_of = lambda c: c.reshape(c.shape[:2] + (c.shape[2] // HEAD_DIM, HEAD_DIM, c.shape[3]))
    p_states = [_row_major(heads_of(c)) for c in prompt_caches] + [jnp.stack(s) for s in st_p]
    s_states = [_row_major(c) for c in new_caches] + [jnp.stack(conv_s), jnp.stack(dre_s), jnp.stack(dim_s)]
    out = [yp.reshape(BATCH, SEQ, D_MODEL), ys.reshape(DEC_BATCH, DEC_SEQ, D_MODEL)]
    for a, b in zip(p_states, s_states):
        out += [a, b]
    return tuple(out)
```

```python
import functools

import jax
import jax.numpy as jnp
from jax import lax
from jax.experimental import pallas as pl
from jax.experimental.pallas import tpu as pltpu

F32 = jnp.float32
BF16 = jnp.bfloat16

D_MODEL = 1024
BATCH = 8
SEQ = 2048
DEPTH = 2
DEC_BATCH = 128
DEC_SEQ = 8
PAST_LEN = 16384
HEAD_DIM = 64
HALF = HEAD_DIM // 2
ROPE_THETA = 10000.0
BLOCK = 128
EPS = 1e-6
NEG_INF = -1e30
A_HEADS = 4
A_GROUPS = ((128, 1), (512, 4), (2048, 16))
A_NG = len(A_GROUPS)
A_W = A_HEADS * HEAD_DIM
A_QKV = A_NG * A_W
B_Q_HEADS = 8
B_KV_HEADS = 2
B_GROUP = B_Q_HEADS // B_KV_HEADS
B_WINDOW = 128
B_QW = B_Q_HEADS * HEAD_DIM
B_KW = B_KV_HEADS * HEAD_DIM
C_WIDTH = 256
C_CONV = 3
D_GROUP = 16
D_NGROUPS = 16
D_WIDTH = 256
D_STATE = 64
D_NS = D_NGROUPS * D_STATE
N_BRANCH = 4
D_FF = 2816
PLE_DIM = 256
MIX_W = 4096
SCALE = HEAD_DIM ** -0.5
N_CACHES = 2 * A_NG + 2

LANES = 128
SUBLANES = 8
VMEM_LIMIT = 56 * 1024 * 1024
PROMPT_TILE = 512
SAMPLE_TILE = 256
SSM_CHUNK = 128

SLAB_A = 3 * A_QKV
SLAB_B = B_QW + 2 * B_KW
SLAB_C = C_WIDTH + D_WIDTH
ROPE_A = 2 * A_QKV
ROPE_B = B_QW + B_KW


def _params(n_axes):
    return pltpu.CompilerParams(dimension_semantics=("arbitrary",) * n_axes, vmem_limit_bytes=VMEM_LIMIT)


def _resident(shape, layer, col=0):
    return pl.BlockSpec((None,) + tuple(shape), lambda *_: (layer, 0, col), pipeline_mode=pl.Buffered(1))


def _rmsnorm(x, g):
    return x * lax.rsqrt(jnp.mean(x * x, axis=-1, keepdims=True) + EPS) * g


def _mm(a, b):
    return jnp.dot(a, b, preferred_element_type=F32)


def _mm_nt(a, b):
    return lax.dot_general(a, b, (((1,), (1,)), ((), ())), preferred_element_type=F32)


def _row_halves(tm):
    if tm >= 2 * SAMPLE_TILE:
        return (slice(0, tm // 2), slice(tm // 2, tm))
    return (slice(0, tm),)


def _low_lanes(rows):
    return lax.broadcasted_iota(jnp.int32, (rows, LANES), 1) < HEAD_DIM


def _inproj_body(*refs, seq_tiles, per_row_state, aliased):
    if per_row_state:
        x_ref, g_ref, w_ref, cos_ref, sin_ref, cw_ref, s1_ref, s2_ref, a_ref, b_ref, c_ref, tail_ref = refs
    else:
        x_ref, g_ref, w_ref, cos_ref, sin_ref, cw_ref = refs[:6]
        n_in = 6 + (N_CACHES if aliased else 0)
        a_ref, b_ref, c_ref, tail_ref = refs[n_in:n_in + 4]
        cache_refs = refs[n_in + 4:n_in + 4 + N_CACHES]
        carry_ref = refs[n_in + 4 + N_CACHES]
    tm = x_ref.shape[0]
    a_halves, b_halves, z_halves = [], [], []
    for rows in _row_halves(tm):
        h = _rmsnorm(x_ref[rows, :], g_ref[...]).astype(BF16)
        cos = cos_ref[rows, :]
        sin = sin_ref[rows, :]
        lane = lax.broadcasted_iota(jnp.int32, cos.shape, 1)
        first_half = (lane & (HEAD_DIM - 1)) < HALF

        def rope(z):
            partner = jnp.where(first_half, pltpu.roll(z, LANES - HALF, axis=1), pltpu.roll(z, HALF, axis=1))
            return z * cos + partner * sin

        za = _mm(h, w_ref[:, 0:SLAB_A])
        chunks = []
        for c in range(SLAB_A // LANES):
            blk = za[:, c * LANES:(c + 1) * LANES]
            chunks.append(rope(blk) if c * LANES < ROPE_A else blk)
            a_ref[rows, c * LANES:(c + 1) * LANES] = chunks[-1]
        a_halves.append(chunks)
        zb = _mm(h, w_ref[:, SLAB_A:SLAB_A + SLAB_B])
        chunks = []
        for c in range(SLAB_B // LANES):
            blk = zb[:, c * LANES:(c + 1) * LANES]
            chunks.append(rope(blk) if c * LANES < ROPE_B else blk)
            b_ref[rows, c * LANES:(c + 1) * LANES] = chunks[-1]
        b_halves.append(chunks)
        z_halves.append(_mm(h, w_ref[:, SLAB_A + SLAB_B:MIX_W]))
    whole = lambda halves, c: jnp.concatenate([chunks[c] for chunks in halves], axis=0)

    if not per_row_state:
        pairs = A_W // LANES
        chunk_of = lambda part, g, j: whole(a_halves, (part * A_QKV + g * A_W) // LANES + j)

        def write(ref, chunk, rows):
            ref[0, :, :] = jnp.concatenate([ch[tm - rows:tm, :].T for ch in chunk], axis=0)

        last_group = A_NG - 1
        assert A_GROUPS[last_group][0] == seq_tiles * tm and A_GROUPS[1][0] == tm and A_GROUPS[0][0] <= tm
        for part in (1, 2):
            write(cache_refs[2 * last_group + part - 1], [chunk_of(part, last_group, j) for j in range(pairs)], tm)

        @pl.when(pl.program_id(0) % seq_tiles == seq_tiles - 1)
        def _():
            for g in range(last_group):
                for part in (1, 2):
                    write(cache_refs[2 * g + part - 1], [chunk_of(part, g, j) for j in range(pairs)], A_GROUPS[g][0])
            write(cache_refs[6], [whole(b_halves, B_QW // LANES)], B_WINDOW)
            write(cache_refs[7], [whole(b_halves, B_QW // LANES + 1)], B_WINDOW)
    z = jnp.concatenate(z_halves, axis=0)
    zc = z[:, 2 * C_WIDTH:3 * C_WIDTH] * z[:, 0:C_WIDTH]

    row = lax.broadcasted_iota(jnp.int32, (tm, C_WIDTH), 0)
    z1 = pltpu.roll(zc, 1, axis=0)
    z2 = pltpu.roll(zc, 2, axis=0)
    if per_row_state:
        t = row & (DEC_SEQ - 1)
        z1 = jnp.where(t == 0, s1_ref[...], z1)
        z2 = jnp.where(t == 0, s2_ref[...], jnp.where(t == 1, s1_ref[...], z2))
        tail_ref[...] = zc
    else:
        @pl.when(pl.program_id(0) % seq_tiles == 0)
        def _():
            carry_ref[...] = jnp.zeros(carry_ref.shape, F32)
        z1 = jnp.where(row == 0, carry_ref[SUBLANES - 1:SUBLANES, :], z1)
        z2 = jnp.where(row == 0, carry_ref[SUBLANES - 2:SUBLANES - 1, :],
                       jnp.where(row == 1, carry_ref[SUBLANES - 1:SUBLANES, :], z2))
        carry_ref[...] = zc[tm - SUBLANES:tm, :]
        tail_ref[...] = zc[tm - SUBLANES:tm, :]
    conv = cw_ref[0:1, :] * z2 + cw_ref[1:2, :] * z1 + cw_ref[2:3, :] * zc
    c_ref[:, 0:C_WIDTH] = z[:, C_WIDTH:2 * C_WIDTH] * conv
    c_ref[:, C_WIDTH:SLAB_C] = z[:, 3 * C_WIDTH:4 * C_WIDTH]


def _inproj(x, layer, g, w_in, cos, sin, convw, tm, table_blocks, seq_tiles, row_state=None, prev_caches=None):
    m = x.shape[0]
    per_row_state = row_state is not None
    aliased = prev_caches is not None
    row = lambda i: (i, 0)
    tab = lambda i: (i % table_blocks, 0)
    in_specs = [pl.BlockSpec((tm, D_MODEL), row), _resident((1, D_MODEL), layer), _resident((D_MODEL, MIX_W), layer),
                pl.BlockSpec((tm, LANES), tab), pl.BlockSpec((tm, LANES), tab), _resident((C_CONV, C_WIDTH), layer)]
    args = [x, g, w_in, cos, sin, convw]
    out_specs = [pl.BlockSpec((tm, SLAB_A), row), pl.BlockSpec((tm, SLAB_B), row), pl.BlockSpec((tm, SLAB_C), row)]
    out_shape = [jax.ShapeDtypeStruct((m, SLAB_A), F32), jax.ShapeDtypeStruct((m, SLAB_B), F32),
                 jax.ShapeDtypeStruct((m, SLAB_C), F32)]
    scratch, aliases = [], {}
    if per_row_state:
        in_specs += [pl.BlockSpec((tm, C_WIDTH), row)] * 2
        args += list(row_state)
        out_specs.append(pl.BlockSpec((tm, C_WIDTH), row))
        out_shape.append(jax.ShapeDtypeStruct((m, C_WIDTH), F32))
    else:
        scratch = [pltpu.VMEM((SUBLANES, C_WIDTH), F32)]
        out_specs.append(pl.BlockSpec((SUBLANES, C_WIDTH), row))
        out_shape.append(jax.ShapeDtypeStruct((m // tm * SUBLANES, C_WIDTH), F32))
        n_seqs = m // (tm * seq_tiles)
        widths = [(A_W, w) for w, _ in A_GROUPS for _ in range(2)] + [(B_KW, B_WINDOW)] * 2
        for j, (rows_, window) in enumerate(widths):
            per_tile = window == tm * seq_tiles
            idx = ((lambda i: (layer, i // seq_tiles, 0, i % seq_tiles)) if per_tile
                   else (lambda i: (layer, i // seq_tiles, 0, 0)))
            out_specs.append(pl.BlockSpec((None, 1, rows_, tm if per_tile else window), idx))
            out_shape.append(jax.ShapeDtypeStruct((DEPTH, n_seqs, rows_, window), F32))
        if aliased:
            aliases = {len(args) + j: 4 + j for j in range(N_CACHES)}
            in_specs += [pl.BlockSpec(memory_space=pl.ANY)] * N_CACHES
            args += list(prev_caches)
    return pl.pallas_call(
        functools.partial(_inproj_body, seq_tiles=seq_tiles, per_row_state=per_row_state, aliased=aliased),
        grid=(m // tm,),
        in_specs=in_specs,
        out_specs=out_specs,
        out_shape=out_shape,
        input_output_aliases=aliases,
        scratch_shapes=scratch,
        compiler_params=_params(1),
        name="in_proj",
    )(*args)


def _band_mask_t(nk):
    kj = lax.broadcasted_iota(jnp.int32, (nk, BLOCK), 0)
    qi = lax.broadcasted_iota(jnp.int32, (nk, BLOCK), 1)
    dist = qi + (nk - BLOCK) - kj
    return (dist >= 0) & (dist <= BLOCK)


def _softmax_t(s, mask, sink=None):
    s = jnp.where(mask, s, NEG_INF)
    m = jnp.max(s, axis=0, keepdims=True)
    if sink is not None:
        m = jnp.maximum(m, sink)
    p = jnp.exp(s - m)
    den = jnp.sum(p, axis=0, keepdims=True)
    if sink is not None:
        den = den + jnp.exp(sink - m)
    return p, 1.0 / den, m + jnp.log(den)


def _band_softmax_t(s, sink=None):
    j = lax.broadcasted_iota(jnp.int32, (BLOCK, BLOCK), 0)
    qi = lax.broadcasted_iota(jnp.int32, (BLOCK, BLOCK), 1)
    above = j > qi
    s_prev, s_own = s[0:BLOCK], s[BLOCK:2 * BLOCK]
    folded = jnp.where(above, s_prev, s_own)
    diag = jnp.sum(jnp.where(j == qi, s_prev, 0.0), axis=0, keepdims=True)
    m = jnp.maximum(jnp.max(folded, axis=0, keepdims=True), diag)
    if sink is not None:
        m = jnp.maximum(m, sink)
    p = jnp.exp(folded - m)
    p_diag = jnp.exp(diag - m)
    den = jnp.sum(p, axis=0, keepdims=True) + p_diag
    if sink is not None:
        den = den + jnp.exp(sink - m)
    unfolded = jnp.concatenate([jnp.where(above, p, 0.0), jnp.where(above, 0.0, p)], axis=0)
    return unfolded, p_diag, 1.0 / den, m + jnp.log(den)


def _attn_a_prompt_body(q_ref, k_ref, v_ref, o_ref, lse_ref, *, d):
    nb = SEQ // d // BLOCK
    pairs = q_ref.shape[-1] // LANES
    first_head_rows = lax.broadcasted_iota(jnp.int32, (LANES, BLOCK), 0) < HEAD_DIM

    def rows(start, n):
        return pl.ds(start, n) if d == 1 else pl.ds(start, n, stride=d)

    def block(q0, k0, nk):
        mask = _band_mask_t(nk)
        lo = _low_lanes(nk)
        for hp in range(pairs):
            cols = slice(hp * LANES, (hp + 1) * LANES)
            q = (q_ref[0, rows(q0, BLOCK), cols] * SCALE).astype(BF16)
            k = k_ref[0, rows(k0, nk), cols]
            v = v_ref[0, rows(k0, nk), cols]
            k0_, k1_ = jnp.where(lo, k, 0.0).astype(BF16), jnp.where(lo, 0.0, k).astype(BF16)
            vt32 = v.T
            vt = vt32.astype(BF16)
            if nk == BLOCK:
                st = _mm_nt(jnp.concatenate([k0_, k1_], axis=0), q)
                p0, r0, l0 = _softmax_t(st[0:nk], mask)
                p1, r1, l1 = _softmax_t(st[nk:2 * nk], mask)
                ot = _mm(vt, jnp.concatenate([p0, p1], axis=1).astype(BF16))
                o_t = jnp.where(first_head_rows, ot[:, 0:BLOCK] * r0, ot[:, BLOCK:2 * BLOCK] * r1)
            else:
                p0, pd0, r0, l0 = _band_softmax_t(_mm_nt(k0_, q))
                p1, pd1, r1, l1 = _band_softmax_t(_mm_nt(k1_, q))
                o_t = jnp.where(first_head_rows, (_mm(vt, p0.astype(BF16)) + vt32[:, 0:BLOCK] * pd0) * r0,
                                (_mm(vt, p1.astype(BF16)) + vt32[:, 0:BLOCK] * pd1) * r1)
            l_t = jnp.where(first_head_rows, jnp.broadcast_to(l0, (LANES, BLOCK)),
                            jnp.broadcast_to(l1, (LANES, BLOCK)))
            o_ref[0, rows(q0, BLOCK), cols] = o_t.T
            lse_ref[0, rows(q0, BLOCK), cols] = l_t.T

    for r in range(d):
        block(r, r, BLOCK)
        if nb > 1:
            def body(i, carry):
                q0 = r + i * (BLOCK * d)
                block(q0, q0 - BLOCK * d, 2 * BLOCK)
                return carry
            lax.fori_loop(1, nb, body, 0, unroll=nb - 1)


def _attn_a_prompt(slab_a, g, d):
    view = slab_a.reshape(BATCH, SEQ, SLAB_A)
    width = A_W if d == 1 else LANES
    steps = A_W // width
    spec = lambda off: pl.BlockSpec((1, SEQ, width), lambda b, hp: (b, 0, off * steps + hp))
    oshape = jax.ShapeDtypeStruct((BATCH, SEQ, A_W), F32)
    o, lse = pl.pallas_call(
        functools.partial(_attn_a_prompt_body, d=d),
        grid=(BATCH, steps),
        in_specs=[spec(g), spec(A_NG + g), spec(2 * A_NG + g)],
        out_specs=[spec(0), spec(0)],
        out_shape=[oshape, oshape],
        compiler_params=_params(2),
        name=f"attn_a{g + 1}_prompt",
    )(view, view, view)
    return o.reshape(BATCH * SEQ, A_W), lse.reshape(BATCH * SEQ, A_W)


def _attn_b_prompt_body(q_ref, k_ref, v_ref, sink_ref, o_ref, *, nb):
    def block(q0, k0, nk):
        mask = _band_mask_t(nk)
        lo = _low_lanes(nk)
        k = k_ref[0, pl.ds(k0, nk), :]
        k_sw = pltpu.roll(k, HEAD_DIM, axis=1)
        vt32 = v_ref[0, pl.ds(k0, nk), :].T
        vt = vt32.astype(BF16)
        for kv in range(B_KV_HEADS):
            on_lo, on_hi = (k, k_sw) if kv == 0 else (k_sw, k)
            k2 = jnp.concatenate([jnp.where(lo, on_lo, 0.0), jnp.where(lo, 0.0, on_hi)], axis=0).astype(BF16)
            cols = slice(2 * kv * LANES, (2 * kv + 2) * LANES)
            q2 = q_ref[0, pl.ds(q0, BLOCK), cols] * SCALE
            q2 = jnp.concatenate([q2[:, 0:LANES], q2[:, LANES:2 * LANES]], axis=0).astype(BF16)
            kv_rows = slice(kv * HEAD_DIM, (kv + 1) * HEAD_DIM)
            ps, rs, diag_terms = [], [], []
            for g in range(B_GROUP):
                half, chunk = g % 2, g // 2
                sink = sink_ref[kv * B_GROUP + g:kv * B_GROUP + g + 1, :]
                s = _mm_nt(k2[half * nk:(half + 1) * nk], q2[chunk * BLOCK:(chunk + 1) * BLOCK])
                if nk == BLOCK:
                    p, r, _ = _softmax_t(s, mask, sink)
                else:
                    p, p_diag, r, _ = _band_softmax_t(s, sink)
                    diag_terms.append(vt32[kv_rows, 0:BLOCK] * p_diag)
                ps.append(p)
                rs.append(r)
            ot = jnp.concatenate([_mm(vt, p.astype(BF16))[kv_rows, :] for p in ps], axis=1)
            if diag_terms:
                ot = ot + jnp.concatenate(diag_terms, axis=1)
            oj = ot * jnp.concatenate(rs, axis=1)
            for chunk in range(2):
                o_t = jnp.concatenate([oj[:, (2 * chunk) * BLOCK:(2 * chunk + 1) * BLOCK],
                                       oj[:, (2 * chunk + 1) * BLOCK:(2 * chunk + 2) * BLOCK]], axis=0)
                c = 2 * kv + chunk
                o_ref[0, pl.ds(q0, BLOCK), c * LANES:(c + 1) * LANES] = o_t.T

    block(0, 0, BLOCK)

    def body(i, carry):
        q0 = pl.multiple_of(i * BLOCK, BLOCK)
        block(q0, pl.multiple_of(q0 - BLOCK, BLOCK), 2 * BLOCK)
        return carry
    lax.fori_loop(1, nb, body, 0, unroll=15)


def _attn_b_prompt(slab_b, layer, sinks):
    view = slab_b.reshape(BATCH, SEQ, SLAB_B)
    nq = B_QW // LANES
    o = pl.pallas_call(
        functools.partial(_attn_b_prompt_body, nb=SEQ // BLOCK),
        grid=(BATCH,),
        in_specs=[pl.BlockSpec((1, SEQ, B_QW), lambda b: (b, 0, 0)),
                  pl.BlockSpec((1, SEQ, B_KW), lambda b: (b, 0, nq)),
                  pl.BlockSpec((1, SEQ, B_KW), lambda b: (b, 0, nq + 1)),
                  _resident((B_Q_HEADS, LANES), layer)],
        out_specs=pl.BlockSpec((1, SEQ, B_QW), lambda b: (b, 0, 0)),
        out_shape=jax.ShapeDtypeStruct((BATCH, SEQ, B_QW), F32),
        compiler_params=_params(1),
        name="attn_b_prompt",
    )(view, view, view, sinks)
    return o.reshape(BATCH * SEQ, B_QW)


def _sample_masks(rows, lb, window, dil):
    assert dil & (dil - 1) == 0 and DEC_SEQ & (DEC_SEQ - 1) == 0
    t_c = lax.broadcasted_iota(jnp.int32, (rows, lb), 0) & (DEC_SEQ - 1)
    dist_c = lb + t_c - lax.broadcasted_iota(jnp.int32, (rows, lb), 1)
    mask_c = (dist_c >= 0) & (dist_c <= window) & ((dist_c & (dil - 1)) == 0)
    t_n = lax.broadcasted_iota(jnp.int32, (rows, DEC_SEQ), 0) & (DEC_SEQ - 1)
    dist_n = t_n - lax.broadcasted_iota(jnp.int32, (rows, DEC_SEQ), 1)
    mask_n = (dist_n >= 0) & (dist_n <= window) & ((dist_n & (dil - 1)) == 0)
    return mask_c, mask_n


def _sample_attend(q, kt, vt, kn, vn, masks, sink=None):
    mask_c, mask_n = masks
    s_c = jnp.where(mask_c, _mm(q.astype(BF16), kt), NEG_INF)
    s_n = jnp.where(mask_n, _mm_nt(q, kn), NEG_INF)
    m = jnp.maximum(jnp.max(s_c, axis=-1, keepdims=True), jnp.max(s_n, axis=-1, keepdims=True))
    if sink is not None:
        m = jnp.maximum(m, sink)
    p_c = jnp.exp(s_c - m)
    p_n = jnp.exp(s_n - m)
    den = jnp.sum(p_c, axis=-1, keepdims=True) + jnp.sum(p_n, axis=-1, keepdims=True)
    if sink is not None:
        den = den + jnp.exp(sink - m)
    o = _mm_nt(p_c.astype(BF16), vt) + _mm(p_n, vn)
    return o * (1.0 / den), m + jnp.log(den)


def _write_shifted_cache(dst_ref, src_ref, s, heads, new_rows, lb):
    padded = jnp.concatenate([jnp.zeros((LANES - DEC_SEQ, LANES), F32), new_rows], axis=0)
    new_t = padded.T
    keep = lax.broadcasted_iota(jnp.int32, (LANES, LANES), 1) < LANES - DEC_SEQ
    rotated = lambda j: pltpu.roll(src_ref[0, s, heads, :, j * LANES:(j + 1) * LANES].reshape(LANES, LANES),
                                   LANES - DEC_SEQ, axis=1)
    n_tiles = lb // LANES
    cur = rotated(0)
    for j in range(n_tiles):
        nxt = rotated(j + 1) if j + 1 < n_tiles else new_t
        dst_ref[0, s, heads, :, j * LANES:(j + 1) * LANES] = jnp.where(keep, cur, nxt).reshape(2, HEAD_DIM, LANES)
        cur = nxt


SAMPLE_SEQS_PER_STEP = 2


def _sample_attn_body(*refs, aliased, n_seq):
    for s in range(n_seq):
        _sample_attn_one(refs, aliased, s)


def _sample_attn_one(refs, aliased, s):
    sa_ref, sb_ref, sink_ref = refs[:3]
    cache_refs = refs[3:11]
    pos = 11 + (8 if aliased else 0)
    oa_ref, ob_ref = refs[pos:pos + 2]
    out_refs = refs[pos + 2:pos + 10]
    lo8 = _low_lanes(DEC_SEQ)
    seq_rows = slice(s * DEC_SEQ, (s + 1) * DEC_SEQ)

    o_g, l_g = [], []
    for g, (window, dil) in enumerate(A_GROUPS):
        kc_ref, vc_ref, ko_ref, vo_ref = cache_refs[2 * g], cache_refs[2 * g + 1], out_refs[2 * g], out_refs[2 * g + 1]
        lb = kc_ref.shape[-1]
        masks = _sample_masks(2 * DEC_SEQ, lb, window, dil)
        o_pairs, l_pairs = [], []
        for c in range(A_W // LANES):
            col = lambda part: slice(part * A_QKV + g * A_W + c * LANES, part * A_QKV + g * A_W + (c + 1) * LANES)
            heads = slice(2 * c, 2 * c + 2)
            q = sa_ref[seq_rows, col(0)] * SCALE
            kn, vn = sa_ref[seq_rows, col(1)], sa_ref[seq_rows, col(2)]
            kt = kc_ref[0, s, heads].reshape(LANES, lb)
            vt = vc_ref[0, s, heads].reshape(LANES, lb)
            q2 = jnp.concatenate([jnp.where(lo8, q, 0.0), jnp.where(lo8, 0.0, q)], axis=0)
            o, lse = _sample_attend(q2, kt.astype(BF16), vt.astype(BF16), kn, vn, masks)
            o_pairs.append(jnp.where(lo8, o[0:DEC_SEQ], o[DEC_SEQ:]))
            l_pairs.append(jnp.where(lo8, lse[0:DEC_SEQ], lse[DEC_SEQ:]))
            _write_shifted_cache(ko_ref, kc_ref, s, heads, kn, lb)
            _write_shifted_cache(vo_ref, vc_ref, s, heads, vn, lb)
        o_g.append(o_pairs)
        l_g.append(l_pairs)
    for c in range(A_W // LANES):
        l1, l2, l3 = l_g[0][c], l_g[1][c], l_g[2][c]
        lm = jnp.maximum(jnp.maximum(l1, l2), l3)
        e1, e2, e3 = jnp.exp(l1 - lm), jnp.exp(l2 - lm), jnp.exp(l3 - lm)
        es = e1 + e2 + e3
        oa_ref[seq_rows, c * LANES:(c + 1) * LANES] = ((e1 / es) * o_g[0][c] + (e2 / es) * o_g[1][c]
                                                       + (e3 / es) * o_g[2][c])

    kc_ref, vc_ref, ko_ref, vo_ref = cache_refs[6], cache_refs[7], out_refs[6], out_refs[7]
    lb = kc_ref.shape[-1]
    rows = B_GROUP * DEC_SEQ
    masks = _sample_masks(rows, lb, B_WINDOW, 1)
    both = slice(0, B_KV_HEADS)
    ktb = kc_ref[0, s].reshape(LANES, lb).astype(BF16)
    vtb = vc_ref[0, s].reshape(LANES, lb).astype(BF16)
    kn, vn = sb_ref[seq_rows, B_QW:B_QW + B_KW], sb_ref[seq_rows, B_QW + B_KW:SLAB_B]
    halves = {}
    for kv in range(B_KV_HEADS):
        parts = []
        for gq in range(B_GROUP):
            head = kv * B_GROUP + gq
            q = sb_ref[seq_rows, (head // 2) * LANES:(head // 2 + 1) * LANES] * SCALE
            q = jnp.where(lo8, q, 0.0) if head % 2 == 0 else jnp.where(lo8, 0.0, q)
            parts.append(q if head % 2 == kv else pltpu.roll(q, HEAD_DIM, axis=1))
        q4 = jnp.concatenate(parts, axis=0)
        sink = sink_ref[kv * rows:(kv + 1) * rows, 0:1]
        o, _ = _sample_attend(q4, ktb, vtb, kn, vn, masks, sink)
        for gq in range(B_GROUP):
            head = kv * B_GROUP + gq
            oh = o[gq * DEC_SEQ:(gq + 1) * DEC_SEQ]
            halves[head] = oh if head % 2 == kv else pltpu.roll(oh, HEAD_DIM, axis=1)
    for c in range(B_QW // LANES):
        ob_ref[seq_rows, c * LANES:(c + 1) * LANES] = jnp.where(lo8, halves[2 * c], halves[2 * c + 1])
    _write_shifted_cache(ko_ref, kc_ref, s, both, kn, lb)
    _write_shifted_cache(vo_ref, vc_ref, s, both, vn, lb)


def _sample_attn(layer, slab_a, slab_b, sinks, caches, prev):
    aliased = prev is not None
    n_seq = SAMPLE_SEQS_PER_STEP
    rows = n_seq * DEC_SEQ
    row = lambda b: (b, 0)
    cspec = lambda c: pl.BlockSpec((1, n_seq) + c.shape[2:], lambda b: (layer, b, 0, 0, 0))
    in_specs = [pl.BlockSpec((rows, SLAB_A), row), pl.BlockSpec((rows, SLAB_B), row),
                _resident(sinks.shape[1:], layer)]
    in_specs += [cspec(c) for c in caches]
    args = [slab_a, slab_b, sinks] + list(caches)
    aliases = {}
    if aliased:
        aliases = {len(args) + j: 2 + j for j in range(8)}
        in_specs += [pl.BlockSpec(memory_space=pl.ANY)] * 8
        args += list(prev)
    m = slab_a.shape[0]
    res = pl.pallas_call(
        functools.partial(_sample_attn_body, aliased=aliased, n_seq=n_seq),
        grid=(DEC_BATCH // n_seq,),
        in_specs=in_specs,
        out_specs=[pl.BlockSpec((rows, A_W), row), pl.BlockSpec((rows, B_QW), row)] + [cspec(c) for c in caches],
        out_shape=[jax.ShapeDtypeStruct((m, A_W), F32), jax.ShapeDtypeStruct((m, B_QW), F32)]
                  + [jax.ShapeDtypeStruct(c.shape, F32) for c in caches],
        input_output_aliases=aliases,
        compiler_params=_params(1),
        name="sample_attn",
    )(*args)
    return res[0], res[1], list(res[2:])


def _ssm_body(u_ref, are_ref, aim_ref, b_ref, cbig_ref, dvec_ref, wglu_ref, x0re_ref, x0im_ref,
              od_ref, sre_ref, sim_ref, bu_ref, xs_ref, *, steps, pitch):
    @pl.when(pl.program_id(1) == 0)
    def _():
        sre_ref[0] = x0re_ref[0]
        sim_ref[0] = x0im_ref[0]

    n_re = D_NS // LANES
    tiles = lambda x: [x[:, j * LANES:(j + 1) * LANES] for j in range(x.shape[1] // LANES)]

    plane = lambda s: slice(s * pitch, s * pitch + steps)
    u = u_ref[...].reshape(SUBLANES * steps, D_WIDTH)
    bu = _mm(u.astype(BF16), b_ref[...])
    for s in range(SUBLANES):
        for j, tile in enumerate(tiles(bu[s * steps:(s + 1) * steps])):
            bu_ref[j, plane(s), :] = tile
    a_re = are_ref[...]
    a_im = aim_ref[...]

    def step(t, carry):
        re, im = carry
        at_t = pl.ds(t, SUBLANES, stride=pitch)
        b_re = jnp.concatenate([bu_ref[j, at_t, :] for j in range(n_re)], axis=1)
        b_im = jnp.concatenate([bu_ref[n_re + j, at_t, :] for j in range(n_re)], axis=1)
        re, im = a_re * re - a_im * im + b_re, a_re * im + a_im * re + b_im
        for j, (tr, ti) in enumerate(zip(tiles(re), tiles(im))):
            xs_ref[j, at_t, :] = tr
            xs_ref[n_re + j, at_t, :] = ti
        return re, im

    re, im = lax.fori_loop(0, steps, step, (sre_ref[0], sim_ref[0]), unroll=min(steps, SUBLANES))
    sre_ref[0] = re
    sim_ref[0] = im
    group = SUBLANES // 2 if SUBLANES * steps >= 2 * SAMPLE_TILE else SUBLANES
    for s0 in range(0, SUBLANES, group):
        seqs = range(s0, s0 + group)
        xs = jnp.concatenate([jnp.concatenate([xs_ref[j, plane(s), :] for j in range(2 * n_re)], axis=1)
                              for s in seqs], axis=0)
        y = _mm(xs.astype(BF16), cbig_ref[...]) + dvec_ref[...] * u[s0 * steps:(s0 + group) * steps]
        y = _mm(jax.nn.gelu(y).astype(BF16), wglu_ref[...])
        od = y[:, 0:D_WIDTH] * jax.nn.sigmoid(y[:, D_WIDTH:2 * D_WIDTH])
        od_ref[s0:s0 + group] = od.reshape(group, steps, D_WIDTH)


def _ssm(slab_c, bsz, length, layer, a_re, a_im, bbig, cbig, dvec, wglu, x0_re, x0_im, steps):
    n = bsz // SUBLANES
    pitch = steps if steps <= SUBLANES else steps + SUBLANES
    view = slab_c.reshape(bsz, length, SLAB_C)
    u_spec = pl.BlockSpec((SUBLANES, steps, D_WIDTH), lambda i, c: (i, c, SLAB_C // D_WIDTH - 1))
    o_spec = pl.BlockSpec((SUBLANES, steps, D_WIDTH), lambda i, c: (i, c, 0))
    st_spec = pl.BlockSpec((1, SUBLANES, D_NS), lambda i, c: (i, 0, 0))
    st_shape = jax.ShapeDtypeStruct((n, SUBLANES, D_NS), F32)
    od, s_re, s_im = pl.pallas_call(
        functools.partial(_ssm_body, steps=steps, pitch=pitch),
        grid=(n, length // steps),
        in_specs=[u_spec, _resident((SUBLANES, D_NS), layer), _resident((SUBLANES, D_NS), layer),
                  _resident((D_WIDTH, 2 * D_NS), layer), _resident((2 * D_NS, D_WIDTH), layer),
                  _resident((1, D_WIDTH), layer), _resident((D_WIDTH, 2 * D_WIDTH), layer), st_spec, st_spec],
        out_specs=[o_spec, st_spec, st_spec],
        out_shape=[jax.ShapeDtypeStruct((bsz, length, D_WIDTH), F32), st_shape, st_shape],
        scratch_shapes=[pltpu.VMEM((2 * D_NS // LANES, SUBLANES * pitch, LANES), F32)] * 2,
        compiler_params=_params(2),
        name="ssm",
    )(view, a_re, a_im, bbig, cbig, dvec, wglu, x0_re, x0_im)
    return od.reshape(bsz * length, D_WIDTH), s_re, s_im


def _merge_body(*refs, n_groups):
    x_ref = refs[0]
    a_refs = refs[1:1 + (2 * n_groups if n_groups > 1 else 1)]
    ob_ref, oc_ref, od_ref, gpre_ref, wgl_ref, wa_ref, wb_ref, wc_ref, wd_ref, wout_ref, gpost_ref, y_ref = \
        refs[1 + len(a_refs):]
    tm = x_ref.shape[0]
    for rows in _row_halves(tm):
        x = x_ref[rows, :]
        h = _rmsnorm(x, gpre_ref[...]).astype(BF16)
        if n_groups > 1:
            lses = [a_refs[2 * g + 1][rows, :] for g in range(n_groups)]
            lm = functools.reduce(jnp.maximum, lses)
            es = [jnp.exp(l - lm) for l in lses]
            den = functools.reduce(lambda a, b: a + b, es)
            o_a = functools.reduce(lambda a, b: a + b,
                                   [(es[g] / den) * a_refs[2 * g][rows, :] for g in range(n_groups)])
        else:
            o_a = a_refs[0][rows, :]
        branches = ((o_a, wa_ref), (ob_ref[rows, :], wb_ref), (oc_ref[rows, :], wc_ref), (od_ref[rows, :], wd_ref))
        merged = None
        for j, (o, w_ref) in enumerate(branches):
            gate = jax.nn.sigmoid(_mm(h, wgl_ref[:, j * D_MODEL:(j + 1) * D_MODEL]))
            term = gate * _mm(o.astype(BF16), w_ref[...])
            merged = term if merged is None else merged + term
        mix = _mm(merged.astype(BF16), wout_ref[...])
        y_ref[rows, :] = x + _rmsnorm(mix, gpost_ref[...])


def _merge(x, a_parts, ob, slab_c, od, layer, gpre, w_in, wa, wb, wc, wd, wout, gpost, tm):
    m = x.shape[0]
    row = lambda i: (i, 0)
    r256 = pl.BlockSpec((tm, 256), row)
    n_groups = max(1, len(a_parts) // 2)
    return pl.pallas_call(
        functools.partial(_merge_body, n_groups=n_groups),
        grid=(m // tm,),
        in_specs=[pl.BlockSpec((tm, D_MODEL), row)] + [r256] * len(a_parts)
                 + [pl.BlockSpec((tm, B_QW), row), r256, r256,
                    _resident((1, D_MODEL), layer), _resident((D_MODEL, N_BRANCH * D_MODEL), layer, col=1),
                    _resident((A_W, D_MODEL), layer), _resident((B_QW, D_MODEL), layer),
                    _resident((C_WIDTH, D_MODEL), layer), _resident((D_WIDTH, D_MODEL), layer),
                    _resident((D_MODEL, D_MODEL), layer), _resident((1, D_MODEL), layer)],
        out_specs=pl.BlockSpec((tm, D_MODEL), row),
        out_shape=jax.ShapeDtypeStruct((m, D_MODEL), F32),
        compiler_params=_params(1),
        name="merge",
    )(x, *a_parts, ob, slab_c, od, gpre, w_in, wa, wb, wc, wd, wout, gpost)


def _ffn_body(x_ref, p_ref, gpre_ref, wg_ref, wu_ref, wdn_ref, gpost_ref, wple_ref, wpg_ref, y_ref):
    x = x_ref[...]
    h = _rmsnorm(x, gpre_ref[...]).astype(BF16)
    act = jax.nn.silu(_mm(h, wg_ref[...])) * _mm(h, wu_ref[...])
    f = _mm(act.astype(BF16), wdn_ref[...])
    x = x + _rmsnorm(f, gpost_ref[...])
    gate = jax.nn.sigmoid(_mm(x.astype(BF16), wpg_ref[...]))
    y_ref[...] = x + gate * _mm(p_ref[...].astype(BF16), wple_ref[...])


def _ffn(x, p_all, layer, gpre, wg, wu, wdn, gpost, wple, wpg, tm):
    m = x.shape[0]
    row = lambda i: (i, 0)
    return pl.pallas_call(
        _ffn_body,
        grid=(m // tm,),
        in_specs=[pl.BlockSpec((tm, D_MODEL), row), pl.BlockSpec((None, tm, PLE_DIM), lambda i: (layer, i, 0)),
                  _resident((1, D_MODEL), layer), _resident((D_MODEL, D_FF), layer), _resident((D_MODEL, D_FF), layer),
                  _resident((D_FF, D_MODEL), layer), _resident((1, D_MODEL), layer),
                  _resident((PLE_DIM, D_MODEL), layer), _resident((D_MODEL, D_MODEL), layer)],
        out_specs=pl.BlockSpec((tm, D_MODEL), row),
        out_shape=jax.ShapeDtypeStruct((m, D_MODEL), F32),
        compiler_params=_params(1),
        name="ffn_ple",
    )(x, p_all, gpre, wg, wu, wdn, gpost, wple, wpg)


def _rope_tables(pos):
    inv = ROPE_THETA ** (-jnp.arange(HALF, dtype=F32) / HALF)
    ang = pos.astype(F32)[:, None] * inv[None, :]
    cos, sin = jnp.cos(ang), jnp.sin(ang)
    cos_t = jnp.tile(cos, (1, LANES // HALF))
    sin_t = jnp.tile(jnp.concatenate([-sin, sin], axis=1), (1, LANES // HEAD_DIM))
    return cos_t, sin_t


def _ssm_weights(lam_re, lam_im, log_dt, b_re, b_im, c_re, c_im):
    lam = lax.complex(lam_re, lam_im)
    dt = jnp.exp(log_dt)[..., None]
    a_bar = jnp.exp(lam * dt)
    b_bar = ((a_bar - 1.0) / lam)[..., None] * lax.complex(b_re, b_im)
    eye = jnp.eye(D_NGROUPS, dtype=F32)
    pack_b = lambda b: jnp.einsum('lgni,gh->lgihn', b, eye).reshape(DEPTH, D_WIDTH, D_NS)
    pack_c = lambda c: jnp.einsum('lgin,gh->lgnhi', c, eye).reshape(DEPTH, D_NS, D_WIDTH)
    bbig = jnp.concatenate([pack_b(jnp.real(b_bar)), pack_b(jnp.imag(b_bar))], axis=2)
    cbig = jnp.concatenate([pack_c(c_re), -pack_c(c_im)], axis=1)
    tile8 = lambda a: jnp.broadcast_to(a.reshape(DEPTH, 1, D_NS), (DEPTH, SUBLANES, D_NS))
    return tile8(jnp.real(a_bar)), tile8(jnp.imag(a_bar)), bbig.astype(BF16), cbig.astype(BF16)


def _key_major(c):
    return jnp.transpose(c, (0, 1, 3, 4, 2))


def _row_major(c):
    return jnp.transpose(c, (0, 1, 4, 2, 3))


def kernel(x_prompt, x_sample, p_prompt, p_sample, cache_a1_k, cache_a1_v, cache_a2_k, cache_a2_v, cache_a3_k, cache_a3_v, cache_b_k, cache_b_v, state_c_conv, state_d_re, state_d_im, norm_mix_pre, norm_mix_post, norm_ffn_pre, norm_ffn_post, w_in, attn_sinks, conv_c_w, ssm_lam_re, ssm_lam_im, ssm_log_dt, ssm_b_re, ssm_b_im, ssm_c_re, ssm_c_im, ssm_d, w_d_glu, w_br_a, w_br_b, w_br_c, w_br_d, w_out, w_ffn_gate, w_ffn_up, w_ffn_down, w_ple, w_ple_gate):
    assert x_prompt.shape == (BATCH, SEQ, D_MODEL) and x_sample.shape == (DEC_BATCH, DEC_SEQ, D_MODEL)
    assert w_in.shape == (DEPTH, D_MODEL, 2 * MIX_W)
    assert all(min(w, PAST_LEN) == w for w, _ in A_GROUPS) and min(B_WINDOW, PAST_LEN) == B_WINDOW
    mp, ms = BATCH * SEQ, DEC_BATCH * DEC_SEQ
    tm_p, tm_s = PROMPT_TILE, SAMPLE_TILE

    cos_p, sin_p = _rope_tables(jnp.arange(SEQ, dtype=jnp.int32))
    cos_s, sin_s = _rope_tables(PAST_LEN + jnp.arange(DEC_SEQ, dtype=jnp.int32))
    cos_s, sin_s = jnp.tile(cos_s, (tm_s // DEC_SEQ, 1)), jnp.tile(sin_s, (tm_s // DEC_SEQ, 1))

    caches = [_key_major(c) for c in (cache_a1_k, cache_a1_v, cache_a2_k, cache_a2_v, cache_a3_k, cache_a3_v,
                                      cache_b_k, cache_b_v)]
    yp = x_prompt.reshape(mp, D_MODEL)
    ys = x_sample.reshape(ms, D_MODEL)
    new_caches = None
    prompt_caches = None
    st_p = [[] for _ in range(3)]
    conv_s, dre_s, dim_s = [], [], []

    rows3 = lambda a: a.reshape(DEPTH, 1, -1)
    w_in_b = w_in.astype(BF16)
    g_mix_pre = rows3(norm_mix_pre)
    ssm_w = _ssm_weights(ssm_lam_re, ssm_lam_im, ssm_log_dt, ssm_b_re, ssm_b_im, ssm_c_re, ssm_c_im) + (
        rows3(ssm_d), w_d_glu.astype(BF16))
    sinks_p = jnp.broadcast_to(attn_sinks.reshape(DEPTH, B_Q_HEADS, 1), (DEPTH, B_Q_HEADS, LANES))
    sinks_s = jnp.broadcast_to(jnp.repeat(attn_sinks.reshape(DEPTH, B_Q_HEADS), DEC_SEQ, axis=1)[..., None],
                               (DEPTH, B_Q_HEADS * DEC_SEQ, LANES))
    merge_w = (g_mix_pre, w_in_b, w_br_a.astype(BF16), w_br_b.astype(BF16), w_br_c.astype(BF16),
               w_br_d.astype(BF16), w_out.astype(BF16), rows3(norm_mix_post))
    ffn_w = (rows3(norm_ffn_pre), w_ffn_gate.astype(BF16), w_ffn_up.astype(BF16), w_ffn_down.astype(BF16),
             rows3(norm_ffn_post), w_ple.astype(BF16), w_ple_gate.astype(BF16))
    zero_state = jnp.zeros((BATCH // SUBLANES, SUBLANES, D_NS), F32)

    for i in range(DEPTH):
        sa, sb, sc, tail, *prompt_caches = _inproj(yp, i, g_mix_pre, w_in_b, cos_p, sin_p, conv_c_w, tm_p,
                                                   SEQ // tm_p, SEQ // tm_p, prev_caches=prompt_caches)
        a_parts = []
        for g, (w, d) in enumerate(A_GROUPS):
            a_parts += list(_attn_a_prompt(sa, g, d))
        ob = _attn_b_prompt(sb, i, sinks_p)
        od, s_re, s_im = _ssm(sc, BATCH, SEQ, i, *ssm_w, zero_state, zero_state, SSM_CHUNK)
        x1 = _merge(yp, a_parts, ob, sc, od, i, *merge_w, tm_p)
        yp = _ffn(x1, p_prompt.reshape(DEPTH, mp, PLE_DIM), i, *ffn_w, tm_p)
        st_p[0].append(tail.reshape(BATCH, SEQ // tm_p, SUBLANES, C_WIDTH)[:, -1, SUBLANES - (C_CONV - 1):])
        st_p[1].append(s_re.reshape(BATCH, D_NGROUPS, D_STATE))
        st_p[2].append(s_im.reshape(BATCH, D_NGROUPS, D_STATE))

        s1 = jnp.broadcast_to(state_c_conv[i][:, None, 1, :], (DEC_BATCH, DEC_SEQ, C_WIDTH)).reshape(ms, C_WIDTH)
        s2 = jnp.broadcast_to(state_c_conv[i][:, None, 0, :], (DEC_BATCH, DEC_SEQ, C_WIDTH)).reshape(ms, C_WIDTH)
        sa, sb, sc, tail = _inproj(ys, i, g_mix_pre, w_in_b, cos_s, sin_s, conv_c_w, tm_s, 1, 1, row_state=(s1, s2))
        o_a, ob, new_caches = _sample_attn(i, sa, sb, sinks_s, caches, new_caches)
        x0_re = state_d_re[i].reshape(DEC_BATCH // SUBLANES, SUBLANES, D_NS)
        x0_im = state_d_im[i].reshape(DEC_BATCH // SUBLANES, SUBLANES, D_NS)
        od, s_re, s_im = _ssm(sc, DEC_BATCH, DEC_SEQ, i, *ssm_w, x0_re, x0_im, DEC_SEQ)
        x1 = _merge(ys, [o_a], ob, sc, od, i, *merge_w, tm_s)
        ys = _ffn(x1, p_sample.reshape(DEPTH, ms, PLE_DIM), i, *ffn_w, tm_s)
        conv_s.append(tail.reshape(DEC_BATCH, DEC_SEQ, C_WIDTH)[:, DEC_SEQ - (C_CONV - 1):])
        dre_s.append(s_re.reshape(DEC_BATCH, D_NGROUPS, D_STATE))
        dim_s.append(s_im.reshape(DEC_BATCH, D_NGROUPS, D_STATE))

    heads_of = lambda c: c.reshape(c.shape[:2] + (c.shape[2] // HEAD_DIM, HEAD_DIM, c.shape[3]))
    p_states = [_row_major(heads_of(c)) for c in prompt_caches] + [jnp.stack(s) for s in st_p]
    s_states = [_row_major(c) for c in new_caches] + [jnp.stack(conv_s), jnp.stack(dre_s), jnp.stack(dim_s)]
    out = [yp.reshape(BATCH, SEQ, D_MODEL), ys.reshape(DEC_BATCH, DEC_SEQ, D_MODEL)]
    for a, b in zip(p_states, s_states):
        out += [a, b]
    return tuple(out)
```

```python
import functools

import jax
import jax.numpy as jnp
from jax import lax
from jax.experimental import pallas as pl
from jax.experimental.pallas import tpu as pltpu

F32 = jnp.float32
BF16 = jnp.bfloat16

D_MODEL = 1024
BATCH = 8
SEQ = 2048
DEPTH = 2
DEC_BATCH = 128
DEC_SEQ = 8
PAST_LEN = 16384
HEAD_DIM = 64
HALF = HEAD_DIM // 2
ROPE_THETA = 10000.0
BLOCK = 128
EPS = 1e-6
NEG_INF = -1e30
A_HEADS = 4
A_GROUPS = ((128, 1), (512, 4), (2048, 16))
A_NG = len(A_GROUPS)
A_W = A_HEADS * HEAD_DIM
A_QKV = A_NG * A_W
B_Q_HEADS = 8
B_KV_HEADS = 2
B_GROUP = B_Q_HEADS // B_KV_HEADS
B_WINDOW = 128
B_QW = B_Q_HEADS * HEAD_DIM
B_KW = B_KV_HEADS * HEAD_DIM
C_WIDTH = 256
C_CONV = 3
D_GROUP = 16
D_NGROUPS = 16
D_WIDTH = 256
D_STATE = 64
D_NS = D_NGROUPS * D_STATE
N_BRANCH = 4
D_FF = 2816
PLE_DIM = 256
MIX_W = 4096
SCALE = HEAD_DIM ** -0.5
N_CACHES = 2 * A_NG + 2

LANES = 128
SUBLANES = 8
VMEM_LIMIT = 56 * 1024 * 1024
PROMPT_TILE = 512
SAMPLE_TILE = 256
SSM_CHUNK = 256

SLAB_A = 3 * A_QKV
SLAB_B = B_QW + 2 * B_KW
SLAB_C = C_WIDTH + D_WIDTH
ROPE_A = 2 * A_QKV
ROPE_B = B_QW + B_KW


def _params(n_axes):
    return pltpu.CompilerParams(dimension_semantics=("arbitrary",) * n_axes, vmem_limit_bytes=VMEM_LIMIT)


def _resident(shape, layer, col=0):
    return pl.BlockSpec((None,) + tuple(shape), lambda *_: (layer, 0, col), pipeline_mode=pl.Buffered(1))


def _rmsnorm(x, g):
    return x * lax.rsqrt(jnp.mean(x * x, axis=-1, keepdims=True) + EPS) * g


def _mm(a, b):
    return jnp.dot(a, b, preferred_element_type=F32)


def _mm_nt(a, b):
    return lax.dot_general(a, b, (((1,), (1,)), ((), ())), preferred_element_type=F32)


def _row_halves(tm):
    if tm >= 2 * SAMPLE_TILE:
        return (slice(0, tm // 2), slice(tm // 2, tm))
    return (slice(0, tm),)


def _low_lanes(rows):
    return lax.broadcasted_iota(jnp.int32, (rows, LANES), 1) < HEAD_DIM


def _inproj_body(*refs, seq_tiles, per_row_state, aliased):
    if per_row_state:
        x_ref, g_ref, w_ref, cos_ref, sin_ref, cw_ref, s1_ref, s2_ref, a_ref, b_ref, c_ref, tail_ref = refs
    else:
        x_ref, g_ref, w_ref, cos_ref, sin_ref, cw_ref = refs[:6]
        n_in = 6 + (N_CACHES if aliased else 0)
        a_ref, b_ref, c_ref, tail_ref = refs[n_in:n_in + 4]
        cache_refs = refs[n_in + 4:n_in + 4 + N_CACHES]
        carry_ref = refs[n_in + 4 + N_CACHES]
    tm = x_ref.shape[0]
    a_halves, b_halves, z_halves = [], [], []
    for rows in _row_halves(tm):
        h = _rmsnorm(x_ref[rows, :], g_ref[...]).astype(BF16)
        cos = cos_ref[rows, :]
        sin = sin_ref[rows, :]
        lane = lax.broadcasted_iota(jnp.int32, cos.shape, 1)
        first_half = (lane & (HEAD_DIM - 1)) < HALF

        def rope(z):
            partner = jnp.where(first_half, pltpu.roll(z, LANES - HALF, axis=1), pltpu.roll(z, HALF, axis=1))
            return z * cos + partner * sin

        za = _mm(h, w_ref[:, 0:SLAB_A])
        chunks = []
        for c in range(SLAB_A // LANES):
            blk = za[:, c * LANES:(c + 1) * LANES]
            chunks.append(rope(blk) if c * LANES < ROPE_A else blk)
            a_ref[rows, c * LANES:(c + 1) * LANES] = chunks[-1]
        a_halves.append(chunks)
        zb = _mm(h, w_ref[:, SLAB_A:SLAB_A + SLAB_B])
        chunks = []
        for c in range(SLAB_B // LANES):
            blk = zb[:, c * LANES:(c + 1) * LANES]
            chunks.append(rope(blk) if c * LANES < ROPE_B else blk)
            b_ref[rows, c * LANES:(c + 1) * LANES] = chunks[-1]
        b_halves.append(chunks)
        z_halves.append(_mm(h, w_ref[:, SLAB_A + SLAB_B:MIX_W]))
    whole = lambda halves, c: jnp.concatenate([chunks[c] for chunks in halves], axis=0)

    if not per_row_state:
        pairs = A_W // LANES
        chunk_of = lambda part, g, j: whole(a_halves, (part * A_QKV + g * A_W) // LANES + j)

        def write(ref, chunk, rows):
            ref[0, :, :] = jnp.concatenate([ch[tm - rows:tm, :].T for ch in chunk], axis=0)

        last_group = A_NG - 1
        assert A_GROUPS[last_group][0] == seq_tiles * tm and A_GROUPS[1][0] == tm and A_GROUPS[0][0] <= tm
        for part in (1, 2):
            write(cache_refs[2 * last_group + part - 1], [chunk_of(part, last_group, j) for j in range(pairs)], tm)

        @pl.when(pl.program_id(0) % seq_tiles == seq_tiles - 1)
        def _():
            for g in range(last_group):
                for part in (1, 2):
                    write(cache_refs[2 * g + part - 1], [chunk_of(part, g, j) for j in range(pairs)], A_GROUPS[g][0])
            write(cache_refs[6], [whole(b_halves, B_QW // LANES)], B_WINDOW)
            write(cache_refs[7], [whole(b_halves, B_QW // LANES + 1)], B_WINDOW)
    z = jnp.concatenate(z_halves, axis=0)
    zc = z[:, 2 * C_WIDTH:3 * C_WIDTH] * z[:, 0:C_WIDTH]

    row = lax.broadcasted_iota(jnp.int32, (tm, C_WIDTH), 0)
    z1 = pltpu.roll(zc, 1, axis=0)
    z2 = pltpu.roll(zc, 2, axis=0)
    if per_row_state:
        t = row & (DEC_SEQ - 1)
        z1 = jnp.where(t == 0, s1_ref[...], z1)
        z2 = jnp.where(t == 0, s2_ref[...], jnp.where(t == 1, s1_ref[...], z2))
        tail_ref[...] = zc
    else:
        @pl.when(pl.program_id(0) % seq_tiles == 0)
        def _():
            carry_ref[...] = jnp.zeros(carry_ref.shape, F32)
        z1 = jnp.where(row == 0, carry_ref[SUBLANES - 1:SUBLANES, :], z1)
        z2 = jnp.where(row == 0, carry_ref[SUBLANES - 2:SUBLANES - 1, :],
                       jnp.where(row == 1, carry_ref[SUBLANES - 1:SUBLANES, :], z2))
        carry_ref[...] = zc[tm - SUBLANES:tm, :]
        tail_ref[...] = zc[tm - SUBLANES:tm, :]
    conv = cw_ref[0:1, :] * z2 + cw_ref[1:2, :] * z1 + cw_ref[2:3, :] * zc
    c_ref[:, 0:C_WIDTH] = z[:, C_WIDTH:2 * C_WIDTH] * conv
    c_ref[:, C_WIDTH:SLAB_C] = z[:, 3 * C_WIDTH:4 * C_WIDTH]


def _inproj(x, layer, g, w_in, cos, sin, convw, tm, table_blocks, seq_tiles, row_state=None, prev_caches=None):
    m = x.shape[0]
    per_row_state = row_state is not None
    aliased = prev_caches is not None
    row = lambda i: (i, 0)
    tab = lambda i: (i % table_blocks, 0)
    in_specs = [pl.BlockSpec((tm, D_MODEL), row), _resident((1, D_MODEL), layer), _resident((D_MODEL, MIX_W), layer),
                pl.BlockSpec((tm, LANES), tab), pl.BlockSpec((tm, LANES), tab), _resident((C_CONV, C_WIDTH), layer)]
    args = [x, g, w_in, cos, sin, convw]
    out_specs = [pl.BlockSpec((tm, SLAB_A), row), pl.BlockSpec((tm, SLAB_B), row), pl.BlockSpec((tm, SLAB_C), row)]
    out_shape = [jax.ShapeDtypeStruct((m, SLAB_A), F32), jax.ShapeDtypeStruct((m, SLAB_B), F32),
                 jax.ShapeDtypeStruct((m, SLAB_C), F32)]
    scratch, aliases = [], {}
    if per_row_state:
        in_specs += [pl.BlockSpec((tm, C_WIDTH), row)] * 2
        args += list(row_state)
        out_specs.append(pl.BlockSpec((tm, C_WIDTH), row))
        out_shape.append(jax.ShapeDtypeStruct((m, C_WIDTH), F32))
    else:
        scratch = [pltpu.VMEM((SUBLANES, C_WIDTH), F32)]
        out_specs.append(pl.BlockSpec((SUBLANES, C_WIDTH), row))
        out_shape.append(jax.ShapeDtypeStruct((m // tm * SUBLANES, C_WIDTH), F32))
        n_seqs = m // (tm * seq_tiles)
        widths = [(A_W, w) for w, _ in A_GROUPS for _ in range(2)] + [(B_KW, B_WINDOW)] * 2
        for j, (rows_, window) in enumerate(widths):
            per_tile = window == tm * seq_tiles
            idx = ((lambda i: (layer, i // seq_tiles, 0, i % seq_tiles)) if per_tile
                   else (lambda i: (layer, i // seq_tiles, 0, 0)))
            out_specs.append(pl.BlockSpec((None, 1, rows_, tm if per_tile else window), idx))
            out_shape.append(jax.ShapeDtypeStruct((DEPTH, n_seqs, rows_, window), F32))
        if aliased:
            aliases = {len(args) + j: 4 + j for j in range(N_CACHES)}
            in_specs += [pl.BlockSpec(memory_space=pl.ANY)] * N_CACHES
            args += list(prev_caches)
    return pl.pallas_call(
        functools.partial(_inproj_body, seq_tiles=seq_tiles, per_row_state=per_row_state, aliased=aliased),
        grid=(m // tm,),
        in_specs=in_specs,
        out_specs=out_specs,
        out_shape=out_shape,
        input_output_aliases=aliases,
        scratch_shapes=scratch,
        compiler_params=_params(1),
        name="in_proj",
    )(*args)


def _band_mask_t(nk):
    kj = lax.broadcasted_iota(jnp.int32, (nk, BLOCK), 0)
    qi = lax.broadcasted_iota(jnp.int32, (nk, BLOCK), 1)
    dist = qi + (nk - BLOCK) - kj
    return (dist >= 0) & (dist <= BLOCK)


def _softmax_t(s, mask, sink=None):
    s = jnp.where(mask, s, NEG_INF)
    m = jnp.max(s, axis=0, keepdims=True)
    if sink is not None:
        m = jnp.maximum(m, sink)
    p = jnp.exp(s - m)
    den = jnp.sum(p, axis=0, keepdims=True)
    if sink is not None:
        den = den + jnp.exp(sink - m)
    return p, 1.0 / den, m + jnp.log(den)


def _band_softmax_t(s, sink=None):
    j = lax.broadcasted_iota(jnp.int32, (BLOCK, BLOCK), 0)
    qi = lax.broadcasted_iota(jnp.int32, (BLOCK, BLOCK), 1)
    above = j > qi
    s_prev, s_own = s[0:BLOCK], s[BLOCK:2 * BLOCK]
    folded = jnp.where(above, s_prev, s_own)
    diag = jnp.sum(jnp.where(j == qi, s_prev, 0.0), axis=0, keepdims=True)
    m = jnp.maximum(jnp.max(folded, axis=0, keepdims=True), diag)
    if sink is not None:
        m = jnp.maximum(m, sink)
    p = jnp.exp(folded - m)
    p_diag = jnp.exp(diag - m)
    den = jnp.sum(p, axis=0, keepdims=True) + p_diag
    if sink is not None:
        den = den + jnp.exp(sink - m)
    unfolded = jnp.concatenate([jnp.where(above, p, 0.0), jnp.where(above, 0.0, p)], axis=0)
    return unfolded, p_diag, 1.0 / den, m + jnp.log(den)


def _attn_a_prompt_body(q_ref, k_ref, v_ref, o_ref, lse_ref, *, d):
    nb = SEQ // d // BLOCK
    pairs = q_ref.shape[-1] // LANES
    first_head_rows = lax.broadcasted_iota(jnp.int32, (LANES, BLOCK), 0) < HEAD_DIM

    def rows(start, n):
        return pl.ds(start, n) if d == 1 else pl.ds(start, n, stride=d)

    def block(q0, k0, nk):
        mask = _band_mask_t(nk)
        lo = _low_lanes(nk)
        for hp in range(pairs):
            cols = slice(hp * LANES, (hp + 1) * LANES)
            q = (q_ref[0, rows(q0, BLOCK), cols] * SCALE).astype(BF16)
            k = k_ref[0, rows(k0, nk), cols]
            v = v_ref[0, rows(k0, nk), cols]
            k0_, k1_ = jnp.where(lo, k, 0.0).astype(BF16), jnp.where(lo, 0.0, k).astype(BF16)
            vt32 = v.T
            vt = vt32.astype(BF16)
            if nk == BLOCK:
                st = _mm_nt(jnp.concatenate([k0_, k1_], axis=0), q)
                p0, r0, l0 = _softmax_t(st[0:nk], mask)
                p1, r1, l1 = _softmax_t(st[nk:2 * nk], mask)
                ot = _mm(vt, jnp.concatenate([p0, p1], axis=1).astype(BF16))
                o_t = jnp.where(first_head_rows, ot[:, 0:BLOCK] * r0, ot[:, BLOCK:2 * BLOCK] * r1)
            else:
                p0, pd0, r0, l0 = _band_softmax_t(_mm_nt(k0_, q))
                p1, pd1, r1, l1 = _band_softmax_t(_mm_nt(k1_, q))
                o_t = jnp.where(first_head_rows, (_mm(vt, p0.astype(BF16)) + vt32[:, 0:BLOCK] * pd0) * r0,
                                (_mm(vt, p1.astype(BF16)) + vt32[:, 0:BLOCK] * pd1) * r1)
            l_t = jnp.where(first_head_rows, jnp.broadcast_to(l0, (LANES, BLOCK)),
                            jnp.broadcast_to(l1, (LANES, BLOCK)))
            o_ref[0, rows(q0, BLOCK), cols] = o_t.T
            lse_ref[0, rows(q0, BLOCK), cols] = l_t.T

    for r in range(d):
        block(r, r, BLOCK)
        if nb > 1:
            def body(i, carry):
                q0 = r + i * (BLOCK * d)
                block(q0, q0 - BLOCK * d, 2 * BLOCK)
                return carry
            lax.fori_loop(1, nb, body, 0, unroll=nb - 1)


def _attn_a_prompt(slab_a, g, d):
    view = slab_a.reshape(BATCH, SEQ, SLAB_A)
    width = A_W if d == 1 else LANES
    steps = A_W // width
    spec = lambda off: pl.BlockSpec((1, SEQ, width), lambda b, hp: (b, 0, off * steps + hp))
    oshape = jax.ShapeDtypeStruct((BATCH, SEQ, A_W), F32)
    o, lse = pl.pallas_call(
        functools.partial(_attn_a_prompt_body, d=d),
        grid=(BATCH, steps),
        in_specs=[spec(g), spec(A_NG + g), spec(2 * A_NG + g)],
        out_specs=[spec(0), spec(0)],
        out_shape=[oshape, oshape],
        compiler_params=_params(2),
        name=f"attn_a{g + 1}_prompt",
    )(view, view, view)
    return o.reshape(BATCH * SEQ, A_W), lse.reshape(BATCH * SEQ, A_W)


def _attn_b_prompt_body(q_ref, k_ref, v_ref, sink_ref, o_ref, *, nb):
    def block(q0, k0, nk):
        mask = _band_mask_t(nk)
        lo = _low_lanes(nk)
        k = k_ref[0, pl.ds(k0, nk), :]
        k_sw = pltpu.roll(k, HEAD_DIM, axis=1)
        vt32 = v_ref[0, pl.ds(k0, nk), :].T
        vt = vt32.astype(BF16)
        for kv in range(B_KV_HEADS):
            on_lo, on_hi = (k, k_sw) if kv == 0 else (k_sw, k)
            k2 = jnp.concatenate([jnp.where(lo, on_lo, 0.0), jnp.where(lo, 0.0, on_hi)], axis=0).astype(BF16)
            cols = slice(2 * kv * LANES, (2 * kv + 2) * LANES)
            q2 = q_ref[0, pl.ds(q0, BLOCK), cols] * SCALE
            q2 = jnp.concatenate([q2[:, 0:LANES], q2[:, LANES:2 * LANES]], axis=0).astype(BF16)
            kv_rows = slice(kv * HEAD_DIM, (kv + 1) * HEAD_DIM)
            ps, rs, diag_terms = [], [], []
            for g in range(B_GROUP):
                half, chunk = g % 2, g // 2
                sink = sink_ref[kv * B_GROUP + g:kv * B_GROUP + g + 1, :]
                s = _mm_nt(k2[half * nk:(half + 1) * nk], q2[chunk * BLOCK:(chunk + 1) * BLOCK])
                if nk == BLOCK:
                    p, r, _ = _softmax_t(s, mask, sink)
                else:
                    p, p_diag, r, _ = _band_softmax_t(s, sink)
                    diag_terms.append(vt32[kv_rows, 0:BLOCK] * p_diag)
                ps.append(p)
                rs.append(r)
            ot = jnp.concatenate([_mm(vt, p.astype(BF16))[kv_rows, :] for p in ps], axis=1)
            if diag_terms:
                ot = ot + jnp.concatenate(diag_terms, axis=1)
            oj = ot * jnp.concatenate(rs, axis=1)
            for chunk in range(2):
                o_t = jnp.concatenate([oj[:, (2 * chunk) * BLOCK:(2 * chunk + 1) * BLOCK],
                                       oj[:, (2 * chunk + 1) * BLOCK:(2 * chunk + 2) * BLOCK]], axis=0)
                c = 2 * kv + chunk
                o_ref[0, pl.ds(q0, BLOCK), c * LANES:(c + 1) * LANES] = o_t.T

    block(0, 0, BLOCK)

    def body(i, carry):
        q0 = pl.multiple_of(i * BLOCK, BLOCK)
        block(q0, pl.multiple_of(q0 - BLOCK, BLOCK), 2 * BLOCK)
        return carry
    lax.fori_loop(1, nb, body, 0, unroll=15)


def _attn_b_prompt(slab_b, layer, sinks):
    view = slab_b.reshape(BATCH, SEQ, SLAB_B)
    nq = B_QW // LANES
    o = pl.pallas_call(
        functools.partial(_attn_b_prompt_body, nb=SEQ // BLOCK),
        grid=(BATCH,),
        in_specs=[pl.BlockSpec((1, SEQ, B_QW), lambda b: (b, 0, 0)),
                  pl.BlockSpec((1, SEQ, B_KW), lambda b: (b, 0, nq)),
                  pl.BlockSpec((1, SEQ, B_KW), lambda b: (b, 0, nq + 1)),
                  _resident((B_Q_HEADS, LANES), layer)],
        out_specs=pl.BlockSpec((1, SEQ, B_QW), lambda b: (b, 0, 0)),
        out_shape=jax.ShapeDtypeStruct((BATCH, SEQ, B_QW), F32),
        compiler_params=_params(1),
        name="attn_b_prompt",
    )(view, view, view, sinks)
    return o.reshape(BATCH * SEQ, B_QW)


def _sample_masks(rows, lb, window, dil):
    assert dil & (dil - 1) == 0 and DEC_SEQ & (DEC_SEQ - 1) == 0
    t_c = lax.broadcasted_iota(jnp.int32, (rows, lb), 0) & (DEC_SEQ - 1)
    dist_c = lb + t_c - lax.broadcasted_iota(jnp.int32, (rows, lb), 1)
    mask_c = (dist_c >= 0) & (dist_c <= window) & ((dist_c & (dil - 1)) == 0)
    t_n = lax.broadcasted_iota(jnp.int32, (rows, DEC_SEQ), 0) & (DEC_SEQ - 1)
    dist_n = t_n - lax.broadcasted_iota(jnp.int32, (rows, DEC_SEQ), 1)
    mask_n = (dist_n >= 0) & (dist_n <= window) & ((dist_n & (dil - 1)) == 0)
    return mask_c, mask_n


def _sample_attend(q, kt, vt, kn, vn, masks, sink=None):
    mask_c, mask_n = masks
    s_c = jnp.where(mask_c, _mm(q.astype(BF16), kt), NEG_INF)
    s_n = jnp.where(mask_n, _mm_nt(q, kn), NEG_INF)
    m = jnp.maximum(jnp.max(s_c, axis=-1, keepdims=True), jnp.max(s_n, axis=-1, keepdims=True))
    if sink is not None:
        m = jnp.maximum(m, sink)
    p_c = jnp.exp(s_c - m)
    p_n = jnp.exp(s_n - m)
    den = jnp.sum(p_c, axis=-1, keepdims=True) + jnp.sum(p_n, axis=-1, keepdims=True)
    if sink is not None:
        den = den + jnp.exp(sink - m)
    o = _mm_nt(p_c.astype(BF16), vt) + _mm(p_n, vn)
    return o * (1.0 / den), m + jnp.log(den)


def _write_shifted_cache(dst_ref, src_ref, s, heads, new_rows, lb):
    padded = jnp.concatenate([jnp.zeros((LANES - DEC_SEQ, LANES), F32), new_rows], axis=0)
    new_t = padded.T
    keep = lax.broadcasted_iota(jnp.int32, (LANES, LANES), 1) < LANES - DEC_SEQ
    rotated = lambda j: pltpu.roll(src_ref[0, s, heads, :, j * LANES:(j + 1) * LANES].reshape(LANES, LANES),
                                   LANES - DEC_SEQ, axis=1)
    n_tiles = lb // LANES
    cur = rotated(0)
    for j in range(n_tiles):
        nxt = rotated(j + 1) if j + 1 < n_tiles else new_t
        dst_ref[0, s, heads, :, j * LANES:(j + 1) * LANES] = jnp.where(keep, cur, nxt).reshape(2, HEAD_DIM, LANES)
        cur = nxt


SAMPLE_SEQS_PER_STEP = 2


def _sample_attn_body(*refs, aliased, n_seq):
    for s in range(n_seq):
        _sample_attn_one(refs, aliased, s)


def _sample_attn_one(refs, aliased, s):
    sa_ref, sb_ref, sink_ref = refs[:3]
    cache_refs = refs[3:11]
    pos = 11 + (8 if aliased else 0)
    oa_ref, ob_ref = refs[pos:pos + 2]
    out_refs = refs[pos + 2:pos + 10]
    lo8 = _low_lanes(DEC_SEQ)
    seq_rows = slice(s * DEC_SEQ, (s + 1) * DEC_SEQ)

    o_g, l_g = [], []
    for g, (window, dil) in enumerate(A_GROUPS):
        kc_ref, vc_ref, ko_ref, vo_ref = cache_refs[2 * g], cache_refs[2 * g + 1], out_refs[2 * g], out_refs[2 * g + 1]
        lb = kc_ref.shape[-1]
        masks = _sample_masks(2 * DEC_SEQ, lb, window, dil)
        o_pairs, l_pairs = [], []
        for c in range(A_W // LANES):
            col = lambda part: slice(part * A_QKV + g * A_W + c * LANES, part * A_QKV + g * A_W + (c + 1) * LANES)
            heads = slice(2 * c, 2 * c + 2)
            q = sa_ref[seq_rows, col(0)] * SCALE
            kn, vn = sa_ref[seq_rows, col(1)], sa_ref[seq_rows, col(2)]
            kt = kc_ref[0, s, heads].reshape(LANES, lb)
            vt = vc_ref[0, s, heads].reshape(LANES, lb)
            q2 = jnp.concatenate([jnp.where(lo8, q, 0.0), jnp.where(lo8, 0.0, q)], axis=0)
            o, lse = _sample_attend(q2, kt.astype(BF16), vt.astype(BF16), kn, vn, masks)
            o_pairs.append(jnp.where(lo8, o[0:DEC_SEQ], o[DEC_SEQ:]))
            l_pairs.append(jnp.where(lo8, lse[0:DEC_SEQ], lse[DEC_SEQ:]))
            _write_shifted_cache(ko_ref, kc_ref, s, heads, kn, lb)
            _write_shifted_cache(vo_ref, vc_ref, s, heads, vn, lb)
        o_g.append(o_pairs)
        l_g.append(l_pairs)
    for c in range(A_W // LANES):
        l1, l2, l3 = l_g[0][c], l_g[1][c], l_g[2][c]
        lm = jnp.maximum(jnp.maximum(l1, l2), l3)
        e1, e2, e3 = jnp.exp(l1 - lm), jnp.exp(l2 - lm), jnp.exp(l3 - lm)
        es = e1 + e2 + e3
        oa_ref[seq_rows, c * LANES:(c + 1) * LANES] = ((e1 / es) * o_g[0][c] + (e2 / es) * o_g[1][c]
                                                       + (e3 / es) * o_g[2][c])

    kc_ref, vc_ref, ko_ref, vo_ref = cache_refs[6], cache_refs[7], out_refs[6], out_refs[7]
    lb = kc_ref.shape[-1]
    rows = B_GROUP * DEC_SEQ
    masks = _sample_masks(rows, lb, B_WINDOW, 1)
    both = slice(0, B_KV_HEADS)
    ktb = kc_ref[0, s].reshape(LANES, lb).astype(BF16)
    vtb = vc_ref[0, s].reshape(LANES, lb).astype(BF16)
    kn, vn = sb_ref[seq_rows, B_QW:B_QW + B_KW], sb_ref[seq_rows, B_QW + B_KW:SLAB_B]
    halves = {}
    for kv in range(B_KV_HEADS):
        parts = []
        for gq in range(B_GROUP):
            head = kv * B_GROUP + gq
            q = sb_ref[seq_rows, (head // 2) * LANES:(head // 2 + 1) * LANES] * SCALE
            q = jnp.where(lo8, q, 0.0) if head % 2 == 0 else jnp.where(lo8, 0.0, q)
            parts.append(q if head % 2 == kv else pltpu.roll(q, HEAD_DIM, axis=1))
        q4 = jnp.concatenate(parts, axis=0)
        sink = sink_ref[kv * rows:(kv + 1) * rows, 0:1]
        o, _ = _sample_attend(q4, ktb, vtb, kn, vn, masks, sink)
        for gq in range(B_GROUP):
            head = kv * B_GROUP + gq
            oh = o[gq * DEC_SEQ:(gq + 1) * DEC_SEQ]
            halves[head] = oh if head % 2 == kv else pltpu.roll(oh, HEAD_DIM, axis=1)
    for c in range(B_QW // LANES):
        ob_ref[seq_rows, c * LANES:(c + 1) * LANES] = jnp.where(lo8, halves[2 * c], halves[2 * c + 1])
    _write_shifted_cache(ko_ref, kc_ref, s, both, kn, lb)
    _write_shifted_cache(vo_ref, vc_ref, s, both, vn, lb)


def _sample_attn(layer, slab_a, slab_b, sinks, caches, prev):
    aliased = prev is not None
    n_seq = SAMPLE_SEQS_PER_STEP
    rows = n_seq * DEC_SEQ
    row = lambda b: (b, 0)
    cspec = lambda c: pl.BlockSpec((1, n_seq) + c.shape[2:], lambda b: (layer, b, 0, 0, 0))
    in_specs = [pl.BlockSpec((rows, SLAB_A), row), pl.BlockSpec((rows, SLAB_B), row),
                _resident(sinks.shape[1:], layer)]
    in_specs += [cspec(c) for c in caches]
    args = [slab_a, slab_b, sinks] + list(caches)
    aliases = {}
    if aliased:
        aliases = {len(args) + j: 2 + j for j in range(8)}
        in_specs += [pl.BlockSpec(memory_space=pl.ANY)] * 8
        args += list(prev)
    m = slab_a.shape[0]
    res = pl.pallas_call(
        functools.partial(_sample_attn_body, aliased=aliased, n_seq=n_seq),
        grid=(DEC_BATCH // n_seq,),
        in_specs=in_specs,
        out_specs=[pl.BlockSpec((rows, A_W), row), pl.BlockSpec((rows, B_QW), row)] + [cspec(c) for c in caches],
        out_shape=[jax.ShapeDtypeStruct((m, A_W), F32), jax.ShapeDtypeStruct((m, B_QW), F32)]
                  + [jax.ShapeDtypeStruct(c.shape, F32) for c in caches],
        input_output_aliases=aliases,
        compiler_params=_params(1),
        name="sample_attn",
    )(*args)
    return res[0], res[1], list(res[2:])


def _ssm_body(u_ref, are_ref, aim_ref, b_ref, cbig_ref, dvec_ref, wglu_ref, x0re_ref, x0im_ref,
              od_ref, sre_ref, sim_ref, bu_ref, xs_ref, *, steps, pitch):
    @pl.when(pl.program_id(1) == 0)
    def _():
        sre_ref[0] = x0re_ref[0]
        sim_ref[0] = x0im_ref[0]

    n_re = D_NS // LANES
    tiles = lambda x: [x[:, j * LANES:(j + 1) * LANES] for j in range(x.shape[1] // LANES)]

    plane = lambda s: slice(s * pitch, s * pitch + steps)
    u = u_ref[...].reshape(SUBLANES * steps, D_WIDTH)
    bu = _mm(u.astype(BF16), b_ref[...])
    for s in range(SUBLANES):
        for j, tile in enumerate(tiles(bu[s * steps:(s + 1) * steps])):
            bu_ref[j, plane(s), :] = tile
    a_re = are_ref[...]
    a_im = aim_ref[...]

    def step(t, carry):
        re, im = carry
        at_t = pl.ds(t, SUBLANES, stride=pitch)
        b_re = jnp.concatenate([bu_ref[j, at_t, :] for j in range(n_re)], axis=1)
        b_im = jnp.concatenate([bu_ref[n_re + j, at_t, :] for j in range(n_re)], axis=1)
        re, im = a_re * re - a_im * im + b_re, a_re * im + a_im * re + b_im
        for j, (tr, ti) in enumerate(zip(tiles(re), tiles(im))):
            xs_ref[j, at_t, :] = tr
            xs_ref[n_re + j, at_t, :] = ti
        return re, im

    re, im = lax.fori_loop(0, steps, step, (sre_ref[0], sim_ref[0]), unroll=min(steps, SUBLANES))
    sre_ref[0] = re
    sim_ref[0] = im
    group = SUBLANES // 2 if SUBLANES * steps >= 2 * SAMPLE_TILE else SUBLANES
    for s0 in range(0, SUBLANES, group):
        seqs = range(s0, s0 + group)
        xs = jnp.concatenate([jnp.concatenate([xs_ref[j, plane(s), :] for j in range(2 * n_re)], axis=1)
                              for s in seqs], axis=0)
        y = _mm(xs.astype(BF16), cbig_ref[...]) + dvec_ref[...] * u[s0 * steps:(s0 + group) * steps]
        y = _mm(jax.nn.gelu(y).astype(BF16), wglu_ref[...])
        od = y[:, 0:D_WIDTH] * jax.nn.sigmoid(y[:, D_WIDTH:2 * D_WIDTH])
        od_ref[s0:s0 + group] = od.reshape(group, steps, D_WIDTH)


def _ssm(slab_c, bsz, length, layer, a_re, a_im, bbig, cbig, dvec, wglu, x0_re, x0_im, steps):
    n = bsz // SUBLANES
    pitch = steps if steps <= SUBLANES else steps + SUBLANES
    view = slab_c.reshape(bsz, length, SLAB_C)
    u_spec = pl.BlockSpec((SUBLANES, steps, D_WIDTH), lambda i, c: (i, c, SLAB_C // D_WIDTH - 1))
    o_spec = pl.BlockSpec((SUBLANES, steps, D_WIDTH), lambda i, c: (i, c, 0))
    st_spec = pl.BlockSpec((1, SUBLANES, D_NS), lambda i, c: (i, 0, 0))
    st_shape = jax.ShapeDtypeStruct((n, SUBLANES, D_NS), F32)
    od, s_re, s_im = pl.pallas_call(
        functools.partial(_ssm_body, steps=steps, pitch=pitch),
        grid=(n, length // steps),
        in_specs=[u_spec, _resident((SUBLANES, D_NS), layer), _resident((SUBLANES, D_NS), layer),
                  _resident((D_WIDTH, 2 * D_NS), layer), _resident((2 * D_NS, D_WIDTH), layer),
                  _resident((1, D_WIDTH), layer), _resident((D_WIDTH, 2 * D_WIDTH), layer), st_spec, st_spec],
        out_specs=[o_spec, st_spec, st_spec],
        out_shape=[jax.ShapeDtypeStruct((bsz, length, D_WIDTH), F32), st_shape, st_shape],
        scratch_shapes=[pltpu.VMEM((2 * D_NS // LANES, SUBLANES * pitch, LANES), F32)] * 2,
        compiler_params=_params(2),
        name="ssm",
    )(view, a_re, a_im, bbig, cbig, dvec, wglu, x0_re, x0_im)
    return od.reshape(bsz * length, D_WIDTH), s_re, s_im


def _merge_body(*refs, n_groups):
    x_ref = refs[0]
    a_refs = refs[1:1 + (2 * n_groups if n_groups > 1 else 1)]
    ob_ref, oc_ref, od_ref, gpre_ref, wgl_ref, wa_ref, wb_ref, wc_ref, wd_ref, wout_ref, gpost_ref, y_ref = \
        refs[1 + len(a_refs):]
    tm = x_ref.shape[0]
    for rows in _row_halves(tm):
        x = x_ref[rows, :]
        h = _rmsnorm(x, gpre_ref[...]).astype(BF16)
        if n_groups > 1:
            lses = [a_refs[2 * g + 1][rows, :] for g in range(n_groups)]
            lm = functools.reduce(jnp.maximum, lses)
            es = [jnp.exp(l - lm) for l in lses]
            den = functools.reduce(lambda a, b: a + b, es)
            o_a = functools.reduce(lambda a, b: a + b,
                                   [(es[g] / den) * a_refs[2 * g][rows, :] for g in range(n_groups)])
        else:
            o_a = a_refs[0][rows, :]
        branches = ((o_a, wa_ref), (ob_ref[rows, :], wb_ref), (oc_ref[rows, :], wc_ref), (od_ref[rows, :], wd_ref))
        merged = None
        for j in (1, 2, 3, 0):
            o, w_ref = branches[j]
            proj = _mm(o.astype(BF16), w_ref[...])
            term = jax.nn.sigmoid(_mm(h, wgl_ref[:, j * D_MODEL:(j + 1) * D_MODEL])) * proj
            merged = term if merged is None else merged + term
        mix = _mm(merged.astype(BF16), wout_ref[...])
        y_ref[rows, :] = x + _rmsnorm(mix, gpost_ref[...])


def _merge(x, a_parts, ob, slab_c, od, layer, gpre, w_in, wa, wb, wc, wd, wout, gpost, tm):
    m = x.shape[0]
    row = lambda i: (i, 0)
    r256 = pl.BlockSpec((tm, 256), row)
    n_groups = max(1, len(a_parts) // 2)
    return pl.pallas_call(
        functools.partial(_merge_body, n_groups=n_groups),
        grid=(m // tm,),
        in_specs=[pl.BlockSpec((tm, D_MODEL), row)] + [r256] * len(a_parts)
                 + [pl.BlockSpec((tm, B_QW), row), r256, r256,
                    _resident((1, D_MODEL), layer), _resident((D_MODEL, N_BRANCH * D_MODEL), layer, col=1),
                    _resident((A_W, D_MODEL), layer), _resident((B_QW, D_MODEL), layer),
                    _resident((C_WIDTH, D_MODEL), layer), _resident((D_WIDTH, D_MODEL), layer),
                    _resident((D_MODEL, D_MODEL), layer), _resident((1, D_MODEL), layer)],
        out_specs=pl.BlockSpec((tm, D_MODEL), row),
        out_shape=jax.ShapeDtypeStruct((m, D_MODEL), F32),
        compiler_params=_params(1),
        name="merge",
    )(x, *a_parts, ob, slab_c, od, gpre, w_in, wa, wb, wc, wd, wout, gpost)


def _ffn_body(x_ref, p_ref, gpre_ref, wg_ref, wu_ref, wdn_ref, gpost_ref, wple_ref, wpg_ref, y_ref):
    x = x_ref[...]
    h = _rmsnorm(x, gpre_ref[...]).astype(BF16)
    act = jax.nn.silu(_mm(h, wg_ref[...])) * _mm(h, wu_ref[...])
    f = _mm(act.astype(BF16), wdn_ref[...])
    x = x + _rmsnorm(f, gpost_ref[...])
    gate = jax.nn.sigmoid(_mm(x.astype(BF16), wpg_ref[...]))
    y_ref[...] = x + gate * _mm(p_ref[...].astype(BF16), wple_ref[...])


def _ffn(x, p_all, layer, gpre, wg, wu, wdn, gpost, wple, wpg, tm):
    m = x.shape[0]
    row = lambda i: (i, 0)
    return pl.pallas_call(
        _ffn_body,
        grid=(m // tm,),
        in_specs=[pl.BlockSpec((tm, D_MODEL), row), pl.BlockSpec((None, tm, PLE_DIM), lambda i: (layer, i, 0)),
                  _resident((1, D_MODEL), layer), _resident((D_MODEL, D_FF), layer), _resident((D_MODEL, D_FF), layer),
                  _resident((D_FF, D_MODEL), layer), _resident((1, D_MODEL), layer),
                  _resident((PLE_DIM, D_MODEL), layer), _resident((D_MODEL, D_MODEL), layer)],
        out_specs=pl.BlockSpec((tm, D_MODEL), row),
        out_shape=jax.ShapeDtypeStruct((m, D_MODEL), F32),
        compiler_params=_params(1),
        name="ffn_ple",
    )(x, p_all, gpre, wg, wu, wdn, gpost, wple, wpg)


def _rope_tables(pos):
    inv = ROPE_THETA ** (-jnp.arange(HALF, dtype=F32) / HALF)
    ang = pos.astype(F32)[:, None] * inv[None, :]
    cos, sin = jnp.cos(ang), jnp.sin(ang)
    cos_t = jnp.tile(cos, (1, LANES // HALF))
    sin_t = jnp.tile(jnp.concatenate([-sin, sin], axis=1), (1, LANES // HEAD_DIM))
    return cos_t, sin_t


def _ssm_weights(lam_re, lam_im, log_dt, b_re, b_im, c_re, c_im):
    lam = lax.complex(lam_re, lam_im)
    dt = jnp.exp(log_dt)[..., None]
    a_bar = jnp.exp(lam * dt)
    b_bar = ((a_bar - 1.0) / lam)[..., None] * lax.complex(b_re, b_im)
    eye = jnp.eye(D_NGROUPS, dtype=F32)
    pack_b = lambda b: jnp.einsum('lgni,gh->lgihn', b, eye).reshape(DEPTH, D_WIDTH, D_NS)
    pack_c = lambda c: jnp.einsum('lgin,gh->lgnhi', c, eye).reshape(DEPTH, D_NS, D_WIDTH)
    bbig = jnp.concatenate([pack_b(jnp.real(b_bar)), pack_b(jnp.imag(b_bar))], axis=2)
    cbig = jnp.concatenate([pack_c(c_re), -pack_c(c_im)], axis=1)
    tile8 = lambda a: jnp.broadcast_to(a.reshape(DEPTH, 1, D_NS), (DEPTH, SUBLANES, D_NS))
    return tile8(jnp.real(a_bar)), tile8(jnp.imag(a_bar)), bbig.astype(BF16), cbig.astype(BF16)


def _key_major(c):
    return jnp.transpose(c, (0, 1, 3, 4, 2))


def _row_major(c):
    return jnp.transpose(c, (0, 1, 4, 2, 3))


def kernel(x_prompt, x_sample, p_prompt, p_sample, cache_a1_k, cache_a1_v, cache_a2_k, cache_a2_v, cache_a3_k, cache_a3_v, cache_b_k, cache_b_v, state_c_conv, state_d_re, state_d_im, norm_mix_pre, norm_mix_post, norm_ffn_pre, norm_ffn_post, w_in, attn_sinks, conv_c_w, ssm_lam_re, ssm_lam_im, ssm_log_dt, ssm_b_re, ssm_b_im, ssm_c_re, ssm_c_im, ssm_d, w_d_glu, w_br_a, w_br_b, w_br_c, w_br_d, w_out, w_ffn_gate, w_ffn_up, w_ffn_down, w_ple, w_ple_gate):
    assert x_prompt.shape == (BATCH, SEQ, D_MODEL) and x_sample.shape == (DEC_BATCH, DEC_SEQ, D_MODEL)
    assert w_in.shape == (DEPTH, D_MODEL, 2 * MIX_W)
    assert all(min(w, PAST_LEN) == w for w, _ in A_GROUPS) and min(B_WINDOW, PAST_LEN) == B_WINDOW
    mp, ms = BATCH * SEQ, DEC_BATCH * DEC_SEQ
    tm_p, tm_s = PROMPT_TILE, SAMPLE_TILE

    cos_p, sin_p = _rope_tables(jnp.arange(SEQ, dtype=jnp.int32))
    cos_s, sin_s = _rope_tables(PAST_LEN + jnp.arange(DEC_SEQ, dtype=jnp.int32))
    cos_s, sin_s = jnp.tile(cos_s, (tm_s // DEC_SEQ, 1)), jnp.tile(sin_s, (tm_s // DEC_SEQ, 1))

    caches = [_key_major(c) for c in (cache_a1_k, cache_a1_v, cache_a2_k, cache_a2_v, cache_a3_k, cache_a3_v,
                                      cache_b_k, cache_b_v)]
    yp = x_prompt.reshape(mp, D_MODEL)
    ys = x_sample.reshape(ms, D_MODEL)
    new_caches = None
    prompt_caches = None
    st_p = [[] for _ in range(3)]
    conv_s, dre_s, dim_s = [], [], []

    rows3 = lambda a: a.reshape(DEPTH, 1, -1)
    w_in_b = w_in.astype(BF16)
    g_mix_pre = rows3(norm_mix_pre)
    ssm_w = _ssm_weights(ssm_lam_re, ssm_lam_im, ssm_log_dt, ssm_b_re, ssm_b_im, ssm_c_re, ssm_c_im) + (
        rows3(ssm_d), w_d_glu.astype(BF16))
    sinks_p = jnp.broadcast_to(attn_sinks.reshape(DEPTH, B_Q_HEADS, 1), (DEPTH, B_Q_HEADS, LANES))
    sinks_s = jnp.broadcast_to(jnp.repeat(attn_sinks.reshape(DEPTH, B_Q_HEADS), DEC_SEQ, axis=1)[..., None],
                               (DEPTH, B_Q_HEADS * DEC_SEQ, LANES))
    merge_w = (g_mix_pre, w_in_b, w_br_a.astype(BF16), w_br_b.astype(BF16), w_br_c.astype(BF16),
               w_br_d.astype(BF16), w_out.astype(BF16), rows3(norm_mix_post))
    ffn_w = (rows3(norm_ffn_pre), w_ffn_gate.astype(BF16), w_ffn_up.astype(BF16), w_ffn_down.astype(BF16),
             rows3(norm_ffn_post), w_ple.astype(BF16), w_ple_gate.astype(BF16))
    zero_state = jnp.zeros((BATCH // SUBLANES, SUBLANES, D_NS), F32)

    for i in range(DEPTH):
        sa, sb, sc, tail, *prompt_caches = _inproj(yp, i, g_mix_pre, w_in_b, cos_p, sin_p, conv_c_w, tm_p,
                                                   SEQ // tm_p, SEQ // tm_p, prev_caches=prompt_caches)
        a_parts = []
        for g, (w, d) in enumerate(A_GROUPS):
            a_parts += list(_attn_a_prompt(sa, g, d))
        ob = _attn_b_prompt(sb, i, sinks_p)
        od, s_re, s_im = _ssm(sc, BATCH, SEQ, i, *ssm_w, zero_state, zero_state, SSM_CHUNK)
        x1 = _merge(yp, a_parts, ob, sc, od, i, *merge_w, tm_p)
        yp = _ffn(x1, p_prompt.reshape(DEPTH, mp, PLE_DIM), i, *ffn_w, tm_p)
        st_p[0].append(tail.reshape(BATCH, SEQ // tm_p, SUBLANES, C_WIDTH)[:, -1, SUBLANES - (C_CONV - 1):])
        st_p[1].append(s_re.reshape(BATCH, D_NGROUPS, D_STATE))
        st_p[2].append(s_im.reshape(BATCH, D_NGROUPS, D_STATE))

        s1 = jnp.broadcast_to(state_c_conv[i][:, None, 1, :], (DEC_BATCH, DEC_SEQ, C_WIDTH)).reshape(ms, C_WIDTH)
        s2 = jnp.broadcast_to(state_c_conv[i][:, None, 0, :], (DEC_BATCH, DEC_SEQ, C_WIDTH)).reshape(ms, C_WIDTH)
        sa, sb, sc, tail = _inproj(ys, i, g_mix_pre, w_in_b, cos_s, sin_s, conv_c_w, tm_s, 1, 1, row_state=(s1, s2))
        o_a, ob, new_caches = _sample_attn(i, sa, sb, sinks_s, caches, new_caches)
        x0_re = state_d_re[i].reshape(DEC_BATCH // SUBLANES, SUBLANES, D_NS)
        x0_im = state_d_im[i].reshape(DEC_BATCH // SUBLANES, SUBLANES, D_NS)
        od, s_re, s_im = _ssm(sc, DEC_BATCH, DEC_SEQ, i, *ssm_w, x0_re, x0_im, DEC_SEQ)
        x1 = _merge(ys, [o_a], ob, sc, od, i, *merge_w, tm_s)
        ys = _ffn(x1, p_sample.reshape(DEPTH, ms, PLE_DIM), i, *ffn_w, tm_s)
        conv_s.append(tail.reshape(DEC_BATCH, DEC_SEQ, C_WIDTH)[:, DEC_SEQ - (C_CONV - 1):])
        dre_s.append(s_re.reshape(DEC_BATCH, D_NGROUPS, D_STATE))
        dim_s.append(s_im.reshape(DEC_BATCH, D_NGROUPS, D_STATE))

    heads_of = lambda c: c.reshape(c.shape[:2] + (c.shape[2] // HEAD_DIM, HEAD_DIM, c.shape[3]))
    p_states = [_row_major(heads_of(c)) for c in prompt_caches] + [jnp.stack(s) for s in st_p]
    s_states = [_row_major(c) for c in new_caches] + [jnp.stack(conv_s), jnp.stack(dre_s), jnp.stack(dim_s)]
    out = [yp.reshape(BATCH, SEQ, D_MODEL), ys.reshape(DEC_BATCH, DEC_SEQ, D_MODEL)]
    for a, b in zip(p_states, s_states):
        out += [a, b]
    return tuple(out)
```

```python
import functools

import jax
import jax.numpy as jnp
from jax import lax
from jax.experimental import pallas as pl
from jax.experimental.pallas import tpu as pltpu

F32 = jnp.float32
BF16 = jnp.bfloat16

D_MODEL = 1024
BATCH = 8
SEQ = 2048
DEPTH = 2
DEC_BATCH = 128
DEC_SEQ = 8
PAST_LEN = 16384
HEAD_DIM = 64
HALF = HEAD_DIM // 2
ROPE_THETA = 10000.0
BLOCK = 128
EPS = 1e-6
NEG_INF = -1e30
A_HEADS = 4
A_GROUPS = ((128, 1), (512, 4), (2048, 16))
A_NG = len(A_GROUPS)
A_W = A_HEADS * HEAD_DIM
A_QKV = A_NG * A_W
B_Q_HEADS = 8
B_KV_HEADS = 2
B_GROUP = B_Q_HEADS // B_KV_HEADS
B_WINDOW = 128
B_QW = B_Q_HEADS * HEAD_DIM
B_KW = B_KV_HEADS * HEAD_DIM
C_WIDTH = 256
C_CONV = 3
D_GROUP = 16
D_NGROUPS = 16
D_WIDTH = 256
D_STATE = 64
D_NS = D_NGROUPS * D_STATE
N_BRANCH = 4
D_FF = 2816
PLE_DIM = 256
MIX_W = 4096
SCALE = HEAD_DIM ** -0.5
N_CACHES = 2 * A_NG + 2

LANES = 128
SUBLANES = 8
VMEM_LIMIT = 56 * 1024 * 1024
PROMPT_TILE = 512
SAMPLE_TILE = 256
SSM_CHUNK = 256

SLAB_A = 3 * A_QKV
SLAB_B = B_QW + 2 * B_KW
SLAB_C = C_WIDTH + D_WIDTH
ROPE_A = 2 * A_QKV
ROPE_B = B_QW + B_KW


def _params(n_axes):
    return pltpu.CompilerParams(dimension_semantics=("arbitrary",) * n_axes, vmem_limit_bytes=VMEM_LIMIT)


def _resident(shape, layer, col=0):
    return pl.BlockSpec((None,) + tuple(shape), lambda *_: (layer, 0, col), pipeline_mode=pl.Buffered(1))


def _rmsnorm(x, g):
    return x * lax.rsqrt(jnp.mean(x * x, axis=-1, keepdims=True) + EPS) * g


def _mm(a, b):
    return jnp.dot(a, b, preferred_element_type=F32)


def _mm_nt(a, b):
    return lax.dot_general(a, b, (((1,), (1,)), ((), ())), preferred_element_type=F32)


def _row_halves(tm):
    if tm >= 2 * SAMPLE_TILE:
        return (slice(0, tm // 2), slice(tm // 2, tm))
    return (slice(0, tm),)


def _low_lanes(rows):
    return lax.broadcasted_iota(jnp.int32, (rows, LANES), 1) < HEAD_DIM


def _inproj_body(*refs, seq_tiles, per_row_state, aliased):
    if per_row_state:
        x_ref, g_ref, w_ref, cos_ref, sin_ref, cw_ref, s1_ref, s2_ref, a_ref, b_ref, c_ref, tail_ref = refs
    else:
        x_ref, g_ref, w_ref, cos_ref, sin_ref, cw_ref = refs[:6]
        n_in = 6 + (N_CACHES if aliased else 0)
        a_ref, b_ref, c_ref, tail_ref = refs[n_in:n_in + 4]
        cache_refs = refs[n_in + 4:n_in + 4 + N_CACHES]
        carry_ref = refs[n_in + 4 + N_CACHES]
    tm = x_ref.shape[0]
    a_halves, b_halves, z_halves = [], [], []
    for rows in _row_halves(tm):
        h = _rmsnorm(x_ref[rows, :], g_ref[...]).astype(BF16)
        cos = cos_ref[rows, :]
        sin = sin_ref[rows, :]
        lane = lax.broadcasted_iota(jnp.int32, cos.shape, 1)
        first_half = (lane & (HEAD_DIM - 1)) < HALF

        def rope(z):
            partner = jnp.where(first_half, pltpu.roll(z, LANES - HALF, axis=1), pltpu.roll(z, HALF, axis=1))
            return z * cos + partner * sin

        za = _mm(h, w_ref[:, 0:SLAB_A])
        chunks = []
        for c in range(SLAB_A // LANES):
            blk = za[:, c * LANES:(c + 1) * LANES]
            chunks.append(rope(blk) if c * LANES < ROPE_A else blk)
            a_ref[rows, c * LANES:(c + 1) * LANES] = chunks[-1]
        a_halves.append(chunks)
        zb = _mm(h, w_ref[:, SLAB_A:SLAB_A + SLAB_B])
        chunks = []
        for c in range(SLAB_B // LANES):
            blk = zb[:, c * LANES:(c + 1) * LANES]
            chunks.append(rope(blk) if c * LANES < ROPE_B else blk)
            b_ref[rows, c * LANES:(c + 1) * LANES] = chunks[-1]
        b_halves.append(chunks)
        z_halves.append(_mm(h, w_ref[:, SLAB_A + SLAB_B:MIX_W]))
    whole = lambda halves, c: jnp.concatenate([chunks[c] for chunks in halves], axis=0)

    if not per_row_state:
        pairs = A_W // LANES
        chunk_of = lambda part, g, j: whole(a_halves, (part * A_QKV + g * A_W) // LANES + j)

        def write(ref, chunk, rows):
            ref[0, :, :] = jnp.concatenate([ch[tm - rows:tm, :].T for ch in chunk], axis=0)

        last_group = A_NG - 1
        assert A_GROUPS[last_group][0] == seq_tiles * tm and A_GROUPS[1][0] == tm and A_GROUPS[0][0] <= tm
        for part in (1, 2):
            write(cache_refs[2 * last_group + part - 1], [chunk_of(part, last_group, j) for j in range(pairs)], tm)

        @pl.when(pl.program_id(0) % seq_tiles == seq_tiles - 1)
        def _():
            for g in range(last_group):
                for part in (1, 2):
                    write(cache_refs[2 * g + part - 1], [chunk_of(part, g, j) for j in range(pairs)], A_GROUPS[g][0])
            write(cache_refs[6], [whole(b_halves, B_QW // LANES)], B_WINDOW)
            write(cache_refs[7], [whole(b_halves, B_QW // LANES + 1)], B_WINDOW)
    z = jnp.concatenate(z_halves, axis=0)
    zc = z[:, 2 * C_WIDTH:3 * C_WIDTH] * z[:, 0:C_WIDTH]

    row = lax.broadcasted_iota(jnp.int32, (tm, C_WIDTH), 0)
    z1 = pltpu.roll(zc, 1, axis=0)
    z2 = pltpu.roll(zc, 2, axis=0)
    if per_row_state:
        t = row & (DEC_SEQ - 1)
        z1 = jnp.where(t == 0, s1_ref[...], z1)
        z2 = jnp.where(t == 0, s2_ref[...], jnp.where(t == 1, s1_ref[...], z2))
        tail_ref[...] = zc
    else:
        @pl.when(pl.program_id(0) % seq_tiles == 0)
        def _():
            carry_ref[...] = jnp.zeros(carry_ref.shape, F32)
        z1 = jnp.where(row == 0, carry_ref[SUBLANES - 1:SUBLANES, :], z1)
        z2 = jnp.where(row == 0, carry_ref[SUBLANES - 2:SUBLANES - 1, :],
                       jnp.where(row == 1, carry_ref[SUBLANES - 1:SUBLANES, :], z2))
        carry_ref[...] = zc[tm - SUBLANES:tm, :]
        tail_ref[...] = zc[tm - SUBLANES:tm, :]
    conv = cw_ref[0:1, :] * z2 + cw_ref[1:2, :] * z1 + cw_ref[2:3, :] * zc
    c_ref[:, 0:C_WIDTH] = z[:, C_WIDTH:2 * C_WIDTH] * conv
    c_ref[:, C_WIDTH:SLAB_C] = z[:, 3 * C_WIDTH:4 * C_WIDTH]


def _inproj(x, layer, g, w_in, cos, sin, convw, tm, table_blocks, seq_tiles, row_state=None, prev_caches=None):
    m = x.shape[0]
    per_row_state = row_state is not None
    aliased = prev_caches is not None
    row = lambda i: (i, 0)
    tab = lambda i: (i % table_blocks, 0)
    in_specs = [pl.BlockSpec((tm, D_MODEL), row), _resident((1, D_MODEL), layer), _resident((D_MODEL, MIX_W), layer),
                pl.BlockSpec((tm, LANES), tab), pl.BlockSpec((tm, LANES), tab), _resident((C_CONV, C_WIDTH), layer)]
    args = [x, g, w_in, cos, sin, convw]
    out_specs = [pl.BlockSpec((tm, SLAB_A), row), pl.BlockSpec((tm, SLAB_B), row), pl.BlockSpec((tm, SLAB_C), row)]
    out_shape = [jax.ShapeDtypeStruct((m, SLAB_A), F32), jax.ShapeDtypeStruct((m, SLAB_B), F32),
                 jax.ShapeDtypeStruct((m, SLAB_C), F32)]
    scratch, aliases = [], {}
    if per_row_state:
        in_specs += [pl.BlockSpec((tm, C_WIDTH), row)] * 2
        args += list(row_state)
        out_specs.append(pl.BlockSpec((tm, C_WIDTH), row))
        out_shape.append(jax.ShapeDtypeStruct((m, C_WIDTH), F32))
    else:
        scratch = [pltpu.VMEM((SUBLANES, C_WIDTH), F32)]
        out_specs.append(pl.BlockSpec((SUBLANES, C_WIDTH), row))
        out_shape.append(jax.ShapeDtypeStruct((m // tm * SUBLANES, C_WIDTH), F32))
        n_seqs = m // (tm * seq_tiles)
        widths = [(A_W, w) for w, _ in A_GROUPS for _ in range(2)] + [(B_KW, B_WINDOW)] * 2
        for j, (rows_, window) in enumerate(widths):
            per_tile = window == tm * seq_tiles
            idx = ((lambda i: (layer, i // seq_tiles, 0, i % seq_tiles)) if per_tile
                   else (lambda i: (layer, i // seq_tiles, 0, 0)))
            out_specs.append(pl.BlockSpec((None, 1, rows_, tm if per_tile else window), idx))
            out_shape.append(jax.ShapeDtypeStruct((DEPTH, n_seqs, rows_, window), F32))
        if aliased:
            aliases = {len(args) + j: 4 + j for j in range(N_CACHES)}
            in_specs += [pl.BlockSpec(memory_space=pl.ANY)] * N_CACHES
            args += list(prev_caches)
    return pl.pallas_call(
        functools.partial(_inproj_body, seq_tiles=seq_tiles, per_row_state=per_row_state, aliased=aliased),
        grid=(m // tm,),
        in_specs=in_specs,
        out_specs=out_specs,
        out_shape=out_shape,
        input_output_aliases=aliases,
        scratch_shapes=scratch,
        compiler_params=_params(1),
        name="in_proj",
    )(*args)


def _band_mask_t(nk):
    kj = lax.broadcasted_iota(jnp.int32, (nk, BLOCK), 0)
    qi = lax.broadcasted_iota(jnp.int32, (nk, BLOCK), 1)
    dist = qi + (nk - BLOCK) - kj
    return (dist >= 0) & (dist <= BLOCK)


def _softmax_t(s, mask, sink=None):
    s = jnp.where(mask, s, NEG_INF)
    m = jnp.max(s, axis=0, keepdims=True)
    if sink is not None:
        m = jnp.maximum(m, sink)
    p = jnp.exp(s - m)
    den = jnp.sum(p, axis=0, keepdims=True)
    if sink is not None:
        den = den + jnp.exp(sink - m)
    return p, 1.0 / den, m + jnp.log(den)


def _band_softmax_t(s, sink=None):
    j = lax.broadcasted_iota(jnp.int32, (BLOCK, BLOCK), 0)
    qi = lax.broadcasted_iota(jnp.int32, (BLOCK, BLOCK), 1)
    above = j > qi
    s_prev, s_own = s[0:BLOCK], s[BLOCK:2 * BLOCK]
    folded = jnp.where(above, s_prev, s_own)
    diag = jnp.sum(jnp.where(j == qi, s_prev, 0.0), axis=0, keepdims=True)
    m = jnp.maximum(jnp.max(folded, axis=0, keepdims=True), diag)
    if sink is not None:
        m = jnp.maximum(m, sink)
    p = jnp.exp(folded - m)
    p_diag = jnp.exp(diag - m)
    den = jnp.sum(p, axis=0, keepdims=True) + p_diag
    if sink is not None:
        den = den + jnp.exp(sink - m)
    unfolded = jnp.concatenate([jnp.where(above, p, 0.0), jnp.where(above, 0.0, p)], axis=0)
    return unfolded, p_diag, 1.0 / den, m + jnp.log(den)


def _attn_a_prompt_body(q_ref, k_ref, v_ref, o_ref, lse_ref, *, d):
    nb = SEQ // d // BLOCK
    pairs = q_ref.shape[-1] // LANES
    first_head_rows = lax.broadcasted_iota(jnp.int32, (LANES, BLOCK), 0) < HEAD_DIM

    def rows(start, n):
        return pl.ds(start, n) if d == 1 else pl.ds(start, n, stride=d)

    def block(q0, k0, nk):
        mask = _band_mask_t(nk)
        lo = _low_lanes(nk)
        for hp in range(pairs):
            cols = slice(hp * LANES, (hp + 1) * LANES)
            q = (q_ref[0, rows(q0, BLOCK), cols] * SCALE).astype(BF16)
            k = k_ref[0, rows(k0, nk), cols]
            v = v_ref[0, rows(k0, nk), cols]
            k0_, k1_ = jnp.where(lo, k, 0.0).astype(BF16), jnp.where(lo, 0.0, k).astype(BF16)
            vt32 = v.T
            vt = vt32.astype(BF16)
            if nk == BLOCK:
                st = _mm_nt(jnp.concatenate([k0_, k1_], axis=0), q)
                p0, r0, l0 = _softmax_t(st[0:nk], mask)
                p1, r1, l1 = _softmax_t(st[nk:2 * nk], mask)
                ot = _mm(vt, jnp.concatenate([p0, p1], axis=1).astype(BF16))
                o_t = jnp.where(first_head_rows, ot[:, 0:BLOCK] * r0, ot[:, BLOCK:2 * BLOCK] * r1)
            else:
                p0, pd0, r0, l0 = _band_softmax_t(_mm_nt(k0_, q))
                p1, pd1, r1, l1 = _band_softmax_t(_mm_nt(k1_, q))
                o_t = jnp.where(first_head_rows, (_mm(vt, p0.astype(BF16)) + vt32[:, 0:BLOCK] * pd0) * r0,
                                (_mm(vt, p1.astype(BF16)) + vt32[:, 0:BLOCK] * pd1) * r1)
            l_t = jnp.where(first_head_rows, jnp.broadcast_to(l0, (LANES, BLOCK)),
                            jnp.broadcast_to(l1, (LANES, BLOCK)))
            o_ref[0, rows(q0, BLOCK), cols] = o_t.T
            lse_ref[0, rows(q0, BLOCK), cols] = l_t.T

    for r in range(d):
        block(r, r, BLOCK)
        if nb > 1:
            def body(i, carry):
                q0 = r + i * (BLOCK * d)
                block(q0, q0 - BLOCK * d, 2 * BLOCK)
                return carry
            lax.fori_loop(1, nb, body, 0, unroll=nb - 1)


def _attn_a_prompt(slab_a, g, d):
    view = slab_a.reshape(BATCH, SEQ, SLAB_A)
    width = A_W if d == 1 else LANES
    steps = A_W // width
    spec = lambda off: pl.BlockSpec((1, SEQ, width), lambda b, hp: (b, 0, off * steps + hp))
    oshape = jax.ShapeDtypeStruct((BATCH, SEQ, A_W), F32)
    o, lse = pl.pallas_call(
        functools.partial(_attn_a_prompt_body, d=d),
        grid=(BATCH, steps),
        in_specs=[spec(g), spec(A_NG + g), spec(2 * A_NG + g)],
        out_specs=[spec(0), spec(0)],
        out_shape=[oshape, oshape],
        compiler_params=_params(2),
        name=f"attn_a{g + 1}_prompt",
    )(view, view, view)
    return o.reshape(BATCH * SEQ, A_W), lse.reshape(BATCH * SEQ, A_W)


def _attn_b_prompt_body(q_ref, k_ref, v_ref, sink_ref, o_ref, *, nb):
    def block(q0, k0, nk):
        mask = _band_mask_t(nk)
        lo = _low_lanes(nk)
        k = k_ref[0, pl.ds(k0, nk), :]
        k_sw = pltpu.roll(k, HEAD_DIM, axis=1)
        vt32 = v_ref[0, pl.ds(k0, nk), :].T
        vt = vt32.astype(BF16)
        for kv in range(B_KV_HEADS):
            on_lo, on_hi = (k, k_sw) if kv == 0 else (k_sw, k)
            k2 = jnp.concatenate([jnp.where(lo, on_lo, 0.0), jnp.where(lo, 0.0, on_hi)], axis=0).astype(BF16)
            cols = slice(2 * kv * LANES, (2 * kv + 2) * LANES)
            q2 = q_ref[0, pl.ds(q0, BLOCK), cols] * SCALE
            q2 = jnp.concatenate([q2[:, 0:LANES], q2[:, LANES:2 * LANES]], axis=0).astype(BF16)
            kv_rows = slice(kv * HEAD_DIM, (kv + 1) * HEAD_DIM)
            ps, rs, diag_terms = [], [], []
            for g in range(B_GROUP):
                half, chunk = g % 2, g // 2
                sink = sink_ref[kv * B_GROUP + g:kv * B_GROUP + g + 1, :]
                s = _mm_nt(k2[half * nk:(half + 1) * nk], q2[chunk * BLOCK:(chunk + 1) * BLOCK])
                if nk == BLOCK:
                    p, r, _ = _softmax_t(s, mask, sink)
                else:
                    p, p_diag, r, _ = _band_softmax_t(s, sink)
                    diag_terms.append(vt32[kv_rows, 0:BLOCK] * p_diag)
                ps.append(p)
                rs.append(r)
            ot = jnp.concatenate([_mm(vt, p.astype(BF16))[kv_rows, :] for p in ps], axis=1)
            if diag_terms:
                ot = ot + jnp.concatenate(diag_terms, axis=1)
            oj = ot * jnp.concatenate(rs, axis=1)
            for chunk in range(2):
                o_t = jnp.concatenate([oj[:, (2 * chunk) * BLOCK:(2 * chunk + 1) * BLOCK],
                                       oj[:, (2 * chunk + 1) * BLOCK:(2 * chunk + 2) * BLOCK]], axis=0)
                c = 2 * kv + chunk
                o_ref[0, pl.ds(q0, BLOCK), c * LANES:(c + 1) * LANES] = o_t.T

    block(0, 0, BLOCK)

    def body(i, carry):
        q0 = pl.multiple_of(i * BLOCK, BLOCK)
        block(q0, pl.multiple_of(q0 - BLOCK, BLOCK), 2 * BLOCK)
        return carry
    lax.fori_loop(1, nb, body, 0, unroll=15)


def _attn_b_prompt(slab_b, layer, sinks):
    view = slab_b.reshape(BATCH, SEQ, SLAB_B)
    nq = B_QW // LANES
    o = pl.pallas_call(
        functools.partial(_attn_b_prompt_body, nb=SEQ // BLOCK),
        grid=(BATCH,),
        in_specs=[pl.BlockSpec((1, SEQ, B_QW), lambda b: (b, 0, 0)),
                  pl.BlockSpec((1, SEQ, B_KW), lambda b: (b, 0, nq)),
                  pl.BlockSpec((1, SEQ, B_KW), lambda b: (b, 0, nq + 1)),
                  _resident((B_Q_HEADS, LANES), layer)],
        out_specs=pl.BlockSpec((1, SEQ, B_QW), lambda b: (b, 0, 0)),
        out_shape=jax.ShapeDtypeStruct((BATCH, SEQ, B_QW), F32),
        compiler_params=_params(1),
        name="attn_b_prompt",
    )(view, view, view, sinks)
    return o.reshape(BATCH * SEQ, B_QW)


def _sample_masks(rows, lb, window, dil):
    assert dil & (dil - 1) == 0 and DEC_SEQ & (DEC_SEQ - 1) == 0
    t_c = lax.broadcasted_iota(jnp.int32, (rows, lb), 0) & (DEC_SEQ - 1)
    dist_c = lb + t_c - lax.broadcasted_iota(jnp.int32, (rows, lb), 1)
    mask_c = (dist_c >= 0) & (dist_c <= window) & ((dist_c & (dil - 1)) == 0)
    t_n = lax.broadcasted_iota(jnp.int32, (rows, DEC_SEQ), 0) & (DEC_SEQ - 1)
    dist_n = t_n - lax.broadcasted_iota(jnp.int32, (rows, DEC_SEQ), 1)
    mask_n = (dist_n >= 0) & (dist_n <= window) & ((dist_n & (dil - 1)) == 0)
    return mask_c, mask_n


def _sample_attend(q, kt, vt, kn, vn, masks, sink=None):
    mask_c, mask_n = masks
    s_c = jnp.where(mask_c, _mm(q.astype(BF16), kt), NEG_INF)
    s_n = jnp.where(mask_n, _mm_nt(q.astype(BF16), kn.astype(BF16)), NEG_INF)
    m = jnp.maximum(jnp.max(s_c, axis=-1, keepdims=True), jnp.max(s_n, axis=-1, keepdims=True))
    if sink is not None:
        m = jnp.maximum(m, sink)
    p_c = jnp.exp(s_c - m)
    p_n = jnp.exp(s_n - m)
    den = jnp.sum(p_c, axis=-1, keepdims=True) + jnp.sum(p_n, axis=-1, keepdims=True)
    if sink is not None:
        den = den + jnp.exp(sink - m)
    o = _mm_nt(p_c.astype(BF16), vt) + _mm(p_n.astype(BF16), vn.astype(BF16))
    return o * (1.0 / den), m + jnp.log(den)


def _write_shifted_cache(dst_ref, src_ref, s, heads, new_rows, lb):
    padded = jnp.concatenate([jnp.zeros((LANES - DEC_SEQ, LANES), F32), new_rows], axis=0)
    new_t = padded.T
    keep = lax.broadcasted_iota(jnp.int32, (LANES, LANES), 1) < LANES - DEC_SEQ
    rotated = lambda j: pltpu.roll(src_ref[0, s, heads, :, j * LANES:(j + 1) * LANES].reshape(LANES, LANES),
                                   LANES - DEC_SEQ, axis=1)
    n_tiles = lb // LANES
    cur = rotated(0)
    for j in range(n_tiles):
        nxt = rotated(j + 1) if j + 1 < n_tiles else new_t
        dst_ref[0, s, heads, :, j * LANES:(j + 1) * LANES] = jnp.where(keep, cur, nxt).reshape(2, HEAD_DIM, LANES)
        cur = nxt


SAMPLE_SEQS_PER_STEP = 2


def _sample_attn_body(*refs, aliased, n_seq):
    for s in range(n_seq):
        _sample_attn_one(refs, aliased, s)


def _sample_attn_one(refs, aliased, s):
    sa_ref, sb_ref, sink_ref = refs[:3]
    cache_refs = refs[3:11]
    pos = 11 + (8 if aliased else 0)
    oa_ref, ob_ref = refs[pos:pos + 2]
    out_refs = refs[pos + 2:pos + 10]
    lo8 = _low_lanes(DEC_SEQ)
    seq_rows = slice(s * DEC_SEQ, (s + 1) * DEC_SEQ)

    o_g, l_g = [], []
    for g, (window, dil) in enumerate(A_GROUPS):
        kc_ref, vc_ref, ko_ref, vo_ref = cache_refs[2 * g], cache_refs[2 * g + 1], out_refs[2 * g], out_refs[2 * g + 1]
        lb = kc_ref.shape[-1]
        masks = _sample_masks(2 * DEC_SEQ, lb, window, dil)
        o_pairs, l_pairs = [], []
        for c in range(A_W // LANES):
            col = lambda part: slice(part * A_QKV + g * A_W + c * LANES, part * A_QKV + g * A_W + (c + 1) * LANES)
            heads = slice(2 * c, 2 * c + 2)
            q = sa_ref[seq_rows, col(0)] * SCALE
            kn, vn = sa_ref[seq_rows, col(1)], sa_ref[seq_rows, col(2)]
            kt = kc_ref[0, s, heads].reshape(LANES, lb)
            vt = vc_ref[0, s, heads].reshape(LANES, lb)
            q2 = jnp.concatenate([jnp.where(lo8, q, 0.0), jnp.where(lo8, 0.0, q)], axis=0)
            o, lse = _sample_attend(q2, kt.astype(BF16), vt.astype(BF16), kn, vn, masks)
            o_pairs.append(jnp.where(lo8, o[0:DEC_SEQ], o[DEC_SEQ:]))
            l_pairs.append(jnp.where(lo8, lse[0:DEC_SEQ], lse[DEC_SEQ:]))
            _write_shifted_cache(ko_ref, kc_ref, s, heads, kn, lb)
            _write_shifted_cache(vo_ref, vc_ref, s, heads, vn, lb)
        o_g.append(o_pairs)
        l_g.append(l_pairs)
    for c in range(A_W // LANES):
        l1, l2, l3 = l_g[0][c], l_g[1][c], l_g[2][c]
        lm = jnp.maximum(jnp.maximum(l1, l2), l3)
        e1, e2, e3 = jnp.exp(l1 - lm), jnp.exp(l2 - lm), jnp.exp(l3 - lm)
        es = e1 + e2 + e3
        oa_ref[seq_rows, c * LANES:(c + 1) * LANES] = ((e1 / es) * o_g[0][c] + (e2 / es) * o_g[1][c]
                                                       + (e3 / es) * o_g[2][c])

    kc_ref, vc_ref, ko_ref, vo_ref = cache_refs[6], cache_refs[7], out_refs[6], out_refs[7]
    lb = kc_ref.shape[-1]
    rows = B_GROUP * DEC_SEQ
    masks = _sample_masks(rows, lb, B_WINDOW, 1)
    both = slice(0, B_KV_HEADS)
    ktb = kc_ref[0, s].reshape(LANES, lb).astype(BF16)
    vtb = vc_ref[0, s].reshape(LANES, lb).astype(BF16)
    kn, vn = sb_ref[seq_rows, B_QW:B_QW + B_KW], sb_ref[seq_rows, B_QW + B_KW:SLAB_B]
    halves = {}
    for kv in range(B_KV_HEADS):
        parts = []
        for gq in range(B_GROUP):
            head = kv * B_GROUP + gq
            q = sb_ref[seq_rows, (head // 2) * LANES:(head // 2 + 1) * LANES] * SCALE
            q = jnp.where(lo8, q, 0.0) if head % 2 == 0 else jnp.where(lo8, 0.0, q)
            parts.append(q if head % 2 == kv else pltpu.roll(q, HEAD_DIM, axis=1))
        q4 = jnp.concatenate(parts, axis=0)
        sink = sink_ref[kv * rows:(kv + 1) * rows, 0:1]
        o, _ = _sample_attend(q4, ktb, vtb, kn, vn, masks, sink)
        for gq in range(B_GROUP):
            head = kv * B_GROUP + gq
            oh = o[gq * DEC_SEQ:(gq + 1) * DEC_SEQ]
            halves[head] = oh if head % 2 == kv else pltpu.roll(oh, HEAD_DIM, axis=1)
    for c in range(B_QW // LANES):
        ob_ref[seq_rows, c * LANES:(c + 1) * LANES] = jnp.where(lo8, halves[2 * c], halves[2 * c + 1])
    _write_shifted_cache(ko_ref, kc_ref, s, both, kn, lb)
    _write_shifted_cache(vo_ref, vc_ref, s, both, vn, lb)


def _sample_attn(layer, slab_a, slab_b, sinks, caches, prev):
    aliased = prev is not None
    n_seq = SAMPLE_SEQS_PER_STEP
    rows = n_seq * DEC_SEQ
    row = lambda b: (b, 0)
    cspec = lambda c: pl.BlockSpec((1, n_seq) + c.shape[2:], lambda b: (layer, b, 0, 0, 0))
    in_specs = [pl.BlockSpec((rows, SLAB_A), row), pl.BlockSpec((rows, SLAB_B), row),
                _resident(sinks.shape[1:], layer)]
    in_specs += [cspec(c) for c in caches]
    args = [slab_a, slab_b, sinks] + list(caches)
    aliases = {}
    if aliased:
        aliases = {len(args) + j: 2 + j for j in range(8)}
        in_specs += [pl.BlockSpec(memory_space=pl.ANY)] * 8
        args += list(prev)
    m = slab_a.shape[0]
    res = pl.pallas_call(
        functools.partial(_sample_attn_body, aliased=aliased, n_seq=n_seq),
        grid=(DEC_BATCH // n_seq,),
        in_specs=in_specs,
        out_specs=[pl.BlockSpec((rows, A_W), row), pl.BlockSpec((rows, B_QW), row)] + [cspec(c) for c in caches],
        out_shape=[jax.ShapeDtypeStruct((m, A_W), F32), jax.ShapeDtypeStruct((m, B_QW), F32)]
                  + [jax.ShapeDtypeStruct(c.shape, F32) for c in caches],
        input_output_aliases=aliases,
        compiler_params=_params(1),
        name="sample_attn",
    )(*args)
    return res[0], res[1], list(res[2:])


def _ssm_body(u_ref, are_ref, aim_ref, b_ref, cbig_ref, dvec_ref, wglu_ref, x0re_ref, x0im_ref,
              od_ref, sre_ref, sim_ref, bu_ref, xs_ref, *, steps, pitch):
    @pl.when(pl.program_id(1) == 0)
    def _():
        sre_ref[0] = x0re_ref[0]
        sim_ref[0] = x0im_ref[0]

    n_re = D_NS // LANES
    tiles = lambda x: [x[:, j * LANES:(j + 1) * LANES] for j in range(x.shape[1] // LANES)]

    plane = lambda s: slice(s * pitch, s * pitch + steps)
    u = u_ref[...].reshape(SUBLANES * steps, D_WIDTH)
    bu = _mm(u.astype(BF16), b_ref[...])
    for s in range(SUBLANES):
        for j, tile in enumerate(tiles(bu[s * steps:(s + 1) * steps])):
            bu_ref[j, plane(s), :] = tile
    a_re = are_ref[...]
    a_im = aim_ref[...]

    def step(t, carry):
        re, im = carry
        at_t = pl.ds(t, SUBLANES, stride=pitch)
        b_re = jnp.concatenate([bu_ref[j, at_t, :] for j in range(n_re)], axis=1)
        b_im = jnp.concatenate([bu_ref[n_re + j, at_t, :] for j in range(n_re)], axis=1)
        re, im = a_re * re - a_im * im + b_re, a_re * im + a_im * re + b_im
        for j, (tr, ti) in enumerate(zip(tiles(re), tiles(im))):
            xs_ref[j, at_t, :] = tr
            xs_ref[n_re + j, at_t, :] = ti
        return re, im

    re, im = lax.fori_loop(0, steps, step, (sre_ref[0], sim_ref[0]), unroll=min(steps, SUBLANES))
    sre_ref[0] = re
    sim_ref[0] = im
    group = SUBLANES // 2 if SUBLANES * steps >= 2 * SAMPLE_TILE else SUBLANES
    for s0 in range(0, SUBLANES, group):
        seqs = range(s0, s0 + group)
        xs = jnp.concatenate([jnp.concatenate([xs_ref[j, plane(s), :] for j in range(2 * n_re)], axis=1)
                              for s in seqs], axis=0)
        y = _mm(xs.astype(BF16), cbig_ref[...]) + dvec_ref[...] * u[s0 * steps:(s0 + group) * steps]
        y = _mm(jax.nn.gelu(y).astype(BF16), wglu_ref[...])
        od = y[:, 0:D_WIDTH] * jax.nn.sigmoid(y[:, D_WIDTH:2 * D_WIDTH])
        od_ref[s0:s0 + group] = od.reshape(group, steps, D_WIDTH)


def _ssm(slab_c, bsz, length, layer, a_re, a_im, bbig, cbig, dvec, wglu, x0_re, x0_im, steps):
    n = bsz // SUBLANES
    pitch = steps if steps <= SUBLANES else steps + SUBLANES
    view = slab_c.reshape(bsz, length, SLAB_C)
    u_spec = pl.BlockSpec((SUBLANES, steps, D_WIDTH), lambda i, c: (i, c, SLAB_C // D_WIDTH - 1))
    o_spec = pl.BlockSpec((SUBLANES, steps, D_WIDTH), lambda i, c: (i, c, 0))
    st_spec = pl.BlockSpec((1, SUBLANES, D_NS), lambda i, c: (i, 0, 0))
    st_shape = jax.ShapeDtypeStruct((n, SUBLANES, D_NS), F32)
    od, s_re, s_im = pl.pallas_call(
        functools.partial(_ssm_body, steps=steps, pitch=pitch),
        grid=(n, length // steps),
        in_specs=[u_spec, _resident((SUBLANES, D_NS), layer), _resident((SUBLANES, D_NS), layer),
                  _resident((D_WIDTH, 2 * D_NS), layer), _resident((2 * D_NS, D_WIDTH), layer),
                  _resident((1, D_WIDTH), layer), _resident((D_WIDTH, 2 * D_WIDTH), layer), st_spec, st_spec],
        out_specs=[o_spec, st_spec, st_spec],
        out_shape=[jax.ShapeDtypeStruct((bsz, length, D_WIDTH), F32), st_shape, st_shape],
        scratch_shapes=[pltpu.VMEM((2 * D_NS // LANES, SUBLANES * pitch, LANES), F32)] * 2,
        compiler_params=_params(2),
        name="ssm",
    )(view, a_re, a_im, bbig, cbig, dvec, wglu, x0_re, x0_im)
    return od.reshape(bsz * length, D_WIDTH), s_re, s_im


def _merge_body(*refs, n_groups):
    x_ref = refs[0]
    a_refs = refs[1:1 + (2 * n_groups if n_groups > 1 else 1)]
    ob_ref, oc_ref, od_ref, gpre_ref, wgl_ref, wa_ref, wb_ref, wc_ref, wd_ref, wout_ref, gpost_ref, y_ref = \
        refs[1 + len(a_refs):]
    tm = x_ref.shape[0]
    for rows in _row_halves(tm):
        x = x_ref[rows, :]
        h = _rmsnorm(x, gpre_ref[...]).astype(BF16)
        if n_groups > 1:
            lses = [a_refs[2 * g + 1][rows, :] for g in range(n_groups)]
            lm = functools.reduce(jnp.maximum, lses)
            es = [jnp.exp(l - lm) for l in lses]
            den = functools.reduce(lambda a, b: a + b, es)
            o_a = functools.reduce(lambda a, b: a + b,
                                   [(es[g] / den) * a_refs[2 * g][rows, :] for g in range(n_groups)])
        else:
            o_a = a_refs[0][rows, :]
        branches = ((o_a, wa_ref), (ob_ref[rows, :], wb_ref), (oc_ref[rows, :], wc_ref), (od_ref[rows, :], wd_ref))
        merged = None
        for j in (1, 2, 3, 0):
            o, w_ref = branches[j]
            proj = _mm(o.astype(BF16), w_ref[...])
            term = jax.nn.sigmoid(_mm(h, wgl_ref[:, j * D_MODEL:(j + 1) * D_MODEL])) * proj
            merged = term if merged is None else merged + term
        mix = _mm(merged.astype(BF16), wout_ref[...])
        y_ref[rows, :] = x + _rmsnorm(mix, gpost_ref[...])


def _merge(x, a_parts, ob, slab_c, od, layer, gpre, w_in, wa, wb, wc, wd, wout, gpost, tm):
    m = x.shape[0]
    row = lambda i: (i, 0)
    r256 = pl.BlockSpec((tm, 256), row)
    n_groups = max(1, len(a_parts) // 2)
    return pl.pallas_call(
        functools.partial(_merge_body, n_groups=n_groups),
        grid=(m // tm,),
        in_specs=[pl.BlockSpec((tm, D_MODEL), row)] + [r256] * len(a_parts)
                 + [pl.BlockSpec((tm, B_QW), row), r256, r256,
                    _resident((1, D_MODEL), layer), _resident((D_MODEL, N_BRANCH * D_MODEL), layer, col=1),
                    _resident((A_W, D_MODEL), layer), _resident((B_QW, D_MODEL), layer),
                    _resident((C_WIDTH, D_MODEL), layer), _resident((D_WIDTH, D_MODEL), layer),
                    _resident((D_MODEL, D_MODEL), layer), _resident((1, D_MODEL), layer)],
        out_specs=pl.BlockSpec((tm, D_MODEL), row),
        out_shape=jax.ShapeDtypeStruct((m, D_MODEL), F32),
        compiler_params=_params(1),
        name="merge",
    )(x, *a_parts, ob, slab_c, od, gpre, w_in, wa, wb, wc, wd, wout, gpost)


def _ffn_body(x_ref, p_ref, gpre_ref, wg_ref, wu_ref, wdn_ref, gpost_ref, wple_ref, wpg_ref, y_ref):
    x = x_ref[...]
    h = _rmsnorm(x, gpre_ref[...]).astype(BF16)
    act = jax.nn.silu(_mm(h, wg_ref[...])) * _mm(h, wu_ref[...])
    f = _mm(act.astype(BF16), wdn_ref[...])
    x = x + _rmsnorm(f, gpost_ref[...])
    gate = jax.nn.sigmoid(_mm(x.astype(BF16), wpg_ref[...]))
    y_ref[...] = x + gate * _mm(p_ref[...].astype(BF16), wple_ref[...])


def _ffn(x, p_all, layer, gpre, wg, wu, wdn, gpost, wple, wpg, tm):
    m = x.shape[0]
    row = lambda i: (i, 0)
    return pl.pallas_call(
        _ffn_body,
        grid=(m // tm,),
        in_specs=[pl.BlockSpec((tm, D_MODEL), row), pl.BlockSpec((None, tm, PLE_DIM), lambda i: (layer, i, 0)),
                  _resident((1, D_MODEL), layer), _resident((D_MODEL, D_FF), layer), _resident((D_MODEL, D_FF), layer),
                  _resident((D_FF, D_MODEL), layer), _resident((1, D_MODEL), layer),
                  _resident((PLE_DIM, D_MODEL), layer), _resident((D_MODEL, D_MODEL), layer)],
        out_specs=pl.BlockSpec((tm, D_MODEL), row),
        out_shape=jax.ShapeDtypeStruct((m, D_MODEL), F32),
        compiler_params=_params(1),
        name="ffn_ple",
    )(x, p_all, gpre, wg, wu, wdn, gpost, wple, wpg)


def _rope_tables(pos):
    inv = ROPE_THETA ** (-jnp.arange(HALF, dtype=F32) / HALF)
    ang = pos.astype(F32)[:, None] * inv[None, :]
    cos, sin = jnp.cos(ang), jnp.sin(ang)
    cos_t = jnp.tile(cos, (1, LANES // HALF))
    sin_t = jnp.tile(jnp.concatenate([-sin, sin], axis=1), (1, LANES // HEAD_DIM))
    return cos_t, sin_t


def _ssm_weights(lam_re, lam_im, log_dt, b_re, b_im, c_re, c_im):
    lam = lax.complex(lam_re, lam_im)
    dt = jnp.exp(log_dt)[..., None]
    a_bar = jnp.exp(lam * dt)
    b_bar = ((a_bar - 1.0) / lam)[..., None] * lax.complex(b_re, b_im)
    eye = jnp.eye(D_NGROUPS, dtype=F32)
    pack_b = lambda b: jnp.einsum('lgni,gh->lgihn', b, eye).reshape(DEPTH, D_WIDTH, D_NS)
    pack_c = lambda c: jnp.einsum('lgin,gh->lgnhi', c, eye).reshape(DEPTH, D_NS, D_WIDTH)
    bbig = jnp.concatenate([pack_b(jnp.real(b_bar)), pack_b(jnp.imag(b_bar))], axis=2)
    cbig = jnp.concatenate([pack_c(c_re), -pack_c(c_im)], axis=1)
    tile8 = lambda a: jnp.broadcast_to(a.reshape(DEPTH, 1, D_NS), (DEPTH, SUBLANES, D_NS))
    return tile8(jnp.real(a_bar)), tile8(jnp.imag(a_bar)), bbig.astype(BF16), cbig.astype(BF16)


def _key_major(c):
    return jnp.transpose(c, (0, 1, 3, 4, 2))


def _row_major(c):
    return jnp.transpose(c, (0, 1, 4, 2, 3))


def kernel(x_prompt, x_sample, p_prompt, p_sample, cache_a1_k, cache_a1_v, cache_a2_k, cache_a2_v, cache_a3_k, cache_a3_v, cache_b_k, cache_b_v, state_c_conv, state_d_re, state_d_im, norm_mix_pre, norm_mix_post, norm_ffn_pre, norm_ffn_post, w_in, attn_sinks, conv_c_w, ssm_lam_re, ssm_lam_im, ssm_log_dt, ssm_b_re, ssm_b_im, ssm_c_re, ssm_c_im, ssm_d, w_d_glu, w_br_a, w_br_b, w_br_c, w_br_d, w_out, w_ffn_gate, w_ffn_up, w_ffn_down, w_ple, w_ple_gate):
    assert x_prompt.shape == (BATCH, SEQ, D_MODEL) and x_sample.shape == (DEC_BATCH, DEC_SEQ, D_MODEL)
    assert w_in.shape == (DEPTH, D_MODEL, 2 * MIX_W)
    assert all(min(w, PAST_LEN) == w for w, _ in A_GROUPS) and min(B_WINDOW, PAST_LEN) == B_WINDOW
    mp, ms = BATCH * SEQ, DEC_BATCH * DEC_SEQ
    tm_p, tm_s = PROMPT_TILE, SAMPLE_TILE

    cos_p, sin_p = _rope_tables(jnp.arange(SEQ, dtype=jnp.int32))
    cos_s, sin_s = _rope_tables(PAST_LEN + jnp.arange(DEC_SEQ, dtype=jnp.int32))
    cos_s, sin_s = jnp.tile(cos_s, (tm_s // DEC_SEQ, 1)), jnp.tile(sin_s, (tm_s // DEC_SEQ, 1))

    caches = [_key_major(c) for c in (cache_a1_k, cache_a1_v, cache_a2_k, cache_a2_v, cache_a3_k, cache_a3_v,
                                      cache_b_k, cache_b_v)]
    yp = x_prompt.reshape(mp, D_MODEL)
    ys = x_sample.reshape(ms, D_MODEL)
    new_caches = None
    prompt_caches = None
    st_p = [[] for _ in range(3)]
    conv_s, dre_s, dim_s = [], [], []

    rows3 = lambda a: a.reshape(DEPTH, 1, -1)
    w_in_b = w_in.astype(BF16)
    g_mix_pre = rows3(norm_mix_pre)
    ssm_w = _ssm_weights(ssm_lam_re, ssm_lam_im, ssm_log_dt, ssm_b_re, ssm_b_im, ssm_c_re, ssm_c_im) + (
        rows3(ssm_d), w_d_glu.astype(BF16))
    sinks_p = jnp.broadcast_to(attn_sinks.reshape(DEPTH, B_Q_HEADS, 1), (DEPTH, B_Q_HEADS, LANES))
    sinks_s = jnp.broadcast_to(jnp.repeat(attn_sinks.reshape(DEPTH, B_Q_HEADS), DEC_SEQ, axis=1)[..., None],
                               (DEPTH, B_Q_HEADS * DEC_SEQ, LANES))
    merge_w = (g_mix_pre, w_in_b, w_br_a.astype(BF16), w_br_b.astype(BF16), w_br_c.astype(BF16),
               w_br_d.astype(BF16), w_out.astype(BF16), rows3(norm_mix_post))
    ffn_w = (rows3(norm_ffn_pre), w_ffn_gate.astype(BF16), w_ffn_up.astype(BF16), w_ffn_down.astype(BF16),
             rows3(norm_ffn_post), w_ple.astype(BF16), w_ple_gate.astype(BF16))
    zero_state = jnp.zeros((BATCH // SUBLANES, SUBLANES, D_NS), F32)

    for i in range(DEPTH):
        sa, sb, sc, tail, *prompt_caches = _inproj(yp, i, g_mix_pre, w_in_b, cos_p, sin_p, conv_c_w, tm_p,
                                                   SEQ // tm_p, SEQ // tm_p, prev_caches=prompt_caches)
        a_parts = []
        for g, (w, d) in enumerate(A_GROUPS):
            a_parts += list(_attn_a_prompt(sa, g, d))
        ob = _attn_b_prompt(sb, i, sinks_p)
        od, s_re, s_im = _ssm(sc, BATCH, SEQ, i, *ssm_w, zero_state, zero_state, SSM_CHUNK)
        x1 = _merge(yp, a_parts, ob, sc, od, i, *merge_w, tm_p)
        yp = _ffn(x1, p_prompt.reshape(DEPTH, mp, PLE_DIM), i, *ffn_w, tm_p)
        st_p[0].append(tail.reshape(BATCH, SEQ // tm_p, SUBLANES, C_WIDTH)[:, -1, SUBLANES - (C_CONV - 1):])
        st_p[1].append(s_re.reshape(BATCH, D_NGROUPS, D_STATE))
        st_p[2].append(s_im.reshape(BATCH, D_NGROUPS, D_STATE))

        s1 = jnp.broadcast_to(state_c_conv[i][:, None, 1, :], (DEC_BATCH, DEC_SEQ, C_WIDTH)).reshape(ms, C_WIDTH)
        s2 = jnp.broadcast_to(state_c_conv[i][:, None, 0, :], (DEC_BATCH, DEC_SEQ, C_WIDTH)).reshape(ms, C_WIDTH)
        sa, sb, sc, tail = _inproj(ys, i, g_mix_pre, w_in_b, cos_s, sin_s, conv_c_w, tm_s, 1, 1, row_state=(s1, s2))
        o_a, ob, new_caches = _sample_attn(i, sa, sb, sinks_s, caches, new_caches)
        x0_re = state_d_re[i].reshape(DEC_BATCH // SUBLANES, SUBLANES, D_NS)
        x0_im = state_d_im[i].reshape(DEC_BATCH // SUBLANES, SUBLANES, D_NS)
        od, s_re, s_im = _ssm(sc, DEC_BATCH, DEC_SEQ, i, *ssm_w, x0_re, x0_im, DEC_SEQ)
        x1 = _merge(ys, [o_a], ob, sc, od, i, *merge_w, tm_s)
        ys = _ffn(x1, p_sample.reshape(DEPTH, ms, PLE_DIM), i, *ffn_w, tm_s)
        conv_s.append(tail.reshape(DEC_BATCH, DEC_SEQ, C_WIDTH)[:, DEC_SEQ - (C_CONV - 1):])
        dre_s.append(s_re.reshape(DEC_BATCH, D_NGROUPS, D_STATE))
        dim_s.append(s_im.reshape(DEC_BATCH, D_NGROUPS, D_STATE))

    heads_of = lambda c: c.reshape(c.shape[:2] + (c.shape[2] // HEAD_DIM, HEAD_DIM, c.shape[3]))
    p_states = [_row_major(heads_of(c)) for c in prompt_caches] + [jnp.stack(s) for s in st_p]
    s_states = [_row_major(c) for c in new_caches] + [jnp.stack(conv_s), jnp.stack(dre_s), jnp.stack(dim_s)]
    out = [yp.reshape(BATCH, SEQ, D_MODEL), ys.reshape(DEC_BATCH, DEC_SEQ, D_MODEL)]
    for a, b in zip(p_states, s_states):
        out += [a, b]
    return tuple(out)
```
